```python
import jax, jax.numpy as jnp
from jax import lax
import numpy as np

D_MODEL = 1024
BATCH = 1
SEQ = 16384
DEPTH = 1
DEC_BATCH = 32
DEC_SEQ = 32
PAST_LEN = 1024

CHUNK = 64
WINDOW = 128
WINDOW_CHUNKS = WINDOW // CHUNK
N_HEADS = 8
N_KV_HEADS = 2
HEAD_DIM = 64
GROUP = N_HEADS // N_KV_HEADS
ATTN_DIM = N_HEADS * HEAD_DIM
KV_DIM = N_KV_HEADS * HEAD_DIM
CONV_DIM = 512
CONV_WIDTH = 3
PLE_DIM = 256
ROPE_THETA = 10000.0
EPS = 1e-6
NEG = -1e30
SPLIT_SIZES = (ATTN_DIM, KV_DIM, KV_DIM, ATTN_DIM, CONV_DIM, CONV_DIM, CONV_DIM, CONV_DIM, D_MODEL, D_MODEL)
IN_DIM = 2 * ATTN_DIM + 2 * KV_DIM + 4 * CONV_DIM + 2 * D_MODEL

kernel_name = "hybrid_swa_sink_shortconv_stream_step"


def rmsnorm(x, g):
    xf = x.astype(jnp.float32)
    y = xf * lax.rsqrt(jnp.mean(xf * xf, axis=-1, keepdims=True) + EPS)
    return (y * g.astype(jnp.float32)).astype(x.dtype)


def rope(x, pos):
    half = HEAD_DIM // 2
    inv_freq = ROPE_THETA ** (-jnp.arange(0, half, dtype=jnp.float32) * 2.0 / HEAD_DIM)
    ang = pos[:, None] * inv_freq[None, :]
    cos = jnp.cos(ang)[None, :, None, :]
    sin = jnp.sin(ang)[None, :, None, :]
    xf = x.astype(jnp.float32)
    x1, x2 = xf[..., :half], xf[..., half:]
    return jnp.concatenate([x1 * cos - x2 * sin, x2 * cos + x1 * sin], axis=-1).astype(x.dtype)


def sink_softmax(s, mask, sink):
    s = jnp.where(mask, s, NEG)
    m = jnp.maximum(jnp.max(s, axis=-1, keepdims=True), sink)
    e = jnp.exp(s - m)
    return e / (jnp.sum(e, axis=-1, keepdims=True) + jnp.exp(sink - m))


def attend_prompt(q, k, v, sink):
    b, t = q.shape[0], q.shape[1]
    nc = t // CHUNK
    nb = WINDOW_CHUNKS + 1
    scale = HEAD_DIM ** -0.5
    qc = q.reshape(b, nc, CHUNK, N_KV_HEADS, GROUP, HEAD_DIM) * scale
    pad = ((0, 0), (WINDOW_CHUNKS, 0), (0, 0), (0, 0), (0, 0))
    kp = jnp.pad(k.reshape(b, nc, CHUNK, N_KV_HEADS, HEAD_DIM), pad)
    vp = jnp.pad(v.reshape(b, nc, CHUNK, N_KV_HEADS, HEAD_DIM), pad)
    kb = jnp.concatenate([kp[:, j:j + nc] for j in range(nb)], axis=2)
    vb = jnp.concatenate([vp[:, j:j + nc] for j in range(nb)], axis=2)
    key_chunk = jnp.arange(nc)[:, None] - WINDOW_CHUNKS + (jnp.arange(nb * CHUNK) // CHUNK)[None, :]
    mask = (key_chunk >= 0)[None, :, None, None, None, :]
    s = jnp.einsum('bncvgd,bnjvd->bnvgcj', qc, kb, preferred_element_type=jnp.float32)
    pr = sink_softmax(s, mask, sink.astype(jnp.float32).reshape(N_KV_HEADS, GROUP, 1, 1))
    o = jnp.einsum('bnvgcj,bnjvd->bncvgd', pr.astype(v.dtype), vb)
    return o.reshape(b, t, ATTN_DIM), k[:, -WINDOW:], v[:, -WINDOW:]


def attend_sample(q, k, v, sink, cache_k, cache_v):
    b, t = q.shape[0], q.shape[1]
    L = cache_k.shape[1]
    scale = HEAD_DIM ** -0.5
    kk = jnp.concatenate([cache_k, k], axis=1)
    vv = jnp.concatenate([cache_v, v], axis=1)
    q_pos = PAST_LEN + jnp.arange(t)
    k_pos = jnp.concatenate([PAST_LEN - L + jnp.arange(L), q_pos])
    qch = (q_pos // CHUNK)[:, None]
    kch = (k_pos // CHUNK)[None, :]
    mask = ((kch <= qch) & (kch >= qch - WINDOW_CHUNKS))[None, None, None]
    qg = q.reshape(b, t, N_KV_HEADS, GROUP, HEAD_DIM) * scale
    s = jnp.einsum('btvgd,bjvd->bvgtj', qg, kk, preferred_element_type=jnp.float32)
    pr = sink_softmax(s, mask, sink.astype(jnp.float32).reshape(N_KV_HEADS, GROUP, 1, 1))
    o = jnp.einsum('bvgtj,bjvd->btvgd', pr.astype(vv.dtype), vv)
    return o.reshape(b, t, ATTN_DIM), kk[:, -L:], vv[:, -L:]


def hybrid_layer(x, p, pos, conv_past, attend, ln_g, w_in, q_norm_g, k_norm_g, sink, conv_w,
                 w_attn_out, w_conv_out, w_o, w_ple_gate, w_ple_proj):
    b, t, _ = x.shape
    h = rmsnorm(x, ln_g)
    split_idx = np.cumsum(SPLIT_SIZES)[:-1].tolist()
    (q, k, v, gate_a, b_gate, c_gate, u_in, gate_c, merge_a, merge_c) = jnp.split(h @ w_in, split_idx, axis=-1)
    q = rope(rmsnorm(q.reshape(b, t, N_HEADS, HEAD_DIM), q_norm_g), pos)
    k = rope(rmsnorm(k.reshape(b, t, N_KV_HEADS, HEAD_DIM), k_norm_g), pos)
    v = v.reshape(b, t, N_KV_HEADS, HEAD_DIM)
    attn, new_k, new_v = attend(q, k, v, sink)
    u = c_gate * u_in
    up = jnp.concatenate([conv_past.astype(u.dtype), u], axis=1)
    conv = up[:, 0:t] * conv_w[0]
    for j in range(1, CONV_WIDTH):
        conv = conv + up[:, j:j + t] * conv_w[j]
    new_conv = up[:, -(CONV_WIDTH - 1):]
    y_a = (attn * jax.nn.silu(gate_a)) @ w_attn_out
    y_c = (b_gate * conv * jax.nn.silu(gate_c)) @ w_conv_out
    r = x + (jax.nn.sigmoid(merge_a) * y_a + jax.nn.sigmoid(merge_c) * y_c) @ w_o
    r = r + jax.nn.sigmoid(r @ w_ple_gate) * (p @ w_ple_proj)
    return r, new_k, new_v, new_conv


def setup_inputs(seed: int = 0) -> dict:
    key = jax.random.key(seed)
    ks = jax.random.split(key, 20)
    f32 = jnp.float32
    L = min(WINDOW, PAST_LEN)
    nrm = lambda k_, shape, s: jax.random.normal(k_, shape, f32) * s
    return {
        "x_prompt": nrm(ks[0], (BATCH, SEQ, D_MODEL), 1.0),
        "x_sample": nrm(ks[1], (DEC_BATCH, DEC_SEQ, D_MODEL), 1.0),
        "p_prompt": nrm(ks[2], (DEPTH, BATCH, SEQ, PLE_DIM), 1.0),
        "p_sample": nrm(ks[3], (DEPTH, DEC_BATCH, DEC_SEQ, PLE_DIM), 1.0),
        "cache_k": nrm(ks[4], (DEPTH, DEC_BATCH, L, N_KV_HEADS, HEAD_DIM), 1.0),
        "cache_v": nrm(ks[5], (DEPTH, DEC_BATCH, L, N_KV_HEADS, HEAD_DIM), 1.0),
        "state_conv": nrm(ks[6], (DEPTH, DEC_BATCH, CONV_WIDTH - 1, CONV_DIM), 1.0),
        "ln_g": 1.0 + nrm(ks[7], (DEPTH, D_MODEL), 0.05),
        "w_in": nrm(ks[8], (DEPTH, D_MODEL, IN_DIM), D_MODEL ** -0.5),
        "q_norm_g": 1.0 + nrm(ks[9], (DEPTH, HEAD_DIM), 0.05),
        "k_norm_g": 1.0 + nrm(ks[10], (DEPTH, HEAD_DIM), 0.05),
        "sink": nrm(ks[11], (DEPTH, N_HEADS), 0.5),
        "conv_w": nrm(ks[12], (DEPTH, CONV_WIDTH, CONV_DIM), CONV_WIDTH ** -0.5),
        "w_attn_out": nrm(ks[13], (DEPTH, ATTN_DIM, D_MODEL), ATTN_DIM ** -0.5),
        "w_conv_out": nrm(ks[14], (DEPTH, CONV_DIM, D_MODEL), CONV_DIM ** -0.5),
        "w_o": nrm(ks[15], (DEPTH, D_MODEL, D_MODEL), D_MODEL ** -0.5),
        "w_ple_gate": nrm(ks[16], (DEPTH, D_MODEL, D_MODEL), D_MODEL ** -0.5),
        "w_ple_proj": nrm(ks[17], (DEPTH, PLE_DIM, D_MODEL), PLE_DIM ** -0.5),
    }


def reference(x_prompt, x_sample, p_prompt, p_sample, cache_k, cache_v, state_conv,
              ln_g, w_in, q_norm_g, k_norm_g, sink, conv_w, w_attn_out, w_conv_out, w_o,
              w_ple_gate, w_ple_proj):
    t_p = x_prompt.shape[1]
    t_s = x_sample.shape[1]
    pos_p = jnp.arange(t_p, dtype=jnp.float32)
    pos_s = PAST_LEN + jnp.arange(t_s, dtype=jnp.float32)
    hp, hs = x_prompt, x_sample
    kp_l, vp_l, cp_l, ks_l, vs_l, cs_l = [], [], [], [], [], []
    for i in range(DEPTH):
        weights = (ln_g[i], w_in[i], q_norm_g[i], k_norm_g[i], sink[i], conv_w[i],
                   w_attn_out[i], w_conv_out[i], w_o[i], w_ple_gate[i], w_ple_proj[i])
        conv0 = jnp.zeros((hp.shape[0], CONV_WIDTH - 1, CONV_DIM), hp.dtype)
        hp, kp, vp, cp = hybrid_layer(hp, p_prompt[i], pos_p, conv0, attend_prompt, *weights)
        ck, cv = cache_k[i], cache_v[i]
        att_s = lambda q, k, v, s, ck=ck, cv=cv: attend_sample(q, k, v, s, ck, cv)
        hs, k_s, v_s, c_s = hybrid_layer(hs, p_sample[i], pos_s, state_conv[i], att_s, *weights)
        kp_l.append(kp); vp_l.append(vp); cp_l.append(cp)
        ks_l.append(k_s); vs_l.append(v_s); cs_l.append(c_s)
    return (hp, hs, jnp.stack(kp_l), jnp.stack(vp_l), jnp.stack(cp_l),
            jnp.stack(ks_l), jnp.stack(vs_l), jnp.stack(cs_l))
```

```python
import functools

import numpy as np
import jax
import jax.numpy as jnp
from jax import lax
from jax.experimental import pallas as pl
from jax.experimental.pallas import tpu as pltpu

F32 = jnp.float32
BF16 = jnp.bfloat16

CHUNK = 64
WINDOW = 128
WINDOW_CHUNKS = WINDOW // CHUNK
N_HEADS = 8
N_KV_HEADS = 2
HEAD_DIM = 64
HALF = HEAD_DIM // 2
ATTN_DIM = N_HEADS * HEAD_DIM
KV_DIM = N_KV_HEADS * HEAD_DIM
CONV_DIM = 512
CONV_WIDTH = 3
PAST_LEN = 1024
ROPE_THETA = 10000.0
EPS = 1e-6
NEG = -1e30

OFF_Q = 0
OFF_K = OFF_Q + ATTN_DIM
OFF_V = OFF_K + KV_DIM
OFF_GA = OFF_V + KV_DIM
OFF_B = OFF_GA + ATTN_DIM
OFF_GC_END = OFF_B + 4 * CONV_DIM
OFF_MA = OFF_GC_END

V7X_LANES = 128
V7X_SUBLANES = 8
V7X_VMEM_LIMIT_BYTES = 56 * 1024 * 1024

PROMPT_TILE = 256
SAMPLE_BATCH_TILE = 8


def _mm(a, w):
    return jnp.dot(a, w, preferred_element_type=F32)


def _mm_t(a, b):
    return lax.dot_general(a, b, (((1,), (1,)), ((), ())), preferred_element_type=F32)


def _sigmoid(x):
    return 1.0 / (1.0 + jnp.exp(-x))


def _silu(x):
    return x * _sigmoid(x)


def _rmsnorm(x, g):
    ms = jnp.mean(x * x, axis=-1, keepdims=True)
    return x * lax.rsqrt(ms + EPS) * g


def _group_mean(t, bd):
    hi = t.astype(BF16)
    lo = (t - hi.astype(F32)).astype(BF16)
    return _mm(hi, bd) + _mm(lo, bd)


def _head_norm(t, bd, g):
    ms = _group_mean(t * t, bd)
    return t * lax.rsqrt(ms + EPS) * g


def _rope(xc, cos_t, sin_s):
    lane = lax.broadcasted_iota(jnp.int32, xc.shape, 1)
    upper = (lane & HALF) != 0
    rot = jnp.where(upper, pltpu.roll(xc, HALF, 1), pltpu.roll(xc, V7X_LANES - HALF, 1))
    return xc * cos_t + rot * sin_s


def _rope_sign(shape):
    lane = lax.broadcasted_iota(jnp.int32, shape, 1)
    return jnp.where((lane & HALF) != 0, 1.0, -1.0).astype(F32)


def _kv_variants(t):
    lane = lax.broadcasted_iota(jnp.int32, t.shape, 1)
    lo = lane < HEAD_DIM
    sw = pltpu.roll(t, HEAD_DIM, 1)
    zero = jnp.zeros_like(t)
    return (jnp.where(lo, t, zero).astype(BF16), jnp.where(lo, zero, sw).astype(BF16),
            jnp.where(lo, sw, zero).astype(BF16), jnp.where(lo, zero, t).astype(BF16))


def _attend_block(qs, ka, kb, va, vb, sink_e, sink_o, bad):
    s_e = _mm_t(qs, ka)
    s_o = _mm_t(qs, kb)
    if bad is not None:
        s_e = jnp.where(bad, NEG, s_e)
        s_o = jnp.where(bad, NEG, s_o)
    m_e = jnp.maximum(jnp.max(s_e, axis=-1, keepdims=True), sink_e)
    m_o = jnp.maximum(jnp.max(s_o, axis=-1, keepdims=True), sink_o)
    e_e = jnp.exp(s_e - m_e)
    e_o = jnp.exp(s_o - m_o)
    r_e = 1.0 / (jnp.sum(e_e, axis=-1, keepdims=True) + jnp.exp(sink_e - m_e))
    r_o = 1.0 / (jnp.sum(e_o, axis=-1, keepdims=True) + jnp.exp(sink_o - m_o))
    o = _mm(e_e.astype(BF16), va) + _mm(e_o.astype(BF16), vb)
    lane = lax.broadcasted_iota(jnp.int32, o.shape, 1)
    return o * jnp.where(lane < HEAD_DIM, r_e, r_o)


def _sink_cols(sink_ref, vh, rows_per_pair):
    row = lax.broadcasted_iota(jnp.int32, (2 * rows_per_pair, 1), 0)
    first = row < rows_per_pair
    h0 = GROUP_HEADS * vh
    sink_e = jnp.where(first, sink_ref[h0], sink_ref[h0 + 2])
    sink_o = jnp.where(first, sink_ref[h0 + 1], sink_ref[h0 + 3])
    return sink_e, sink_o


GROUP_HEADS = N_HEADS // N_KV_HEADS


def _qkv(x, lng_ref, win_ref, gq_ref, gk_ref, bdq_ref, bdk_ref, cos_t, sin_s):
    hb = _rmsnorm(x, lng_ref[...]).astype(BF16)
    qkv = _mm(hb, win_ref[:, OFF_Q:OFF_GA])
    half_q = ATTN_DIM // 2
    q_cols = []
    for j in range(2):
        t = qkv[:, j * half_q:(j + 1) * half_q]
        tn = _head_norm(t, bdq_ref[...], gq_ref[:, j * half_q:(j + 1) * half_q])
        for c in range(half_q // V7X_LANES):
            q_cols.append(_rope(tn[:, c * V7X_LANES:(c + 1) * V7X_LANES], cos_t, sin_s).astype(BF16))
    kn = _head_norm(qkv[:, OFF_K:OFF_V], bdk_ref[...], gk_ref[...])
    kr = _rope(kn, cos_t, sin_s)
    v = qkv[:, OFF_V:OFF_GA]
    return hb, q_cols, kr, v


def _tail(x, p, hb, attn, conv_fn, win_ref, wa_ref, wb_ref, wo_ref, wpg_ref, wpp_ref):
    ga = _mm(hb, win_ref[:, OFF_GA:OFF_B])
    ya = _mm((attn * _silu(ga)).astype(BF16), wa_ref[...])
    bcug = _mm(hb, win_ref[:, OFF_B:OFF_GC_END])
    b_gate = bcug[:, 0:CONV_DIM]
    u = bcug[:, CONV_DIM:2 * CONV_DIM] * bcug[:, 2 * CONV_DIM:3 * CONV_DIM]
    gate_c = bcug[:, 3 * CONV_DIM:4 * CONV_DIM]
    conv = conv_fn(u)
    yc = _mm((b_gate * conv * _silu(gate_c)).astype(BF16), wb_ref[...])
    d_model = x.shape[-1]
    mamc = _mm(hb, win_ref[:, OFF_MA:OFF_MA + 2 * d_model])
    mix = _sigmoid(mamc[:, 0:d_model]) * ya + _sigmoid(mamc[:, d_model:2 * d_model]) * yc
    r = x + _mm(mix.astype(BF16), wo_ref[...])
    gate = _sigmoid(_mm(r.astype(BF16), wpg_ref[...]))
    return r + gate * _mm(p.astype(BF16), wpp_ref[...])


def _prompt_kernel(x_ref, p_ref, lng_ref, win_ref, gq_ref, gk_ref, invf_ref, sink_ref, cw_ref,
                   wa_ref, wb_ref, wo_ref, wpg_ref, wpp_ref, bdq_ref, bdk_ref,
                   y_ref, ko_ref, vo_ref, co_ref,
                   kbuf, vbuf, ubuf, attn_buf, tab):
    i = pl.program_id(0)
    tm = x_ref.shape[0]
    n_chunks = tm // CHUNK
    invf = invf_ref[...]

    @pl.when(i == 0)
    def _init():
        kbuf[:, 0:WINDOW, :] = jnp.zeros((4, WINDOW, V7X_LANES), BF16)
        vbuf[:, 0:WINDOW, :] = jnp.zeros((4, WINDOW, V7X_LANES), BF16)
        ubuf[0:V7X_SUBLANES, :] = jnp.zeros((V7X_SUBLANES, CONV_DIM), F32)
        r = lax.broadcasted_iota(jnp.int32, (tm, V7X_LANES), 0).astype(F32)
        ang = r * invf
        sgn = _rope_sign((tm, V7X_LANES))
        c_r = jnp.cos(ang)
        s_r = jnp.sin(ang)
        tab[0] = c_r
        tab[1] = s_r
        tab[2] = c_r * sgn
        tab[3] = s_r * sgn

    base = (i * tm).astype(F32) * invf
    cb = jnp.cos(base)
    sb = jnp.sin(base)
    cos_t = tab[0] * cb - tab[1] * sb
    sin_s = tab[3] * cb + tab[2] * sb

    x = x_ref[...]
    hb, q_cols, kr, v = _qkv(x, lng_ref, win_ref, gq_ref, gk_ref, bdq_ref, bdk_ref, cos_t, sin_s)

    ko_ref[...] = kr[tm - WINDOW:tm, :]
    vo_ref[...] = v[tm - WINDOW:tm, :]
    for n, t in enumerate(_kv_variants(kr)):
        kbuf[n, WINDOW:WINDOW + tm, :] = t
    for n, t in enumerate(_kv_variants(v)):
        vbuf[n, WINDOW:WINDOW + tm, :] = t

    n_keys = (WINDOW_CHUNKS + 1) * CHUNK
    for c in range(n_chunks):
        rows = slice(c * CHUNK, (c + 1) * CHUNK)
        win = slice(c * CHUNK, c * CHUNK + n_keys)
        bad = None
        if c < WINDOW_CHUNKS:
            col = lax.broadcasted_iota(jnp.int32, (2 * CHUNK, n_keys), 1)
            bad = (col < (WINDOW_CHUNKS - c) * CHUNK) & (i == 0)
        for vh in range(N_KV_HEADS):
            qs = jnp.concatenate([q_cols[2 * vh][rows], q_cols[2 * vh + 1][rows]], axis=0)
            sink_e, sink_o = _sink_cols(sink_ref, vh, CHUNK)
            o = _attend_block(qs, kbuf[2 * vh, win, :], kbuf[2 * vh + 1, win, :],
                              vbuf[2 * vh, win, :], vbuf[2 * vh + 1, win, :], sink_e, sink_o, bad)
            attn_buf[rows, (2 * vh) * V7X_LANES:(2 * vh + 1) * V7X_LANES] = o[0:CHUNK]
            attn_buf[rows, (2 * vh + 1) * V7X_LANES:(2 * vh + 2) * V7X_LANES] = o[CHUNK:2 * CHUNK]

    kbuf[:, 0:WINDOW, :] = kbuf[:, tm:tm + WINDOW, :]
    vbuf[:, 0:WINDOW, :] = vbuf[:, tm:tm + WINDOW, :]

    def conv_fn(u):
        pad = V7X_SUBLANES
        ubuf[pad:pad + tm, :] = u
        conv = ubuf[pad - 2:pad - 2 + tm, :] * cw_ref[0:1, :]
        conv = conv + ubuf[pad - 1:pad - 1 + tm, :] * cw_ref[1:2, :]
        conv = conv + u * cw_ref[2:3, :]
        co_ref[...] = ubuf[pad + tm - 2:pad + tm, :]
        ubuf[0:pad, :] = ubuf[tm:tm + pad, :]
        return conv

    y_ref[...] = _tail(x, p_ref[...], hb, attn_buf[...], conv_fn, win_ref,
                       wa_ref, wb_ref, wo_ref, wpg_ref, wpp_ref)


def _sample_kernel(x_ref, p_ref, ck_ref, cv_ref, sc_ref, lng_ref, win_ref, gq_ref, gk_ref,
                   invf_ref, sink_ref, cw_ref, wa_ref, wb_ref, wo_ref, wpg_ref, wpp_ref,
                   bdq_ref, bdk_ref,
                   y_ref, ko_ref, vo_ref, co_ref,
                   ubuf, attn_buf, conv_buf, tab):
    i = pl.program_id(0)
    bb, cache_len, _ = ck_ref.shape
    rows_total = x_ref.shape[0]
    t_new = rows_total // bb
    n_keys = cache_len + t_new

    @pl.when(i == 0)
    def _init():
        r = lax.broadcasted_iota(jnp.int32, (rows_total, V7X_LANES), 0)
        pos = (PAST_LEN + lax.rem(r, t_new)).astype(F32)
        ang = pos * invf_ref[...]
        tab[0] = jnp.cos(ang)
        tab[1] = jnp.sin(ang) * _rope_sign((rows_total, V7X_LANES))

    x = x_ref[...]
    hb, q_cols, kr, v = _qkv(x, lng_ref, win_ref, gq_ref, gk_ref, bdq_ref, bdk_ref, tab[0], tab[1])

    qi = lax.broadcasted_iota(jnp.int32, (2 * t_new, n_keys), 0)
    q_pos = PAST_LEN + lax.rem(qi, t_new)
    k_pos = PAST_LEN - cache_len + lax.broadcasted_iota(jnp.int32, (2 * t_new, n_keys), 1)
    q_ch = q_pos // CHUNK
    k_ch = k_pos // CHUNK
    bad = jnp.logical_not((k_ch <= q_ch) & (k_ch >= q_ch - WINDOW_CHUNKS))

    for b in range(bb):
        rows = slice(b * t_new, (b + 1) * t_new)
        kcat = jnp.concatenate([ck_ref[b], kr[rows]], axis=0)
        vcat = jnp.concatenate([cv_ref[b], v[rows]], axis=0)
        ko_ref[b] = kcat[n_keys - cache_len:n_keys]
        vo_ref[b] = vcat[n_keys - cache_len:n_keys]
        kvar = _kv_variants(kcat)
        vvar = _kv_variants(vcat)
        for vh in range(N_KV_HEADS):
            qs = jnp.concatenate([q_cols[2 * vh][rows], q_cols[2 * vh + 1][rows]], axis=0)
            sink_e, sink_o = _sink_cols(sink_ref, vh, t_new)
            o = _attend_block(qs, kvar[2 * vh], kvar[2 * vh + 1], vvar[2 * vh], vvar[2 * vh + 1],
                              sink_e, sink_o, bad)
            attn_buf[rows, (2 * vh) * V7X_LANES:(2 * vh + 1) * V7X_LANES] = o[0:t_new]
            attn_buf[rows, (2 * vh + 1) * V7X_LANES:(2 * vh + 2) * V7X_LANES] = o[t_new:2 * t_new]

    def conv_fn(u):
        pad = V7X_SUBLANES
        for b in range(bb):
            rows = slice(b * t_new, (b + 1) * t_new)
            ub = u[rows]
            ubuf[b, pad - (CONV_WIDTH - 1):pad, :] = sc_ref[b]
            ubuf[b, pad:pad + t_new, :] = ub
            conv = ubuf[b, pad - 2:pad - 2 + t_new, :] * cw_ref[0:1, :]
            conv = conv + ubuf[b, pad - 1:pad - 1 + t_new, :] * cw_ref[1:2, :]
            conv_buf[rows, :] = conv + ub * cw_ref[2:3, :]
            co_ref[b] = ubuf[b, pad + t_new - (CONV_WIDTH - 1):pad + t_new, :]
        return conv_buf[...]

    y_ref[...] = _tail(x, p_ref[...], hb, attn_buf[...], conv_fn, win_ref,
                       wa_ref, wb_ref, wo_ref, wpg_ref, wpp_ref)


def _const_spec(shape):
    nd = len(shape)
    return pl.BlockSpec(shape, lambda i: (0,) * nd, pipeline_mode=pl.Buffered(1))


def _weight_specs(w):
    return [_const_spec(a.shape) for a in w]


def _smem_spec():
    return pl.BlockSpec(memory_space=pltpu.SMEM)


def _block_diag_mean(width):
    idx = np.arange(width) // HEAD_DIM
    return jnp.asarray((idx[:, None] == idx[None, :]).astype(np.float32) / HEAD_DIM, dtype=BF16)


def _layer_consts(ln_g, w_in, q_norm_g, k_norm_g, sink, conv_w, w_attn_out, w_conv_out, w_o,
                  w_ple_gate, w_ple_proj):
    scale = HEAD_DIM ** -0.5
    inv_freq = ROPE_THETA ** (-jnp.arange(0, HALF, dtype=F32) * 2.0 / HEAD_DIM)
    return dict(
        lng=ln_g.reshape(1, -1).astype(F32),
        win=w_in.astype(BF16),
        gq=(jnp.tile(q_norm_g.astype(F32), N_HEADS) * scale).reshape(1, ATTN_DIM),
        gk=jnp.tile(k_norm_g.astype(F32), N_KV_HEADS).reshape(1, KV_DIM),
        invf=jnp.tile(inv_freq, V7X_LANES // HALF).reshape(1, V7X_LANES),
        sink=sink.astype(F32),
        cw=conv_w.astype(F32),
        wa=w_attn_out.astype(BF16),
        wb=w_conv_out.astype(BF16),
        wo=w_o.astype(BF16),
        wpg=w_ple_gate.astype(BF16),
        wpp=w_ple_proj.astype(BF16),
        bdq=_block_diag_mean(ATTN_DIM // 2),
        bdk=_block_diag_mean(KV_DIM),
    )


_VEC_KEYS = ("lng", "win", "gq", "gk", "invf")
_MAT_KEYS = ("cw", "wa", "wb", "wo", "wpg", "wpp", "bdq", "bdk")


def _prompt_layer(x, p, c):
    t, d = x.shape
    tm = PROMPT_TILE
    assert t % tm == 0 and tm % CHUNK == 0 and tm >= WINDOW
    pre = [c[k] for k in _VEC_KEYS]
    post = [c[k] for k in _MAT_KEYS]
    in_specs = ([pl.BlockSpec((tm, d), lambda i: (i, 0)),
                 pl.BlockSpec((tm, p.shape[1]), lambda i: (i, 0))]
                + _weight_specs(pre) + [_smem_spec()] + _weight_specs(post))
    out_shape = (jax.ShapeDtypeStruct((t, d), F32),
                 jax.ShapeDtypeStruct((WINDOW, KV_DIM), F32),
                 jax.ShapeDtypeStruct((WINDOW, KV_DIM), F32),
                 jax.ShapeDtypeStruct((CONV_WIDTH - 1, CONV_DIM), F32))
    out_specs = (pl.BlockSpec((tm, d), lambda i: (i, 0)),
                 pl.BlockSpec((WINDOW, KV_DIM), lambda i: (0, 0)),
                 pl.BlockSpec((WINDOW, KV_DIM), lambda i: (0, 0)),
                 pl.BlockSpec((CONV_WIDTH - 1, CONV_DIM), lambda i: (0, 0)))
    scratch = [pltpu.VMEM((4, WINDOW + tm, V7X_LANES), BF16),
               pltpu.VMEM((4, WINDOW + tm, V7X_LANES), BF16),
               pltpu.VMEM((V7X_SUBLANES + tm, CONV_DIM), F32),
               pltpu.VMEM((tm, ATTN_DIM), F32),
               pltpu.VMEM((4, tm, V7X_LANES), F32)]
    return pl.pallas_call(
        _prompt_kernel,
        grid=(t // tm,),
        in_specs=in_specs,
        out_specs=out_specs,
        out_shape=out_shape,
        scratch_shapes=scratch,
        compiler_params=pltpu.CompilerParams(dimension_semantics=("arbitrary",),
                                             vmem_limit_bytes=V7X_VMEM_LIMIT_BYTES),
        name="prompt_layer",
    )(x, p, *pre, c["sink"], *post)


def _sample_layer(x, p, cache_k, cache_v, state_conv, c):
    nb, t_new, d = x.shape
    cache_len = cache_k.shape[1]
    bb = SAMPLE_BATCH_TILE
    assert nb % bb == 0 and t_new >= CONV_WIDTH - 1 and t_new <= cache_len
    rows = bb * t_new
    x2 = x.reshape(nb * t_new, d)
    p2 = p.reshape(nb * t_new, p.shape[-1])
    ck = cache_k.reshape(nb, cache_len, KV_DIM)
    cv = cache_v.reshape(nb, cache_len, KV_DIM)
    pre = [c[k] for k in _VEC_KEYS]
    post = [c[k] for k in _MAT_KEYS]
    in_specs = ([pl.BlockSpec((rows, d), lambda i: (i, 0)),
                 pl.BlockSpec((rows, p2.shape[1]), lambda i: (i, 0)),
                 pl.BlockSpec((bb, cache_len, KV_DIM), lambda i: (i, 0, 0)),
                 pl.BlockSpec((bb, cache_len, KV_DIM), lambda i: (i, 0, 0)),
                 pl.BlockSpec((bb, CONV_WIDTH - 1, CONV_DIM), lambda i: (i, 0, 0))]
                + _weight_specs(pre) + [_smem_spec()] + _weight_specs(post))
    out_shape = (jax.ShapeDtypeStruct((nb * t_new, d), F32),
                 jax.ShapeDtypeStruct((nb, cache_len, KV_DIM), F32),
                 jax.ShapeDtypeStruct((nb, cache_len, KV_DIM), F32),
                 jax.ShapeDtypeStruct((nb, CONV_WIDTH - 1, CONV_DIM), F32))
    out_specs = (pl.BlockSpec((rows, d), lambda i: (i, 0)),
                 pl.BlockSpec((bb, cache_len, KV_DIM), lambda i: (i, 0, 0)),
                 pl.BlockSpec((bb, cache_len, KV_DIM), lambda i: (i, 0, 0)),
                 pl.BlockSpec((bb, CONV_WIDTH - 1, CONV_DIM), lambda i: (i, 0, 0)))
    scratch = [pltpu.VMEM((bb, V7X_SUBLANES + t_new, CONV_DIM), F32),
               pltpu.VMEM((rows, ATTN_DIM), F32),
               pltpu.VMEM((rows, CONV_DIM), F32),
               pltpu.VMEM((2, rows, V7X_LANES), F32)]
    y, ko, vo, co = pl.pallas_call(
        _sample_kernel,
        grid=(nb // bb,),
        in_specs=in_specs,
        out_specs=out_specs,
        out_shape=out_shape,
        scratch_shapes=scratch,
        compiler_params=pltpu.CompilerParams(dimension_semantics=("arbitrary",),
                                             vmem_limit_bytes=V7X_VMEM_LIMIT_BYTES),
        name="sample_layer",
    )(x2, p2, ck, cv, state_conv, *pre, c["sink"], *post)
    return y.reshape(nb, t_new, d), ko, vo, co


def kernel(x_prompt, x_sample, p_prompt, p_sample, cache_k, cache_v, state_conv, ln_g, w_in,
           q_norm_g, k_norm_g, sink, conv_w, w_attn_out, w_conv_out, w_o, w_ple_gate, w_ple_proj):
    depth = ln_g.shape[0]
    batch, seq, d_model = x_prompt.shape
    hp, hs = x_prompt, x_sample
    kp_l, vp_l, cp_l, ks_l, vs_l, cs_l = [], [], [], [], [], []
    for i in range(depth):
        c = _layer_consts(ln_g[i], w_in[i], q_norm_g[i], k_norm_g[i], sink[i], conv_w[i],
                          w_attn_out[i], w_conv_out[i], w_o[i], w_ple_gate[i], w_ple_proj[i])
        ys, kps, vps, cps = [], [], [], []
        for b in range(batch):
            y, ko, vo, co = _prompt_layer(hp[b], p_prompt[i, b], c)
            ys.append(y)
            kps.append(ko.reshape(WINDOW, N_KV_HEADS, HEAD_DIM))
            vps.append(vo.reshape(WINDOW, N_KV_HEADS, HEAD_DIM))
            cps.append(co)
        hp = jnp.stack(ys)
        kp_l.append(jnp.stack(kps))
        vp_l.append(jnp.stack(vps))
        cp_l.append(jnp.stack(cps))
        hs, ko, vo, co = _sample_layer(hs, p_sample[i], cache_k[i], cache_v[i], state_conv[i], c)
        nb, cache_len = ko.shape[0], ko.shape[1]
        ks_l.append(ko.reshape(nb, cache_len, N_KV_HEADS, HEAD_DIM))
        vs_l.append(vo.reshape(nb, cache_len, N_KV_HEADS, HEAD_DIM))
        cs_l.append(co)
    return (hp, hs, jnp.stack(kp_l), jnp.stack(vp_l), jnp.stack(cp_l),
            jnp.stack(ks_l), jnp.stack(vs_l), jnp.stack(cs_l))
```

```python
import numpy as np
import jax
import jax.numpy as jnp
from jax import lax
from jax.experimental import pallas as pl
from jax.experimental.pallas import tpu as pltpu

F32 = jnp.float32
BF16 = jnp.bfloat16

CHUNK = 64
WINDOW = 128
WINDOW_CHUNKS = WINDOW // CHUNK
N_HEADS = 8
N_KV_HEADS = 2
GROUP_HEADS = N_HEADS // N_KV_HEADS
HEAD_DIM = 64
HALF = HEAD_DIM // 2
ATTN_DIM = N_HEADS * HEAD_DIM
KV_DIM = N_KV_HEADS * HEAD_DIM
CONV_DIM = 512
CONV_WIDTH = 3
PAST_LEN = 1024
ROPE_THETA = 10000.0
EPS = 1e-6
NEG = -1e30

OFF_Q = 0
OFF_K = OFF_Q + ATTN_DIM
OFF_V = OFF_K + KV_DIM
OFF_GA = OFF_V + KV_DIM
OFF_B = OFF_GA + ATTN_DIM
OFF_GC_END = OFF_B + 4 * CONV_DIM
OFF_MA = OFF_GC_END

V7X_LANES = 128
V7X_SUBLANES = 8
V7X_VMEM_LIMIT_BYTES = 56 * 1024 * 1024

PROMPT_TILE = 256
SAMPLE_BATCH_TILE = 8


def _mm(a, w):
    return jnp.dot(a, w, preferred_element_type=F32)


def _mm_t(a, b):
    return lax.dot_general(a, b, (((1,), (1,)), ((), ())), preferred_element_type=F32)


def _sigmoid(x):
    return 1.0 / (1.0 + jnp.exp(-x))


def _silu(x):
    return x * _sigmoid(x)


def _rmsnorm(x, g):
    ms = jnp.mean(x * x, axis=-1, keepdims=True)
    return x * lax.rsqrt(ms + EPS) * g


def _group_mean(t, bd):
    hi = t.astype(BF16)
    lo = (t - hi.astype(F32)).astype(BF16)
    return _mm(hi, bd) + _mm(lo, bd)


def _head_norm(t, bd, g):
    ms = _group_mean(t * t, bd)
    return t * lax.rsqrt(ms + EPS) * g


def _rope(xc, cos_t, sin_s):
    lane = lax.broadcasted_iota(jnp.int32, xc.shape, 1)
    upper = (lane & HALF) != 0
    rot = jnp.where(upper, pltpu.roll(xc, HALF, 1), pltpu.roll(xc, V7X_LANES - HALF, 1))
    return xc * cos_t + rot * sin_s


def _rope_sign(shape):
    lane = lax.broadcasted_iota(jnp.int32, shape, 1)
    return jnp.where((lane & HALF) != 0, 1.0, -1.0).astype(F32)


def _kv_variants(t):
    lane = lax.broadcasted_iota(jnp.int32, t.shape, 1)
    lo = lane < HEAD_DIM
    sw = pltpu.roll(t, HEAD_DIM, 1)
    zero = jnp.zeros_like(t)
    return (jnp.where(lo, t, zero).astype(BF16), jnp.where(lo, zero, sw).astype(BF16),
            jnp.where(lo, sw, zero).astype(BF16), jnp.where(lo, zero, t).astype(BF16))


def _scores(qs, ka, kb, bad):
    s_e = _mm_t(qs, ka)
    s_o = _mm_t(qs, kb)
    if bad is not None:
        s_e = jnp.where(bad, NEG, s_e)
        s_o = jnp.where(bad, NEG, s_o)
    return s_e, s_o


def _softmax_parts(s, sink):
    m = jnp.maximum(jnp.max(s, axis=-1, keepdims=True), sink)
    e = jnp.exp(s - m)
    r = 1.0 / (jnp.sum(e, axis=-1, keepdims=True) + jnp.exp(sink - m))
    return e.astype(BF16), r


def _pv(p_e, p_o, va, vb, r_e, r_o):
    o = _mm(p_e, va) + _mm(p_o, vb)
    lane = lax.broadcasted_iota(jnp.int32, o.shape, 1)
    return o * jnp.where(lane < HEAD_DIM, r_e, r_o)


def _sink_cols(sink_ref, n_blocks, rows_per_pair):
    row = lax.broadcasted_iota(jnp.int32, (n_blocks * 2 * rows_per_pair, 1), 0)
    second_pair = (row // rows_per_pair) % 2 == 1
    second_kv = (row // (2 * rows_per_pair)) % 2 == 1

    def pick(odd):
        kv0 = jnp.where(second_pair, sink_ref[2 + odd], sink_ref[odd])
        kv1 = jnp.where(second_pair, sink_ref[GROUP_HEADS + 2 + odd], sink_ref[GROUP_HEADS + odd])
        return jnp.where(second_kv, kv1, kv0)

    return pick(0), pick(1)


def _qkv_proj(x, lng_ref, win_ref):
    hb = _rmsnorm(x, lng_ref[...]).astype(BF16)
    return hb, _mm(hb, win_ref[:, OFF_Q:OFF_GA])


def _qk_norm_rope(qkv, gq_ref, gk_ref, bdq_ref, bdk_ref, cos_t, sin_s):
    half_q = ATTN_DIM // 2
    q_cols = []
    for j in range(2):
        t = qkv[:, j * half_q:(j + 1) * half_q]
        tn = _head_norm(t, bdq_ref[...], gq_ref[:, j * half_q:(j + 1) * half_q])
        for c in range(half_q // V7X_LANES):
            q_cols.append(_rope(tn[:, c * V7X_LANES:(c + 1) * V7X_LANES], cos_t, sin_s).astype(BF16))
    kn = _head_norm(qkv[:, OFF_K:OFF_V], bdk_ref[...], gk_ref[...])
    kr = _rope(kn, cos_t, sin_s)
    v = qkv[:, OFF_V:OFF_GA]
    return q_cols, kr, v


def _merge_gates(hb, win_ref, d_model):
    mamc = _mm(hb, win_ref[:, OFF_MA:OFF_MA + 2 * d_model])
    return _sigmoid(mamc[:, 0:d_model]), _sigmoid(mamc[:, d_model:2 * d_model])


def _conv_branch(hb, win_ref, wb_ref, conv_fn):
    bcug = _mm(hb, win_ref[:, OFF_B:OFF_GC_END])
    b_gate = bcug[:, 0:CONV_DIM]
    u = bcug[:, CONV_DIM:2 * CONV_DIM] * bcug[:, 2 * CONV_DIM:3 * CONV_DIM]
    gate_c = bcug[:, 3 * CONV_DIM:4 * CONV_DIM]
    conv = conv_fn(u)
    return _mm((b_gate * conv * _silu(gate_c)).astype(BF16), wb_ref[...])


def _finish(x, p, hb, attn, yc, sig_a, sig_c, win_ref, wa_ref, wo_ref, wpg_ref, wpp_ref):
    ga = _mm(hb, win_ref[:, OFF_GA:OFF_B])
    ya = _mm((attn * _silu(ga)).astype(BF16), wa_ref[...])
    mix = sig_a * ya + sig_c * yc
    r = x + _mm(mix.astype(BF16), wo_ref[...])
    gate = _sigmoid(_mm(r.astype(BF16), wpg_ref[...]))
    return r + gate * _mm(p.astype(BF16), wpp_ref[...])


def _prompt_kernel(x_ref, p_ref, lng_ref, win_ref, gq_ref, gk_ref, invf_ref, sink_ref, cw_ref,
                   wa_ref, wb_ref, wo_ref, wpg_ref, wpp_ref, bdq_ref, bdk_ref,
                   y_ref, ko_ref, vo_ref, co_ref,
                   kbuf, vbuf, ubuf, attn_buf, tab):
    i = pl.program_id(0)
    tm, d_model = x_ref.shape
    n_chunks = tm // CHUNK
    invf = invf_ref[...]

    @pl.when(i == 0)
    def _init():
        kbuf[:, 0:WINDOW, :] = jnp.zeros((4, WINDOW, V7X_LANES), BF16)
        vbuf[:, 0:WINDOW, :] = jnp.zeros((4, WINDOW, V7X_LANES), BF16)
        ubuf[0:V7X_SUBLANES, :] = jnp.zeros((V7X_SUBLANES, CONV_DIM), F32)
        r = lax.broadcasted_iota(jnp.int32, (tm, V7X_LANES), 0).astype(F32)
        ang = r * invf
        sgn = _rope_sign((tm, V7X_LANES))
        c_r = jnp.cos(ang)
        s_r = jnp.sin(ang)
        tab[0] = c_r
        tab[1] = s_r
        tab[2] = c_r * sgn
        tab[3] = s_r * sgn

    base = (i * tm).astype(F32) * invf
    cb = jnp.cos(base)
    sb = jnp.sin(base)
    cos_t = tab[0] * cb - tab[1] * sb
    sin_s = tab[3] * cb + tab[2] * sb

    def conv_fn(u):
        pad = V7X_SUBLANES
        ubuf[pad:pad + tm, :] = u
        conv = ubuf[pad - 2:pad - 2 + tm, :] * cw_ref[0:1, :]
        conv = conv + ubuf[pad - 1:pad - 1 + tm, :] * cw_ref[1:2, :]
        conv = conv + u * cw_ref[2:3, :]
        co_ref[...] = ubuf[pad + tm - 2:pad + tm, :]
        ubuf[0:pad, :] = ubuf[tm:tm + pad, :]
        return conv

    x = x_ref[...]
    hb, qkv = _qkv_proj(x, lng_ref, win_ref)
    sig_a, sig_c = _merge_gates(hb, win_ref, d_model)
    q_cols, kr, v = _qk_norm_rope(qkv, gq_ref, gk_ref, bdq_ref, bdk_ref, cos_t, sin_s)

    ko_ref[...] = kr[tm - WINDOW:tm, :]
    vo_ref[...] = v[tm - WINDOW:tm, :]
    for n, t in enumerate(_kv_variants(kr)):
        kbuf[n, WINDOW:WINDOW + tm, :] = t
    for n, t in enumerate(_kv_variants(v)):
        vbuf[n, WINDOW:WINDOW + tm, :] = t

    n_keys = (WINDOW_CHUNKS + 1) * CHUNK
    blocks = [(c, vh) for c in range(n_chunks) for vh in range(N_KV_HEADS)]
    s_e, s_o = [], []
    for c, vh in blocks:
        rows = slice(c * CHUNK, (c + 1) * CHUNK)
        win = slice(c * CHUNK, c * CHUNK + n_keys)
        bad = None
        if c < WINDOW_CHUNKS:
            col = lax.broadcasted_iota(jnp.int32, (2 * CHUNK, n_keys), 1)
            bad = (col < (WINDOW_CHUNKS - c) * CHUNK) & (i == 0)
        qs = jnp.concatenate([q_cols[2 * vh][rows], q_cols[2 * vh + 1][rows]], axis=0)
        se, so = _scores(qs, kbuf[2 * vh, win, :], kbuf[2 * vh + 1, win, :], bad)
        s_e.append(se)
        s_o.append(so)

    yc = _conv_branch(hb, win_ref, wb_ref, conv_fn)

    sink_e, sink_o = _sink_cols(sink_ref, len(blocks), CHUNK)
    p_e, r_e = _softmax_parts(jnp.concatenate(s_e, axis=0), sink_e)
    p_o, r_o = _softmax_parts(jnp.concatenate(s_o, axis=0), sink_o)

    for n, (c, vh) in enumerate(blocks):
        rows = slice(c * CHUNK, (c + 1) * CHUNK)
        win = slice(c * CHUNK, c * CHUNK + n_keys)
        br = slice(n * 2 * CHUNK, (n + 1) * 2 * CHUNK)
        o = _pv(p_e[br], p_o[br], vbuf[2 * vh, win, :], vbuf[2 * vh + 1, win, :], r_e[br], r_o[br])
        attn_buf[rows, (2 * vh) * V7X_LANES:(2 * vh + 1) * V7X_LANES] = o[0:CHUNK]
        attn_buf[rows, (2 * vh + 1) * V7X_LANES:(2 * vh + 2) * V7X_LANES] = o[CHUNK:2 * CHUNK]

    kbuf[:, 0:WINDOW, :] = kbuf[:, tm:tm + WINDOW, :]
    vbuf[:, 0:WINDOW, :] = vbuf[:, tm:tm + WINDOW, :]

    y_ref[...] = _finish(x, p_ref[...], hb, attn_buf[...], yc, sig_a, sig_c, win_ref,
                         wa_ref, wo_ref, wpg_ref, wpp_ref)


def _sample_kernel(x_ref, p_ref, ck_ref, cv_ref, sc_ref, lng_ref, win_ref, gq_ref, gk_ref,
                   invf_ref, sink_ref, cw_ref, wa_ref, wb_ref, wo_ref, wpg_ref, wpp_ref,
                   bdq_ref, bdk_ref,
                   y_ref, ko_ref, vo_ref, co_ref,
                   ubuf, attn_buf, conv_buf, tab):
    i = pl.program_id(0)
    bb, cache_len, _ = ck_ref.shape
    rows_total, d_model = x_ref.shape
    t_new = rows_total // bb
    n_keys = cache_len + t_new

    @pl.when(i == 0)
    def _init():
        r = lax.broadcasted_iota(jnp.int32, (rows_total, V7X_LANES), 0)
        pos = (PAST_LEN + lax.rem(r, t_new)).astype(F32)
        ang = pos * invf_ref[...]
        tab[0] = jnp.cos(ang)
        tab[1] = jnp.sin(ang) * _rope_sign((rows_total, V7X_LANES))

    def conv_fn(u):
        pad = V7X_SUBLANES
        for b in range(bb):
            rows = slice(b * t_new, (b + 1) * t_new)
            ub = u[rows]
            ubuf[b, pad - (CONV_WIDTH - 1):pad, :] = sc_ref[b]
            ubuf[b, pad:pad + t_new, :] = ub
            conv = ubuf[b, pad - 2:pad - 2 + t_new, :] * cw_ref[0:1, :]
            conv = conv + ubuf[b, pad - 1:pad - 1 + t_new, :] * cw_ref[1:2, :]
            conv_buf[rows, :] = conv + ub * cw_ref[2:3, :]
            co_ref[b] = ubuf[b, pad + t_new - (CONV_WIDTH - 1):pad + t_new, :]
        return conv_buf[...]

    x = x_ref[...]
    hb, qkv = _qkv_proj(x, lng_ref, win_ref)
    sig_a, sig_c = _merge_gates(hb, win_ref, d_model)
    q_cols, kr, v = _qk_norm_rope(qkv, gq_ref, gk_ref, bdq_ref, bdk_ref, tab[0], tab[1])

    qi = lax.broadcasted_iota(jnp.int32, (2 * t_new, n_keys), 0)
    q_pos = PAST_LEN + lax.rem(qi, t_new)
    k_pos = PAST_LEN - cache_len + lax.broadcasted_iota(jnp.int32, (2 * t_new, n_keys), 1)
    q_ch = q_pos // CHUNK
    k_ch = k_pos // CHUNK
    bad = jnp.logical_not((k_ch <= q_ch) & (k_ch >= q_ch - WINDOW_CHUNKS))

    blocks = [(b, vh) for b in range(bb) for vh in range(N_KV_HEADS)]
    s_e, s_o, vvars = [], [], []
    for b in range(bb):
        rows = slice(b * t_new, (b + 1) * t_new)
        kcat = jnp.concatenate([ck_ref[b], kr[rows]], axis=0)
        vcat = jnp.concatenate([cv_ref[b], v[rows]], axis=0)
        ko_ref[b] = kcat[n_keys - cache_len:n_keys]
        vo_ref[b] = vcat[n_keys - cache_len:n_keys]
        kvar = _kv_variants(kcat)
        vvars.append(_kv_variants(vcat))
        for vh in range(N_KV_HEADS):
            qs = jnp.concatenate([q_cols[2 * vh][rows], q_cols[2 * vh + 1][rows]], axis=0)
            se, so = _scores(qs, kvar[2 * vh], kvar[2 * vh + 1], bad)
            s_e.append(se)
            s_o.append(so)

    yc = _conv_branch(hb, win_ref, wb_ref, conv_fn)

    sink_e, sink_o = _sink_cols(sink_ref, len(blocks), t_new)
    p_e, r_e = _softmax_parts(jnp.concatenate(s_e, axis=0), sink_e)
    p_o, r_o = _softmax_parts(jnp.concatenate(s_o, axis=0), sink_o)

    for n, (b, vh) in enumerate(blocks):
        rows = slice(b * t_new, (b + 1) * t_new)
        br = slice(n * 2 * t_new, (n + 1) * 2 * t_new)
        o = _pv(p_e[br], p_o[br], vvars[b][2 * vh], vvars[b][2 * vh + 1], r_e[br], r_o[br])
        attn_buf[rows, (2 * vh) * V7X_LANES:(2 * vh + 1) * V7X_LANES] = o[0:t_new]
        attn_buf[rows, (2 * vh + 1) * V7X_LANES:(2 * vh + 2) * V7X_LANES] = o[t_new:2 * t_new]

    y_ref[...] = _finish(x, p_ref[...], hb, attn_buf[...], yc, sig_a, sig_c, win_ref,
                         wa_ref, wo_ref, wpg_ref, wpp_ref)


def _const_spec(shape):
    nd = len(shape)
    return pl.BlockSpec(shape, lambda i: (0,) * nd, pipeline_mode=pl.Buffered(1))


def _weight_specs(w):
    return [_const_spec(a.shape) for a in w]


def _smem_spec():
    return pl.BlockSpec(memory_space=pltpu.SMEM)


def _block_diag_mean(width):
    idx = np.arange(width) // HEAD_DIM
    return jnp.asarray((idx[:, None] == idx[None, :]).astype(np.float32) / HEAD_DIM, dtype=BF16)


def _layer_consts(ln_g, w_in, q_norm_g, k_norm_g, sink, conv_w, w_attn_out, w_conv_out, w_o,
                  w_ple_gate, w_ple_proj):
    scale = HEAD_DIM ** -0.5
    inv_freq = ROPE_THETA ** (-jnp.arange(0, HALF, dtype=F32) * 2.0 / HEAD_DIM)
    return dict(
        lng=ln_g.reshape(1, -1).astype(F32),
        win=w_in.astype(BF16),
        gq=(jnp.tile(q_norm_g.astype(F32), N_HEADS) * scale).reshape(1, ATTN_DIM),
        gk=jnp.tile(k_norm_g.astype(F32), N_KV_HEADS).reshape(1, KV_DIM),
        invf=jnp.tile(inv_freq, V7X_LANES // HALF).reshape(1, V7X_LANES),
        sink=sink.astype(F32),
        cw=conv_w.astype(F32),
        wa=w_attn_out.astype(BF16),
        wb=w_conv_out.astype(BF16),
        wo=w_o.astype(BF16),
        wpg=w_ple_gate.astype(BF16),
        wpp=w_ple_proj.astype(BF16),
        bdq=_block_diag_mean(ATTN_DIM // 2),
        bdk=_block_diag_mean(KV_DIM),
    )


_VEC_KEYS = ("lng", "win", "gq", "gk", "invf")
_MAT_KEYS = ("cw", "wa", "wb", "wo", "wpg", "wpp", "bdq", "bdk")


def _prompt_layer(x, p, c):
    t, d = x.shape
    tm = PROMPT_TILE
    assert t % tm == 0 and tm % CHUNK == 0 and tm >= WINDOW
    pre = [c[k] for k in _VEC_KEYS]
    post = [c[k] for k in _MAT_KEYS]
    in_specs = ([pl.BlockSpec((tm, d), lambda i: (i, 0)),
                 pl.BlockSpec((tm, p.shape[1]), lambda i: (i, 0))]
                + _weight_specs(pre) + [_smem_spec()] + _weight_specs(post))
    out_shape = (jax.ShapeDtypeStruct((t, d), F32),
                 jax.ShapeDtypeStruct((WINDOW, KV_DIM), F32),
                 jax.ShapeDtypeStruct((WINDOW, KV_DIM), F32),
                 jax.ShapeDtypeStruct((CONV_WIDTH - 1, CONV_DIM), F32))
    out_specs = (pl.BlockSpec((tm, d), lambda i: (i, 0)),
                 pl.BlockSpec((WINDOW, KV_DIM), lambda i: (0, 0)),
                 pl.BlockSpec((WINDOW, KV_DIM), lambda i: (0, 0)),
                 pl.BlockSpec((CONV_WIDTH - 1, CONV_DIM), lambda i: (0, 0)))
    scratch = [pltpu.VMEM((4, WINDOW + tm, V7X_LANES), BF16),
               pltpu.VMEM((4, WINDOW + tm, V7X_LANES), BF16),
               pltpu.VMEM((V7X_SUBLANES + tm, CONV_DIM), F32),
               pltpu.VMEM((tm, ATTN_DIM), F32),
               pltpu.VMEM((4, tm, V7X_LANES), F32)]
    return pl.pallas_call(
        _prompt_kernel,
        grid=(t // tm,),
        in_specs=in_specs,
        out_specs=out_specs,
        out_shape=out_shape,
        scratch_shapes=scratch,
        compiler_params=pltpu.CompilerParams(dimension_semantics=("arbitrary",),
                                             vmem_limit_bytes=V7X_VMEM_LIMIT_BYTES),
        name="prompt_layer",
    )(x, p, *pre, c["sink"], *post)


def _sample_layer(x, p, cache_k, cache_v, state_conv, c):
    nb, t_new, d = x.shape
    cache_len = cache_k.shape[1]
    bb = SAMPLE_BATCH_TILE
    assert nb % bb == 0 and t_new >= CONV_WIDTH - 1 and t_new <= cache_len
    rows = bb * t_new
    x2 = x.reshape(nb * t_new, d)
    p2 = p.reshape(nb * t_new, p.shape[-1])
    ck = cache_k.reshape(nb, cache_len, KV_DIM)
    cv = cache_v.reshape(nb, cache_len, KV_DIM)
    pre = [c[k] for k in _VEC_KEYS]
    post = [c[k] for k in _MAT_KEYS]
    in_specs = ([pl.BlockSpec((rows, d), lambda i: (i, 0)),
                 pl.BlockSpec((rows, p2.shape[1]), lambda i: (i, 0)),
                 pl.BlockSpec((bb, cache_len, KV_DIM), lambda i: (i, 0, 0)),
                 pl.BlockSpec((bb, cache_len, KV_DIM), lambda i: (i, 0, 0)),
                 pl.BlockSpec((bb, CONV_WIDTH - 1, CONV_DIM), lambda i: (i, 0, 0))]
                + _weight_specs(pre) + [_smem_spec()] + _weight_specs(post))
    out_shape = (jax.ShapeDtypeStruct((nb * t_new, d), F32),
                 jax.ShapeDtypeStruct((nb, cache_len, KV_DIM), F32),
                 jax.ShapeDtypeStruct((nb, cache_len, KV_DIM), F32),
                 jax.ShapeDtypeStruct((nb, CONV_WIDTH - 1, CONV_DIM), F32))
    out_specs = (pl.BlockSpec((rows, d), lambda i: (i, 0)),
                 pl.BlockSpec((bb, cache_len, KV_DIM), lambda i: (i, 0, 0)),
                 pl.BlockSpec((bb, cache_len, KV_DIM), lambda i: (i, 0, 0)),
                 pl.BlockSpec((bb, CONV_WIDTH - 1, CONV_DIM), lambda i: (i, 0, 0)))
    scratch = [pltpu.VMEM((bb, V7X_SUBLANES + t_new, CONV_DIM), F32),
               pltpu.VMEM((rows, ATTN_DIM), F32),
               pltpu.VMEM((rows, CONV_DIM), F32),
               pltpu.VMEM((2, rows, V7X_LANES), F32)]
    y, ko, vo, co = pl.pallas_call(
        _sample_kernel,
        grid=(nb // bb,),
        in_specs=in_specs,
        out_specs=out_specs,
        out_shape=out_shape,
        scratch_shapes=scratch,
        compiler_params=pltpu.CompilerParams(dimension_semantics=("arbitrary",),
                                             vmem_limit_bytes=V7X_VMEM_LIMIT_BYTES),
        name="sample_layer",
    )(x2, p2, ck, cv, state_conv, *pre, c["sink"], *post)
    return y.reshape(nb, t_new, d), ko, vo, co


def kernel(x_prompt, x_sample, p_prompt, p_sample, cache_k, cache_v, state_conv, ln_g, w_in,
           q_norm_g, k_norm_g, sink, conv_w, w_attn_out, w_conv_out, w_o, w_ple_gate, w_ple_proj):
    depth = ln_g.shape[0]
    batch = x_prompt.shape[0]
    hp, hs = x_prompt, x_sample
    kp_l, vp_l, cp_l, ks_l, vs_l, cs_l = [], [], [], [], [], []
    for i in range(depth):
        c = _layer_consts(ln_g[i], w_in[i], q_norm_g[i], k_norm_g[i], sink[i], conv_w[i],
                          w_attn_out[i], w_conv_out[i], w_o[i], w_ple_gate[i], w_ple_proj[i])
        ys, kps, vps, cps = [], [], [], []
        for b in range(batch):
            y, ko, vo, co = _prompt_layer(hp[b], p_prompt[i, b], c)
            ys.append(y)
            kps.append(ko.reshape(WINDOW, N_KV_HEADS, HEAD_DIM))
            vps.append(vo.reshape(WINDOW, N_KV_HEADS, HEAD_DIM))
            cps.append(co)
        hp = jnp.stack(ys)
        kp_l.append(jnp.stack(kps))
        vp_l.append(jnp.stack(vps))
        cp_l.append(jnp.stack(cps))
        hs, ko, vo, co = _sample_layer(hs, p_sample[i], cache_k[i], cache_v[i], state_conv[i], c)
        nb, cache_len = ko.shape[0], ko.shape[1]
        ks_l.append(ko.reshape(nb, cache_len, N_KV_HEADS, HEAD_DIM))
        vs_l.append(vo.reshape(nb, cache_len, N_KV_HEADS, HEAD_DIM))
        cs_l.append(co)
    return (hp, hs, jnp.stack(kp_l), jnp.stack(vp_l), jnp.stack(cp_l),
            jnp.stack(ks_l), jnp.stack(vs_l), jnp.stack(cs_l))
```

```python
import numpy as np
import jax
import jax.numpy as jnp
from jax import lax
from jax.experimental import pallas as pl
from jax.experimental.pallas import tpu as pltpu

F32 = jnp.float32
BF16 = jnp.bfloat16

CHUNK = 64
WINDOW = 128
WINDOW_CHUNKS = WINDOW // CHUNK
N_HEADS = 8
N_KV_HEADS = 2
GROUP_HEADS = N_HEADS // N_KV_HEADS
HEAD_DIM = 64
HALF = HEAD_DIM // 2
ATTN_DIM = N_HEADS * HEAD_DIM
KV_DIM = N_KV_HEADS * HEAD_DIM
CONV_DIM = 512
CONV_WIDTH = 3
PAST_LEN = 1024
ROPE_THETA = 10000.0
EPS = 1e-6
NEG = -1e30

OFF_Q = 0
OFF_K = OFF_Q + ATTN_DIM
OFF_V = OFF_K + KV_DIM
OFF_GA = OFF_V + KV_DIM
OFF_B = OFF_GA + ATTN_DIM
OFF_GC_END = OFF_B + 4 * CONV_DIM
OFF_MA = OFF_GC_END

V7X_LANES = 128
V7X_SUBLANES = 8
V7X_VMEM_LIMIT_BYTES = 56 * 1024 * 1024

PROMPT_TILE = 512
PROMPT_SUBTILE = 256
PROMPT_STAGE_SKEW = 1
SAMPLE_BATCH_TILE = 8


def _mm(a, w):
    return jnp.dot(a, w, preferred_element_type=F32)


def _mm_t(a, b):
    return lax.dot_general(a, b, (((1,), (1,)), ((), ())), preferred_element_type=F32)


def _sigmoid(x):
    return 1.0 / (1.0 + jnp.exp(-x))


def _silu(x):
    return x * _sigmoid(x)


def _rmsnorm(x, g):
    ms = jnp.mean(x * x, axis=-1, keepdims=True)
    return x * lax.rsqrt(ms + EPS) * g


def _group_mean(t, bd):
    hi = t.astype(BF16)
    lo = (t - hi.astype(F32)).astype(BF16)
    return _mm(hi, bd) + _mm(lo, bd)


def _head_norm(t, bd, g):
    ms = _group_mean(t * t, bd)
    return t * lax.rsqrt(ms + EPS) * g


def _rope(xc, cos_t, sin_s):
    lane = lax.broadcasted_iota(jnp.int32, xc.shape, 1)
    upper = (lane & HALF) != 0
    rot = jnp.where(upper, pltpu.roll(xc, HALF, 1), pltpu.roll(xc, V7X_LANES - HALF, 1))
    return xc * cos_t + rot * sin_s


def _rope_sign(shape):
    lane = lax.broadcasted_iota(jnp.int32, shape, 1)
    return jnp.where((lane & HALF) != 0, 1.0, -1.0).astype(F32)


def _kv_variants(t):
    lane = lax.broadcasted_iota(jnp.int32, t.shape, 1)
    lo = lane < HEAD_DIM
    sw = pltpu.roll(t, HEAD_DIM, 1)
    zero = jnp.zeros_like(t)
    return (jnp.where(lo, t, zero).astype(BF16), jnp.where(lo, zero, sw).astype(BF16),
            jnp.where(lo, sw, zero).astype(BF16), jnp.where(lo, zero, t).astype(BF16))


def _scores(qs, ka, kb, bad):
    s_e = _mm_t(qs, ka)
    s_o = _mm_t(qs, kb)
    if bad is not None:
        s_e = jnp.where(bad, NEG, s_e)
        s_o = jnp.where(bad, NEG, s_o)
    return s_e, s_o


def _softmax_parts(s, sink):
    m = jnp.maximum(jnp.max(s, axis=-1, keepdims=True), sink)
    e = jnp.exp(s - m)
    r = 1.0 / (jnp.sum(e, axis=-1, keepdims=True) + jnp.exp(sink - m))
    return e.astype(BF16), r


def _pv(p_e, p_o, va, vb, r_e, r_o):
    o = _mm(p_e, va) + _mm(p_o, vb)
    lane = lax.broadcasted_iota(jnp.int32, o.shape, 1)
    return o * jnp.where(lane < HEAD_DIM, r_e, r_o)


def _sink_cols(sink_ref, n_blocks, rows_per_pair):
    row = lax.broadcasted_iota(jnp.int32, (n_blocks * 2 * rows_per_pair, 1), 0)
    second_pair = (row // rows_per_pair) % 2 == 1
    second_kv = (row // (2 * rows_per_pair)) % 2 == 1

    def pick(odd):
        kv0 = jnp.where(second_pair, sink_ref[2 + odd], sink_ref[odd])
        kv1 = jnp.where(second_pair, sink_ref[GROUP_HEADS + 2 + odd], sink_ref[GROUP_HEADS + odd])
        return jnp.where(second_kv, kv1, kv0)

    return pick(0), pick(1)


def _qkv_proj(x, lng_ref, win_ref):
    hb = _rmsnorm(x, lng_ref[...]).astype(BF16)
    return hb, _mm(hb, win_ref[:, OFF_Q:OFF_GA])


def _qk_norm_rope(qkv, gq_ref, gk_ref, bdq_ref, bdk_ref, cos_t, sin_s):
    half_q = ATTN_DIM // 2
    q_cols = []
    for j in range(2):
        t = qkv[:, j * half_q:(j + 1) * half_q]
        tn = _head_norm(t, bdq_ref[...], gq_ref[:, j * half_q:(j + 1) * half_q])
        for c in range(half_q // V7X_LANES):
            q_cols.append(_rope(tn[:, c * V7X_LANES:(c + 1) * V7X_LANES], cos_t, sin_s).astype(BF16))
    kn = _head_norm(qkv[:, OFF_K:OFF_V], bdk_ref[...], gk_ref[...])
    kr = _rope(kn, cos_t, sin_s)
    v = qkv[:, OFF_V:OFF_GA]
    return q_cols, kr, v


def _merge_gates(hb, win_ref, d_model):
    mamc = _mm(hb, win_ref[:, OFF_MA:OFF_MA + 2 * d_model])
    return _sigmoid(mamc[:, 0:d_model]), _sigmoid(mamc[:, d_model:2 * d_model])


def _conv_branch(hb, win_ref, wb_ref, conv_fn):
    bcug = _mm(hb, win_ref[:, OFF_B:OFF_GC_END])
    b_gate = bcug[:, 0:CONV_DIM]
    u = bcug[:, CONV_DIM:2 * CONV_DIM] * bcug[:, 2 * CONV_DIM:3 * CONV_DIM]
    gate_c = bcug[:, 3 * CONV_DIM:4 * CONV_DIM]
    conv = conv_fn(u)
    return _mm((b_gate * conv * _silu(gate_c)).astype(BF16), wb_ref[...])


def _finish(x, p, hb, attn, yc, sig_a, sig_c, win_ref, wa_ref, wo_ref, wpg_ref, wpp_ref):
    ga = _mm(hb, win_ref[:, OFF_GA:OFF_B])
    ya = _mm((attn * _silu(ga)).astype(BF16), wa_ref[...])
    mix = sig_a * ya + sig_c * yc
    r = x + _mm(mix.astype(BF16), wo_ref[...])
    gate = _sigmoid(_mm(r.astype(BF16), wpg_ref[...]))
    return r + gate * _mm(p.astype(BF16), wpp_ref[...])


def _prompt_kernel(x_ref, p_ref, lng_ref, win_ref, gq_ref, gk_ref, invf_ref, sink_ref, cw_ref,
                   wa_ref, wb_ref, wo_ref, wpg_ref, wpp_ref, bdq_ref, bdk_ref,
                   y_ref, ko_ref, vo_ref, co_ref,
                   kbuf, vbuf, ubuf, attn_buf, tab):
    i = pl.program_id(0)
    tm, d_model = x_ref.shape
    invf = invf_ref[...]

    @pl.when(i == 0)
    def _init():
        kbuf[:, 0:WINDOW, :] = jnp.zeros((4, WINDOW, V7X_LANES), BF16)
        vbuf[:, 0:WINDOW, :] = jnp.zeros((4, WINDOW, V7X_LANES), BF16)
        ubuf[0:V7X_SUBLANES, :] = jnp.zeros((V7X_SUBLANES, CONV_DIM), F32)
        r = lax.broadcasted_iota(jnp.int32, (tm, V7X_LANES), 0).astype(F32)
        ang = r * invf
        sgn = _rope_sign((tm, V7X_LANES))
        c_r = jnp.cos(ang)
        s_r = jnp.sin(ang)
        tab[0] = c_r
        tab[1] = s_r
        tab[2] = c_r * sgn
        tab[3] = s_r * sgn

    base = (i * tm).astype(F32) * invf
    cb = jnp.cos(base)
    sb = jnp.sin(base)
    cos_t = tab[0] * cb - tab[1] * sb
    sin_s = tab[3] * cb + tab[2] * sb

    sub = PROMPT_SUBTILE
    n_sub = tm // sub
    sub_chunks = sub // CHUNK
    n_keys = (WINDOW_CHUNKS + 1) * CHUNK
    pad = V7X_SUBLANES
    blocks = [(c, vh) for c in range(sub_chunks) for vh in range(N_KV_HEADS)]
    st = [dict(r0=s * sub) for s in range(n_sub)]

    def stage_qkv(s):
        s["x"] = x_ref[s["r0"]:s["r0"] + sub, :]
        s["hb"], s["qkv"] = _qkv_proj(s["x"], lng_ref, win_ref)

    def stage_gates(s):
        s["sig_a"], s["sig_c"] = _merge_gates(s["hb"], win_ref, d_model)

    def stage_qk_norm(s):
        r0 = s["r0"]
        s["q_cols"], kr, v = _qk_norm_rope(s.pop("qkv"), gq_ref, gk_ref, bdq_ref, bdk_ref,
                                           cos_t[r0:r0 + sub], sin_s[r0:r0 + sub])
        for n, t in enumerate(_kv_variants(kr)):
            kbuf[n, WINDOW + r0:WINDOW + r0 + sub, :] = t
        for n, t in enumerate(_kv_variants(v)):
            vbuf[n, WINDOW + r0:WINDOW + r0 + sub, :] = t
        if r0 + sub == tm:
            ko_ref[...] = kr[sub - WINDOW:sub, :]
            vo_ref[...] = v[sub - WINDOW:sub, :]

    def stage_scores(s):
        r0 = s["r0"]
        s_e, s_o = [], []
        for c, vh in blocks:
            rows = slice(c * CHUNK, (c + 1) * CHUNK)
            win = slice(r0 + c * CHUNK, r0 + c * CHUNK + n_keys)
            bad = None
            if r0 + c * CHUNK < WINDOW:
                col = lax.broadcasted_iota(jnp.int32, (2 * CHUNK, n_keys), 1)
                bad = (col < WINDOW - (r0 + c * CHUNK)) & (i == 0)
            qs = jnp.concatenate([s["q_cols"][2 * vh][rows], s["q_cols"][2 * vh + 1][rows]], axis=0)
            se, so = _scores(qs, kbuf[2 * vh, win, :], kbuf[2 * vh + 1, win, :], bad)
            s_e.append(se)
            s_o.append(so)
        s["s_e"] = jnp.concatenate(s_e, axis=0)
        s["s_o"] = jnp.concatenate(s_o, axis=0)
        del s["q_cols"]

    def stage_conv(s):
        r0 = s["r0"]

        def conv_fn(u):
            ubuf[pad + r0:pad + r0 + sub, :] = u
            conv = ubuf[pad + r0 - 2:pad + r0 - 2 + sub, :] * cw_ref[0:1, :]
            conv = conv + ubuf[pad + r0 - 1:pad + r0 - 1 + sub, :] * cw_ref[1:2, :]
            return conv + u * cw_ref[2:3, :]

        s["yc"] = _conv_branch(s["hb"], win_ref, wb_ref, conv_fn)

    sink_e, sink_o = _sink_cols(sink_ref, len(blocks), CHUNK)

    def stage_softmax(s):
        s["p_e"], s["r_e"] = _softmax_parts(s.pop("s_e"), sink_e)
        s["p_o"], s["r_o"] = _softmax_parts(s.pop("s_o"), sink_o)

    def stage_pv(s):
        r0 = s["r0"]
        for n, (c, vh) in enumerate(blocks):
            rows = slice(r0 + c * CHUNK, r0 + (c + 1) * CHUNK)
            win = slice(r0 + c * CHUNK, r0 + c * CHUNK + n_keys)
            br = slice(n * 2 * CHUNK, (n + 1) * 2 * CHUNK)
            o = _pv(s["p_e"][br], s["p_o"][br], vbuf[2 * vh, win, :], vbuf[2 * vh + 1, win, :],
                    s["r_e"][br], s["r_o"][br])
            attn_buf[rows, (2 * vh) * V7X_LANES:(2 * vh + 1) * V7X_LANES] = o[0:CHUNK]
            attn_buf[rows, (2 * vh + 1) * V7X_LANES:(2 * vh + 2) * V7X_LANES] = o[CHUNK:2 * CHUNK]

    def stage_finish(s):
        r0 = s["r0"]
        y_ref[r0:r0 + sub, :] = _finish(s["x"], p_ref[r0:r0 + sub, :], s["hb"],
                                        attn_buf[r0:r0 + sub, :], s["yc"], s["sig_a"], s["sig_c"],
                                        win_ref, wa_ref, wo_ref, wpg_ref, wpp_ref)

    stages = [stage_qkv, stage_qk_norm, stage_gates, stage_scores, stage_softmax, stage_conv,
              stage_pv, stage_finish]
    for t in range(len(stages) + PROMPT_STAGE_SKEW * (n_sub - 1)):
        for j, s in enumerate(st):
            k = t - j * PROMPT_STAGE_SKEW
            if 0 <= k < len(stages):
                stages[k](s)

    co_ref[...] = ubuf[pad + tm - 2:pad + tm, :]
    ubuf[0:pad, :] = ubuf[tm:tm + pad, :]
    kbuf[:, 0:WINDOW, :] = kbuf[:, tm:tm + WINDOW, :]
    vbuf[:, 0:WINDOW, :] = vbuf[:, tm:tm + WINDOW, :]


def _sample_kernel(x_ref, p_ref, ck_ref, cv_ref, sc_ref, lng_ref, win_ref, gq_ref, gk_ref,
                   invf_ref, sink_ref, cw_ref, wa_ref, wb_ref, wo_ref, wpg_ref, wpp_ref,
                   bdq_ref, bdk_ref,
                   y_ref, ko_ref, vo_ref, co_ref,
                   ubuf, attn_buf, conv_buf, tab):
    i = pl.program_id(0)
    bb, cache_len, _ = ck_ref.shape
    rows_total, d_model = x_ref.shape
    t_new = rows_total // bb
    n_keys = cache_len + t_new

    @pl.when(i == 0)
    def _init():
        r = lax.broadcasted_iota(jnp.int32, (rows_total, V7X_LANES), 0)
        pos = (PAST_LEN + lax.rem(r, t_new)).astype(F32)
        ang = pos * invf_ref[...]
        tab[0] = jnp.cos(ang)
        tab[1] = jnp.sin(ang) * _rope_sign((rows_total, V7X_LANES))

    def conv_fn(u):
        pad = V7X_SUBLANES
        for b in range(bb):
            rows = slice(b * t_new, (b + 1) * t_new)
            ub = u[rows]
            ubuf[b, pad - (CONV_WIDTH - 1):pad, :] = sc_ref[b]
            ubuf[b, pad:pad + t_new, :] = ub
            conv = ubuf[b, pad - 2:pad - 2 + t_new, :] * cw_ref[0:1, :]
            conv = conv + ubuf[b, pad - 1:pad - 1 + t_new, :] * cw_ref[1:2, :]
            conv_buf[rows, :] = conv + ub * cw_ref[2:3, :]
            co_ref[b] = ubuf[b, pad + t_new - (CONV_WIDTH - 1):pad + t_new, :]
        return conv_buf[...]

    x = x_ref[...]
    hb, qkv = _qkv_proj(x, lng_ref, win_ref)
    sig_a, sig_c = _merge_gates(hb, win_ref, d_model)
    q_cols, kr, v = _qk_norm_rope(qkv, gq_ref, gk_ref, bdq_ref, bdk_ref, tab[0], tab[1])

    qi = lax.broadcasted_iota(jnp.int32, (2 * t_new, n_keys), 0)
    q_pos = PAST_LEN + lax.rem(qi, t_new)
    k_pos = PAST_LEN - cache_len + lax.broadcasted_iota(jnp.int32, (2 * t_new, n_keys), 1)
    q_ch = q_pos // CHUNK
    k_ch = k_pos // CHUNK
    bad = jnp.logical_not((k_ch <= q_ch) & (k_ch >= q_ch - WINDOW_CHUNKS))

    blocks = [(b, vh) for b in range(bb) for vh in range(N_KV_HEADS)]
    s_e, s_o, vvars = [], [], []
    for b in range(bb):
        rows = slice(b * t_new, (b + 1) * t_new)
        kcat = jnp.concatenate([ck_ref[b], kr[rows]], axis=0)
        vcat = jnp.concatenate([cv_ref[b], v[rows]], axis=0)
        ko_ref[b] = kcat[n_keys - cache_len:n_keys]
        vo_ref[b] = vcat[n_keys - cache_len:n_keys]
        kvar = _kv_variants(kcat)
        vvars.append(_kv_variants(vcat))
        for vh in range(N_KV_HEADS):
            qs = jnp.concatenate([q_cols[2 * vh][rows], q_cols[2 * vh + 1][rows]], axis=0)
            se, so = _scores(qs, kvar[2 * vh], kvar[2 * vh + 1], bad)
            s_e.append(se)
            s_o.append(so)

    yc = _conv_branch(hb, win_ref, wb_ref, conv_fn)

    sink_e, sink_o = _sink_cols(sink_ref, len(blocks), t_new)
    p_e, r_e = _softmax_parts(jnp.concatenate(s_e, axis=0), sink_e)
    p_o, r_o = _softmax_parts(jnp.concatenate(s_o, axis=0), sink_o)

    for n, (b, vh) in enumerate(blocks):
        rows = slice(b * t_new, (b + 1) * t_new)
        br = slice(n * 2 * t_new, (n + 1) * 2 * t_new)
        o = _pv(p_e[br], p_o[br], vvars[b][2 * vh], vvars[b][2 * vh + 1], r_e[br], r_o[br])
        attn_buf[rows, (2 * vh) * V7X_LANES:(2 * vh + 1) * V7X_LANES] = o[0:t_new]
        attn_buf[rows, (2 * vh + 1) * V7X_LANES:(2 * vh + 2) * V7X_LANES] = o[t_new:2 * t_new]

    y_ref[...] = _finish(x, p_ref[...], hb, attn_buf[...], yc, sig_a, sig_c, win_ref,
                         wa_ref, wo_ref, wpg_ref, wpp_ref)


def _const_spec(shape):
    nd = len(shape)
    return pl.BlockSpec(shape, lambda i: (0,) * nd, pipeline_mode=pl.Buffered(1))


def _weight_specs(w):
    return [_const_spec(a.shape) for a in w]


def _smem_spec():
    return pl.BlockSpec(memory_space=pltpu.SMEM)


def _block_diag_mean(width):
    idx = np.arange(width) // HEAD_DIM
    return jnp.asarray((idx[:, None] == idx[None, :]).astype(np.float32) / HEAD_DIM, dtype=BF16)


def _layer_consts(ln_g, w_in, q_norm_g, k_norm_g, sink, conv_w, w_attn_out, w_conv_out, w_o,
                  w_ple_gate, w_ple_proj):
    scale = HEAD_DIM ** -0.5
    inv_freq = ROPE_THETA ** (-jnp.arange(0, HALF, dtype=F32) * 2.0 / HEAD_DIM)
    return dict(
        lng=ln_g.reshape(1, -1).astype(F32),
        win=w_in.astype(BF16),
        gq=(jnp.tile(q_norm_g.astype(F32), N_HEADS) * scale).reshape(1, ATTN_DIM),
        gk=jnp.tile(k_norm_g.astype(F32), N_KV_HEADS).reshape(1, KV_DIM),
        invf=jnp.tile(inv_freq, V7X_LANES // HALF).reshape(1, V7X_LANES),
        sink=sink.astype(F32),
        cw=conv_w.astype(F32),
        wa=w_attn_out.astype(BF16),
        wb=w_conv_out.astype(BF16),
        wo=w_o.astype(BF16),
        wpg=w_ple_gate.astype(BF16),
        wpp=w_ple_proj.astype(BF16),
        bdq=_block_diag_mean(ATTN_DIM // 2),
        bdk=_block_diag_mean(KV_DIM),
    )


_VEC_KEYS = ("lng", "win", "gq", "gk", "invf")
_MAT_KEYS = ("cw", "wa", "wb", "wo", "wpg", "wpp", "bdq", "bdk")


def _prompt_layer(x, p, c):
    t, d = x.shape
    tm = PROMPT_TILE
    assert t % tm == 0 and tm % PROMPT_SUBTILE == 0
    assert PROMPT_SUBTILE % CHUNK == 0 and PROMPT_SUBTILE >= WINDOW
    pre = [c[k] for k in _VEC_KEYS]
    post = [c[k] for k in _MAT_KEYS]
    in_specs = ([pl.BlockSpec((tm, d), lambda i: (i, 0)),
                 pl.BlockSpec((tm, p.shape[1]), lambda i: (i, 0))]
                + _weight_specs(pre) + [_smem_spec()] + _weight_specs(post))
    out_shape = (jax.ShapeDtypeStruct((t, d), F32),
                 jax.ShapeDtypeStruct((WINDOW, KV_DIM), F32),
                 jax.ShapeDtypeStruct((WINDOW, KV_DIM), F32),
                 jax.ShapeDtypeStruct((CONV_WIDTH - 1, CONV_DIM), F32))
    out_specs = (pl.BlockSpec((tm, d), lambda i: (i, 0)),
                 pl.BlockSpec((WINDOW, KV_DIM), lambda i: (0, 0)),
                 pl.BlockSpec((WINDOW, KV_DIM), lambda i: (0, 0)),
                 pl.BlockSpec((CONV_WIDTH - 1, CONV_DIM), lambda i: (0, 0)))
    scratch = [pltpu.VMEM((4, WINDOW + tm, V7X_LANES), BF16),
               pltpu.VMEM((4, WINDOW + tm, V7X_LANES), BF16),
               pltpu.VMEM((V7X_SUBLANES + tm, CONV_DIM), F32),
               pltpu.VMEM((tm, ATTN_DIM), F32),
               pltpu.VMEM((4, tm, V7X_LANES), F32)]
    return pl.pallas_call(
        _prompt_kernel,
        grid=(t // tm,),
        in_specs=in_specs,
        out_specs=out_specs,
        out_shape=out_shape,
        scratch_shapes=scratch,
        compiler_params=pltpu.CompilerParams(dimension_semantics=("arbitrary",),
                                             vmem_limit_bytes=V7X_VMEM_LIMIT_BYTES),
        name="prompt_layer",
    )(x, p, *pre, c["sink"], *post)


def _sample_layer(x, p, cache_k, cache_v, state_conv, c):
    nb, t_new, d = x.shape
    cache_len = cache_k.shape[1]
    bb = SAMPLE_BATCH_TILE
    assert nb % bb == 0 and t_new >= CONV_WIDTH - 1 and t_new <= cache_len
    rows = bb * t_new
    x2 = x.reshape(nb * t_new, d)
    p2 = p.reshape(nb * t_new, p.shape[-1])
    ck = cache_k.reshape(nb, cache_len, KV_DIM)
    cv = cache_v.reshape(nb, cache_len, KV_DIM)
    pre = [c[k] for k in _VEC_KEYS]
    post = [c[k] for k in _MAT_KEYS]
    in_specs = ([pl.BlockSpec((rows, d), lambda i: (i, 0)),
                 pl.BlockSpec((rows, p2.shape[1]), lambda i: (i, 0)),
                 pl.BlockSpec((bb, cache_len, KV_DIM), lambda i: (i, 0, 0)),
                 pl.BlockSpec((bb, cache_len, KV_DIM), lambda i: (i, 0, 0)),
                 pl.BlockSpec((bb, CONV_WIDTH - 1, CONV_DIM), lambda i: (i, 0, 0))]
                + _weight_specs(pre) + [_smem_spec()] + _weight_specs(post))
    out_shape = (jax.ShapeDtypeStruct((nb * t_new, d), F32),
                 jax.ShapeDtypeStruct((nb, cache_len, KV_DIM), F32),
                 jax.ShapeDtypeStruct((nb, cache_len, KV_DIM), F32),
                 jax.ShapeDtypeStruct((nb, CONV_WIDTH - 1, CONV_DIM), F32))
    out_specs = (pl.BlockSpec((rows, d), lambda i: (i, 0)),
                 pl.BlockSpec((bb, cache_len, KV_DIM), lambda i: (i, 0, 0)),
                 pl.BlockSpec((bb, cache_len, KV_DIM), lambda i: (i, 0, 0)),
                 pl.BlockSpec((bb, CONV_WIDTH - 1, CONV_DIM), lambda i: (i, 0, 0)))
    scratch = [pltpu.VMEM((bb, V7X_SUBLANES + t_new, CONV_DIM), F32),
               pltpu.VMEM((rows, ATTN_DIM), F32),
               pltpu.VMEM((rows, CONV_DIM), F32),
               pltpu.VMEM((2, rows, V7X_LANES), F32)]
    y, ko, vo, co = pl.pallas_call(
        _sample_kernel,
        grid=(nb // bb,),
        in_specs=in_specs,
        out_specs=out_specs,
        out_shape=out_shape,
        scratch_shapes=scratch,
        compiler_params=pltpu.CompilerParams(dimension_semantics=("arbitrary",),
                                             vmem_limit_bytes=V7X_VMEM_LIMIT_BYTES),
        name="sample_layer",
    )(x2, p2, ck, cv, state_conv, *pre, c["sink"], *post)
    return y.reshape(nb, t_new, d), ko, vo, co


def kernel(x_prompt, x_sample, p_prompt, p_sample, cache_k, cache_v, state_conv, ln_g, w_in,
           q_norm_g, k_norm_g, sink, conv_w, w_attn_out, w_conv_out, w_o, w_ple_gate, w_ple_proj):
    depth = ln_g.shape[0]
    batch = x_prompt.shape[0]
    hp, hs = x_prompt, x_sample
    kp_l, vp_l, cp_l, ks_l, vs_l, cs_l = [], [], [], [], [], []
    for i in range(depth):
        c = _layer_consts(ln_g[i], w_in[i], q_norm_g[i], k_norm_g[i], sink[i], conv_w[i],
                          w_attn_out[i], w_conv_out[i], w_o[i], w_ple_gate[i], w_ple_proj[i])
        ys, kps, vps, cps = [], [], [], []
        for b in range(batch):
            y, ko, vo, co = _prompt_layer(hp[b], p_prompt[i, b], c)
            ys.append(y)
            kps.append(ko.reshape(WINDOW, N_KV_HEADS, HEAD_DIM))
            vps.append(vo.reshape(WINDOW, N_KV_HEADS, HEAD_DIM))
            cps.append(co)
        hp = jnp.stack(ys)
        kp_l.append(jnp.stack(kps))
        vp_l.append(jnp.stack(vps))
        cp_l.append(jnp.stack(cps))
        hs, ko, vo, co = _sample_layer(hs, p_sample[i], cache_k[i], cache_v[i], state_conv[i], c)
        nb, cache_len = ko.shape[0], ko.shape[1]
        ks_l.append(ko.reshape(nb, cache_len, N_KV_HEADS, HEAD_DIM))
        vs_l.append(vo.reshape(nb, cache_len, N_KV_HEADS, HEAD_DIM))
        cs_l.append(co)
    return (hp, hs, jnp.stack(kp_l), jnp.stack(vp_l), jnp.stack(cp_l),
            jnp.stack(ks_l), jnp.stack(vs_l), jnp.stack(cs_l))
```

```python
import numpy as np
import jax
import jax.numpy as jnp
from jax import lax
from jax.experimental import pallas as pl
from jax.experimental.pallas import tpu as pltpu

F32 = jnp.float32
BF16 = jnp.bfloat16

CHUNK = 64
WINDOW = 128
WINDOW_CHUNKS = WINDOW // CHUNK
N_HEADS = 8
N_KV_HEADS = 2
GROUP_HEADS = N_HEADS // N_KV_HEADS
HEAD_DIM = 64
HALF = HEAD_DIM // 2
ATTN_DIM = N_HEADS * HEAD_DIM
KV_DIM = N_KV_HEADS * HEAD_DIM
CONV_DIM = 512
CONV_WIDTH = 3
PAST_LEN = 1024
ROPE_THETA = 10000.0
EPS = 1e-6
NEG = -1e30

OFF_Q = 0
OFF_K = OFF_Q + ATTN_DIM
OFF_V = OFF_K + KV_DIM
OFF_GA = OFF_V + KV_DIM
OFF_B = OFF_GA + ATTN_DIM
OFF_GC_END = OFF_B + 4 * CONV_DIM
OFF_MA = OFF_GC_END

V7X_LANES = 128
V7X_SUBLANES = 8
V7X_VMEM_LIMIT_BYTES = 56 * 1024 * 1024

PROMPT_TILE = 512
PROMPT_SUBTILE = 256
PROMPT_STAGE_SKEW = 1
STAGE_ROWS = 256
STAGE_COLS = 1024
SAMPLE_BATCH_TILE = 8


def _mm(a, w):
    return jnp.dot(a, w, preferred_element_type=F32)


def _mm_t(a, b):
    return lax.dot_general(a, b, (((1,), (1,)), ((), ())), preferred_element_type=F32)


def _sigmoid(x):
    return 1.0 / (1.0 + jnp.exp(-x))


def _silu(x):
    return x * _sigmoid(x)


def _rmsnorm(x, g):
    ms = jnp.mean(x * x, axis=-1, keepdims=True)
    return x * lax.rsqrt(ms + EPS) * g


def _group_mean(t, bd):
    hi = t.astype(BF16)
    lo = (t - hi.astype(F32)).astype(BF16)
    return _mm(hi, bd) + _mm(lo, bd)


def _head_norm(t, bd, g):
    ms = _group_mean(t * t, bd)
    return t * lax.rsqrt(ms + EPS) * g


def _rope(xc, cos_t, sin_s):
    lane = lax.broadcasted_iota(jnp.int32, xc.shape, 1)
    upper = (lane & HALF) != 0
    rot = jnp.where(upper, pltpu.roll(xc, HALF, 1), pltpu.roll(xc, V7X_LANES - HALF, 1))
    return xc * cos_t + rot * sin_s


def _rope_sign(shape):
    lane = lax.broadcasted_iota(jnp.int32, shape, 1)
    return jnp.where((lane & HALF) != 0, 1.0, -1.0).astype(F32)


def _kv_variants(t):
    lane = lax.broadcasted_iota(jnp.int32, t.shape, 1)
    lo = lane < HEAD_DIM
    sw = pltpu.roll(t, HEAD_DIM, 1)
    zero = jnp.zeros_like(t)
    return (jnp.where(lo, t, zero).astype(BF16), jnp.where(lo, zero, sw).astype(BF16),
            jnp.where(lo, sw, zero).astype(BF16), jnp.where(lo, zero, t).astype(BF16))


def _scores(qs, ka, kb, bad):
    s_e = _mm_t(qs, ka)
    s_o = _mm_t(qs, kb)
    if bad is not None:
        s_e = jnp.where(bad, NEG, s_e)
        s_o = jnp.where(bad, NEG, s_o)
    return s_e, s_o


def _softmax_parts(s, sink):
    m = jnp.maximum(jnp.max(s, axis=-1, keepdims=True), sink)
    e = jnp.exp(s - m)
    r = 1.0 / (jnp.sum(e, axis=-1, keepdims=True) + jnp.exp(sink - m))
    return e.astype(BF16), r


def _pv(p_e, p_o, va, vb, r_e, r_o):
    o = _mm(p_e, va) + _mm(p_o, vb)
    lane = lax.broadcasted_iota(jnp.int32, o.shape, 1)
    return o * jnp.where(lane < HEAD_DIM, r_e, r_o)


def _sink_cols(sink_ref, n_blocks, rows_per_pair):
    row = lax.broadcasted_iota(jnp.int32, (n_blocks * 2 * rows_per_pair, 1), 0)
    second_pair = (row // rows_per_pair) % 2 == 1
    second_kv = (row // (2 * rows_per_pair)) % 2 == 1

    def pick(odd):
        kv0 = jnp.where(second_pair, sink_ref[2 + odd], sink_ref[odd])
        kv1 = jnp.where(second_pair, sink_ref[GROUP_HEADS + 2 + odd], sink_ref[GROUP_HEADS + odd])
        return jnp.where(second_kv, kv1, kv0)

    return pick(0), pick(1)


def _stage_chunks(shape):
    rows, cols = shape
    assert rows % STAGE_ROWS == 0 and cols % V7X_LANES == 0
    width = max(c for c in range(V7X_LANES, STAGE_COLS + 1, V7X_LANES) if cols % c == 0)
    return [(r0, c0, STAGE_ROWS, width)
            for r0 in range(0, rows, STAGE_ROWS) for c0 in range(0, cols, width)]


def _stage_weights(pairs, stage, sem):
    chunks = [(src, dst) + ch for src, dst in pairs for ch in _stage_chunks(src.shape)]

    def copy(n):
        src, _, r0, c0, rows, cols = chunks[n]
        return pltpu.make_async_copy(src.at[pl.ds(r0, rows), pl.ds(c0, cols)],
                                     stage.at[n % 2, pl.ds(0, rows), pl.ds(0, cols)],
                                     sem.at[n % 2])

    copy(0).start()
    for n, (_, dst, r0, c0, rows, cols) in enumerate(chunks):
        if n + 1 < len(chunks):
            copy(n + 1).start()
        copy(n).wait()
        dst[r0:r0 + rows, c0:c0 + cols] = stage[n % 2, 0:rows, 0:cols].astype(BF16)


def _qkv_proj(x, lng_ref, win_ref):
    hb = _rmsnorm(x, lng_ref[...]).astype(BF16)
    return hb, _mm(hb, win_ref[:, OFF_Q:OFF_GA])


def _qk_norm_rope(qkv, gq_ref, gk_ref, bdq_ref, bdk_ref, cos_t, sin_s):
    half_q = ATTN_DIM // 2
    q_cols = []
    for j in range(2):
        t = qkv[:, j * half_q:(j + 1) * half_q]
        tn = _head_norm(t, bdq_ref[...], gq_ref[:, j * half_q:(j + 1) * half_q])
        for c in range(half_q // V7X_LANES):
            q_cols.append(_rope(tn[:, c * V7X_LANES:(c + 1) * V7X_LANES], cos_t, sin_s).astype(BF16))
    kn = _head_norm(qkv[:, OFF_K:OFF_V], bdk_ref[...], gk_ref[...])
    kr = _rope(kn, cos_t, sin_s)
    v = qkv[:, OFF_V:OFF_GA]
    return q_cols, kr, v


def _merge_gates(hb, win_ref, d_model):
    mamc = _mm(hb, win_ref[:, OFF_MA:OFF_MA + 2 * d_model])
    return _sigmoid(mamc[:, 0:d_model]), _sigmoid(mamc[:, d_model:2 * d_model])


def _conv_branch(hb, win_ref, wb_ref, conv_fn):
    bcug = _mm(hb, win_ref[:, OFF_B:OFF_GC_END])
    b_gate = bcug[:, 0:CONV_DIM]
    u = bcug[:, CONV_DIM:2 * CONV_DIM] * bcug[:, 2 * CONV_DIM:3 * CONV_DIM]
    gate_c = bcug[:, 3 * CONV_DIM:4 * CONV_DIM]
    conv = conv_fn(u)
    return _mm((b_gate * conv * _silu(gate_c)).astype(BF16), wb_ref[...])


def _finish(x, p, hb, attn, yc, sig_a, sig_c, win_ref, wa_ref, wo_ref, wpg_ref, wpp_ref):
    ga = _mm(hb, win_ref[:, OFF_GA:OFF_B])
    ya = _mm((attn * _silu(ga)).astype(BF16), wa_ref[...])
    mix = sig_a * ya + sig_c * yc
    r = x + _mm(mix.astype(BF16), wo_ref[...])
    gate = _sigmoid(_mm(r.astype(BF16), wpg_ref[...]))
    return r + gate * _mm(p.astype(BF16), wpp_ref[...])


def _prompt_kernel(x_ref, p_ref, lng_ref, win_hbm, gq_ref, gk_ref, invf_ref, sink_ref, cw_ref,
                   wa_hbm, wb_hbm, wo_hbm, wpg_hbm, wpp_hbm, bdq_ref, bdk_ref,
                   y_ref, ko_ref, vo_ref, co_ref,
                   kbuf, vbuf, ubuf, attn_buf, tab,
                   win_ref, wa_ref, wb_ref, wo_ref, wpg_ref, wpp_ref, stage, sem):
    i = pl.program_id(0)
    tm, d_model = x_ref.shape
    invf = invf_ref[...]

    @pl.when(i == 0)
    def _init():
        _stage_weights([(win_hbm, win_ref), (wa_hbm, wa_ref), (wb_hbm, wb_ref), (wo_hbm, wo_ref),
                        (wpg_hbm, wpg_ref), (wpp_hbm, wpp_ref)], stage, sem)
        kbuf[:, 0:WINDOW, :] = jnp.zeros((4, WINDOW, V7X_LANES), BF16)
        vbuf[:, 0:WINDOW, :] = jnp.zeros((4, WINDOW, V7X_LANES), BF16)
        ubuf[0:V7X_SUBLANES, :] = jnp.zeros((V7X_SUBLANES, CONV_DIM), F32)
        r = lax.broadcasted_iota(jnp.int32, (tm, V7X_LANES), 0).astype(F32)
        ang = r * invf
        sgn = _rope_sign((tm, V7X_LANES))
        c_r = jnp.cos(ang)
        s_r = jnp.sin(ang)
        tab[0] = c_r
        tab[1] = s_r
        tab[2] = c_r * sgn
        tab[3] = s_r * sgn

    base = (i * tm).astype(F32) * invf
    cb = jnp.cos(base)
    sb = jnp.sin(base)
    cos_t = tab[0] * cb - tab[1] * sb
    sin_s = tab[3] * cb + tab[2] * sb

    sub = PROMPT_SUBTILE
    n_sub = tm // sub
    sub_chunks = sub // CHUNK
    n_keys = (WINDOW_CHUNKS + 1) * CHUNK
    pad = V7X_SUBLANES
    blocks = [(c, vh) for c in range(sub_chunks) for vh in range(N_KV_HEADS)]
    st = [dict(r0=s * sub) for s in range(n_sub)]

    def stage_qkv(s):
        s["x"] = x_ref[s["r0"]:s["r0"] + sub, :]
        s["hb"], s["qkv"] = _qkv_proj(s["x"], lng_ref, win_ref)

    def stage_gates(s):
        s["sig_a"], s["sig_c"] = _merge_gates(s["hb"], win_ref, d_model)

    def stage_qk_norm(s):
        r0 = s["r0"]
        s["q_cols"], kr, v = _qk_norm_rope(s.pop("qkv"), gq_ref, gk_ref, bdq_ref, bdk_ref,
                                           cos_t[r0:r0 + sub], sin_s[r0:r0 + sub])
        for n, t in enumerate(_kv_variants(kr)):
            kbuf[n, WINDOW + r0:WINDOW + r0 + sub, :] = t
        for n, t in enumerate(_kv_variants(v)):
            vbuf[n, WINDOW + r0:WINDOW + r0 + sub, :] = t
        if r0 + sub == tm:
            ko_ref[...] = kr[sub - WINDOW:sub, :]
            vo_ref[...] = v[sub - WINDOW:sub, :]

    def stage_scores(s):
        r0 = s["r0"]
        s_e, s_o = [], []
        for c, vh in blocks:
            rows = slice(c * CHUNK, (c + 1) * CHUNK)
            win = slice(r0 + c * CHUNK, r0 + c * CHUNK + n_keys)
            bad = None
            if r0 + c * CHUNK < WINDOW:
                col = lax.broadcasted_iota(jnp.int32, (2 * CHUNK, n_keys), 1)
                bad = (col < WINDOW - (r0 + c * CHUNK)) & (i == 0)
            qs = jnp.concatenate([s["q_cols"][2 * vh][rows], s["q_cols"][2 * vh + 1][rows]], axis=0)
            se, so = _scores(qs, kbuf[2 * vh, win, :], kbuf[2 * vh + 1, win, :], bad)
            s_e.append(se)
            s_o.append(so)
        s["s_e"] = jnp.concatenate(s_e, axis=0)
        s["s_o"] = jnp.concatenate(s_o, axis=0)
        del s["q_cols"]

    def stage_conv(s):
        r0 = s["r0"]

        def conv_fn(u):
            ubuf[pad + r0:pad + r0 + sub, :] = u
            conv = ubuf[pad + r0 - 2:pad + r0 - 2 + sub, :] * cw_ref[0:1, :]
            conv = conv + ubuf[pad + r0 - 1:pad + r0 - 1 + sub, :] * cw_ref[1:2, :]
            return conv + u * cw_ref[2:3, :]

        s["yc"] = _conv_branch(s["hb"], win_ref, wb_ref, conv_fn)

    sink_e, sink_o = _sink_cols(sink_ref, len(blocks), CHUNK)

    def stage_softmax(s):
        s["p_e"], s["r_e"] = _softmax_parts(s.pop("s_e"), sink_e)
        s["p_o"], s["r_o"] = _softmax_parts(s.pop("s_o"), sink_o)

    def stage_pv(s):
        r0 = s["r0"]
        for n, (c, vh) in enumerate(blocks):
            rows = slice(r0 + c * CHUNK, r0 + (c + 1) * CHUNK)
            win = slice(r0 + c * CHUNK, r0 + c * CHUNK + n_keys)
            br = slice(n * 2 * CHUNK, (n + 1) * 2 * CHUNK)
            o = _pv(s["p_e"][br], s["p_o"][br], vbuf[2 * vh, win, :], vbuf[2 * vh + 1, win, :],
                    s["r_e"][br], s["r_o"][br])
            attn_buf[rows, (2 * vh) * V7X_LANES:(2 * vh + 1) * V7X_LANES] = o[0:CHUNK]
            attn_buf[rows, (2 * vh + 1) * V7X_LANES:(2 * vh + 2) * V7X_LANES] = o[CHUNK:2 * CHUNK]

    def stage_attn_gate(s):
        s["silu_ga"] = _silu(_mm(s.pop("hb"), win_ref[:, OFF_GA:OFF_B]))
        s["pp"] = _mm(p_ref[s["r0"]:s["r0"] + sub, :].astype(BF16), wpp_ref[...])

    def stage_attn_out(s):
        r0 = s["r0"]
        a_in = (attn_buf[r0:r0 + sub, :] * s.pop("silu_ga")).astype(BF16)
        ya = _mm(a_in, wa_ref[...])
        s["mix"] = (s.pop("sig_a") * ya + s.pop("sig_c") * s.pop("yc")).astype(BF16)

    def stage_out_proj(s):
        s["r"] = s.pop("x") + _mm(s.pop("mix"), wo_ref[...])

    def stage_ple(s):
        r0 = s["r0"]
        r = s.pop("r")
        gate = _sigmoid(_mm(r.astype(BF16), wpg_ref[...]))
        y_ref[r0:r0 + sub, :] = r + gate * s.pop("pp")

    stages = [stage_qkv, stage_qk_norm, stage_gates, stage_scores, stage_conv, stage_softmax,
              stage_pv, stage_attn_gate, stage_attn_out, stage_out_proj, stage_ple]
    for t in range(len(stages) + PROMPT_STAGE_SKEW * (n_sub - 1)):
        for j, s in enumerate(st):
            k = t - j * PROMPT_STAGE_SKEW
            if 0 <= k < len(stages):
                stages[k](s)

    co_ref[...] = ubuf[pad + tm - 2:pad + tm, :]
    ubuf[0:pad, :] = ubuf[tm:tm + pad, :]
    kbuf[:, 0:WINDOW, :] = kbuf[:, tm:tm + WINDOW, :]
    vbuf[:, 0:WINDOW, :] = vbuf[:, tm:tm + WINDOW, :]


def _sample_kernel(x_ref, p_ref, ck_ref, cv_ref, sc_ref, lng_ref, win_hbm, gq_ref, gk_ref,
                   invf_ref, sink_ref, cw_ref, wa_hbm, wb_hbm, wo_hbm, wpg_hbm, wpp_hbm,
                   bdq_ref, bdk_ref,
                   y_ref, ko_ref, vo_ref, co_ref,
                   ubuf, attn_buf, conv_buf, tab,
                   win_ref, wa_ref, wb_ref, wo_ref, wpg_ref, wpp_ref, stage, sem):
    i = pl.program_id(0)
    bb, cache_len, _ = ck_ref.shape
    rows_total, d_model = x_ref.shape
    t_new = rows_total // bb
    n_keys = cache_len + t_new

    @pl.when(i == 0)
    def _init():
        _stage_weights([(win_hbm, win_ref), (wa_hbm, wa_ref), (wb_hbm, wb_ref), (wo_hbm, wo_ref),
                        (wpg_hbm, wpg_ref), (wpp_hbm, wpp_ref)], stage, sem)
        r = lax.broadcasted_iota(jnp.int32, (rows_total, V7X_LANES), 0)
        pos = (PAST_LEN + lax.rem(r, t_new)).astype(F32)
        ang = pos * invf_ref[...]
        tab[0] = jnp.cos(ang)
        tab[1] = jnp.sin(ang) * _rope_sign((rows_total, V7X_LANES))

    def conv_fn(u):
        pad = V7X_SUBLANES
        for b in range(bb):
            rows = slice(b * t_new, (b + 1) * t_new)
            ub = u[rows]
            ubuf[b, pad - (CONV_WIDTH - 1):pad, :] = sc_ref[b]
            ubuf[b, pad:pad + t_new, :] = ub
            conv = ubuf[b, pad - 2:pad - 2 + t_new, :] * cw_ref[0:1, :]
            conv = conv + ubuf[b, pad - 1:pad - 1 + t_new, :] * cw_ref[1:2, :]
            conv_buf[rows, :] = conv + ub * cw_ref[2:3, :]
            co_ref[b] = ubuf[b, pad + t_new - (CONV_WIDTH - 1):pad + t_new, :]
        return conv_buf[...]

    x = x_ref[...]
    hb, qkv = _qkv_proj(x, lng_ref, win_ref)
    sig_a, sig_c = _merge_gates(hb, win_ref, d_model)
    q_cols, kr, v = _qk_norm_rope(qkv, gq_ref, gk_ref, bdq_ref, bdk_ref, tab[0], tab[1])

    qi = lax.broadcasted_iota(jnp.int32, (2 * t_new, n_keys), 0)
    q_pos = PAST_LEN + lax.rem(qi, t_new)
    k_pos = PAST_LEN - cache_len + lax.broadcasted_iota(jnp.int32, (2 * t_new, n_keys), 1)
    q_ch = q_pos // CHUNK
    k_ch = k_pos // CHUNK
    bad = jnp.logical_not((k_ch <= q_ch) & (k_ch >= q_ch - WINDOW_CHUNKS))

    blocks = [(b, vh) for b in range(bb) for vh in range(N_KV_HEADS)]
    s_e, s_o, vvars = [], [], []
    for b in range(bb):
        rows = slice(b * t_new, (b + 1) * t_new)
        kcat = jnp.concatenate([ck_ref[b], kr[rows]], axis=0)
        vcat = jnp.concatenate([cv_ref[b], v[rows]], axis=0)
        ko_ref[b] = kcat[n_keys - cache_len:n_keys]
        vo_ref[b] = vcat[n_keys - cache_len:n_keys]
        kvar = _kv_variants(kcat)
        vvars.append(_kv_variants(vcat))
        for vh in range(N_KV_HEADS):
            qs = jnp.concatenate([q_cols[2 * vh][rows], q_cols[2 * vh + 1][rows]], axis=0)
            se, so = _scores(qs, kvar[2 * vh], kvar[2 * vh + 1], bad)
            s_e.append(se)
            s_o.append(so)

    yc = _conv_branch(hb, win_ref, wb_ref, conv_fn)

    sink_e, sink_o = _sink_cols(sink_ref, len(blocks), t_new)
    p_e, r_e = _softmax_parts(jnp.concatenate(s_e, axis=0), sink_e)
    p_o, r_o = _softmax_parts(jnp.concatenate(s_o, axis=0), sink_o)

    for n, (b, vh) in enumerate(blocks):
        rows = slice(b * t_new, (b + 1) * t_new)
        br = slice(n * 2 * t_new, (n + 1) * 2 * t_new)
        o = _pv(p_e[br], p_o[br], vvars[b][2 * vh], vvars[b][2 * vh + 1], r_e[br], r_o[br])
        attn_buf[rows, (2 * vh) * V7X_LANES:(2 * vh + 1) * V7X_LANES] = o[0:t_new]
        attn_buf[rows, (2 * vh + 1) * V7X_LANES:(2 * vh + 2) * V7X_LANES] = o[t_new:2 * t_new]

    y_ref[...] = _finish(x, p_ref[...], hb, attn_buf[...], yc, sig_a, sig_c, win_ref,
                         wa_ref, wo_ref, wpg_ref, wpp_ref)


def _const_spec(shape):
    nd = len(shape)
    return pl.BlockSpec(shape, lambda i: (0,) * nd, pipeline_mode=pl.Buffered(1))


def _operand_specs(c, keys):
    return [pl.BlockSpec(memory_space=pl.ANY) if k in _STAGED_KEYS else _const_spec(c[k].shape)
            for k in keys]


def _staging_scratch(c):
    return ([pltpu.VMEM(c[k].shape, BF16) for k in _STAGED_KEYS]
            + [pltpu.VMEM((2, STAGE_ROWS, STAGE_COLS), F32), pltpu.SemaphoreType.DMA((2,))])


def _smem_spec():
    return pl.BlockSpec(memory_space=pltpu.SMEM)


def _block_diag_mean(width):
    idx = np.arange(width) // HEAD_DIM
    return jnp.asarray((idx[:, None] == idx[None, :]).astype(np.float32) / HEAD_DIM, dtype=BF16)


def _layer_consts(ln_g, w_in, q_norm_g, k_norm_g, sink, conv_w, w_attn_out, w_conv_out, w_o,
                  w_ple_gate, w_ple_proj):
    scale = HEAD_DIM ** -0.5
    inv_freq = ROPE_THETA ** (-jnp.arange(0, HALF, dtype=F32) * 2.0 / HEAD_DIM)
    return dict(
        lng=ln_g.reshape(1, -1).astype(F32),
        win=w_in.astype(F32),
        gq=(jnp.tile(q_norm_g.astype(F32), N_HEADS) * scale).reshape(1, ATTN_DIM),
        gk=jnp.tile(k_norm_g.astype(F32), N_KV_HEADS).reshape(1, KV_DIM),
        invf=jnp.tile(inv_freq, V7X_LANES // HALF).reshape(1, V7X_LANES),
        sink=sink.astype(F32),
        cw=conv_w.astype(F32),
        wa=w_attn_out.astype(F32),
        wb=w_conv_out.astype(F32),
        wo=w_o.astype(F32),
        wpg=w_ple_gate.astype(F32),
        wpp=w_ple_proj.astype(F32),
        bdq=_block_diag_mean(ATTN_DIM // 2),
        bdk=_block_diag_mean(KV_DIM),
    )


_VEC_KEYS = ("lng", "win", "gq", "gk", "invf")
_MAT_KEYS = ("cw", "wa", "wb", "wo", "wpg", "wpp", "bdq", "bdk")
_STAGED_KEYS = ("win", "wa", "wb", "wo", "wpg", "wpp")


def _prompt_layer(x, p, c):
    t, d = x.shape
    tm = PROMPT_TILE
    assert t % tm == 0 and tm % PROMPT_SUBTILE == 0
    assert PROMPT_SUBTILE % CHUNK == 0 and PROMPT_SUBTILE >= WINDOW
    pre = [c[k] for k in _VEC_KEYS]
    post = [c[k] for k in _MAT_KEYS]
    in_specs = ([pl.BlockSpec((tm, d), lambda i: (i, 0)),
                 pl.BlockSpec((tm, p.shape[1]), lambda i: (i, 0))]
                + _operand_specs(c, _VEC_KEYS) + [_smem_spec()] + _operand_specs(c, _MAT_KEYS))
    out_shape = (jax.ShapeDtypeStruct((t, d), F32),
                 jax.ShapeDtypeStruct((WINDOW, KV_DIM), F32),
                 jax.ShapeDtypeStruct((WINDOW, KV_DIM), F32),
                 jax.ShapeDtypeStruct((CONV_WIDTH - 1, CONV_DIM), F32))
    out_specs = (pl.BlockSpec((tm, d), lambda i: (i, 0)),
                 pl.BlockSpec((WINDOW, KV_DIM), lambda i: (0, 0)),
                 pl.BlockSpec((WINDOW, KV_DIM), lambda i: (0, 0)),
                 pl.BlockSpec((CONV_WIDTH - 1, CONV_DIM), lambda i: (0, 0)))
    scratch = [pltpu.VMEM((4, WINDOW + tm, V7X_LANES), BF16),
               pltpu.VMEM((4, WINDOW + tm, V7X_LANES), BF16),
               pltpu.VMEM((V7X_SUBLANES + tm, CONV_DIM), F32),
               pltpu.VMEM((tm, ATTN_DIM), F32),
               pltpu.VMEM((4, tm, V7X_LANES), F32)] + _staging_scratch(c)
    return pl.pallas_call(
        _prompt_kernel,
        grid=(t // tm,),
        in_specs=in_specs,
        out_specs=out_specs,
        out_shape=out_shape,
        scratch_shapes=scratch,
        compiler_params=pltpu.CompilerParams(dimension_semantics=("arbitrary",),
                                             vmem_limit_bytes=V7X_VMEM_LIMIT_BYTES),
        name="prompt_layer",
    )(x, p, *pre, c["sink"], *post)


def _sample_layer(x, p, cache_k, cache_v, state_conv, c):
    nb, t_new, d = x.shape
    cache_len = cache_k.shape[1]
    bb = SAMPLE_BATCH_TILE
    assert nb % bb == 0 and t_new >= CONV_WIDTH - 1 and t_new <= cache_len
    rows = bb * t_new
    x2 = x.reshape(nb * t_new, d)
    p2 = p.reshape(nb * t_new, p.shape[-1])
    ck = cache_k.reshape(nb, cache_len, KV_DIM)
    cv = cache_v.reshape(nb, cache_len, KV_DIM)
    pre = [c[k] for k in _VEC_KEYS]
    post = [c[k] for k in _MAT_KEYS]
    in_specs = ([pl.BlockSpec((rows, d), lambda i: (i, 0)),
                 pl.BlockSpec((rows, p2.shape[1]), lambda i: (i, 0)),
                 pl.BlockSpec((bb, cache_len, KV_DIM), lambda i: (i, 0, 0)),
                 pl.BlockSpec((bb, cache_len, KV_DIM), lambda i: (i, 0, 0)),
                 pl.BlockSpec((bb, CONV_WIDTH - 1, CONV_DIM), lambda i: (i, 0, 0))]
                + _operand_specs(c, _VEC_KEYS) + [_smem_spec()] + _operand_specs(c, _MAT_KEYS))
    out_shape = (jax.ShapeDtypeStruct((nb * t_new, d), F32),
                 jax.ShapeDtypeStruct((nb, cache_len, KV_DIM), F32),
                 jax.ShapeDtypeStruct((nb, cache_len, KV_DIM), F32),
                 jax.ShapeDtypeStruct((nb, CONV_WIDTH - 1, CONV_DIM), F32))
    out_specs = (pl.BlockSpec((rows, d), lambda i: (i, 0)),
                 pl.BlockSpec((bb, cache_len, KV_DIM), lambda i: (i, 0, 0)),
                 pl.BlockSpec((bb, cache_len, KV_DIM), lambda i: (i, 0, 0)),
                 pl.BlockSpec((bb, CONV_WIDTH - 1, CONV_DIM), lambda i: (i, 0, 0)))
    scratch = [pltpu.VMEM((bb, V7X_SUBLANES + t_new, CONV_DIM), F32),
               pltpu.VMEM((rows, ATTN_DIM), F32),
               pltpu.VMEM((rows, CONV_DIM), F32),
               pltpu.VMEM((2, rows, V7X_LANES), F32)] + _staging_scratch(c)
    y, ko, vo, co = pl.pallas_call(
        _sample_kernel,
        grid=(nb // bb,),
        in_specs=in_specs,
        out_specs=out_specs,
        out_shape=out_shape,
        scratch_shapes=scratch,
        compiler_params=pltpu.CompilerParams(dimension_semantics=("arbitrary",),
                                             vmem_limit_bytes=V7X_VMEM_LIMIT_BYTES),
        name="sample_layer",
    )(x2, p2, ck, cv, state_conv, *pre, c["sink"], *post)
    return y.reshape(nb, t_new, d), ko, vo, co


def kernel(x_prompt, x_sample, p_prompt, p_sample, cache_k, cache_v, state_conv, ln_g, w_in,
           q_norm_g, k_norm_g, sink, conv_w, w_attn_out, w_conv_out, w_o, w_ple_gate, w_ple_proj):
    depth = ln_g.shape[0]
    batch = x_prompt.shape[0]
    hp, hs = x_prompt, x_sample
    kp_l, vp_l, cp_l, ks_l, vs_l, cs_l = [], [], [], [], [], []
    for i in range(depth):
        c = _layer_consts(ln_g[i], w_in[i], q_norm_g[i], k_norm_g[i], sink[i], conv_w[i],
                          w_attn_out[i], w_conv_out[i], w_o[i], w_ple_gate[i], w_ple_proj[i])
        ys, kps, vps, cps = [], [], [], []
        for b in range(batch):
            y, ko, vo, co = _prompt_layer(hp[b], p_prompt[i, b], c)
            ys.append(y)
            kps.append(ko.reshape(WINDOW, N_KV_HEADS, HEAD_DIM))
            vps.append(vo.reshape(WINDOW, N_KV_HEADS, HEAD_DIM))
            cps.append(co)
        hp = jnp.stack(ys)
        kp_l.append(jnp.stack(kps))
        vp_l.append(jnp.stack(vps))
        cp_l.append(jnp.stack(cps))
        hs, ko, vo, co = _sample_layer(hs, p_sample[i], cache_k[i], cache_v[i], state_conv[i], c)
        nb, cache_len = ko.shape[0], ko.shape[1]
        ks_l.append(ko.reshape(nb, cache_len, N_KV_HEADS, HEAD_DIM))
        vs_l.append(vo.reshape(nb, cache_len, N_KV_HEADS, HEAD_DIM))
        cs_l.append(co)
    return (hp, hs, jnp.stack(kp_l), jnp.stack(vp_l), jnp.stack(cp_l),
            jnp.stack(ks_l), jnp.stack(vs_l), jnp.stack(cs_l))
```

```python
import numpy as np
import jax
import jax.numpy as jnp
from jax import lax
from jax.experimental import pallas as pl
from jax.experimental.pallas import tpu as pltpu

F32 = jnp.float32
BF16 = jnp.bfloat16

CHUNK = 64
WINDOW = 128
WINDOW_CHUNKS = WINDOW // CHUNK
N_HEADS = 8
N_KV_HEADS = 2
GROUP_HEADS = N_HEADS // N_KV_HEADS
HEAD_DIM = 64
HALF = HEAD_DIM // 2
ATTN_DIM = N_HEADS * HEAD_DIM
KV_DIM = N_KV_HEADS * HEAD_DIM
CONV_DIM = 512
CONV_WIDTH = 3
PAST_LEN = 1024
ROPE_THETA = 10000.0
EPS = 1e-6
NEG = -1e30

OFF_Q = 0
OFF_K = OFF_Q + ATTN_DIM
OFF_V = OFF_K + KV_DIM
OFF_GA = OFF_V + KV_DIM
OFF_B = OFF_GA + ATTN_DIM
OFF_GC_END = OFF_B + 4 * CONV_DIM
OFF_MA = OFF_GC_END

V7X_LANES = 128
V7X_SUBLANES = 8
V7X_VMEM_LIMIT_BYTES = 56 * 1024 * 1024

PROMPT_TILE = 512
PROMPT_SUBTILE = 256
PROMPT_STAGE_SKEW = 1
STAGE_ROWS = 256
STAGE_COLS = 1024
STAGE_SLOTS = 4
SAMPLE_BATCH_TILE = 8


def _mm(a, w):
    return jnp.dot(a, w, preferred_element_type=F32)


def _mm_t(a, b):
    return lax.dot_general(a, b, (((1,), (1,)), ((), ())), preferred_element_type=F32)


def _sigmoid(x):
    return 1.0 / (1.0 + jnp.exp(-x))


def _silu(x):
    return x * _sigmoid(x)


def _rmsnorm(x, g):
    ms = jnp.mean(x * x, axis=-1, keepdims=True)
    return x * lax.rsqrt(ms + EPS) * g


def _group_mean(t, bd):
    hi = t.astype(BF16)
    lo = (t - hi.astype(F32)).astype(BF16)
    return _mm(hi, bd) + _mm(lo, bd)


def _head_norm(t, bd, g):
    ms = _group_mean(t * t, bd)
    return t * lax.rsqrt(ms + EPS) * g


def _rope(xc, cos_t, sin_s):
    lane = lax.broadcasted_iota(jnp.int32, xc.shape, 1)
    upper = (lane & HALF) != 0
    rot = jnp.where(upper, pltpu.roll(xc, HALF, 1), pltpu.roll(xc, V7X_LANES - HALF, 1))
    return xc * cos_t + rot * sin_s


def _rope_sign(shape):
    lane = lax.broadcasted_iota(jnp.int32, shape, 1)
    return jnp.where((lane & HALF) != 0, 1.0, -1.0).astype(F32)


def _kv_variants(t):
    lane = lax.broadcasted_iota(jnp.int32, t.shape, 1)
    lo = lane < HEAD_DIM
    sw = pltpu.roll(t, HEAD_DIM, 1)
    zero = jnp.zeros_like(t)
    return (jnp.where(lo, t, zero).astype(BF16), jnp.where(lo, zero, sw).astype(BF16),
            jnp.where(lo, sw, zero).astype(BF16), jnp.where(lo, zero, t).astype(BF16))


def _scores(qs, ka, kb, bad):
    s_e = _mm_t(qs, ka)
    s_o = _mm_t(qs, kb)
    if bad is not None:
        s_e = jnp.where(bad, NEG, s_e)
        s_o = jnp.where(bad, NEG, s_o)
    return s_e, s_o


def _softmax_parts(s, sink):
    m = jnp.maximum(jnp.max(s, axis=-1, keepdims=True), sink)
    e = jnp.exp(s - m)
    r = 1.0 / (jnp.sum(e, axis=-1, keepdims=True) + jnp.exp(sink - m))
    return e.astype(BF16), r


def _pv(p_e, p_o, va, vb, r_e, r_o):
    o = _mm(p_e, va) + _mm(p_o, vb)
    lane = lax.broadcasted_iota(jnp.int32, o.shape, 1)
    return o * jnp.where(lane < HEAD_DIM, r_e, r_o)


def _sink_cols(sink_ref, n_blocks, rows_per_pair):
    row = lax.broadcasted_iota(jnp.int32, (n_blocks * 2 * rows_per_pair, 1), 0)
    second_pair = (row // rows_per_pair) % 2 == 1
    second_kv = (row // (2 * rows_per_pair)) % 2 == 1

    def pick(odd):
        kv0 = jnp.where(second_pair, sink_ref[2 + odd], sink_ref[odd])
        kv1 = jnp.where(second_pair, sink_ref[GROUP_HEADS + 2 + odd], sink_ref[GROUP_HEADS + odd])
        return jnp.where(second_kv, kv1, kv0)

    return pick(0), pick(1)


def _stage_chunks(shape):
    rows, cols = shape
    assert rows % STAGE_ROWS == 0 and cols % V7X_LANES == 0
    width = max(c for c in range(V7X_LANES, STAGE_COLS + 1, V7X_LANES) if cols % c == 0)
    return [(r0, c0, STAGE_ROWS, width)
            for r0 in range(0, rows, STAGE_ROWS) for c0 in range(0, cols, width)]


def _stage_weights(pairs, stage, sem):
    chunks = [(src, dst) + ch for src, dst in pairs for ch in _stage_chunks(src.shape)]

    def copy(n):
        src, _, r0, c0, rows, cols = chunks[n]
        slot = n % STAGE_SLOTS
        return pltpu.make_async_copy(src.at[pl.ds(r0, rows), pl.ds(c0, cols)],
                                     stage.at[slot, pl.ds(0, rows), pl.ds(0, cols)],
                                     sem.at[slot])

    for n in range(min(STAGE_SLOTS, len(chunks))):
        copy(n).start()
    for n, (_, dst, r0, c0, rows, cols) in enumerate(chunks):
        copy(n).wait()
        dst[r0:r0 + rows, c0:c0 + cols] = stage[n % STAGE_SLOTS, 0:rows, 0:cols].astype(BF16)
        if n + STAGE_SLOTS < len(chunks):
            copy(n + STAGE_SLOTS).start()


def _qkv_proj(x, lng_ref, win_ref):
    hb = _rmsnorm(x, lng_ref[...]).astype(BF16)
    return hb, _mm(hb, win_ref[:, OFF_Q:OFF_GA])


def _qk_norm_rope(qkv, gq_ref, gk_ref, bdq_ref, bdk_ref, cos_t, sin_s):
    half_q = ATTN_DIM // 2
    q_cols = []
    for j in range(2):
        t = qkv[:, j * half_q:(j + 1) * half_q]
        tn = _head_norm(t, bdq_ref[...], gq_ref[:, j * half_q:(j + 1) * half_q])
        for c in range(half_q // V7X_LANES):
            q_cols.append(_rope(tn[:, c * V7X_LANES:(c + 1) * V7X_LANES], cos_t, sin_s).astype(BF16))
    kn = _head_norm(qkv[:, OFF_K:OFF_V], bdk_ref[...], gk_ref[...])
    kr = _rope(kn, cos_t, sin_s)
    v = qkv[:, OFF_V:OFF_GA]
    return q_cols, kr, v


def _merge_gates(hb, win_ref, d_model):
    mamc = _mm(hb, win_ref[:, OFF_MA:OFF_MA + 2 * d_model])
    return _sigmoid(mamc[:, 0:d_model]), _sigmoid(mamc[:, d_model:2 * d_model])


def _conv_branch(hb, win_ref, wb_ref, conv_fn):
    bcug = _mm(hb, win_ref[:, OFF_B:OFF_GC_END])
    b_gate = bcug[:, 0:CONV_DIM]
    u = bcug[:, CONV_DIM:2 * CONV_DIM] * bcug[:, 2 * CONV_DIM:3 * CONV_DIM]
    gate_c = bcug[:, 3 * CONV_DIM:4 * CONV_DIM]
    conv = conv_fn(u)
    return _mm((b_gate * conv * _silu(gate_c)).astype(BF16), wb_ref[...])


def _finish(x, p, hb, attn, yc, sig_a, sig_c, win_ref, wa_ref, wo_ref, wpg_ref, wpp_ref):
    ga = _mm(hb, win_ref[:, OFF_GA:OFF_B])
    ya = _mm((attn * _silu(ga)).astype(BF16), wa_ref[...])
    mix = sig_a * ya + sig_c * yc
    r = x + _mm(mix.astype(BF16), wo_ref[...])
    gate = _sigmoid(_mm(r.astype(BF16), wpg_ref[...]))
    return r + gate * _mm(p.astype(BF16), wpp_ref[...])


def _prompt_kernel(x_ref, p_ref, lng_ref, win_hbm, gq_ref, gk_ref, invf_ref, sink_ref, cw_ref,
                   wa_hbm, wb_hbm, wo_hbm, wpg_hbm, wpp_hbm, bdq_ref, bdk_ref,
                   y_ref, ko_ref, vo_ref, co_ref,
                   kbuf, vbuf, ubuf, attn_buf, tab,
                   win_ref, wa_ref, wb_ref, wo_ref, wpg_ref, wpp_ref, stage, sem):
    i = pl.program_id(0)
    tm, d_model = x_ref.shape
    invf = invf_ref[...]

    @pl.when(i == 0)
    def _init():
        _stage_weights([(win_hbm, win_ref), (wa_hbm, wa_ref), (wb_hbm, wb_ref), (wo_hbm, wo_ref),
                        (wpg_hbm, wpg_ref), (wpp_hbm, wpp_ref)], stage, sem)
        kbuf[:, 0:WINDOW, :] = jnp.zeros((4, WINDOW, V7X_LANES), BF16)
        vbuf[:, 0:WINDOW, :] = jnp.zeros((4, WINDOW, V7X_LANES), BF16)
        ubuf[0:V7X_SUBLANES, :] = jnp.zeros((V7X_SUBLANES, CONV_DIM), F32)
        r = lax.broadcasted_iota(jnp.int32, (tm, V7X_LANES), 0).astype(F32)
        ang = r * invf
        sgn = _rope_sign((tm, V7X_LANES))
        c_r = jnp.cos(ang)
        s_r = jnp.sin(ang)
        tab[0] = c_r
        tab[1] = s_r
        tab[2] = c_r * sgn
        tab[3] = s_r * sgn

    base = (i * tm).astype(F32) * invf
    cb = jnp.cos(base)
    sb = jnp.sin(base)
    cos_t = tab[0] * cb - tab[1] * sb
    sin_s = tab[3] * cb + tab[2] * sb

    sub = PROMPT_SUBTILE
    n_sub = tm // sub
    sub_chunks = sub // CHUNK
    n_keys = (WINDOW_CHUNKS + 1) * CHUNK
    pad = V7X_SUBLANES
    blocks = [(c, vh) for c in range(sub_chunks) for vh in range(N_KV_HEADS)]
    st = [dict(r0=s * sub) for s in range(n_sub)]

    def stage_qkv(s):
        s["x"] = x_ref[s["r0"]:s["r0"] + sub, :]
        s["hb"], s["qkv"] = _qkv_proj(s["x"], lng_ref, win_ref)

    def stage_gates(s):
        s["sig_a"], s["sig_c"] = _merge_gates(s["hb"], win_ref, d_model)

    def stage_qk_norm(s):
        r0 = s["r0"]
        s["q_cols"], kr, v = _qk_norm_rope(s.pop("qkv"), gq_ref, gk_ref, bdq_ref, bdk_ref,
                                           cos_t[r0:r0 + sub], sin_s[r0:r0 + sub])
        for n, t in enumerate(_kv_variants(kr)):
            kbuf[n, WINDOW + r0:WINDOW + r0 + sub, :] = t
        for n, t in enumerate(_kv_variants(v)):
            vbuf[n, WINDOW + r0:WINDOW + r0 + sub, :] = t
        if r0 + sub == tm:
            ko_ref[...] = kr[sub - WINDOW:sub, :]
            vo_ref[...] = v[sub - WINDOW:sub, :]

    def stage_scores(s):
        r0 = s["r0"]
        s_e, s_o = [], []
        for c, vh in blocks:
            rows = slice(c * CHUNK, (c + 1) * CHUNK)
            win = slice(r0 + c * CHUNK, r0 + c * CHUNK + n_keys)
            bad = None
            if r0 + c * CHUNK < WINDOW:
                col = lax.broadcasted_iota(jnp.int32, (2 * CHUNK, n_keys), 1)
                bad = (col < WINDOW - (r0 + c * CHUNK)) & (i == 0)
            qs = jnp.concatenate([s["q_cols"][2 * vh][rows], s["q_cols"][2 * vh + 1][rows]], axis=0)
            se, so = _scores(qs, kbuf[2 * vh, win, :], kbuf[2 * vh + 1, win, :], bad)
            s_e.append(se)
            s_o.append(so)
        s["s_e"] = jnp.concatenate(s_e, axis=0)
        s["s_o"] = jnp.concatenate(s_o, axis=0)
        del s["q_cols"]

    def stage_conv(s):
        r0 = s["r0"]

        def conv_fn(u):
            ubuf[pad + r0:pad + r0 + sub, :] = u
            conv = ubuf[pad + r0 - 2:pad + r0 - 2 + sub, :] * cw_ref[0:1, :]
            conv = conv + ubuf[pad + r0 - 1:pad + r0 - 1 + sub, :] * cw_ref[1:2, :]
            return conv + u * cw_ref[2:3, :]

        s["yc"] = _conv_branch(s["hb"], win_ref, wb_ref, conv_fn)

    sink_e, sink_o = _sink_cols(sink_ref, len(blocks), CHUNK)

    def stage_softmax(s):
        s["p_e"], s["r_e"] = _softmax_parts(s.pop("s_e"), sink_e)
        s["p_o"], s["r_o"] = _softmax_parts(s.pop("s_o"), sink_o)

    def stage_pv(s):
        r0 = s["r0"]
        for n, (c, vh) in enumerate(blocks):
            rows = slice(r0 + c * CHUNK, r0 + (c + 1) * CHUNK)
            win = slice(r0 + c * CHUNK, r0 + c * CHUNK + n_keys)
            br = slice(n * 2 * CHUNK, (n + 1) * 2 * CHUNK)
            o = _pv(s["p_e"][br], s["p_o"][br], vbuf[2 * vh, win, :], vbuf[2 * vh + 1, win, :],
                    s["r_e"][br], s["r_o"][br])
            attn_buf[rows, (2 * vh) * V7X_LANES:(2 * vh + 1) * V7X_LANES] = o[0:CHUNK]
            attn_buf[rows, (2 * vh + 1) * V7X_LANES:(2 * vh + 2) * V7X_LANES] = o[CHUNK:2 * CHUNK]

    def stage_attn_gate(s):
        s["silu_ga"] = _silu(_mm(s.pop("hb"), win_ref[:, OFF_GA:OFF_B]))
        s["pp"] = _mm(p_ref[s["r0"]:s["r0"] + sub, :].astype(BF16), wpp_ref[...])

    def stage_attn_out(s):
        r0 = s["r0"]
        a_in = (attn_buf[r0:r0 + sub, :] * s.pop("silu_ga")).astype(BF16)
        ya = _mm(a_in, wa_ref[...])
        s["mix"] = (s.pop("sig_a") * ya + s.pop("sig_c") * s.pop("yc")).astype(BF16)

    def stage_out_proj(s):
        s["r"] = s.pop("x") + _mm(s.pop("mix"), wo_ref[...])

    def stage_ple(s):
        r0 = s["r0"]
        r = s.pop("r")
        gate = _sigmoid(_mm(r.astype(BF16), wpg_ref[...]))
        y_ref[r0:r0 + sub, :] = r + gate * s.pop("pp")

    stages = [stage_qkv, stage_qk_norm, stage_gates, stage_scores, stage_conv, stage_softmax,
              stage_pv, stage_attn_gate, stage_attn_out, stage_out_proj, stage_ple]
    for t in range(len(stages) + PROMPT_STAGE_SKEW * (n_sub - 1)):
        for j, s in enumerate(st):
            k = t - j * PROMPT_STAGE_SKEW
            if 0 <= k < len(stages):
                stages[k](s)

    co_ref[...] = ubuf[pad + tm - 2:pad + tm, :]
    ubuf[0:pad, :] = ubuf[tm:tm + pad, :]
    kbuf[:, 0:WINDOW, :] = kbuf[:, tm:tm + WINDOW, :]
    vbuf[:, 0:WINDOW, :] = vbuf[:, tm:tm + WINDOW, :]


def _sample_kernel(x_ref, p_ref, ck_ref, cv_ref, sc_ref, lng_ref, win_hbm, gq_ref, gk_ref,
                   invf_ref, sink_ref, cw_ref, wa_hbm, wb_hbm, wo_hbm, wpg_hbm, wpp_hbm,
                   bdq_ref, bdk_ref,
                   y_ref, ko_ref, vo_ref, co_ref,
                   ubuf, attn_buf, conv_buf, tab,
                   win_ref, wa_ref, wb_ref, wo_ref, wpg_ref, wpp_ref, stage, sem):
    i = pl.program_id(0)
    bb, cache_len, _ = ck_ref.shape
    rows_total, d_model = x_ref.shape
    t_new = rows_total // bb
    n_keys = cache_len + t_new

    @pl.when(i == 0)
    def _init():
        _stage_weights([(win_hbm, win_ref), (wa_hbm, wa_ref), (wb_hbm, wb_ref), (wo_hbm, wo_ref),
                        (wpg_hbm, wpg_ref), (wpp_hbm, wpp_ref)], stage, sem)
        r = lax.broadcasted_iota(jnp.int32, (rows_total, V7X_LANES), 0)
        pos = (PAST_LEN + lax.rem(r, t_new)).astype(F32)
        ang = pos * invf_ref[...]
        tab[0] = jnp.cos(ang)
        tab[1] = jnp.sin(ang) * _rope_sign((rows_total, V7X_LANES))

    def conv_fn(u):
        pad = V7X_SUBLANES
        for b in range(bb):
            rows = slice(b * t_new, (b + 1) * t_new)
            ub = u[rows]
            ubuf[b, pad - (CONV_WIDTH - 1):pad, :] = sc_ref[b]
            ubuf[b, pad:pad + t_new, :] = ub
            conv = ubuf[b, pad - 2:pad - 2 + t_new, :] * cw_ref[0:1, :]
            conv = conv + ubuf[b, pad - 1:pad - 1 + t_new, :] * cw_ref[1:2, :]
            conv_buf[rows, :] = conv + ub * cw_ref[2:3, :]
            co_ref[b] = ubuf[b, pad + t_new - (CONV_WIDTH - 1):pad + t_new, :]
        return conv_buf[...]

    x = x_ref[...]
    hb, qkv = _qkv_proj(x, lng_ref, win_ref)
    sig_a, sig_c = _merge_gates(hb, win_ref, d_model)
    q_cols, kr, v = _qk_norm_rope(qkv, gq_ref, gk_ref, bdq_ref, bdk_ref, tab[0], tab[1])

    qi = lax.broadcasted_iota(jnp.int32, (2 * t_new, n_keys), 0)
    q_pos = PAST_LEN + lax.rem(qi, t_new)
    k_pos = PAST_LEN - cache_len + lax.broadcasted_iota(jnp.int32, (2 * t_new, n_keys), 1)
    q_ch = q_pos // CHUNK
    k_ch = k_pos // CHUNK
    bad = jnp.logical_not((k_ch <= q_ch) & (k_ch >= q_ch - WINDOW_CHUNKS))

    blocks = [(b, vh) for b in range(bb) for vh in range(N_KV_HEADS)]
    s_e, s_o, vvars = [], [], []
    for b in range(bb):
        rows = slice(b * t_new, (b + 1) * t_new)
        kcat = jnp.concatenate([ck_ref[b], kr[rows]], axis=0)
        vcat = jnp.concatenate([cv_ref[b], v[rows]], axis=0)
        ko_ref[b] = kcat[n_keys - cache_len:n_keys]
        vo_ref[b] = vcat[n_keys - cache_len:n_keys]
        kvar = _kv_variants(kcat)
        vvars.append(_kv_variants(vcat))
        for vh in range(N_KV_HEADS):
            qs = jnp.concatenate([q_cols[2 * vh][rows], q_cols[2 * vh + 1][rows]], axis=0)
            se, so = _scores(qs, kvar[2 * vh], kvar[2 * vh + 1], bad)
            s_e.append(se)
            s_o.append(so)

    yc = _conv_branch(hb, win_ref, wb_ref, conv_fn)

    sink_e, sink_o = _sink_cols(sink_ref, len(blocks), t_new)
    p_e, r_e = _softmax_parts(jnp.concatenate(s_e, axis=0), sink_e)
    p_o, r_o = _softmax_parts(jnp.concatenate(s_o, axis=0), sink_o)

    for n, (b, vh) in enumerate(blocks):
        rows = slice(b * t_new, (b + 1) * t_new)
        br = slice(n * 2 * t_new, (n + 1) * 2 * t_new)
        o = _pv(p_e[br], p_o[br], vvars[b][2 * vh], vvars[b][2 * vh + 1], r_e[br], r_o[br])
        attn_buf[rows, (2 * vh) * V7X_LANES:(2 * vh + 1) * V7X_LANES] = o[0:t_new]
        attn_buf[rows, (2 * vh + 1) * V7X_LANES:(2 * vh + 2) * V7X_LANES] = o[t_new:2 * t_new]

    y_ref[...] = _finish(x, p_ref[...], hb, attn_buf[...], yc, sig_a, sig_c, win_ref,
                         wa_ref, wo_ref, wpg_ref, wpp_ref)


def _const_spec(shape):
    nd = len(shape)
    return pl.BlockSpec(shape, lambda i: (0,) * nd, pipeline_mode=pl.Buffered(1))


def _operand_specs(c, keys):
    return [pl.BlockSpec(memory_space=pl.ANY) if k in _STAGED_KEYS else _const_spec(c[k].shape)
            for k in keys]


def _staging_scratch(c):
    return ([pltpu.VMEM(c[k].shape, BF16) for k in _STAGED_KEYS]
            + [pltpu.VMEM((STAGE_SLOTS, STAGE_ROWS, STAGE_COLS), F32),
               pltpu.SemaphoreType.DMA((STAGE_SLOTS,))])


def _smem_spec():
    return pl.BlockSpec(memory_space=pltpu.SMEM)


def _block_diag_mean(width):
    idx = np.arange(width) // HEAD_DIM
    return jnp.asarray((idx[:, None] == idx[None, :]).astype(np.float32) / HEAD_DIM, dtype=BF16)


def _layer_consts(ln_g, w_in, q_norm_g, k_norm_g, sink, conv_w, w_attn_out, w_conv_out, w_o,
                  w_ple_gate, w_ple_proj):
    scale = HEAD_DIM ** -0.5
    inv_freq = ROPE_THETA ** (-jnp.arange(0, HALF, dtype=F32) * 2.0 / HEAD_DIM)
    return dict(
        lng=ln_g.reshape(1, -1).astype(F32),
        win=w_in.astype(F32),
        gq=(jnp.tile(q_norm_g.astype(F32), N_HEADS) * scale).reshape(1, ATTN_DIM),
        gk=jnp.tile(k_norm_g.astype(F32), N_KV_HEADS).reshape(1, KV_DIM),
        invf=jnp.tile(inv_freq, V7X_LANES // HALF).reshape(1, V7X_LANES),
        sink=sink.astype(F32),
        cw=conv_w.astype(F32),
        wa=w_attn_out.astype(F32),
        wb=w_conv_out.astype(F32),
        wo=w_o.astype(F32),
        wpg=w_ple_gate.astype(F32),
        wpp=w_ple_proj.astype(F32),
        bdq=_block_diag_mean(ATTN_DIM // 2),
        bdk=_block_diag_mean(KV_DIM),
    )


_VEC_KEYS = ("lng", "win", "gq", "gk", "invf")
_MAT_KEYS = ("cw", "wa", "wb", "wo", "wpg", "wpp", "bdq", "bdk")
_STAGED_KEYS = ("win", "wa", "wb", "wo", "wpg", "wpp")


def _prompt_layer(x, p, c):
    t, d = x.shape
    tm = PROMPT_TILE
    assert t % tm == 0 and tm % PROMPT_SUBTILE == 0
    assert PROMPT_SUBTILE % CHUNK == 0 and PROMPT_SUBTILE >= WINDOW
    pre = [c[k] for k in _VEC_KEYS]
    post = [c[k] for k in _MAT_KEYS]
    in_specs = ([pl.BlockSpec((tm, d), lambda i: (i, 0)),
                 pl.BlockSpec((tm, p.shape[1]), lambda i: (i, 0))]
                + _operand_specs(c, _VEC_KEYS) + [_smem_spec()] + _operand_specs(c, _MAT_KEYS))
    out_shape = (jax.ShapeDtypeStruct((t, d), F32),
                 jax.ShapeDtypeStruct((WINDOW, KV_DIM), F32),
                 jax.ShapeDtypeStruct((WINDOW, KV_DIM), F32),
                 jax.ShapeDtypeStruct((CONV_WIDTH - 1, CONV_DIM), F32))
    out_specs = (pl.BlockSpec((tm, d), lambda i: (i, 0)),
                 pl.BlockSpec((WINDOW, KV_DIM), lambda i: (0, 0)),
                 pl.BlockSpec((WINDOW, KV_DIM), lambda i: (0, 0)),
                 pl.BlockSpec((CONV_WIDTH - 1, CONV_DIM), lambda i: (0, 0)))
    scratch = [pltpu.VMEM((4, WINDOW + tm, V7X_LANES), BF16),
               pltpu.VMEM((4, WINDOW + tm, V7X_LANES), BF16),
               pltpu.VMEM((V7X_SUBLANES + tm, CONV_DIM), F32),
               pltpu.VMEM((tm, ATTN_DIM), F32),
               pltpu.VMEM((4, tm, V7X_LANES), F32)] + _staging_scratch(c)
    return pl.pallas_call(
        _prompt_kernel,
        grid=(t // tm,),
        in_specs=in_specs,
        out_specs=out_specs,
        out_shape=out_shape,
        scratch_shapes=scratch,
        compiler_params=pltpu.CompilerParams(dimension_semantics=("arbitrary",),
                                             vmem_limit_bytes=V7X_VMEM_LIMIT_BYTES),
        name="prompt_layer",
    )(x, p, *pre, c["sink"], *post)


def _sample_layer(x, p, cache_k, cache_v, state_conv, c):
    nb, t_new, d = x.shape
    cache_len = cache_k.shape[1]
    bb = SAMPLE_BATCH_TILE
    assert nb % bb == 0 and t_new >= CONV_WIDTH - 1 and t_new <= cache_len
    rows = bb * t_new
    x2 = x.reshape(nb * t_new, d)
    p2 = p.reshape(nb * t_new, p.shape[-1])
    ck = cache_k.reshape(nb, cache_len, KV_DIM)
    cv = cache_v.reshape(nb, cache_len, KV_DIM)
    pre = [c[k] for k in _VEC_KEYS]
    post = [c[k] for k in _MAT_KEYS]
    in_specs = ([pl.BlockSpec((rows, d), lambda i: (i, 0)),
                 pl.BlockSpec((rows, p2.shape[1]), lambda i: (i, 0)),
                 pl.BlockSpec((bb, cache_len, KV_DIM), lambda i: (i, 0, 0)),
                 pl.BlockSpec((bb, cache_len, KV_DIM), lambda i: (i, 0, 0)),
                 pl.BlockSpec((bb, CONV_WIDTH - 1, CONV_DIM), lambda i: (i, 0, 0))]
                + _operand_specs(c, _VEC_KEYS) + [_smem_spec()] + _operand_specs(c, _MAT_KEYS))
    out_shape = (jax.ShapeDtypeStruct((nb * t_new, d), F32),
                 jax.ShapeDtypeStruct((nb, cache_len, KV_DIM), F32),
                 jax.ShapeDtypeStruct((nb, cache_len, KV_DIM), F32),
                 jax.ShapeDtypeStruct((nb, CONV_WIDTH - 1, CONV_DIM), F32))
    out_specs = (pl.BlockSpec((rows, d), lambda i: (i, 0)),
                 pl.BlockSpec((bb, cache_len, KV_DIM), lambda i: (i, 0, 0)),
                 pl.BlockSpec((bb, cache_len, KV_DIM), lambda i: (i, 0, 0)),
                 pl.BlockSpec((bb, CONV_WIDTH - 1, CONV_DIM), lambda i: (i, 0, 0)))
    scratch = [pltpu.VMEM((bb, V7X_SUBLANES + t_new, CONV_DIM), F32),
               pltpu.VMEM((rows, ATTN_DIM), F32),
               pltpu.VMEM((rows, CONV_DIM), F32),
               pltpu.VMEM((2, rows, V7X_LANES), F32)] + _staging_scratch(c)
    y, ko, vo, co = pl.pallas_call(
        _sample_kernel,
        grid=(nb // bb,),
        in_specs=in_specs,
        out_specs=out_specs,
        out_shape=out_shape,
        scratch_shapes=scratch,
        compiler_params=pltpu.CompilerParams(dimension_semantics=("arbitrary",),
                                             vmem_limit_bytes=V7X_VMEM_LIMIT_BYTES),
        name="sample_layer",
    )(x2, p2, ck, cv, state_conv, *pre, c["sink"], *post)
    return y.reshape(nb, t_new, d), ko, vo, co


def kernel(x_prompt, x_sample, p_prompt, p_sample, cache_k, cache_v, state_conv, ln_g, w_in,
           q_norm_g, k_norm_g, sink, conv_w, w_attn_out, w_conv_out, w_o, w_ple_gate, w_ple_proj):
    depth = ln_g.shape[0]
    batch = x_prompt.shape[0]
    hp, hs = x_prompt, x_sample
    kp_l, vp_l, cp_l, ks_l, vs_l, cs_l = [], [], [], [], [], []
    for i in range(depth):
        c = _layer_consts(ln_g[i], w_in[i], q_norm_g[i], k_norm_g[i], sink[i], conv_w[i],
                          w_attn_out[i], w_conv_out[i], w_o[i], w_ple_gate[i], w_ple_proj[i])
        ys, kps, vps, cps = [], [], [], []
        for b in range(batch):
            y, ko, vo, co = _prompt_layer(hp[b], p_prompt[i, b], c)
            ys.append(y)
            kps.append(ko.reshape(WINDOW, N_KV_HEADS, HEAD_DIM))
            vps.append(vo.reshape(WINDOW, N_KV_HEADS, HEAD_DIM))
            cps.append(co)
        hp = jnp.stack(ys)
        kp_l.append(jnp.stack(kps))
        vp_l.append(jnp.stack(vps))
        cp_l.append(jnp.stack(cps))
        hs, ko, vo, co = _sample_layer(hs, p_sample[i], cache_k[i], cache_v[i], state_conv[i], c)
        nb, cache_len = ko.shape[0], ko.shape[1]
        ks_l.append(ko.reshape(nb, cache_len, N_KV_HEADS, HEAD_DIM))
        vs_l.append(vo.reshape(nb, cache_len, N_KV_HEADS, HEAD_DIM))
        cs_l.append(co)
    return (hp, hs, jnp.stack(kp_l), jnp.stack(vp_l), jnp.stack(cp_l),
            jnp.stack(ks_l), jnp.stack(vs_l), jnp.stack(cs_l))
```

```python
import numpy as np
import jax
import jax.numpy as jnp
from jax import lax
from jax.experimental import pallas as pl
from jax.experimental.pallas import tpu as pltpu

F32 = jnp.float32
BF16 = jnp.bfloat16

CHUNK = 64
WINDOW = 128
WINDOW_CHUNKS = WINDOW // CHUNK
N_HEADS = 8
N_KV_HEADS = 2
GROUP_HEADS = N_HEADS // N_KV_HEADS
HEAD_DIM = 64
HALF = HEAD_DIM // 2
ATTN_DIM = N_HEADS * HEAD_DIM
KV_DIM = N_KV_HEADS * HEAD_DIM
CONV_DIM = 512
CONV_WIDTH = 3
PAST_LEN = 1024
ROPE_THETA = 10000.0
EPS = 1e-6
NEG = -1e30

OFF_Q = 0
OFF_K = OFF_Q + ATTN_DIM
OFF_V = OFF_K + KV_DIM
OFF_GA = OFF_V + KV_DIM
OFF_B = OFF_GA + ATTN_DIM
OFF_GC_END = OFF_B + 4 * CONV_DIM
OFF_MA = OFF_GC_END

V7X_LANES = 128
V7X_SUBLANES = 8
V7X_VMEM_LIMIT_BYTES = 56 * 1024 * 1024

PROMPT_TILE = 512
PROMPT_SUBTILE = 256
PROMPT_STAGE_SKEW = 1
STAGE_ROWS = 256
STAGE_COLS = 1024
STAGE_SLOTS = 4
SAMPLE_BATCH_TILE = 8


def _mm(a, w):
    return jnp.dot(a, w, preferred_element_type=F32)


def _mm_t(a, b):
    return lax.dot_general(a, b, (((1,), (1,)), ((), ())), preferred_element_type=F32)


def _sigmoid(x):
    return 1.0 / (1.0 + jnp.exp(-x))


def _silu(x):
    return x * _sigmoid(x)


def _rmsnorm(x, g):
    ms = jnp.mean(x * x, axis=-1, keepdims=True)
    return x * lax.rsqrt(ms + EPS) * g


def _group_mean(t, bd):
    hi = t.astype(BF16)
    lo = (t - hi.astype(F32)).astype(BF16)
    return _mm(hi, bd) + _mm(lo, bd)


def _head_norm(t, bd, g):
    ms = _group_mean(t * t, bd)
    return t * lax.rsqrt(ms + EPS) * g


def _rope(xc, cos_t, sin_s):
    lane = lax.broadcasted_iota(jnp.int32, xc.shape, 1)
    upper = (lane & HALF) != 0
    rot = jnp.where(upper, pltpu.roll(xc, HALF, 1), pltpu.roll(xc, V7X_LANES - HALF, 1))
    return xc * cos_t + rot * sin_s


def _rope_sign(shape):
    lane = lax.broadcasted_iota(jnp.int32, shape, 1)
    return jnp.where((lane & HALF) != 0, 1.0, -1.0).astype(F32)


def _kv_variants(t):
    lane = lax.broadcasted_iota(jnp.int32, t.shape, 1)
    lo = lane < HEAD_DIM
    sw = pltpu.roll(t, HEAD_DIM, 1)
    zero = jnp.zeros_like(t)
    return (jnp.where(lo, t, zero).astype(BF16), jnp.where(lo, zero, sw).astype(BF16),
            jnp.where(lo, sw, zero).astype(BF16), jnp.where(lo, zero, t).astype(BF16))


def _scores(qs, ka, kb, bad):
    s_e = _mm_t(qs, ka)
    s_o = _mm_t(qs, kb)
    if bad is not None:
        s_e = jnp.where(bad, NEG, s_e)
        s_o = jnp.where(bad, NEG, s_o)
    return s_e, s_o


def _softmax_parts(s, sink):
    m = jnp.maximum(jnp.max(s, axis=-1, keepdims=True), sink)
    e = jnp.exp(s - m)
    r = 1.0 / (jnp.sum(e, axis=-1, keepdims=True) + jnp.exp(sink - m))
    return e.astype(BF16), r


def _pv(p_e, p_o, va, vb, r_e, r_o):
    o = _mm(p_e, va) + _mm(p_o, vb)
    lane = lax.broadcasted_iota(jnp.int32, o.shape, 1)
    return o * jnp.where(lane < HEAD_DIM, r_e, r_o)


def _sink_cols(sink_ref, n_blocks, rows_per_pair):
    row = lax.broadcasted_iota(jnp.int32, (n_blocks * 2 * rows_per_pair, 1), 0)
    second_pair = (row // rows_per_pair) % 2 == 1
    second_kv = (row // (2 * rows_per_pair)) % 2 == 1

    def pick(odd):
        kv0 = jnp.where(second_pair, sink_ref[2 + odd], sink_ref[odd])
        kv1 = jnp.where(second_pair, sink_ref[GROUP_HEADS + 2 + odd], sink_ref[GROUP_HEADS + odd])
        return jnp.where(second_kv, kv1, kv0)

    return pick(0), pick(1)


def _stage_chunks(shape):
    rows, cols = shape
    assert rows % STAGE_ROWS == 0 and cols % V7X_LANES == 0
    width = max(c for c in range(V7X_LANES, STAGE_COLS + 1, V7X_LANES) if cols % c == 0)
    return [(r0, c0, STAGE_ROWS, width)
            for r0 in range(0, rows, STAGE_ROWS) for c0 in range(0, cols, width)]


def _stage_weights(pairs, stage, sem):
    chunks = [(src, dst) + ch for src, dst in pairs for ch in _stage_chunks(src.shape)]

    def copy(n):
        src, _, r0, c0, rows, cols = chunks[n]
        slot = n % STAGE_SLOTS
        return pltpu.make_async_copy(src.at[pl.ds(r0, rows), pl.ds(c0, cols)],
                                     stage.at[slot, pl.ds(0, rows), pl.ds(0, cols)],
                                     sem.at[slot])

    for n in range(min(STAGE_SLOTS, len(chunks))):
        copy(n).start()
    for n, (_, dst, r0, c0, rows, cols) in enumerate(chunks):
        copy(n).wait()
        dst[r0:r0 + rows, c0:c0 + cols] = stage[n % STAGE_SLOTS, 0:rows, 0:cols].astype(BF16)
        if n + STAGE_SLOTS < len(chunks):
            copy(n + STAGE_SLOTS).start()


def _qkv_proj(x, lng_ref, win_ref):
    hb = _rmsnorm(x, lng_ref[...]).astype(BF16)
    return hb, _mm(hb, win_ref[:, OFF_Q:OFF_GA])


def _qk_norm_rope(qkv, gq_ref, gk_ref, bdq_ref, bdk_ref, cos_t, sin_s):
    half_q = ATTN_DIM // 2
    q_cols = []
    for j in range(2):
        t = qkv[:, j * half_q:(j + 1) * half_q]
        tn = _head_norm(t, bdq_ref[...], gq_ref[:, j * half_q:(j + 1) * half_q])
        for c in range(half_q // V7X_LANES):
            q_cols.append(_rope(tn[:, c * V7X_LANES:(c + 1) * V7X_LANES], cos_t, sin_s).astype(BF16))
    kn = _head_norm(qkv[:, OFF_K:OFF_V], bdk_ref[...], gk_ref[...])
    kr = _rope(kn, cos_t, sin_s)
    v = qkv[:, OFF_V:OFF_GA]
    return q_cols, kr, v


def _merge_gates(hb, win_ref, d_model):
    mamc = _mm(hb, win_ref[:, OFF_MA:OFF_MA + 2 * d_model])
    return _sigmoid(mamc[:, 0:d_model]), _sigmoid(mamc[:, d_model:2 * d_model])


def _conv_branch(hb, win_ref, wb_ref, conv_fn):
    bcug = _mm(hb, win_ref[:, OFF_B:OFF_GC_END])
    b_gate = bcug[:, 0:CONV_DIM]
    u = bcug[:, CONV_DIM:2 * CONV_DIM] * bcug[:, 2 * CONV_DIM:3 * CONV_DIM]
    gate_c = bcug[:, 3 * CONV_DIM:4 * CONV_DIM]
    conv = conv_fn(u)
    return _mm((b_gate * conv * _silu(gate_c)).astype(BF16), wb_ref[...])


def _finish(x, p, hb, attn, yc, sig_a, sig_c, win_ref, wa_ref, wo_ref, wpg_ref, wpp_ref):
    ga = _mm(hb, win_ref[:, OFF_GA:OFF_B])
    ya = _mm((attn * _silu(ga)).astype(BF16), wa_ref[...])
    mix = sig_a * ya + sig_c * yc
    r = x + _mm(mix.astype(BF16), wo_ref[...])
    gate = _sigmoid(_mm(r.astype(BF16), wpg_ref[...]))
    return r + gate * _mm(p.astype(BF16), wpp_ref[...])


def _prompt_kernel(x_ref, p_ref, lng_ref, win_hbm, gq_ref, gk_ref, invf_ref, sink_ref, cw_ref,
                   wa_hbm, wb_hbm, wo_hbm, wpg_hbm, wpp_hbm, bdq_ref, bdk_ref,
                   y_ref, ko_ref, vo_ref, co_ref,
                   win_out, wa_out, wb_out, wo_out, wpg_out, wpp_out,
                   kbuf, vbuf, ubuf, attn_buf, tab,
                   win_ref, wa_ref, wb_ref, wo_ref, wpg_ref, wpp_ref, stage, sem, out_sem):
    i = pl.program_id(0)
    tm, d_model = x_ref.shape
    invf = invf_ref[...]
    exports = [pltpu.make_async_copy(src, dst, out_sem.at[n]) for n, (src, dst) in enumerate(
        [(win_ref, win_out), (wa_ref, wa_out), (wb_ref, wb_out), (wo_ref, wo_out),
         (wpg_ref, wpg_out), (wpp_ref, wpp_out)])]

    @pl.when(i == 0)
    def _init():
        _stage_weights([(win_hbm, win_ref), (wa_hbm, wa_ref), (wb_hbm, wb_ref), (wo_hbm, wo_ref),
                        (wpg_hbm, wpg_ref), (wpp_hbm, wpp_ref)], stage, sem)
        for e in exports:
            e.start()
        kbuf[:, 0:WINDOW, :] = jnp.zeros((4, WINDOW, V7X_LANES), BF16)
        vbuf[:, 0:WINDOW, :] = jnp.zeros((4, WINDOW, V7X_LANES), BF16)
        ubuf[0:V7X_SUBLANES, :] = jnp.zeros((V7X_SUBLANES, CONV_DIM), F32)
        r = lax.broadcasted_iota(jnp.int32, (tm, V7X_LANES), 0).astype(F32)
        ang = r * invf
        sgn = _rope_sign((tm, V7X_LANES))
        c_r = jnp.cos(ang)
        s_r = jnp.sin(ang)
        tab[0] = c_r
        tab[1] = s_r
        tab[2] = c_r * sgn
        tab[3] = s_r * sgn

    base = (i * tm).astype(F32) * invf
    cb = jnp.cos(base)
    sb = jnp.sin(base)
    cos_t = tab[0] * cb - tab[1] * sb
    sin_s = tab[3] * cb + tab[2] * sb

    sub = PROMPT_SUBTILE
    n_sub = tm // sub
    sub_chunks = sub // CHUNK
    n_keys = (WINDOW_CHUNKS + 1) * CHUNK
    pad = V7X_SUBLANES
    blocks = [(c, vh) for c in range(sub_chunks) for vh in range(N_KV_HEADS)]
    st = [dict(r0=s * sub) for s in range(n_sub)]

    def stage_qkv(s):
        s["x"] = x_ref[s["r0"]:s["r0"] + sub, :]
        s["hb"], s["qkv"] = _qkv_proj(s["x"], lng_ref, win_ref)

    def stage_gates(s):
        s["sig_a"], s["sig_c"] = _merge_gates(s["hb"], win_ref, d_model)

    def stage_qk_norm(s):
        r0 = s["r0"]
        s["q_cols"], kr, v = _qk_norm_rope(s.pop("qkv"), gq_ref, gk_ref, bdq_ref, bdk_ref,
                                           cos_t[r0:r0 + sub], sin_s[r0:r0 + sub])
        for n, t in enumerate(_kv_variants(kr)):
            kbuf[n, WINDOW + r0:WINDOW + r0 + sub, :] = t
        for n, t in enumerate(_kv_variants(v)):
            vbuf[n, WINDOW + r0:WINDOW + r0 + sub, :] = t
        if r0 + sub == tm:
            ko_ref[...] = kr[sub - WINDOW:sub, :]
            vo_ref[...] = v[sub - WINDOW:sub, :]

    def stage_scores(s):
        r0 = s["r0"]
        s_e, s_o = [], []
        for c, vh in blocks:
            rows = slice(c * CHUNK, (c + 1) * CHUNK)
            win = slice(r0 + c * CHUNK, r0 + c * CHUNK + n_keys)
            bad = None
            if r0 + c * CHUNK < WINDOW:
                col = lax.broadcasted_iota(jnp.int32, (2 * CHUNK, n_keys), 1)
                bad = (col < WINDOW - (r0 + c * CHUNK)) & (i == 0)
            qs = jnp.concatenate([s["q_cols"][2 * vh][rows], s["q_cols"][2 * vh + 1][rows]], axis=0)
            se, so = _scores(qs, kbuf[2 * vh, win, :], kbuf[2 * vh + 1, win, :], bad)
            s_e.append(se)
            s_o.append(so)
        s["s_e"] = jnp.concatenate(s_e, axis=0)
        s["s_o"] = jnp.concatenate(s_o, axis=0)
        del s["q_cols"]

    def stage_conv(s):
        r0 = s["r0"]

        def conv_fn(u):
            ubuf[pad + r0:pad + r0 + sub, :] = u
            conv = ubuf[pad + r0 - 2:pad + r0 - 2 + sub, :] * cw_ref[0:1, :]
            conv = conv + ubuf[pad + r0 - 1:pad + r0 - 1 + sub, :] * cw_ref[1:2, :]
            return conv + u * cw_ref[2:3, :]

        s["yc"] = _conv_branch(s["hb"], win_ref, wb_ref, conv_fn)

    sink_e, sink_o = _sink_cols(sink_ref, len(blocks), CHUNK)

    def stage_softmax(s):
        s["p_e"], s["r_e"] = _softmax_parts(s.pop("s_e"), sink_e)
        s["p_o"], s["r_o"] = _softmax_parts(s.pop("s_o"), sink_o)

    def stage_pv(s):
        r0 = s["r0"]
        for n, (c, vh) in enumerate(blocks):
            rows = slice(r0 + c * CHUNK, r0 + (c + 1) * CHUNK)
            win = slice(r0 + c * CHUNK, r0 + c * CHUNK + n_keys)
            br = slice(n * 2 * CHUNK, (n + 1) * 2 * CHUNK)
            o = _pv(s["p_e"][br], s["p_o"][br], vbuf[2 * vh, win, :], vbuf[2 * vh + 1, win, :],
                    s["r_e"][br], s["r_o"][br])
            attn_buf[rows, (2 * vh) * V7X_LANES:(2 * vh + 1) * V7X_LANES] = o[0:CHUNK]
            attn_buf[rows, (2 * vh + 1) * V7X_LANES:(2 * vh + 2) * V7X_LANES] = o[CHUNK:2 * CHUNK]

    def stage_attn_gate(s):
        s["silu_ga"] = _silu(_mm(s.pop("hb"), win_ref[:, OFF_GA:OFF_B]))
        s["pp"] = _mm(p_ref[s["r0"]:s["r0"] + sub, :].astype(BF16), wpp_ref[...])

    def stage_attn_out(s):
        r0 = s["r0"]
        a_in = (attn_buf[r0:r0 + sub, :] * s.pop("silu_ga")).astype(BF16)
        ya = _mm(a_in, wa_ref[...])
        s["mix"] = (s.pop("sig_a") * ya + s.pop("sig_c") * s.pop("yc")).astype(BF16)

    def stage_out_proj(s):
        s["r"] = s.pop("x") + _mm(s.pop("mix"), wo_ref[...])

    def stage_ple(s):
        r0 = s["r0"]
        r = s.pop("r")
        gate = _sigmoid(_mm(r.astype(BF16), wpg_ref[...]))
        y_ref[r0:r0 + sub, :] = r + gate * s.pop("pp")

    stages = [stage_qkv, stage_qk_norm, stage_gates, stage_scores, stage_conv, stage_softmax,
              stage_pv, stage_attn_gate, stage_attn_out, stage_out_proj, stage_ple]
    for t in range(len(stages) + PROMPT_STAGE_SKEW * (n_sub - 1)):
        for j, s in enumerate(st):
            k = t - j * PROMPT_STAGE_SKEW
            if 0 <= k < len(stages):
                stages[k](s)

    co_ref[...] = ubuf[pad + tm - 2:pad + tm, :]
    ubuf[0:pad, :] = ubuf[tm:tm + pad, :]
    kbuf[:, 0:WINDOW, :] = kbuf[:, tm:tm + WINDOW, :]
    vbuf[:, 0:WINDOW, :] = vbuf[:, tm:tm + WINDOW, :]

    @pl.when(i == pl.num_programs(0) - 1)
    def _drain():
        for e in exports:
            e.wait()


def _sample_kernel(x_ref, p_ref, ck_ref, cv_ref, sc_ref, lng_ref, win_ref, gq_ref, gk_ref,
                   invf_ref, sink_ref, cw_ref, wa_ref, wb_ref, wo_ref, wpg_ref, wpp_ref,
                   bdq_ref, bdk_ref,
                   y_ref, ko_ref, vo_ref, co_ref,
                   ubuf, attn_buf, conv_buf, tab):
    i = pl.program_id(0)
    bb, cache_len, _ = ck_ref.shape
    rows_total, d_model = x_ref.shape
    t_new = rows_total // bb
    n_keys = cache_len + t_new

    @pl.when(i == 0)
    def _init():
        r = lax.broadcasted_iota(jnp.int32, (rows_total, V7X_LANES), 0)
        pos = (PAST_LEN + lax.rem(r, t_new)).astype(F32)
        ang = pos * invf_ref[...]
        tab[0] = jnp.cos(ang)
        tab[1] = jnp.sin(ang) * _rope_sign((rows_total, V7X_LANES))

    def conv_fn(u):
        pad = V7X_SUBLANES
        for b in range(bb):
            rows = slice(b * t_new, (b + 1) * t_new)
            ub = u[rows]
            ubuf[b, pad - (CONV_WIDTH - 1):pad, :] = sc_ref[b]
            ubuf[b, pad:pad + t_new, :] = ub
            conv = ubuf[b, pad - 2:pad - 2 + t_new, :] * cw_ref[0:1, :]
            conv = conv + ubuf[b, pad - 1:pad - 1 + t_new, :] * cw_ref[1:2, :]
            conv_buf[rows, :] = conv + ub * cw_ref[2:3, :]
            co_ref[b] = ubuf[b, pad + t_new - (CONV_WIDTH - 1):pad + t_new, :]
        return conv_buf[...]

    x = x_ref[...]
    hb, qkv = _qkv_proj(x, lng_ref, win_ref)
    sig_a, sig_c = _merge_gates(hb, win_ref, d_model)
    q_cols, kr, v = _qk_norm_rope(qkv, gq_ref, gk_ref, bdq_ref, bdk_ref, tab[0], tab[1])

    qi = lax.broadcasted_iota(jnp.int32, (2 * t_new, n_keys), 0)
    q_pos = PAST_LEN + lax.rem(qi, t_new)
    k_pos = PAST_LEN - cache_len + lax.broadcasted_iota(jnp.int32, (2 * t_new, n_keys), 1)
    q_ch = q_pos // CHUNK
    k_ch = k_pos // CHUNK
    bad = jnp.logical_not((k_ch <= q_ch) & (k_ch >= q_ch - WINDOW_CHUNKS))

    blocks = [(b, vh) for b in range(bb) for vh in range(N_KV_HEADS)]
    s_e, s_o, vvars = [], [], []
    for b in range(bb):
        rows = slice(b * t_new, (b + 1) * t_new)
        kcat = jnp.concatenate([ck_ref[b], kr[rows]], axis=0)
        vcat = jnp.concatenate([cv_ref[b], v[rows]], axis=0)
        ko_ref[b] = kcat[n_keys - cache_len:n_keys]
        vo_ref[b] = vcat[n_keys - cache_len:n_keys]
        kvar = _kv_variants(kcat)
        vvars.append(_kv_variants(vcat))
        for vh in range(N_KV_HEADS):
            qs = jnp.concatenate([q_cols[2 * vh][rows], q_cols[2 * vh + 1][rows]], axis=0)
            se, so = _scores(qs, kvar[2 * vh], kvar[2 * vh + 1], bad)
            s_e.append(se)
            s_o.append(so)

    yc = _conv_branch(hb, win_ref, wb_ref, conv_fn)

    sink_e, sink_o = _sink_cols(sink_ref, len(blocks), t_new)
    p_e, r_e = _softmax_parts(jnp.concatenate(s_e, axis=0), sink_e)
    p_o, r_o = _softmax_parts(jnp.concatenate(s_o, axis=0), sink_o)

    for n, (b, vh) in enumerate(blocks):
        rows = slice(b * t_new, (b + 1) * t_new)
        br = slice(n * 2 * t_new, (n + 1) * 2 * t_new)
        o = _pv(p_e[br], p_o[br], vvars[b][2 * vh], vvars[b][2 * vh + 1], r_e[br], r_o[br])
        attn_buf[rows, (2 * vh) * V7X_LANES:(2 * vh + 1) * V7X_LANES] = o[0:t_new]
        attn_buf[rows, (2 * vh + 1) * V7X_LANES:(2 * vh + 2) * V7X_LANES] = o[t_new:2 * t_new]

    y_ref[...] = _finish(x, p_ref[...], hb, attn_buf[...], yc, sig_a, sig_c, win_ref,
                         wa_ref, wo_ref, wpg_ref, wpp_ref)


def _const_spec(shape):
    nd = len(shape)
    return pl.BlockSpec(shape, lambda i: (0,) * nd, pipeline_mode=pl.Buffered(1))


def _operand_specs(c, keys, staged):
    return [pl.BlockSpec(memory_space=pl.ANY) if staged and k in _STAGED_KEYS
            else _const_spec(c[k].shape) for k in keys]


def _staging_scratch(c):
    return ([pltpu.VMEM(c[k].shape, BF16) for k in _STAGED_KEYS]
            + [pltpu.VMEM((STAGE_SLOTS, STAGE_ROWS, STAGE_COLS), F32),
               pltpu.SemaphoreType.DMA((STAGE_SLOTS,)),
               pltpu.SemaphoreType.DMA((len(_STAGED_KEYS),))])


def _smem_spec():
    return pl.BlockSpec(memory_space=pltpu.SMEM)


def _block_diag_mean(width):
    idx = np.arange(width) // HEAD_DIM
    return jnp.asarray((idx[:, None] == idx[None, :]).astype(np.float32) / HEAD_DIM, dtype=BF16)


def _layer_consts(ln_g, w_in, q_norm_g, k_norm_g, sink, conv_w, w_attn_out, w_conv_out, w_o,
                  w_ple_gate, w_ple_proj):
    scale = HEAD_DIM ** -0.5
    inv_freq = ROPE_THETA ** (-jnp.arange(0, HALF, dtype=F32) * 2.0 / HEAD_DIM)
    return dict(
        lng=ln_g.reshape(1, -1).astype(F32),
        win=w_in.astype(F32),
        gq=(jnp.tile(q_norm_g.astype(F32), N_HEADS) * scale).reshape(1, ATTN_DIM),
        gk=jnp.tile(k_norm_g.astype(F32), N_KV_HEADS).reshape(1, KV_DIM),
        invf=jnp.tile(inv_freq, V7X_LANES // HALF).reshape(1, V7X_LANES),
        sink=sink.astype(F32),
        cw=conv_w.astype(F32),
        wa=w_attn_out.astype(F32),
        wb=w_conv_out.astype(F32),
        wo=w_o.astype(F32),
        wpg=w_ple_gate.astype(F32),
        wpp=w_ple_proj.astype(F32),
        bdq=_block_diag_mean(ATTN_DIM // 2),
        bdk=_block_diag_mean(KV_DIM),
    )


_VEC_KEYS = ("lng", "win", "gq", "gk", "invf")
_MAT_KEYS = ("cw", "wa", "wb", "wo", "wpg", "wpp", "bdq", "bdk")
_STAGED_KEYS = ("win", "wa", "wb", "wo", "wpg", "wpp")


def _prompt_layer(x, p, c):
    t, d = x.shape
    tm = PROMPT_TILE
    assert t % tm == 0 and tm % PROMPT_SUBTILE == 0
    assert PROMPT_SUBTILE % CHUNK == 0 and PROMPT_SUBTILE >= WINDOW
    pre = [c[k] for k in _VEC_KEYS]
    post = [c[k] for k in _MAT_KEYS]
    in_specs = ([pl.BlockSpec((tm, d), lambda i: (i, 0)),
                 pl.BlockSpec((tm, p.shape[1]), lambda i: (i, 0))]
                + _operand_specs(c, _VEC_KEYS, True) + [_smem_spec()]
                + _operand_specs(c, _MAT_KEYS, True))
    out_shape = ((jax.ShapeDtypeStruct((t, d), F32),
                  jax.ShapeDtypeStruct((WINDOW, KV_DIM), F32),
                  jax.ShapeDtypeStruct((WINDOW, KV_DIM), F32),
                  jax.ShapeDtypeStruct((CONV_WIDTH - 1, CONV_DIM), F32))
                 + tuple(jax.ShapeDtypeStruct(c[k].shape, BF16) for k in _STAGED_KEYS))
    out_specs = ((pl.BlockSpec((tm, d), lambda i: (i, 0)),
                  pl.BlockSpec((WINDOW, KV_DIM), lambda i: (0, 0)),
                  pl.BlockSpec((WINDOW, KV_DIM), lambda i: (0, 0)),
                  pl.BlockSpec((CONV_WIDTH - 1, CONV_DIM), lambda i: (0, 0)))
                 + tuple(pl.BlockSpec(memory_space=pl.ANY) for _ in _STAGED_KEYS))
    scratch = [pltpu.VMEM((4, WINDOW + tm, V7X_LANES), BF16),
               pltpu.VMEM((4, WINDOW + tm, V7X_LANES), BF16),
               pltpu.VMEM((V7X_SUBLANES + tm, CONV_DIM), F32),
               pltpu.VMEM((tm, ATTN_DIM), F32),
               pltpu.VMEM((4, tm, V7X_LANES), F32)] + _staging_scratch(c)
    outs = pl.pallas_call(
        _prompt_kernel,
        grid=(t // tm,),
        in_specs=in_specs,
        out_specs=out_specs,
        out_shape=out_shape,
        scratch_shapes=scratch,
        compiler_params=pltpu.CompilerParams(dimension_semantics=("arbitrary",),
                                             vmem_limit_bytes=V7X_VMEM_LIMIT_BYTES),
        name="prompt_layer",
    )(x, p, *pre, c["sink"], *post)
    return outs[:4], dict(c, **dict(zip(_STAGED_KEYS, outs[4:])))


def _sample_layer(x, p, cache_k, cache_v, state_conv, c):
    nb, t_new, d = x.shape
    cache_len = cache_k.shape[1]
    bb = SAMPLE_BATCH_TILE
    assert nb % bb == 0 and t_new >= CONV_WIDTH - 1 and t_new <= cache_len
    rows = bb * t_new
    x2 = x.reshape(nb * t_new, d)
    p2 = p.reshape(nb * t_new, p.shape[-1])
    ck = cache_k.reshape(nb, cache_len, KV_DIM)
    cv = cache_v.reshape(nb, cache_len, KV_DIM)
    pre = [c[k] for k in _VEC_KEYS]
    post = [c[k] for k in _MAT_KEYS]
    in_specs = ([pl.BlockSpec((rows, d), lambda i: (i, 0)),
                 pl.BlockSpec((rows, p2.shape[1]), lambda i: (i, 0)),
                 pl.BlockSpec((bb, cache_len, KV_DIM), lambda i: (i, 0, 0)),
                 pl.BlockSpec((bb, cache_len, KV_DIM), lambda i: (i, 0, 0)),
                 pl.BlockSpec((bb, CONV_WIDTH - 1, CONV_DIM), lambda i: (i, 0, 0))]
                + _operand_specs(c, _VEC_KEYS, False) + [_smem_spec()]
                + _operand_specs(c, _MAT_KEYS, False))
    out_shape = (jax.ShapeDtypeStruct((nb * t_new, d), F32),
                 jax.ShapeDtypeStruct((nb, cache_len, KV_DIM), F32),
                 jax.ShapeDtypeStruct((nb, cache_len, KV_DIM), F32),
                 jax.ShapeDtypeStruct((nb, CONV_WIDTH - 1, CONV_DIM), F32))
    out_specs = (pl.BlockSpec((rows, d), lambda i: (i, 0)),
                 pl.BlockSpec((bb, cache_len, KV_DIM), lambda i: (i, 0, 0)),
                 pl.BlockSpec((bb, cache_len, KV_DIM), lambda i: (i, 0, 0)),
                 pl.BlockSpec((bb, CONV_WIDTH - 1, CONV_DIM), lambda i: (i, 0, 0)))
    scratch = [pltpu.VMEM((bb, V7X_SUBLANES + t_new, CONV_DIM), F32),
               pltpu.VMEM((rows, ATTN_DIM), F32),
               pltpu.VMEM((rows, CONV_DIM), F32),
               pltpu.VMEM((2, rows, V7X_LANES), F32)]
    y, ko, vo, co = pl.pallas_call(
        _sample_kernel,
        grid=(nb // bb,),
        in_specs=in_specs,
        out_specs=out_specs,
        out_shape=out_shape,
        scratch_shapes=scratch,
        compiler_params=pltpu.CompilerParams(dimension_semantics=("arbitrary",),
                                             vmem_limit_bytes=V7X_VMEM_LIMIT_BYTES),
        name="sample_layer",
    )(x2, p2, ck, cv, state_conv, *pre, c["sink"], *post)
    return y.reshape(nb, t_new, d), ko, vo, co


def kernel(x_prompt, x_sample, p_prompt, p_sample, cache_k, cache_v, state_conv, ln_g, w_in,
           q_norm_g, k_norm_g, sink, conv_w, w_attn_out, w_conv_out, w_o, w_ple_gate, w_ple_proj):
    depth = ln_g.shape[0]
    batch = x_prompt.shape[0]
    hp, hs = x_prompt, x_sample
    kp_l, vp_l, cp_l, ks_l, vs_l, cs_l = [], [], [], [], [], []
    for i in range(depth):
        c = _layer_consts(ln_g[i], w_in[i], q_norm_g[i], k_norm_g[i], sink[i], conv_w[i],
                          w_attn_out[i], w_conv_out[i], w_o[i], w_ple_gate[i], w_ple_proj[i])
        ys, kps, vps, cps = [], [], [], []
        for b in range(batch):
            (y, ko, vo, co), c_bf16 = _prompt_layer(hp[b], p_prompt[i, b], c)
            ys.append(y)
            kps.append(ko.reshape(WINDOW, N_KV_HEADS, HEAD_DIM))
            vps.append(vo.reshape(WINDOW, N_KV_HEADS, HEAD_DIM))
            cps.append(co)
        hp = jnp.stack(ys)
        kp_l.append(jnp.stack(kps))
        vp_l.append(jnp.stack(vps))
        cp_l.append(jnp.stack(cps))
        hs, ko, vo, co = _sample_layer(hs, p_sample[i], cache_k[i], cache_v[i], state_conv[i], c_bf16)
        nb, cache_len = ko.shape[0], ko.shape[1]
        ks_l.append(ko.reshape(nb, cache_len, N_KV_HEADS, HEAD_DIM))
        vs_l.append(vo.reshape(nb, cache_len, N_KV_HEADS, HEAD_DIM))
        cs_l.append(co)
    return (hp, hs, jnp.stack(kp_l), jnp.stack(vp_l), jnp.stack(cp_l),
            jnp.stack(ks_l), jnp.stack(vs_l), jnp.stack(cs_l))
```

```python
import numpy as np
import jax
import jax.numpy as jnp
from jax import lax
from jax.experimental import pallas as pl
from jax.experimental.pallas import tpu as pltpu

F32 = jnp.float32
BF16 = jnp.bfloat16

CHUNK = 64
WINDOW = 128
WINDOW_CHUNKS = WINDOW // CHUNK
N_HEADS = 8
N_KV_HEADS = 2
GROUP_HEADS = N_HEADS // N_KV_HEADS
HEAD_DIM = 64
HALF = HEAD_DIM // 2
ATTN_DIM = N_HEADS * HEAD_DIM
KV_DIM = N_KV_HEADS * HEAD_DIM
CONV_DIM = 512
CONV_WIDTH = 3
PAST_LEN = 1024
ROPE_THETA = 10000.0
EPS = 1e-6
NEG = -1e30

OFF_Q = 0
OFF_K = OFF_Q + ATTN_DIM
OFF_V = OFF_K + KV_DIM
OFF_GA = OFF_V + KV_DIM
OFF_B = OFF_GA + ATTN_DIM
OFF_GC_END = OFF_B + 4 * CONV_DIM
OFF_MA = OFF_GC_END

V7X_LANES = 128
V7X_SUBLANES = 8
V7X_VMEM_LIMIT_BYTES = 56 * 1024 * 1024

PROMPT_TILE = 512
PROMPT_SUBTILE = 256
PROMPT_STAGE_SKEW = 1
STAGE_ROWS = 256
STAGE_COLS = 1024
STAGE_SLOTS = 4
SAMPLE_BATCH_TILE = 8


def _mm(a, w):
    return jnp.dot(a, w, preferred_element_type=F32)


def _mm_t(a, b):
    return lax.dot_general(a, b, (((1,), (1,)), ((), ())), preferred_element_type=F32)


def _sigmoid(x):
    return 1.0 / (1.0 + jnp.exp(-x))


def _silu(x):
    return x * _sigmoid(x)


def _rmsnorm(x, g):
    ms = jnp.mean(x * x, axis=-1, keepdims=True)
    return x * lax.rsqrt(ms + EPS) * g


def _group_mean(t, bd):
    hi = t.astype(BF16)
    lo = (t - hi.astype(F32)).astype(BF16)
    return _mm(hi, bd) + _mm(lo, bd)


def _head_norm(t, bd, g):
    ms = _group_mean(t * t, bd)
    return t * lax.rsqrt(ms + EPS) * g


def _rope(xc, cos_t, sin_s):
    lane = lax.broadcasted_iota(jnp.int32, xc.shape, 1)
    upper = (lane & HALF) != 0
    rot = jnp.where(upper, pltpu.roll(xc, HALF, 1), pltpu.roll(xc, V7X_LANES - HALF, 1))
    return xc * cos_t + rot * sin_s


def _rope_sign(shape):
    lane = lax.broadcasted_iota(jnp.int32, shape, 1)
    return jnp.where((lane & HALF) != 0, 1.0, -1.0).astype(F32)


def _kv_variants(t):
    lane = lax.broadcasted_iota(jnp.int32, t.shape, 1)
    lo = lane < HEAD_DIM
    sw = pltpu.roll(t, HEAD_DIM, 1)
    zero = jnp.zeros_like(t)
    return (jnp.where(lo, t, zero).astype(BF16), jnp.where(lo, zero, sw).astype(BF16),
            jnp.where(lo, sw, zero).astype(BF16), jnp.where(lo, zero, t).astype(BF16))


def _kv_variants_t(t):
    tb = t.astype(BF16)
    h0, h1 = tb[0:HEAD_DIM], tb[HEAD_DIM:2 * HEAD_DIM]
    zero = jnp.zeros_like(h0)
    return (jnp.concatenate([h0, zero], axis=0), jnp.concatenate([zero, h0], axis=0),
            jnp.concatenate([h1, zero], axis=0), jnp.concatenate([zero, h1], axis=0))


def _scores(qs, ka, kb, bad, keys_on_lanes=False):
    dot = _mm if keys_on_lanes else _mm_t
    s_e = dot(qs, ka)
    s_o = dot(qs, kb)
    if bad is not None:
        s_e = jnp.where(bad, NEG, s_e)
        s_o = jnp.where(bad, NEG, s_o)
    return s_e, s_o


def _softmax_parts(s, sink):
    m = jnp.maximum(jnp.max(s, axis=-1, keepdims=True), sink)
    e = jnp.exp(s - m)
    r = 1.0 / (jnp.sum(e, axis=-1, keepdims=True) + jnp.exp(sink - m))
    return e.astype(BF16), r


def _pv(p_e, p_o, va, vb, r_e, r_o, keys_on_lanes=False):
    dot = _mm_t if keys_on_lanes else _mm
    o = dot(p_e, va) + dot(p_o, vb)
    lane = lax.broadcasted_iota(jnp.int32, o.shape, 1)
    return o * jnp.where(lane < HEAD_DIM, r_e, r_o)


def _sink_cols(sink_ref, n_blocks, rows_per_pair):
    row = lax.broadcasted_iota(jnp.int32, (n_blocks * 2 * rows_per_pair, 1), 0)
    second_pair = (row // rows_per_pair) % 2 == 1
    second_kv = (row // (2 * rows_per_pair)) % 2 == 1

    def pick(odd):
        kv0 = jnp.where(second_pair, sink_ref[2 + odd], sink_ref[odd])
        kv1 = jnp.where(second_pair, sink_ref[GROUP_HEADS + 2 + odd], sink_ref[GROUP_HEADS + odd])
        return jnp.where(second_kv, kv1, kv0)

    return pick(0), pick(1)


def _stage_chunks(shape):
    rows, cols = shape
    assert rows % STAGE_ROWS == 0 and cols % V7X_LANES == 0
    width = max(c for c in range(V7X_LANES, STAGE_COLS + 1, V7X_LANES) if cols % c == 0)
    return [(r0, c0, STAGE_ROWS, width)
            for r0 in range(0, rows, STAGE_ROWS) for c0 in range(0, cols, width)]


def _stage_weights(pairs, stage, sem):
    chunks = [(src, dst) + ch for src, dst in pairs for ch in _stage_chunks(src.shape)]

    def copy(n):
        src, _, r0, c0, rows, cols = chunks[n]
        slot = n % STAGE_SLOTS
        return pltpu.make_async_copy(src.at[pl.ds(r0, rows), pl.ds(c0, cols)],
                                     stage.at[slot, pl.ds(0, rows), pl.ds(0, cols)],
                                     sem.at[slot])

    for n in range(min(STAGE_SLOTS, len(chunks))):
        copy(n).start()
    for n, (_, dst, r0, c0, rows, cols) in enumerate(chunks):
        copy(n).wait()
        dst[r0:r0 + rows, c0:c0 + cols] = stage[n % STAGE_SLOTS, 0:rows, 0:cols].astype(BF16)
        if n + STAGE_SLOTS < len(chunks):
            copy(n + STAGE_SLOTS).start()


def _qkv_proj(x, lng_ref, win_ref):
    hb = _rmsnorm(x, lng_ref[...]).astype(BF16)
    return hb, _mm(hb, win_ref[:, OFF_Q:OFF_GA])


def _qk_norm_rope(qkv, gq_ref, gk_ref, bdq_ref, bdk_ref, cos_t, sin_s):
    half_q = ATTN_DIM // 2
    q_cols = []
    for j in range(2):
        t = qkv[:, j * half_q:(j + 1) * half_q]
        tn = _head_norm(t, bdq_ref[...], gq_ref[:, j * half_q:(j + 1) * half_q])
        for c in range(half_q // V7X_LANES):
            q_cols.append(_rope(tn[:, c * V7X_LANES:(c + 1) * V7X_LANES], cos_t, sin_s).astype(BF16))
    kn = _head_norm(qkv[:, OFF_K:OFF_V], bdk_ref[...], gk_ref[...])
    kr = _rope(kn, cos_t, sin_s)
    v = qkv[:, OFF_V:OFF_GA]
    return q_cols, kr, v


def _merge_gates(hb, win_ref, d_model):
    mamc = _mm(hb, win_ref[:, OFF_MA:OFF_MA + 2 * d_model])
    return _sigmoid(mamc[:, 0:d_model]), _sigmoid(mamc[:, d_model:2 * d_model])


def _conv_branch(hb, win_ref, wb_ref, conv_fn):
    bcug = _mm(hb, win_ref[:, OFF_B:OFF_GC_END])
    b_gate = bcug[:, 0:CONV_DIM]
    u = bcug[:, CONV_DIM:2 * CONV_DIM] * bcug[:, 2 * CONV_DIM:3 * CONV_DIM]
    gate_c = bcug[:, 3 * CONV_DIM:4 * CONV_DIM]
    conv = conv_fn(u)
    return _mm((b_gate * conv * _silu(gate_c)).astype(BF16), wb_ref[...])


def _finish(x, p, hb, attn, yc, sig_a, sig_c, win_ref, wa_ref, wo_ref, wpg_ref, wpp_ref):
    ga = _mm(hb, win_ref[:, OFF_GA:OFF_B])
    ya = _mm((attn * _silu(ga)).astype(BF16), wa_ref[...])
    mix = sig_a * ya + sig_c * yc
    r = x + _mm(mix.astype(BF16), wo_ref[...])
    gate = _sigmoid(_mm(r.astype(BF16), wpg_ref[...]))
    return r + gate * _mm(p.astype(BF16), wpp_ref[...])


def _prompt_kernel(x_ref, p_ref, lng_ref, win_hbm, gq_ref, gk_ref, invf_ref, sink_ref, cw_ref,
                   wa_hbm, wb_hbm, wo_hbm, wpg_hbm, wpp_hbm, bdq_ref, bdk_ref,
                   y_ref, ko_ref, vo_ref, co_ref,
                   win_out, wa_out, wb_out, wo_out, wpg_out, wpp_out,
                   kbuf, vbuf, ubuf, attn_buf, tab,
                   win_ref, wa_ref, wb_ref, wo_ref, wpg_ref, wpp_ref, stage, sem, out_sem):
    i = pl.program_id(0)
    tm, d_model = x_ref.shape
    invf = invf_ref[...]
    exports = [pltpu.make_async_copy(src, dst, out_sem.at[n]) for n, (src, dst) in enumerate(
        [(win_ref, win_out), (wa_ref, wa_out), (wb_ref, wb_out), (wo_ref, wo_out),
         (wpg_ref, wpg_out), (wpp_ref, wpp_out)])]

    @pl.when(i == 0)
    def _init():
        _stage_weights([(win_hbm, win_ref), (wa_hbm, wa_ref), (wb_hbm, wb_ref), (wo_hbm, wo_ref),
                        (wpg_hbm, wpg_ref), (wpp_hbm, wpp_ref)], stage, sem)
        for e in exports:
            e.start()
        kbuf[:, 0:WINDOW, :] = jnp.zeros((4, WINDOW, V7X_LANES), BF16)
        vbuf[:, 0:WINDOW, :] = jnp.zeros((4, WINDOW, V7X_LANES), BF16)
        ubuf[0:V7X_SUBLANES, :] = jnp.zeros((V7X_SUBLANES, CONV_DIM), F32)
        r = lax.broadcasted_iota(jnp.int32, (tm, V7X_LANES), 0).astype(F32)
        ang = r * invf
        sgn = _rope_sign((tm, V7X_LANES))
        c_r = jnp.cos(ang)
        s_r = jnp.sin(ang)
        tab[0] = c_r
        tab[1] = s_r
        tab[2] = c_r * sgn
        tab[3] = s_r * sgn

    base = (i * tm).astype(F32) * invf
    cb = jnp.cos(base)
    sb = jnp.sin(base)
    cos_t = tab[0] * cb - tab[1] * sb
    sin_s = tab[3] * cb + tab[2] * sb

    sub = PROMPT_SUBTILE
    n_sub = tm // sub
    sub_chunks = sub // CHUNK
    n_keys = (WINDOW_CHUNKS + 1) * CHUNK
    pad = V7X_SUBLANES
    blocks = [(c, vh) for c in range(sub_chunks) for vh in range(N_KV_HEADS)]
    st = [dict(r0=s * sub) for s in range(n_sub)]

    def stage_qkv(s):
        s["x"] = x_ref[s["r0"]:s["r0"] + sub, :]
        s["hb"], s["qkv"] = _qkv_proj(s["x"], lng_ref, win_ref)

    def stage_gates(s):
        s["sig_a"], s["sig_c"] = _merge_gates(s["hb"], win_ref, d_model)

    def stage_qk_norm(s):
        r0 = s["r0"]
        s["q_cols"], kr, v = _qk_norm_rope(s.pop("qkv"), gq_ref, gk_ref, bdq_ref, bdk_ref,
                                           cos_t[r0:r0 + sub], sin_s[r0:r0 + sub])
        for n, t in enumerate(_kv_variants(kr)):
            kbuf[n, WINDOW + r0:WINDOW + r0 + sub, :] = t
        for n, t in enumerate(_kv_variants(v)):
            vbuf[n, WINDOW + r0:WINDOW + r0 + sub, :] = t
        if r0 + sub == tm:
            ko_ref[...] = kr[sub - WINDOW:sub, :].T
            vo_ref[...] = v[sub - WINDOW:sub, :].T

    def stage_scores(s):
        r0 = s["r0"]
        s_e, s_o = [], []
        for c, vh in blocks:
            rows = slice(c * CHUNK, (c + 1) * CHUNK)
            win = slice(r0 + c * CHUNK, r0 + c * CHUNK + n_keys)
            bad = None
            if r0 + c * CHUNK < WINDOW:
                col = lax.broadcasted_iota(jnp.int32, (2 * CHUNK, n_keys), 1)
                bad = (col < WINDOW - (r0 + c * CHUNK)) & (i == 0)
            qs = jnp.concatenate([s["q_cols"][2 * vh][rows], s["q_cols"][2 * vh + 1][rows]], axis=0)
            se, so = _scores(qs, kbuf[2 * vh, win, :], kbuf[2 * vh + 1, win, :], bad)
            s_e.append(se)
            s_o.append(so)
        s["s_e"] = jnp.concatenate(s_e, axis=0)
        s["s_o"] = jnp.concatenate(s_o, axis=0)
        del s["q_cols"]

    def stage_conv(s):
        r0 = s["r0"]

        def conv_fn(u):
            ubuf[pad + r0:pad + r0 + sub, :] = u
            conv = ubuf[pad + r0 - 2:pad + r0 - 2 + sub, :] * cw_ref[0:1, :]
            conv = conv + ubuf[pad + r0 - 1:pad + r0 - 1 + sub, :] * cw_ref[1:2, :]
            return conv + u * cw_ref[2:3, :]

        s["yc"] = _conv_branch(s["hb"], win_ref, wb_ref, conv_fn)

    sink_e, sink_o = _sink_cols(sink_ref, len(blocks), CHUNK)

    def stage_softmax(s):
        s["p_e"], s["r_e"] = _softmax_parts(s.pop("s_e"), sink_e)
        s["p_o"], s["r_o"] = _softmax_parts(s.pop("s_o"), sink_o)

    def stage_pv(s):
        r0 = s["r0"]
        for n, (c, vh) in enumerate(blocks):
            rows = slice(r0 + c * CHUNK, r0 + (c + 1) * CHUNK)
            win = slice(r0 + c * CHUNK, r0 + c * CHUNK + n_keys)
            br = slice(n * 2 * CHUNK, (n + 1) * 2 * CHUNK)
            o = _pv(s["p_e"][br], s["p_o"][br], vbuf[2 * vh, win, :], vbuf[2 * vh + 1, win, :],
                    s["r_e"][br], s["r_o"][br])
            attn_buf[rows, (2 * vh) * V7X_LANES:(2 * vh + 1) * V7X_LANES] = o[0:CHUNK]
            attn_buf[rows, (2 * vh + 1) * V7X_LANES:(2 * vh + 2) * V7X_LANES] = o[CHUNK:2 * CHUNK]

    def stage_attn_gate(s):
        s["silu_ga"] = _silu(_mm(s.pop("hb"), win_ref[:, OFF_GA:OFF_B]))
        s["pp"] = _mm(p_ref[s["r0"]:s["r0"] + sub, :].astype(BF16), wpp_ref[...])

    def stage_attn_out(s):
        r0 = s["r0"]
        a_in = (attn_buf[r0:r0 + sub, :] * s.pop("silu_ga")).astype(BF16)
        ya = _mm(a_in, wa_ref[...])
        s["mix"] = (s.pop("sig_a") * ya + s.pop("sig_c") * s.pop("yc")).astype(BF16)

    def stage_out_proj(s):
        s["r"] = s.pop("x") + _mm(s.pop("mix"), wo_ref[...])

    def stage_ple(s):
        r0 = s["r0"]
        r = s.pop("r")
        gate = _sigmoid(_mm(r.astype(BF16), wpg_ref[...]))
        y_ref[r0:r0 + sub, :] = r + gate * s.pop("pp")

    stages = [stage_qkv, stage_qk_norm, stage_gates, stage_scores, stage_conv, stage_softmax,
              stage_pv, stage_attn_gate, stage_attn_out, stage_out_proj, stage_ple]
    for t in range(len(stages) + PROMPT_STAGE_SKEW * (n_sub - 1)):
        for j, s in enumerate(st):
            k = t - j * PROMPT_STAGE_SKEW
            if 0 <= k < len(stages):
                stages[k](s)

    co_ref[...] = ubuf[pad + tm - 2:pad + tm, :]
    ubuf[0:pad, :] = ubuf[tm:tm + pad, :]
    kbuf[:, 0:WINDOW, :] = kbuf[:, tm:tm + WINDOW, :]
    vbuf[:, 0:WINDOW, :] = vbuf[:, tm:tm + WINDOW, :]

    @pl.when(i == pl.num_programs(0) - 1)
    def _drain():
        for e in exports:
            e.wait()


def _sample_kernel(x_ref, p_ref, ck_ref, cv_ref, sc_ref, lng_ref, win_ref, gq_ref, gk_ref,
                   invf_ref, sink_ref, cw_ref, wa_ref, wb_ref, wo_ref, wpg_ref, wpp_ref,
                   bdq_ref, bdk_ref,
                   y_ref, ko_ref, vo_ref, co_ref,
                   ubuf, attn_buf, conv_buf, tab):
    i = pl.program_id(0)
    bb, _, cache_len = ck_ref.shape
    rows_total, d_model = x_ref.shape
    t_new = rows_total // bb
    n_keys = cache_len + t_new

    @pl.when(i == 0)
    def _init():
        r = lax.broadcasted_iota(jnp.int32, (rows_total, V7X_LANES), 0)
        pos = (PAST_LEN + lax.rem(r, t_new)).astype(F32)
        ang = pos * invf_ref[...]
        tab[0] = jnp.cos(ang)
        tab[1] = jnp.sin(ang) * _rope_sign((rows_total, V7X_LANES))

    def conv_fn(u):
        pad = V7X_SUBLANES
        for b in range(bb):
            rows = slice(b * t_new, (b + 1) * t_new)
            ub = u[rows]
            ubuf[b, pad - (CONV_WIDTH - 1):pad, :] = sc_ref[b]
            ubuf[b, pad:pad + t_new, :] = ub
            conv = ubuf[b, pad - 2:pad - 2 + t_new, :] * cw_ref[0:1, :]
            conv = conv + ubuf[b, pad - 1:pad - 1 + t_new, :] * cw_ref[1:2, :]
            conv_buf[rows, :] = conv + ub * cw_ref[2:3, :]
            co_ref[b] = ubuf[b, pad + t_new - (CONV_WIDTH - 1):pad + t_new, :]
        return conv_buf[...]

    x = x_ref[...]
    hb, qkv = _qkv_proj(x, lng_ref, win_ref)
    sig_a, sig_c = _merge_gates(hb, win_ref, d_model)
    q_cols, kr, v = _qk_norm_rope(qkv, gq_ref, gk_ref, bdq_ref, bdk_ref, tab[0], tab[1])

    qi = lax.broadcasted_iota(jnp.int32, (2 * t_new, n_keys), 0)
    q_pos = PAST_LEN + lax.rem(qi, t_new)
    k_pos = PAST_LEN - cache_len + lax.broadcasted_iota(jnp.int32, (2 * t_new, n_keys), 1)
    q_ch = q_pos // CHUNK
    k_ch = k_pos // CHUNK
    bad = jnp.logical_not((k_ch <= q_ch) & (k_ch >= q_ch - WINDOW_CHUNKS))

    blocks = [(b, vh) for b in range(bb) for vh in range(N_KV_HEADS)]
    s_e, s_o, vvars = [], [], []
    kr_t = kr.T
    v_t = v.T
    for b in range(bb):
        rows = slice(b * t_new, (b + 1) * t_new)
        kcat = jnp.concatenate([ck_ref[b], kr_t[:, rows]], axis=1)
        vcat = jnp.concatenate([cv_ref[b], v_t[:, rows]], axis=1)
        ko_ref[b] = kcat[:, n_keys - cache_len:n_keys]
        vo_ref[b] = vcat[:, n_keys - cache_len:n_keys]
        kvar = _kv_variants_t(kcat)
        vvars.append(_kv_variants_t(vcat))
        for vh in range(N_KV_HEADS):
            qs = jnp.concatenate([q_cols[2 * vh][rows], q_cols[2 * vh + 1][rows]], axis=0)
            se, so = _scores(qs, kvar[2 * vh], kvar[2 * vh + 1], bad, keys_on_lanes=True)
            s_e.append(se)
            s_o.append(so)

    yc = _conv_branch(hb, win_ref, wb_ref, conv_fn)

    sink_e, sink_o = _sink_cols(sink_ref, len(blocks), t_new)
    p_e, r_e = _softmax_parts(jnp.concatenate(s_e, axis=0), sink_e)
    p_o, r_o = _softmax_parts(jnp.concatenate(s_o, axis=0), sink_o)

    for n, (b, vh) in enumerate(blocks):
        rows = slice(b * t_new, (b + 1) * t_new)
        br = slice(n * 2 * t_new, (n + 1) * 2 * t_new)
        o = _pv(p_e[br], p_o[br], vvars[b][2 * vh], vvars[b][2 * vh + 1], r_e[br], r_o[br],
                keys_on_lanes=True)
        attn_buf[rows, (2 * vh) * V7X_LANES:(2 * vh + 1) * V7X_LANES] = o[0:t_new]
        attn_buf[rows, (2 * vh + 1) * V7X_LANES:(2 * vh + 2) * V7X_LANES] = o[t_new:2 * t_new]

    y_ref[...] = _finish(x, p_ref[...], hb, attn_buf[...], yc, sig_a, sig_c, win_ref,
                         wa_ref, wo_ref, wpg_ref, wpp_ref)


def _const_spec(shape):
    nd = len(shape)
    return pl.BlockSpec(shape, lambda i: (0,) * nd, pipeline_mode=pl.Buffered(1))


def _operand_specs(c, keys, staged):
    return [pl.BlockSpec(memory_space=pl.ANY) if staged and k in _STAGED_KEYS
            else _const_spec(c[k].shape) for k in keys]


def _staging_scratch(c):
    return ([pltpu.VMEM(c[k].shape, BF16) for k in _STAGED_KEYS]
            + [pltpu.VMEM((STAGE_SLOTS, STAGE_ROWS, STAGE_COLS), F32),
               pltpu.SemaphoreType.DMA((STAGE_SLOTS,)),
               pltpu.SemaphoreType.DMA((len(_STAGED_KEYS),))])


def _smem_spec():
    return pl.BlockSpec(memory_space=pltpu.SMEM)


def _block_diag_mean(width):
    idx = np.arange(width) // HEAD_DIM
    return jnp.asarray((idx[:, None] == idx[None, :]).astype(np.float32) / HEAD_DIM, dtype=BF16)


def _layer_consts(ln_g, w_in, q_norm_g, k_norm_g, sink, conv_w, w_attn_out, w_conv_out, w_o,
                  w_ple_gate, w_ple_proj):
    scale = HEAD_DIM ** -0.5
    inv_freq = ROPE_THETA ** (-jnp.arange(0, HALF, dtype=F32) * 2.0 / HEAD_DIM)
    return dict(
        lng=ln_g.reshape(1, -1).astype(F32),
        win=w_in.astype(F32),
        gq=(jnp.tile(q_norm_g.astype(F32), N_HEADS) * scale).reshape(1, ATTN_DIM),
        gk=jnp.tile(k_norm_g.astype(F32), N_KV_HEADS).reshape(1, KV_DIM),
        invf=jnp.tile(inv_freq, V7X_LANES // HALF).reshape(1, V7X_LANES),
        sink=sink.astype(F32),
        cw=conv_w.astype(F32),
        wa=w_attn_out.astype(F32),
        wb=w_conv_out.astype(F32),
        wo=w_o.astype(F32),
        wpg=w_ple_gate.astype(F32),
        wpp=w_ple_proj.astype(F32),
        bdq=_block_diag_mean(ATTN_DIM // 2),
        bdk=_block_diag_mean(KV_DIM),
    )


_VEC_KEYS = ("lng", "win", "gq", "gk", "invf")
_MAT_KEYS = ("cw", "wa", "wb", "wo", "wpg", "wpp", "bdq", "bdk")
_STAGED_KEYS = ("win", "wa", "wb", "wo", "wpg", "wpp")


def _heads_first(t):
    t = jnp.moveaxis(t, -3, -1)
    return t.reshape(t.shape[:-3] + (KV_DIM, t.shape[-1]))


def _heads_last(t):
    t = t.reshape(t.shape[:-2] + (N_KV_HEADS, HEAD_DIM, t.shape[-1]))
    return jnp.moveaxis(t, -1, -3)


def _prompt_layer(x, p, c):
    t, d = x.shape
    tm = PROMPT_TILE
    assert t % tm == 0 and tm % PROMPT_SUBTILE == 0
    assert PROMPT_SUBTILE % CHUNK == 0 and PROMPT_SUBTILE >= WINDOW
    pre = [c[k] for k in _VEC_KEYS]
    post = [c[k] for k in _MAT_KEYS]
    in_specs = ([pl.BlockSpec((tm, d), lambda i: (i, 0)),
                 pl.BlockSpec((tm, p.shape[1]), lambda i: (i, 0))]
                + _operand_specs(c, _VEC_KEYS, True) + [_smem_spec()]
                + _operand_specs(c, _MAT_KEYS, True))
    out_shape = ((jax.ShapeDtypeStruct((t, d), F32),
                  jax.ShapeDtypeStruct((KV_DIM, WINDOW), F32),
                  jax.ShapeDtypeStruct((KV_DIM, WINDOW), F32),
                  jax.ShapeDtypeStruct((CONV_WIDTH - 1, CONV_DIM), F32))
                 + tuple(jax.ShapeDtypeStruct(c[k].shape, BF16) for k in _STAGED_KEYS))
    out_specs = ((pl.BlockSpec((tm, d), lambda i: (i, 0)),
                  pl.BlockSpec((KV_DIM, WINDOW), lambda i: (0, 0)),
                  pl.BlockSpec((KV_DIM, WINDOW), lambda i: (0, 0)),
                  pl.BlockSpec((CONV_WIDTH - 1, CONV_DIM), lambda i: (0, 0)))
                 + tuple(pl.BlockSpec(memory_space=pl.ANY) for _ in _STAGED_KEYS))
    scratch = [pltpu.VMEM((4, WINDOW + tm, V7X_LANES), BF16),
               pltpu.VMEM((4, WINDOW + tm, V7X_LANES), BF16),
               pltpu.VMEM((V7X_SUBLANES + tm, CONV_DIM), F32),
               pltpu.VMEM((tm, ATTN_DIM), F32),
               pltpu.VMEM((4, tm, V7X_LANES), F32)] + _staging_scratch(c)
    outs = pl.pallas_call(
        _prompt_kernel,
        grid=(t // tm,),
        in_specs=in_specs,
        out_specs=out_specs,
        out_shape=out_shape,
        scratch_shapes=scratch,
        compiler_params=pltpu.CompilerParams(dimension_semantics=("arbitrary",),
                                             vmem_limit_bytes=V7X_VMEM_LIMIT_BYTES),
        name="prompt_layer",
    )(x, p, *pre, c["sink"], *post)
    y, ko, vo, co = outs[:4]
    return (y, _heads_last(ko), _heads_last(vo), co), dict(c, **dict(zip(_STAGED_KEYS, outs[4:])))


def _sample_layer(x, p, cache_k, cache_v, state_conv, c):
    nb, t_new, d = x.shape
    cache_len = cache_k.shape[1]
    bb = SAMPLE_BATCH_TILE
    assert nb % bb == 0 and t_new >= CONV_WIDTH - 1 and t_new <= cache_len
    rows = bb * t_new
    x2 = x.reshape(nb * t_new, d)
    p2 = p.reshape(nb * t_new, p.shape[-1])
    ck = _heads_first(cache_k)
    cv = _heads_first(cache_v)
    pre = [c[k] for k in _VEC_KEYS]
    post = [c[k] for k in _MAT_KEYS]
    in_specs = ([pl.BlockSpec((rows, d), lambda i: (i, 0)),
                 pl.BlockSpec((rows, p2.shape[1]), lambda i: (i, 0)),
                 pl.BlockSpec((bb, KV_DIM, cache_len), lambda i: (i, 0, 0)),
                 pl.BlockSpec((bb, KV_DIM, cache_len), lambda i: (i, 0, 0)),
                 pl.BlockSpec((bb, CONV_WIDTH - 1, CONV_DIM), lambda i: (i, 0, 0))]
                + _operand_specs(c, _VEC_KEYS, False) + [_smem_spec()]
                + _operand_specs(c, _MAT_KEYS, False))
    out_shape = (jax.ShapeDtypeStruct((nb * t_new, d), F32),
                 jax.ShapeDtypeStruct((nb, KV_DIM, cache_len), F32),
                 jax.ShapeDtypeStruct((nb, KV_DIM, cache_len), F32),
                 jax.ShapeDtypeStruct((nb, CONV_WIDTH - 1, CONV_DIM), F32))
    out_specs = (pl.BlockSpec((rows, d), lambda i: (i, 0)),
                 pl.BlockSpec((bb, KV_DIM, cache_len), lambda i: (i, 0, 0)),
                 pl.BlockSpec((bb, KV_DIM, cache_len), lambda i: (i, 0, 0)),
                 pl.BlockSpec((bb, CONV_WIDTH - 1, CONV_DIM), lambda i: (i, 0, 0)))
    scratch = [pltpu.VMEM((bb, V7X_SUBLANES + t_new, CONV_DIM), F32),
               pltpu.VMEM((rows, ATTN_DIM), F32),
               pltpu.VMEM((rows, CONV_DIM), F32),
               pltpu.VMEM((2, rows, V7X_LANES), F32)]
    y, ko, vo, co = pl.pallas_call(
        _sample_kernel,
        grid=(nb // bb,),
        in_specs=in_specs,
        out_specs=out_specs,
        out_shape=out_shape,
        scratch_shapes=scratch,
        compiler_params=pltpu.CompilerParams(dimension_semantics=("arbitrary",),
                                             vmem_limit_bytes=V7X_VMEM_LIMIT_BYTES),
        name="sample_layer",
    )(x2, p2, ck, cv, state_conv, *pre, c["sink"], *post)
    return y.reshape(nb, t_new, d), _heads_last(ko), _heads_last(vo), co


def kernel(x_prompt, x_sample, p_prompt, p_sample, cache_k, cache_v, state_conv, ln_g, w_in,
           q_norm_g, k_norm_g, sink, conv_w, w_attn_out, w_conv_out, w_o, w_ple_gate, w_ple_proj):
    depth = ln_g.shape[0]
    batch = x_prompt.shape[0]
    hp, hs = x_prompt, x_sample
    kp_l, vp_l, cp_l, ks_l, vs_l, cs_l = [], [], [], [], [], []
    for i in range(depth):
        c = _layer_consts(ln_g[i], w_in[i], q_norm_g[i], k_norm_g[i], sink[i], conv_w[i],
                          w_attn_out[i], w_conv_out[i], w_o[i], w_ple_gate[i], w_ple_proj[i])
        ys, kps, vps, cps = [], [], [], []
        for b in range(batch):
            (y, ko, vo, co), c_bf16 = _prompt_layer(hp[b], p_prompt[i, b], c)
            ys.append(y)
            kps.append(ko)
            vps.append(vo)
            cps.append(co)
        hp = jnp.stack(ys)
        kp_l.append(jnp.stack(kps))
        vp_l.append(jnp.stack(vps))
        cp_l.append(jnp.stack(cps))
        hs, ko, vo, co = _sample_layer(hs, p_sample[i], cache_k[i], cache_v[i], state_conv[i], c_bf16)
        ks_l.append(ko)
        vs_l.append(vo)
        cs_l.append(co)
    return (hp, hs, jnp.stack(kp_l), jnp.stack(vp_l), jnp.stack(cp_l),
            jnp.stack(ks_l), jnp.stack(vs_l), jnp.stack(cs_l))
```

```python
import numpy as np
import jax
import jax.numpy as jnp
from jax import lax
from jax.experimental import pallas as pl
from jax.experimental.pallas import tpu as pltpu

F32 = jnp.float32
BF16 = jnp.bfloat16

CHUNK = 64
WINDOW = 128
WINDOW_CHUNKS = WINDOW // CHUNK
N_HEADS = 8
N_KV_HEADS = 2
GROUP_HEADS = N_HEADS // N_KV_HEADS
HEAD_DIM = 64
HALF = HEAD_DIM // 2
ATTN_DIM = N_HEADS * HEAD_DIM
KV_DIM = N_KV_HEADS * HEAD_DIM
CONV_DIM = 512
CONV_WIDTH = 3
PAST_LEN = 1024
ROPE_THETA = 10000.0
EPS = 1e-6
NEG = -1e30

OFF_Q = 0
OFF_K = OFF_Q + ATTN_DIM
OFF_V = OFF_K + KV_DIM
OFF_GA = OFF_V + KV_DIM
OFF_B = OFF_GA + ATTN_DIM
OFF_GC_END = OFF_B + 4 * CONV_DIM
OFF_MA = OFF_GC_END

V7X_LANES = 128
V7X_SUBLANES = 8
V7X_VMEM_LIMIT_BYTES = 56 * 1024 * 1024

PROMPT_TILE = 512
PROMPT_SUBTILE = 256
PROMPT_STAGE_SKEW = 1
STAGE_ROWS = 256
STAGE_COLS = 1024
STAGE_SLOTS = 4
SAMPLE_BATCH_TILE = 8


def _mm(a, w):
    return jnp.dot(a, w, preferred_element_type=F32)


def _mm_t(a, b):
    return lax.dot_general(a, b, (((1,), (1,)), ((), ())), preferred_element_type=F32)


def _sigmoid(x):
    return 1.0 / (1.0 + jnp.exp(-x))


def _silu(x):
    return x * _sigmoid(x)


def _rmsnorm(x, g):
    ms = jnp.mean(x * x, axis=-1, keepdims=True)
    return x * lax.rsqrt(ms + EPS) * g


def _group_mean(t, bd):
    return _mm(t.astype(BF16), bd)


def _head_norm(t, bd, g):
    ms = _group_mean(t * t, bd)
    return t * lax.rsqrt(ms + EPS) * g


def _rope(xc, cos_t, sin_s):
    lane = lax.broadcasted_iota(jnp.int32, xc.shape, 1)
    upper = (lane & HALF) != 0
    rot = jnp.where(upper, pltpu.roll(xc, HALF, 1), pltpu.roll(xc, V7X_LANES - HALF, 1))
    return xc * cos_t + rot * sin_s


def _rope_sign(shape):
    lane = lax.broadcasted_iota(jnp.int32, shape, 1)
    return jnp.where((lane & HALF) != 0, 1.0, -1.0).astype(F32)


def _kv_variants(t):
    lane = lax.broadcasted_iota(jnp.int32, t.shape, 1)
    lo = lane < HEAD_DIM
    sw = pltpu.roll(t, HEAD_DIM, 1)
    zero = jnp.zeros_like(t)
    return (jnp.where(lo, t, zero).astype(BF16), jnp.where(lo, zero, sw).astype(BF16),
            jnp.where(lo, sw, zero).astype(BF16), jnp.where(lo, zero, t).astype(BF16))


def _kv_variants_t(t):
    tb = t.astype(BF16)
    h0, h1 = tb[0:HEAD_DIM], tb[HEAD_DIM:2 * HEAD_DIM]
    zero = jnp.zeros_like(h0)
    return (jnp.concatenate([h0, zero], axis=0), jnp.concatenate([zero, h0], axis=0),
            jnp.concatenate([h1, zero], axis=0), jnp.concatenate([zero, h1], axis=0))


def _scores(qs, ka, kb, bad, keys_on_lanes=False):
    dot = _mm if keys_on_lanes else _mm_t
    s_e = dot(qs, ka)
    s_o = dot(qs, kb)
    if bad is not None:
        s_e = jnp.where(bad, NEG, s_e)
        s_o = jnp.where(bad, NEG, s_o)
    return s_e, s_o


def _softmax_parts(s, sink):
    m = jnp.maximum(jnp.max(s, axis=-1, keepdims=True), sink)
    e = jnp.exp(s - m)
    r = 1.0 / (jnp.sum(e, axis=-1, keepdims=True) + jnp.exp(sink - m))
    return e.astype(BF16), r


def _pv(p_e, p_o, va, vb, r_e, r_o, keys_on_lanes=False):
    dot = _mm_t if keys_on_lanes else _mm
    o = dot(p_e, va) + dot(p_o, vb)
    lane = lax.broadcasted_iota(jnp.int32, o.shape, 1)
    return o * jnp.where(lane < HEAD_DIM, r_e, r_o)


def _sink_cols(sink_ref, n_blocks, rows_per_pair):
    row = lax.broadcasted_iota(jnp.int32, (n_blocks * 2 * rows_per_pair, 1), 0)
    second_pair = (row // rows_per_pair) % 2 == 1
    second_kv = (row // (2 * rows_per_pair)) % 2 == 1

    def pick(odd):
        kv0 = jnp.where(second_pair, sink_ref[2 + odd], sink_ref[odd])
        kv1 = jnp.where(second_pair, sink_ref[GROUP_HEADS + 2 + odd], sink_ref[GROUP_HEADS + odd])
        return jnp.where(second_kv, kv1, kv0)

    return pick(0), pick(1)


def _stage_chunks(shape):
    rows, cols = shape
    assert rows % STAGE_ROWS == 0 and cols % V7X_LANES == 0
    width = max(c for c in range(V7X_LANES, STAGE_COLS + 1, V7X_LANES) if cols % c == 0)
    return [(r0, c0, STAGE_ROWS, width)
            for r0 in range(0, rows, STAGE_ROWS) for c0 in range(0, cols, width)]


def _stage_weights(pairs, stage, sem):
    chunks = [(src, dst) + ch for src, dst in pairs for ch in _stage_chunks(src.shape)]

    def copy(n):
        src, _, r0, c0, rows, cols = chunks[n]
        slot = n % STAGE_SLOTS
        return pltpu.make_async_copy(src.at[pl.ds(r0, rows), pl.ds(c0, cols)],
                                     stage.at[slot, pl.ds(0, rows), pl.ds(0, cols)],
                                     sem.at[slot])

    for n in range(min(STAGE_SLOTS, len(chunks))):
        copy(n).start()
    for n, (_, dst, r0, c0, rows, cols) in enumerate(chunks):
        copy(n).wait()
        dst[r0:r0 + rows, c0:c0 + cols] = stage[n % STAGE_SLOTS, 0:rows, 0:cols].astype(BF16)
        if n + STAGE_SLOTS < len(chunks):
            copy(n + STAGE_SLOTS).start()


def _qkv_proj(x, lng_ref, win_ref):
    hb = _rmsnorm(x, lng_ref[...]).astype(BF16)
    return hb, _mm(hb, win_ref[:, OFF_Q:OFF_GA])


def _qk_norm_rope(qkv, gq_ref, gk_ref, bdq_ref, bdk_ref, cos_t, sin_s):
    half_q = ATTN_DIM // 2
    q_cols = []
    for j in range(2):
        t = qkv[:, j * half_q:(j + 1) * half_q]
        tn = _head_norm(t, bdq_ref[...], gq_ref[:, j * half_q:(j + 1) * half_q])
        for c in range(half_q // V7X_LANES):
            q_cols.append(_rope(tn[:, c * V7X_LANES:(c + 1) * V7X_LANES], cos_t, sin_s).astype(BF16))
    kn = _head_norm(qkv[:, OFF_K:OFF_V], bdk_ref[...], gk_ref[...])
    kr = _rope(kn, cos_t, sin_s)
    v = qkv[:, OFF_V:OFF_GA]
    return q_cols, kr, v


def _merge_gates(hb, win_ref, d_model):
    mamc = _mm(hb, win_ref[:, OFF_MA:OFF_MA + 2 * d_model])
    return _sigmoid(mamc[:, 0:d_model]), _sigmoid(mamc[:, d_model:2 * d_model])


def _conv_branch(hb, win_ref, wb_ref, conv_fn):
    bcug = _mm(hb, win_ref[:, OFF_B:OFF_GC_END])
    b_gate = bcug[:, 0:CONV_DIM]
    u = bcug[:, CONV_DIM:2 * CONV_DIM] * bcug[:, 2 * CONV_DIM:3 * CONV_DIM]
    gate_c = bcug[:, 3 * CONV_DIM:4 * CONV_DIM]
    conv = conv_fn(u)
    return _mm((b_gate * conv * _silu(gate_c)).astype(BF16), wb_ref[...])


def _finish(x, p, hb, attn, yc, sig_a, sig_c, win_ref, wa_ref, wo_ref, wpg_ref, wpp_ref):
    ga = _mm(hb, win_ref[:, OFF_GA:OFF_B])
    ya = _mm((attn * _silu(ga)).astype(BF16), wa_ref[...])
    mix = sig_a * ya + sig_c * yc
    r = x + _mm(mix.astype(BF16), wo_ref[...])
    gate = _sigmoid(_mm(r.astype(BF16), wpg_ref[...]))
    return r + gate * _mm(p.astype(BF16), wpp_ref[...])


def _prompt_kernel(x_ref, p_ref, lng_ref, win_hbm, gq_ref, gk_ref, invf_ref, sink_ref, cw_ref,
                   wa_hbm, wb_hbm, wo_hbm, wpg_hbm, wpp_hbm, bdq_ref, bdk_ref,
                   y_ref, ko_ref, vo_ref, co_ref,
                   win_out, wa_out, wb_out, wo_out, wpg_out, wpp_out,
                   kbuf, vbuf, ubuf, attn_buf, tab,
                   win_ref, wa_ref, wb_ref, wo_ref, wpg_ref, wpp_ref, stage, sem, out_sem):
    i = pl.program_id(0)
    tm, d_model = x_ref.shape
    invf = invf_ref[...]
    exports = [pltpu.make_async_copy(src, dst, out_sem.at[n]) for n, (src, dst) in enumerate(
        [(win_ref, win_out), (wa_ref, wa_out), (wb_ref, wb_out), (wo_ref, wo_out),
         (wpg_ref, wpg_out), (wpp_ref, wpp_out)])]

    @pl.when(i == 0)
    def _init():
        _stage_weights([(win_hbm, win_ref), (wa_hbm, wa_ref), (wb_hbm, wb_ref), (wo_hbm, wo_ref),
                        (wpg_hbm, wpg_ref), (wpp_hbm, wpp_ref)], stage, sem)
        for e in exports:
            e.start()
        kbuf[:, :, 0:WINDOW] = jnp.zeros((4, KV_DIM, WINDOW), BF16)
        vbuf[:, 0:WINDOW, :] = jnp.zeros((4, WINDOW, V7X_LANES), BF16)
        ubuf[0:V7X_SUBLANES, :] = jnp.zeros((V7X_SUBLANES, CONV_DIM), F32)
        r = lax.broadcasted_iota(jnp.int32, (tm, V7X_LANES), 0).astype(F32)
        ang = r * invf
        sgn = _rope_sign((tm, V7X_LANES))
        c_r = jnp.cos(ang)
        s_r = jnp.sin(ang)
        tab[0] = c_r
        tab[1] = s_r
        tab[2] = c_r * sgn
        tab[3] = s_r * sgn

    base = (i * tm).astype(F32) * invf
    cb = jnp.cos(base)
    sb = jnp.sin(base)
    cos_t = tab[0] * cb - tab[1] * sb
    sin_s = tab[3] * cb + tab[2] * sb

    sub = PROMPT_SUBTILE
    n_sub = tm // sub
    sub_chunks = sub // CHUNK
    assert 2 * CHUNK == V7X_LANES and WINDOW == V7X_LANES
    n_keys = 2 * V7X_LANES

    def key_window_start(r0, c):
        return ((r0 + c * CHUNK) // V7X_LANES) * V7X_LANES

    def mask_window(sc, r0, c):
        lo, hi = sc[:, 0:V7X_LANES], sc[:, V7X_LANES:n_keys]
        lane = lax.broadcasted_iota(jnp.int32, lo.shape, 1)
        no_carry = (i == 0) if key_window_start(r0, c) < WINDOW else None
        if ((r0 + c * CHUNK) // CHUNK) % 2 == 0:
            hi = jnp.where(lane >= CHUNK, NEG, hi)
            if no_carry is not None:
                lo = jnp.where(no_carry, NEG, lo)
        else:
            bad = lane < CHUNK
            lo = jnp.where(bad if no_carry is None else bad | no_carry, NEG, lo)
        return jnp.concatenate([lo, hi], axis=1)
    pad = V7X_SUBLANES
    blocks = [(c, vh) for c in range(sub_chunks) for vh in range(N_KV_HEADS)]
    st = [dict(r0=s * sub) for s in range(n_sub)]

    def stage_qkv(s):
        s["x"] = x_ref[s["r0"]:s["r0"] + sub, :]
        s["hb"], s["qkv"] = _qkv_proj(s["x"], lng_ref, win_ref)

    def stage_gates(s):
        s["sig_a"], s["sig_c"] = _merge_gates(s["hb"], win_ref, d_model)

    def stage_qk_norm(s):
        r0 = s["r0"]
        s["q_cols"], kr, v = _qk_norm_rope(s.pop("qkv"), gq_ref, gk_ref, bdq_ref, bdk_ref,
                                           cos_t[r0:r0 + sub], sin_s[r0:r0 + sub])
        kr_t = kr.T
        for n, t in enumerate(_kv_variants_t(kr_t)):
            kbuf[n, :, WINDOW + r0:WINDOW + r0 + sub] = t
        for n, t in enumerate(_kv_variants(v)):
            vbuf[n, WINDOW + r0:WINDOW + r0 + sub, :] = t
        if r0 + sub == tm:
            ko_ref[...] = kr_t[:, sub - WINDOW:sub]
            vo_ref[...] = v[sub - WINDOW:sub, :].T

    def stage_scores(s):
        r0 = s["r0"]
        s_e, s_o = [], []
        for c, vh in blocks:
            rows = slice(c * CHUNK, (c + 1) * CHUNK)
            start = key_window_start(r0, c)
            win = slice(start, start + n_keys)
            qs = jnp.concatenate([s["q_cols"][2 * vh][rows], s["q_cols"][2 * vh + 1][rows]], axis=0)
            se, so = _scores(qs, kbuf[2 * vh, :, win], kbuf[2 * vh + 1, :, win], None,
                             keys_on_lanes=True)
            s_e.append(mask_window(se, r0, c))
            s_o.append(mask_window(so, r0, c))
        s["s_e"] = jnp.concatenate(s_e, axis=0)
        s["s_o"] = jnp.concatenate(s_o, axis=0)
        del s["q_cols"]

    def stage_conv(s):
        r0 = s["r0"]

        def conv_fn(u):
            ubuf[pad + r0:pad + r0 + sub, :] = u
            conv = ubuf[pad + r0 - 2:pad + r0 - 2 + sub, :] * cw_ref[0:1, :]
            conv = conv + ubuf[pad + r0 - 1:pad + r0 - 1 + sub, :] * cw_ref[1:2, :]
            return conv + u * cw_ref[2:3, :]

        s["yc"] = _conv_branch(s["hb"], win_ref, wb_ref, conv_fn)

    sink_e, sink_o = _sink_cols(sink_ref, len(blocks), CHUNK)

    def stage_softmax(s):
        s["p_e"], s["r_e"] = _softmax_parts(s.pop("s_e"), sink_e)
        s["p_o"], s["r_o"] = _softmax_parts(s.pop("s_o"), sink_o)

    def stage_pv(s):
        r0 = s["r0"]
        for n, (c, vh) in enumerate(blocks):
            rows = slice(r0 + c * CHUNK, r0 + (c + 1) * CHUNK)
            start = key_window_start(r0, c)
            win = slice(start, start + n_keys)
            br = slice(n * 2 * CHUNK, (n + 1) * 2 * CHUNK)
            o = _pv(s["p_e"][br], s["p_o"][br], vbuf[2 * vh, win, :], vbuf[2 * vh + 1, win, :],
                    s["r_e"][br], s["r_o"][br])
            attn_buf[rows, (2 * vh) * V7X_LANES:(2 * vh + 1) * V7X_LANES] = o[0:CHUNK]
            attn_buf[rows, (2 * vh + 1) * V7X_LANES:(2 * vh + 2) * V7X_LANES] = o[CHUNK:2 * CHUNK]

    def stage_attn_gate(s):
        s["silu_ga"] = _silu(_mm(s.pop("hb"), win_ref[:, OFF_GA:OFF_B]))
        s["pp"] = _mm(p_ref[s["r0"]:s["r0"] + sub, :].astype(BF16), wpp_ref[...])

    def stage_attn_out(s):
        r0 = s["r0"]
        a_in = (attn_buf[r0:r0 + sub, :] * s.pop("silu_ga")).astype(BF16)
        ya = _mm(a_in, wa_ref[...])
        s["mix"] = (s.pop("sig_a") * ya + s.pop("sig_c") * s.pop("yc")).astype(BF16)

    def stage_out_proj(s):
        s["r"] = s.pop("x") + _mm(s.pop("mix"), wo_ref[...])

    def stage_ple(s):
        r0 = s["r0"]
        r = s.pop("r")
        gate = _sigmoid(_mm(r.astype(BF16), wpg_ref[...]))
        y_ref[r0:r0 + sub, :] = r + gate * s.pop("pp")

    stages = [stage_qkv, stage_qk_norm, stage_gates, stage_scores, stage_conv, stage_softmax,
              stage_pv, stage_attn_gate, stage_attn_out, stage_out_proj, stage_ple]
    for t in range(len(stages) + PROMPT_STAGE_SKEW * (n_sub - 1)):
        for j, s in enumerate(st):
            k = t - j * PROMPT_STAGE_SKEW
            if 0 <= k < len(stages):
                stages[k](s)

    co_ref[...] = ubuf[pad + tm - 2:pad + tm, :]
    ubuf[0:pad, :] = ubuf[tm:tm + pad, :]
    kbuf[:, :, 0:WINDOW] = kbuf[:, :, tm:tm + WINDOW]
    vbuf[:, 0:WINDOW, :] = vbuf[:, tm:tm + WINDOW, :]

    @pl.when(i == pl.num_programs(0) - 1)
    def _drain():
        for e in exports:
            e.wait()


def _sample_kernel(x_ref, p_ref, ck_ref, cv_ref, sc_ref, lng_ref, win_ref, gq_ref, gk_ref,
                   invf_ref, sink_ref, cw_ref, wa_ref, wb_ref, wo_ref, wpg_ref, wpp_ref,
                   bdq_ref, bdk_ref,
                   y_ref, ko_ref, vo_ref, co_ref,
                   ubuf, attn_buf, conv_buf, tab):
    i = pl.program_id(0)
    bb, _, cache_len = ck_ref.shape
    rows_total, d_model = x_ref.shape
    t_new = rows_total // bb
    n_keys = cache_len + t_new

    @pl.when(i == 0)
    def _init():
        r = lax.broadcasted_iota(jnp.int32, (rows_total, V7X_LANES), 0)
        pos = (PAST_LEN + lax.rem(r, t_new)).astype(F32)
        ang = pos * invf_ref[...]
        tab[0] = jnp.cos(ang)
        tab[1] = jnp.sin(ang) * _rope_sign((rows_total, V7X_LANES))

    def conv_fn(u):
        pad = V7X_SUBLANES
        for b in range(bb):
            rows = slice(b * t_new, (b + 1) * t_new)
            ub = u[rows]
            ubuf[b, pad - (CONV_WIDTH - 1):pad, :] = sc_ref[b]
            ubuf[b, pad:pad + t_new, :] = ub
            conv = ubuf[b, pad - 2:pad - 2 + t_new, :] * cw_ref[0:1, :]
            conv = conv + ubuf[b, pad - 1:pad - 1 + t_new, :] * cw_ref[1:2, :]
            conv_buf[rows, :] = conv + ub * cw_ref[2:3, :]
            co_ref[b] = ubuf[b, pad + t_new - (CONV_WIDTH - 1):pad + t_new, :]
        return conv_buf[...]

    x = x_ref[...]
    hb, qkv = _qkv_proj(x, lng_ref, win_ref)
    sig_a, sig_c = _merge_gates(hb, win_ref, d_model)
    q_cols, kr, v = _qk_norm_rope(qkv, gq_ref, gk_ref, bdq_ref, bdk_ref, tab[0], tab[1])

    qi = lax.broadcasted_iota(jnp.int32, (2 * t_new, n_keys), 0)
    q_pos = PAST_LEN + lax.rem(qi, t_new)
    k_pos = PAST_LEN - cache_len + lax.broadcasted_iota(jnp.int32, (2 * t_new, n_keys), 1)
    q_ch = q_pos // CHUNK
    k_ch = k_pos // CHUNK
    bad = jnp.logical_not((k_ch <= q_ch) & (k_ch >= q_ch - WINDOW_CHUNKS))

    blocks = [(b, vh) for b in range(bb) for vh in range(N_KV_HEADS)]
    s_e, s_o, vvars = [], [], []
    kr_t = kr.T
    v_t = v.T
    for b in range(bb):
        rows = slice(b * t_new, (b + 1) * t_new)
        kcat = jnp.concatenate([ck_ref[b], kr_t[:, rows]], axis=1)
        vcat = jnp.concatenate([cv_ref[b], v_t[:, rows]], axis=1)
        ko_ref[b] = kcat[:, n_keys - cache_len:n_keys]
        vo_ref[b] = vcat[:, n_keys - cache_len:n_keys]
        kvar = _kv_variants_t(kcat)
        vvars.append(_kv_variants_t(vcat))
        for vh in range(N_KV_HEADS):
            qs = jnp.concatenate([q_cols[2 * vh][rows], q_cols[2 * vh + 1][rows]], axis=0)
            se, so = _scores(qs, kvar[2 * vh], kvar[2 * vh + 1], bad, keys_on_lanes=True)
            s_e.append(se)
            s_o.append(so)

    yc = _conv_branch(hb, win_ref, wb_ref, conv_fn)

    sink_e, sink_o = _sink_cols(sink_ref, len(blocks), t_new)
    p_e, r_e = _softmax_parts(jnp.concatenate(s_e, axis=0), sink_e)
    p_o, r_o = _softmax_parts(jnp.concatenate(s_o, axis=0), sink_o)

    for n, (b, vh) in enumerate(blocks):
        rows = slice(b * t_new, (b + 1) * t_new)
        br = slice(n * 2 * t_new, (n + 1) * 2 * t_new)
        o = _pv(p_e[br], p_o[br], vvars[b][2 * vh], vvars[b][2 * vh + 1], r_e[br], r_o[br],
                keys_on_lanes=True)
        attn_buf[rows, (2 * vh) * V7X_LANES:(2 * vh + 1) * V7X_LANES] = o[0:t_new]
        attn_buf[rows, (2 * vh + 1) * V7X_LANES:(2 * vh + 2) * V7X_LANES] = o[t_new:2 * t_new]

    y_ref[...] = _finish(x, p_ref[...], hb, attn_buf[...], yc, sig_a, sig_c, win_ref,
                         wa_ref, wo_ref, wpg_ref, wpp_ref)


def _const_spec(shape):
    nd = len(shape)
    return pl.BlockSpec(shape, lambda i: (0,) * nd, pipeline_mode=pl.Buffered(1))


def _operand_specs(c, keys, staged):
    return [pl.BlockSpec(memory_space=pl.ANY) if staged and k in _STAGED_KEYS
            else _const_spec(c[k].shape) for k in keys]


def _staging_scratch(c):
    return ([pltpu.VMEM(c[k].shape, BF16) for k in _STAGED_KEYS]
            + [pltpu.VMEM((STAGE_SLOTS, STAGE_ROWS, STAGE_COLS), F32),
               pltpu.SemaphoreType.DMA((STAGE_SLOTS,)),
               pltpu.SemaphoreType.DMA((len(_STAGED_KEYS),))])


def _smem_spec():
    return pl.BlockSpec(memory_space=pltpu.SMEM)


def _block_diag_mean(width):
    idx = np.arange(width) // HEAD_DIM
    return jnp.asarray((idx[:, None] == idx[None, :]).astype(np.float32) / HEAD_DIM, dtype=BF16)


def _layer_consts(ln_g, w_in, q_norm_g, k_norm_g, sink, conv_w, w_attn_out, w_conv_out, w_o,
                  w_ple_gate, w_ple_proj):
    scale = HEAD_DIM ** -0.5
    inv_freq = ROPE_THETA ** (-jnp.arange(0, HALF, dtype=F32) * 2.0 / HEAD_DIM)
    return dict(
        lng=ln_g.reshape(1, -1).astype(F32),
        win=w_in.astype(F32),
        gq=(jnp.tile(q_norm_g.astype(F32), N_HEADS) * scale).reshape(1, ATTN_DIM),
        gk=jnp.tile(k_norm_g.astype(F32), N_KV_HEADS).reshape(1, KV_DIM),
        invf=jnp.tile(inv_freq, V7X_LANES // HALF).reshape(1, V7X_LANES),
        sink=sink.astype(F32),
        cw=conv_w.astype(F32),
        wa=w_attn_out.astype(F32),
        wb=w_conv_out.astype(F32),
        wo=w_o.astype(F32),
        wpg=w_ple_gate.astype(F32),
        wpp=w_ple_proj.astype(F32),
        bdq=_block_diag_mean(ATTN_DIM // 2),
        bdk=_block_diag_mean(KV_DIM),
    )


_VEC_KEYS = ("lng", "win", "gq", "gk", "invf")
_MAT_KEYS = ("cw", "wa", "wb", "wo", "wpg", "wpp", "bdq", "bdk")
_STAGED_KEYS = ("win", "wa", "wb", "wo", "wpg", "wpp")


def _heads_first(t):
    t = jnp.moveaxis(t, -3, -1)
    return t.reshape(t.shape[:-3] + (KV_DIM, t.shape[-1]))


def _heads_last(t):
    t = t.reshape(t.shape[:-2] + (N_KV_HEADS, HEAD_DIM, t.shape[-1]))
    return jnp.moveaxis(t, -1, -3)


def _prompt_layer(x, p, c):
    t, d = x.shape
    tm = PROMPT_TILE
    assert t % tm == 0 and tm % PROMPT_SUBTILE == 0
    assert PROMPT_SUBTILE % CHUNK == 0 and PROMPT_SUBTILE >= WINDOW
    pre = [c[k] for k in _VEC_KEYS]
    post = [c[k] for k in _MAT_KEYS]
    in_specs = ([pl.BlockSpec((tm, d), lambda i: (i, 0)),
                 pl.BlockSpec((tm, p.shape[1]), lambda i: (i, 0))]
                + _operand_specs(c, _VEC_KEYS, True) + [_smem_spec()]
                + _operand_specs(c, _MAT_KEYS, True))
    out_shape = ((jax.ShapeDtypeStruct((t, d), F32),
                  jax.ShapeDtypeStruct((KV_DIM, WINDOW), F32),
                  jax.ShapeDtypeStruct((KV_DIM, WINDOW), F32),
                  jax.ShapeDtypeStruct((CONV_WIDTH - 1, CONV_DIM), F32))
                 + tuple(jax.ShapeDtypeStruct(c[k].shape, BF16) for k in _STAGED_KEYS))
    out_specs = ((pl.BlockSpec((tm, d), lambda i: (i, 0)),
                  pl.BlockSpec((KV_DIM, WINDOW), lambda i: (0, 0)),
                  pl.BlockSpec((KV_DIM, WINDOW), lambda i: (0, 0)),
                  pl.BlockSpec((CONV_WIDTH - 1, CONV_DIM), lambda i: (0, 0)))
                 + tuple(pl.BlockSpec(memory_space=pl.ANY) for _ in _STAGED_KEYS))
    scratch = [pltpu.VMEM((4, KV_DIM, WINDOW + tm), BF16),
               pltpu.VMEM((4, WINDOW + tm, V7X_LANES), BF16),
               pltpu.VMEM((V7X_SUBLANES + tm, CONV_DIM), F32),
               pltpu.VMEM((tm, ATTN_DIM), F32),
               pltpu.VMEM((4, tm, V7X_LANES), F32)] + _staging_scratch(c)
    outs = pl.pallas_call(
        _prompt_kernel,
        grid=(t // tm,),
        in_specs=in_specs,
        out_specs=out_specs,
        out_shape=out_shape,
        scratch_shapes=scratch,
        compiler_params=pltpu.CompilerParams(dimension_semantics=("arbitrary",),
                                             vmem_limit_bytes=V7X_VMEM_LIMIT_BYTES),
        name="prompt_layer",
    )(x, p, *pre, c["sink"], *post)
    y, ko, vo, co = outs[:4]
    return (y, _heads_last(ko), _heads_last(vo), co), dict(c, **dict(zip(_STAGED_KEYS, outs[4:])))


def _sample_layer(x, p, cache_k, cache_v, state_conv, c):
    nb, t_new, d = x.shape
    cache_len = cache_k.shape[1]
    bb = SAMPLE_BATCH_TILE
    assert nb % bb == 0 and t_new >= CONV_WIDTH - 1 and t_new <= cache_len
    rows = bb * t_new
    x2 = x.reshape(nb * t_new, d)
    p2 = p.reshape(nb * t_new, p.shape[-1])
    ck = _heads_first(cache_k)
    cv = _heads_first(cache_v)
    pre = [c[k] for k in _VEC_KEYS]
    post = [c[k] for k in _MAT_KEYS]
    in_specs = ([pl.BlockSpec((rows, d), lambda i: (i, 0)),
                 pl.BlockSpec((rows, p2.shape[1]), lambda i: (i, 0)),
                 pl.BlockSpec((bb, KV_DIM, cache_len), lambda i: (i, 0, 0)),
                 pl.BlockSpec((bb, KV_DIM, cache_len), lambda i: (i, 0, 0)),
                 pl.BlockSpec((bb, CONV_WIDTH - 1, CONV_DIM), lambda i: (i, 0, 0))]
                + _operand_specs(c, _VEC_KEYS, False) + [_smem_spec()]
                + _operand_specs(c, _MAT_KEYS, False))
    out_shape = (jax.ShapeDtypeStruct((nb * t_new, d), F32),
                 jax.ShapeDtypeStruct((nb, KV_DIM, cache_len), F32),
                 jax.ShapeDtypeStruct((nb, KV_DIM, cache_len), F32),
                 jax.ShapeDtypeStruct((nb, CONV_WIDTH - 1, CONV_DIM), F32))
    out_specs = (pl.BlockSpec((rows, d), lambda i: (i, 0)),
                 pl.BlockSpec((bb, KV_DIM, cache_len), lambda i: (i, 0, 0)),
                 pl.BlockSpec((bb, KV_DIM, cache_len), lambda i: (i, 0, 0)),
                 pl.BlockSpec((bb, CONV_WIDTH - 1, CONV_DIM), lambda i: (i, 0, 0)))
    scratch = [pltpu.VMEM((bb, V7X_SUBLANES + t_new, CONV_DIM), F32),
               pltpu.VMEM((rows, ATTN_DIM), F32),
               pltpu.VMEM((rows, CONV_DIM), F32),
               pltpu.VMEM((2, rows, V7X_LANES), F32)]
    y, ko, vo, co = pl.pallas_call(
        _sample_kernel,
        grid=(nb // bb,),
        in_specs=in_specs,
        out_specs=out_specs,
        out_shape=out_shape,
        scratch_shapes=scratch,
        compiler_params=pltpu.CompilerParams(dimension_semantics=("arbitrary",),
                                             vmem_limit_bytes=V7X_VMEM_LIMIT_BYTES),
        name="sample_layer",
    )(x2, p2, ck, cv, state_conv, *pre, c["sink"], *post)
    return y.reshape(nb, t_new, d), _heads_last(ko), _heads_last(vo), co


def kernel(x_prompt, x_sample, p_prompt, p_sample, cache_k, cache_v, state_conv, ln_g, w_in,
           q_norm_g, k_norm_g, sink, conv_w, w_attn_out, w_conv_out, w_o, w_ple_gate, w_ple_proj):
    depth = ln_g.shape[0]
    batch = x_prompt.shape[0]
    hp, hs = x_prompt, x_sample
    kp_l, vp_l, cp_l, ks_l, vs_l, cs_l = [], [], [], [], [], []
    for i in range(depth):
        c = _layer_consts(ln_g[i], w_in[i], q_norm_g[i], k_norm_g[i], sink[i], conv_w[i],
                          w_attn_out[i], w_conv_out[i], w_o[i], w_ple_gate[i], w_ple_proj[i])
        ys, kps, vps, cps = [], [], [], []
        for b in range(batch):
            (y, ko, vo, co), c_bf16 = _prompt_layer(hp[b], p_prompt[i, b], c)
            ys.append(y)
            kps.append(ko)
            vps.append(vo)
            cps.append(co)
        hp = jnp.stack(ys)
        kp_l.append(jnp.stack(kps))
        vp_l.append(jnp.stack(vps))
        cp_l.append(jnp.stack(cps))
        hs, ko, vo, co = _sample_layer(hs, p_sample[i], cache_k[i], cache_v[i], state_conv[i], c_bf16)
        ks_l.append(ko)
        vs_l.append(vo)
        cs_l.append(co)
    return (hp, hs, jnp.stack(kp_l), jnp.stack(vp_l), jnp.stack(cp_l),
            jnp.stack(ks_l), jnp.stack(vs_l), jnp.stack(cs_l))
```

```python
import numpy as np
import jax
import jax.numpy as jnp
from jax import lax
from jax.experimental import pallas as pl
from jax.experimental.pallas import tpu as pltpu

F32 = jnp.float32
BF16 = jnp.bfloat16

CHUNK = 64
WINDOW = 128
WINDOW_CHUNKS = WINDOW // CHUNK
N_HEADS = 8
N_KV_HEADS = 2
GROUP_HEADS = N_HEADS // N_KV_HEADS
HEAD_DIM = 64
HALF = HEAD_DIM // 2
ATTN_DIM = N_HEADS * HEAD_DIM
KV_DIM = N_KV_HEADS * HEAD_DIM
CONV_DIM = 512
CONV_WIDTH = 3
PAST_LEN = 1024
ROPE_THETA = 10000.0
EPS = 1e-6
NEG = -1e30

OFF_Q = 0
OFF_K = OFF_Q + ATTN_DIM
OFF_V = OFF_K + KV_DIM
OFF_GA = OFF_V + KV_DIM
OFF_B = OFF_GA + ATTN_DIM
OFF_GC_END = OFF_B + 4 * CONV_DIM
OFF_MA = OFF_GC_END

V7X_LANES = 128
V7X_SUBLANES = 8
V7X_VMEM_LIMIT_BYTES = 56 * 1024 * 1024

PROMPT_TILE = 512
PROMPT_SUBTILE = 256
PROMPT_STAGE_SKEW = 1
STAGE_ROWS = 256
STAGE_COLS = 1024
STAGE_SLOTS = 4
SAMPLE_BATCH_TILE = 8


def _mm(a, w):
    return jnp.dot(a, w, preferred_element_type=F32)


def _mm_t(a, b):
    return lax.dot_general(a, b, (((1,), (1,)), ((), ())), preferred_element_type=F32)


def _sigmoid(x):
    return 1.0 / (1.0 + jnp.exp(-x))


def _silu(x):
    return x * _sigmoid(x)


def _rmsnorm(x, g):
    ms = jnp.mean(x * x, axis=-1, keepdims=True)
    return x * lax.rsqrt(ms + EPS) * g


def _group_mean(t, bd):
    return _mm(t.astype(BF16), bd)


def _head_norm(t, bd, g):
    ms = _group_mean(t * t, bd)
    return t * lax.rsqrt(ms + EPS) * g


def _rope(xc, cos_t, sin_s):
    lane = lax.broadcasted_iota(jnp.int32, xc.shape, 1)
    upper = (lane & HALF) != 0
    rot = jnp.where(upper, pltpu.roll(xc, HALF, 1), pltpu.roll(xc, V7X_LANES - HALF, 1))
    return xc * cos_t + rot * sin_s


def _rope_sign(shape):
    lane = lax.broadcasted_iota(jnp.int32, shape, 1)
    return jnp.where((lane & HALF) != 0, 1.0, -1.0).astype(F32)


def _kv_variants(t):
    lane = lax.broadcasted_iota(jnp.int32, t.shape, 1)
    lo = lane < HEAD_DIM
    sw = pltpu.roll(t, HEAD_DIM, 1)
    zero = jnp.zeros_like(t)
    return (jnp.where(lo, t, zero).astype(BF16), jnp.where(lo, zero, sw).astype(BF16),
            jnp.where(lo, sw, zero).astype(BF16), jnp.where(lo, zero, t).astype(BF16))


def _kv_variants_t(t):
    tb = t.astype(BF16)
    h0, h1 = tb[0:HEAD_DIM], tb[HEAD_DIM:2 * HEAD_DIM]
    zero = jnp.zeros_like(h0)
    return (jnp.concatenate([h0, zero], axis=0), jnp.concatenate([zero, h0], axis=0),
            jnp.concatenate([h1, zero], axis=0), jnp.concatenate([zero, h1], axis=0))


def _scores(qs, ka, kb, bad, keys_on_lanes=False):
    dot = _mm if keys_on_lanes else _mm_t
    s_e = dot(qs, ka)
    s_o = dot(qs, kb)
    if bad is not None:
        s_e = jnp.where(bad, NEG, s_e)
        s_o = jnp.where(bad, NEG, s_o)
    return s_e, s_o


def _softmax_parts(s, sink):
    m = jnp.maximum(jnp.max(s, axis=-1, keepdims=True), sink)
    e = jnp.exp(s - m)
    r = 1.0 / (jnp.sum(e, axis=-1, keepdims=True) + jnp.exp(sink - m))
    return e.astype(BF16), r


def _pv(p_e, p_o, va, vb, r_e, r_o, keys_on_lanes=False):
    dot = _mm_t if keys_on_lanes else _mm
    o = dot(p_e, va) + dot(p_o, vb)
    lane = lax.broadcasted_iota(jnp.int32, o.shape, 1)
    return o * jnp.where(lane < HEAD_DIM, r_e, r_o)


def _sink_cols(sink_ref, n_blocks, rows_per_pair):
    row = lax.broadcasted_iota(jnp.int32, (n_blocks * 2 * rows_per_pair, 1), 0)
    second_pair = (row // rows_per_pair) % 2 == 1
    second_kv = (row // (2 * rows_per_pair)) % 2 == 1

    def pick(odd):
        kv0 = jnp.where(second_pair, sink_ref[2 + odd], sink_ref[odd])
        kv1 = jnp.where(second_pair, sink_ref[GROUP_HEADS + 2 + odd], sink_ref[GROUP_HEADS + odd])
        return jnp.where(second_kv, kv1, kv0)

    return pick(0), pick(1)


def _stage_chunks(shape):
    rows, cols = shape
    assert rows % STAGE_ROWS == 0 and cols % V7X_LANES == 0
    width = max(c for c in range(V7X_LANES, STAGE_COLS + 1, V7X_LANES) if cols % c == 0)
    return [(r0, c0, STAGE_ROWS, width)
            for r0 in range(0, rows, STAGE_ROWS) for c0 in range(0, cols, width)]


def _stage_weights(pairs, stage, sem):
    chunks = [(src, dst) + ch for src, dst in pairs for ch in _stage_chunks(src.shape)]

    def copy(n):
        src, _, r0, c0, rows, cols = chunks[n]
        slot = n % STAGE_SLOTS
        return pltpu.make_async_copy(src.at[pl.ds(r0, rows), pl.ds(c0, cols)],
                                     stage.at[slot, pl.ds(0, rows), pl.ds(0, cols)],
                                     sem.at[slot])

    for n in range(min(STAGE_SLOTS, len(chunks))):
        copy(n).start()
    for n, (_, dst, r0, c0, rows, cols) in enumerate(chunks):
        copy(n).wait()
        dst[r0:r0 + rows, c0:c0 + cols] = stage[n % STAGE_SLOTS, 0:rows, 0:cols].astype(BF16)
        if n + STAGE_SLOTS < len(chunks):
            copy(n + STAGE_SLOTS).start()


def _qkv_proj(x, lng_ref, win_ref):
    hb = _rmsnorm(x, lng_ref[...]).astype(BF16)
    return hb, _mm(hb, win_ref[:, OFF_Q:OFF_GA])


def _qk_norm_rope(qkv, gq_ref, gk_ref, bdq_ref, bdk_ref, cos_t, sin_s):
    half_q = ATTN_DIM // 2
    q_cols = []
    for j in range(2):
        t = qkv[:, j * half_q:(j + 1) * half_q]
        tn = _head_norm(t, bdq_ref[...], gq_ref[:, j * half_q:(j + 1) * half_q])
        for c in range(half_q // V7X_LANES):
            q_cols.append(_rope(tn[:, c * V7X_LANES:(c + 1) * V7X_LANES], cos_t, sin_s).astype(BF16))
    kn = _head_norm(qkv[:, OFF_K:OFF_V], bdk_ref[...], gk_ref[...])
    kr = _rope(kn, cos_t, sin_s)
    v = qkv[:, OFF_V:OFF_GA]
    return q_cols, kr, v


def _merge_gates(hb, win_ref, d_model):
    mamc = _mm(hb, win_ref[:, OFF_MA:OFF_MA + 2 * d_model])
    return _sigmoid(mamc[:, 0:d_model]), _sigmoid(mamc[:, d_model:2 * d_model])


def _conv_proj(hb, win_ref):
    return _mm(hb, win_ref[:, OFF_B:OFF_GC_END])


def _conv_input(bcug, conv_fn):
    b_gate = bcug[:, 0:CONV_DIM]
    u = bcug[:, CONV_DIM:2 * CONV_DIM] * bcug[:, 2 * CONV_DIM:3 * CONV_DIM]
    gate_c = bcug[:, 3 * CONV_DIM:4 * CONV_DIM]
    conv = conv_fn(u)
    return (b_gate * conv * _silu(gate_c)).astype(BF16)


def _conv_branch(hb, win_ref, wb_ref, conv_fn):
    return _mm(_conv_input(_conv_proj(hb, win_ref), conv_fn), wb_ref[...])


def _finish(x, p, hb, attn, yc, sig_a, sig_c, win_ref, wa_ref, wo_ref, wpg_ref, wpp_ref):
    ga = _mm(hb, win_ref[:, OFF_GA:OFF_B])
    ya = _mm((attn * _silu(ga)).astype(BF16), wa_ref[...])
    mix = sig_a * ya + sig_c * yc
    r = x + _mm(mix.astype(BF16), wo_ref[...])
    gate = _sigmoid(_mm(r.astype(BF16), wpg_ref[...]))
    return r + gate * _mm(p.astype(BF16), wpp_ref[...])


def _prompt_kernel(x_ref, p_ref, lng_ref, win_hbm, gq_ref, gk_ref, invf_ref, sink_ref, cw_ref,
                   wa_hbm, wb_hbm, wo_hbm, wpg_hbm, wpp_hbm, bdq_ref, bdk_ref,
                   y_ref, ko_ref, vo_ref, co_ref,
                   win_out, wa_out, wb_out, wo_out, wpg_out, wpp_out,
                   kbuf, vbuf, ubuf, attn_buf, tab,
                   win_ref, wa_ref, wb_ref, wo_ref, wpg_ref, wpp_ref, stage, sem, out_sem):
    i = pl.program_id(0)
    tm, d_model = x_ref.shape
    invf = invf_ref[...]
    exports = [pltpu.make_async_copy(src, dst, out_sem.at[n]) for n, (src, dst) in enumerate(
        [(win_ref, win_out), (wa_ref, wa_out), (wb_ref, wb_out), (wo_ref, wo_out),
         (wpg_ref, wpg_out), (wpp_ref, wpp_out)])]

    @pl.when(i == 0)
    def _init():
        _stage_weights([(win_hbm, win_ref), (wa_hbm, wa_ref), (wb_hbm, wb_ref), (wo_hbm, wo_ref),
                        (wpg_hbm, wpg_ref), (wpp_hbm, wpp_ref)], stage, sem)
        for e in exports:
            e.start()
        kbuf[:, :, 0:WINDOW] = jnp.zeros((4, KV_DIM, WINDOW), BF16)
        vbuf[:, 0:WINDOW, :] = jnp.zeros((4, WINDOW, V7X_LANES), BF16)
        ubuf[0:V7X_SUBLANES, :] = jnp.zeros((V7X_SUBLANES, CONV_DIM), F32)
        r = lax.broadcasted_iota(jnp.int32, (tm, V7X_LANES), 0).astype(F32)
        ang = r * invf
        sgn = _rope_sign((tm, V7X_LANES))
        c_r = jnp.cos(ang)
        s_r = jnp.sin(ang)
        tab[0] = c_r
        tab[1] = s_r
        tab[2] = c_r * sgn
        tab[3] = s_r * sgn

    base = (i * tm).astype(F32) * invf
    cb = jnp.cos(base)
    sb = jnp.sin(base)
    cos_t = tab[0] * cb - tab[1] * sb
    sin_s = tab[3] * cb + tab[2] * sb

    sub = PROMPT_SUBTILE
    n_sub = tm // sub
    sub_chunks = sub // CHUNK
    assert 2 * CHUNK == V7X_LANES and WINDOW == V7X_LANES
    n_keys = 2 * V7X_LANES

    def key_window_start(r0, c):
        return ((r0 + c * CHUNK) // V7X_LANES) * V7X_LANES

    def mask_window(sc, r0, c):
        lo, hi = sc[:, 0:V7X_LANES], sc[:, V7X_LANES:n_keys]
        lane = lax.broadcasted_iota(jnp.int32, lo.shape, 1)
        no_carry = (i == 0) if key_window_start(r0, c) < WINDOW else None
        if ((r0 + c * CHUNK) // CHUNK) % 2 == 0:
            hi = jnp.where(lane >= CHUNK, NEG, hi)
            if no_carry is not None:
                lo = jnp.where(no_carry, NEG, lo)
        else:
            bad = lane < CHUNK
            lo = jnp.where(bad if no_carry is None else bad | no_carry, NEG, lo)
        return jnp.concatenate([lo, hi], axis=1)
    pad = V7X_SUBLANES
    blocks = [(c, vh) for c in range(sub_chunks) for vh in range(N_KV_HEADS)]
    st = [dict(r0=s * sub) for s in range(n_sub)]

    def stage_rms(s):
        s["x"] = x_ref[s["r0"]:s["r0"] + sub, :]
        s["hb"] = _rmsnorm(s["x"], lng_ref[...]).astype(BF16)

    def stage_qkv(s):
        s["qkv"] = _mm(s["hb"], win_ref[:, OFF_Q:OFF_GA])

    def stage_gate_a(s):
        s["sig_a"] = _sigmoid(_mm(s["hb"], win_ref[:, OFF_MA:OFF_MA + d_model]))

    def stage_gate_c(s):
        s["sig_c"] = _sigmoid(_mm(s["hb"], win_ref[:, OFF_MA + d_model:OFF_MA + 2 * d_model]))

    def stage_qk_norm(s):
        r0 = s["r0"]
        s["q_cols"], kr, v = _qk_norm_rope(s.pop("qkv"), gq_ref, gk_ref, bdq_ref, bdk_ref,
                                           cos_t[r0:r0 + sub], sin_s[r0:r0 + sub])
        kr_t = kr.T
        for n, t in enumerate(_kv_variants_t(kr_t)):
            kbuf[n, :, WINDOW + r0:WINDOW + r0 + sub] = t
        for n, t in enumerate(_kv_variants(v)):
            vbuf[n, WINDOW + r0:WINDOW + r0 + sub, :] = t
        if r0 + sub == tm:
            ko_ref[...] = kr_t[:, sub - WINDOW:sub]
            vo_ref[...] = v[sub - WINDOW:sub, :].T

    def stage_scores(s):
        r0 = s["r0"]
        s_e, s_o = [], []
        for c, vh in blocks:
            rows = slice(c * CHUNK, (c + 1) * CHUNK)
            start = key_window_start(r0, c)
            win = slice(start, start + n_keys)
            qs = jnp.concatenate([s["q_cols"][2 * vh][rows], s["q_cols"][2 * vh + 1][rows]], axis=0)
            se, so = _scores(qs, kbuf[2 * vh, :, win], kbuf[2 * vh + 1, :, win], None,
                             keys_on_lanes=True)
            s_e.append(mask_window(se, r0, c))
            s_o.append(mask_window(so, r0, c))
        s["s_e"] = jnp.concatenate(s_e, axis=0)
        s["s_o"] = jnp.concatenate(s_o, axis=0)
        del s["q_cols"]

    def stage_conv_proj(s):
        s["bcug"] = _conv_proj(s["hb"], win_ref)

    def stage_conv(s):
        r0 = s["r0"]

        def conv_fn(u):
            ubuf[pad + r0:pad + r0 + sub, :] = u
            conv = ubuf[pad + r0 - 2:pad + r0 - 2 + sub, :] * cw_ref[0:1, :]
            conv = conv + ubuf[pad + r0 - 1:pad + r0 - 1 + sub, :] * cw_ref[1:2, :]
            return conv + u * cw_ref[2:3, :]

        s["c_in"] = _conv_input(s.pop("bcug"), conv_fn)

    def stage_conv_out(s):
        s["yc"] = _mm(s.pop("c_in"), wb_ref[...])

    sink_e, sink_o = _sink_cols(sink_ref, len(blocks), CHUNK)

    def stage_softmax(s):
        s["p_e"], s["r_e"] = _softmax_parts(s.pop("s_e"), sink_e)
        s["p_o"], s["r_o"] = _softmax_parts(s.pop("s_o"), sink_o)

    def stage_pv(s):
        r0 = s["r0"]
        for n, (c, vh) in enumerate(blocks):
            rows = slice(r0 + c * CHUNK, r0 + (c + 1) * CHUNK)
            start = key_window_start(r0, c)
            win = slice(start, start + n_keys)
            br = slice(n * 2 * CHUNK, (n + 1) * 2 * CHUNK)
            o = _pv(s["p_e"][br], s["p_o"][br], vbuf[2 * vh, win, :], vbuf[2 * vh + 1, win, :],
                    s["r_e"][br], s["r_o"][br])
            attn_buf[rows, (2 * vh) * V7X_LANES:(2 * vh + 1) * V7X_LANES] = o[0:CHUNK]
            attn_buf[rows, (2 * vh + 1) * V7X_LANES:(2 * vh + 2) * V7X_LANES] = o[CHUNK:2 * CHUNK]

    def stage_attn_gate(s):
        s["silu_ga"] = _silu(_mm(s.pop("hb"), win_ref[:, OFF_GA:OFF_B]))
        s["pp"] = _mm(p_ref[s["r0"]:s["r0"] + sub, :].astype(BF16), wpp_ref[...])

    def stage_attn_out(s):
        r0 = s["r0"]
        a_in = (attn_buf[r0:r0 + sub, :] * s.pop("silu_ga")).astype(BF16)
        ya = _mm(a_in, wa_ref[...])
        s["mix"] = (s.pop("sig_a") * ya + s.pop("sig_c") * s.pop("yc")).astype(BF16)

    def stage_out_proj(s):
        s["r"] = s.pop("x") + _mm(s.pop("mix"), wo_ref[...])

    def stage_ple(s):
        r0 = s["r0"]
        r = s.pop("r")
        gate = _sigmoid(_mm(r.astype(BF16), wpg_ref[...]))
        y_ref[r0:r0 + sub, :] = r + gate * s.pop("pp")

    stages = [stage_rms, stage_qkv, stage_qk_norm, stage_gate_a, stage_scores, stage_gate_c,
              stage_softmax, stage_conv_proj, stage_conv, stage_attn_gate, stage_conv_out,
              stage_pv, stage_attn_out, stage_out_proj, stage_ple]
    for t in range(len(stages) + PROMPT_STAGE_SKEW * (n_sub - 1)):
        for j, s in enumerate(st):
            k = t - j * PROMPT_STAGE_SKEW
            if 0 <= k < len(stages):
                stages[k](s)

    co_ref[...] = ubuf[pad + tm - 2:pad + tm, :]
    ubuf[0:pad, :] = ubuf[tm:tm + pad, :]
    kbuf[:, :, 0:WINDOW] = kbuf[:, :, tm:tm + WINDOW]
    vbuf[:, 0:WINDOW, :] = vbuf[:, tm:tm + WINDOW, :]

    @pl.when(i == pl.num_programs(0) - 1)
    def _drain():
        for e in exports:
            e.wait()


def _sample_kernel(x_ref, p_ref, ck_ref, cv_ref, sc_ref, lng_ref, win_ref, gq_ref, gk_ref,
                   invf_ref, sink_ref, cw_ref, wa_ref, wb_ref, wo_ref, wpg_ref, wpp_ref,
                   bdq_ref, bdk_ref,
                   y_ref, ko_ref, vo_ref, co_ref,
                   ubuf, attn_buf, conv_buf, tab):
    i = pl.program_id(0)
    bb, _, cache_len = ck_ref.shape
    rows_total, d_model = x_ref.shape
    t_new = rows_total // bb
    n_keys = cache_len + t_new

    @pl.when(i == 0)
    def _init():
        r = lax.broadcasted_iota(jnp.int32, (rows_total, V7X_LANES), 0)
        pos = (PAST_LEN + lax.rem(r, t_new)).astype(F32)
        ang = pos * invf_ref[...]
        tab[0] = jnp.cos(ang)
        tab[1] = jnp.sin(ang) * _rope_sign((rows_total, V7X_LANES))

    def conv_fn(u):
        pad = V7X_SUBLANES
        for b in range(bb):
            rows = slice(b * t_new, (b + 1) * t_new)
            ub = u[rows]
            ubuf[b, pad - (CONV_WIDTH - 1):pad, :] = sc_ref[b]
            ubuf[b, pad:pad + t_new, :] = ub
            conv = ubuf[b, pad - 2:pad - 2 + t_new, :] * cw_ref[0:1, :]
            conv = conv + ubuf[b, pad - 1:pad - 1 + t_new, :] * cw_ref[1:2, :]
            conv_buf[rows, :] = conv + ub * cw_ref[2:3, :]
            co_ref[b] = ubuf[b, pad + t_new - (CONV_WIDTH - 1):pad + t_new, :]
        return conv_buf[...]

    x = x_ref[...]
    hb, qkv = _qkv_proj(x, lng_ref, win_ref)
    sig_a, sig_c = _merge_gates(hb, win_ref, d_model)
    q_cols, kr, v = _qk_norm_rope(qkv, gq_ref, gk_ref, bdq_ref, bdk_ref, tab[0], tab[1])

    qi = lax.broadcasted_iota(jnp.int32, (2 * t_new, n_keys), 0)
    q_pos = PAST_LEN + lax.rem(qi, t_new)
    k_pos = PAST_LEN - cache_len + lax.broadcasted_iota(jnp.int32, (2 * t_new, n_keys), 1)
    q_ch = q_pos // CHUNK
    k_ch = k_pos // CHUNK
    bad = jnp.logical_not((k_ch <= q_ch) & (k_ch >= q_ch - WINDOW_CHUNKS))

    blocks = [(b, vh) for b in range(bb) for vh in range(N_KV_HEADS)]
    s_e, s_o, vvars = [], [], []
    kr_t = kr.T
    v_t = v.T
    for b in range(bb):
        rows = slice(b * t_new, (b + 1) * t_new)
        kcat = jnp.concatenate([ck_ref[b], kr_t[:, rows]], axis=1)
        vcat = jnp.concatenate([cv_ref[b], v_t[:, rows]], axis=1)
        ko_ref[b] = kcat[:, n_keys - cache_len:n_keys]
        vo_ref[b] = vcat[:, n_keys - cache_len:n_keys]
        kvar = _kv_variants_t(kcat)
        vvars.append(_kv_variants_t(vcat))
        for vh in range(N_KV_HEADS):
            qs = jnp.concatenate([q_cols[2 * vh][rows], q_cols[2 * vh + 1][rows]], axis=0)
            se, so = _scores(qs, kvar[2 * vh], kvar[2 * vh + 1], bad, keys_on_lanes=True)
            s_e.append(se)
            s_o.append(so)

    yc = _conv_branch(hb, win_ref, wb_ref, conv_fn)

    sink_e, sink_o = _sink_cols(sink_ref, len(blocks), t_new)
    p_e, r_e = _softmax_parts(jnp.concatenate(s_e, axis=0), sink_e)
    p_o, r_o = _softmax_parts(jnp.concatenate(s_o, axis=0), sink_o)

    for n, (b, vh) in enumerate(blocks):
        rows = slice(b * t_new, (b + 1) * t_new)
        br = slice(n * 2 * t_new, (n + 1) * 2 * t_new)
        o = _pv(p_e[br], p_o[br], vvars[b][2 * vh], vvars[b][2 * vh + 1], r_e[br], r_o[br],
                keys_on_lanes=True)
        attn_buf[rows, (2 * vh) * V7X_LANES:(2 * vh + 1) * V7X_LANES] = o[0:t_new]
        attn_buf[rows, (2 * vh + 1) * V7X_LANES:(2 * vh + 2) * V7X_LANES] = o[t_new:2 * t_new]

    y_ref[...] = _finish(x, p_ref[...], hb, attn_buf[...], yc, sig_a, sig_c, win_ref,
                         wa_ref, wo_ref, wpg_ref, wpp_ref)


def _const_spec(shape):
    nd = len(shape)
    return pl.BlockSpec(shape, lambda i: (0,) * nd, pipeline_mode=pl.Buffered(1))


def _operand_specs(c, keys, staged):
    return [pl.BlockSpec(memory_space=pl.ANY) if staged and k in _STAGED_KEYS
            else _const_spec(c[k].shape) for k in keys]


def _staging_scratch(c):
    return ([pltpu.VMEM(c[k].shape, BF16) for k in _STAGED_KEYS]
            + [pltpu.VMEM((STAGE_SLOTS, STAGE_ROWS, STAGE_COLS), F32),
               pltpu.SemaphoreType.DMA((STAGE_SLOTS,)),
               pltpu.SemaphoreType.DMA((len(_STAGED_KEYS),))])


def _smem_spec():
    return pl.BlockSpec(memory_space=pltpu.SMEM)


def _block_diag_mean(width):
    idx = np.arange(width) // HEAD_DIM
    return jnp.asarray((idx[:, None] == idx[None, :]).astype(np.float32) / HEAD_DIM, dtype=BF16)


def _layer_consts(ln_g, w_in, q_norm_g, k_norm_g, sink, conv_w, w_attn_out, w_conv_out, w_o,
                  w_ple_gate, w_ple_proj):
    scale = HEAD_DIM ** -0.5
    inv_freq = ROPE_THETA ** (-jnp.arange(0, HALF, dtype=F32) * 2.0 / HEAD_DIM)
    return dict(
        lng=ln_g.reshape(1, -1).astype(F32),
        win=w_in.astype(F32),
        gq=(jnp.tile(q_norm_g.astype(F32), N_HEADS) * scale).reshape(1, ATTN_DIM),
        gk=jnp.tile(k_norm_g.astype(F32), N_KV_HEADS).reshape(1, KV_DIM),
        invf=jnp.tile(inv_freq, V7X_LANES // HALF).reshape(1, V7X_LANES),
        sink=sink.astype(F32),
        cw=conv_w.astype(F32),
        wa=w_attn_out.astype(F32),
        wb=w_conv_out.astype(F32),
        wo=w_o.astype(F32),
        wpg=w_ple_gate.astype(F32),
        wpp=w_ple_proj.astype(F32),
        bdq=_block_diag_mean(ATTN_DIM // 2),
        bdk=_block_diag_mean(KV_DIM),
    )


_VEC_KEYS = ("lng", "win", "gq", "gk", "invf")
_MAT_KEYS = ("cw", "wa", "wb", "wo", "wpg", "wpp", "bdq", "bdk")
_STAGED_KEYS = ("win", "wa", "wb", "wo", "wpg", "wpp")


def _heads_first(t):
    t = jnp.moveaxis(t, -3, -1)
    return t.reshape(t.shape[:-3] + (KV_DIM, t.shape[-1]))


def _heads_last(t):
    t = t.reshape(t.shape[:-2] + (N_KV_HEADS, HEAD_DIM, t.shape[-1]))
    return jnp.moveaxis(t, -1, -3)


def _prompt_layer(x, p, c):
    t, d = x.shape
    tm = PROMPT_TILE
    assert t % tm == 0 and tm % PROMPT_SUBTILE == 0
    assert PROMPT_SUBTILE % CHUNK == 0 and PROMPT_SUBTILE >= WINDOW
    pre = [c[k] for k in _VEC_KEYS]
    post = [c[k] for k in _MAT_KEYS]
    in_specs = ([pl.BlockSpec((tm, d), lambda i: (i, 0)),
                 pl.BlockSpec((tm, p.shape[1]), lambda i: (i, 0))]
                + _operand_specs(c, _VEC_KEYS, True) + [_smem_spec()]
                + _operand_specs(c, _MAT_KEYS, True))
    out_shape = ((jax.ShapeDtypeStruct((t, d), F32),
                  jax.ShapeDtypeStruct((KV_DIM, WINDOW), F32),
                  jax.ShapeDtypeStruct((KV_DIM, WINDOW), F32),
                  jax.ShapeDtypeStruct((CONV_WIDTH - 1, CONV_DIM), F32))
                 + tuple(jax.ShapeDtypeStruct(c[k].shape, BF16) for k in _STAGED_KEYS))
    out_specs = ((pl.BlockSpec((tm, d), lambda i: (i, 0)),
                  pl.BlockSpec((KV_DIM, WINDOW), lambda i: (0, 0)),
                  pl.BlockSpec((KV_DIM, WINDOW), lambda i: (0, 0)),
                  pl.BlockSpec((CONV_WIDTH - 1, CONV_DIM), lambda i: (0, 0)))
                 + tuple(pl.BlockSpec(memory_space=pl.ANY) for _ in _STAGED_KEYS))
    scratch = [pltpu.VMEM((4, KV_DIM, WINDOW + tm), BF16),
               pltpu.VMEM((4, WINDOW + tm, V7X_LANES), BF16),
               pltpu.VMEM((V7X_SUBLANES + tm, CONV_DIM), F32),
               pltpu.VMEM((tm, ATTN_DIM), F32),
               pltpu.VMEM((4, tm, V7X_LANES), F32)] + _staging_scratch(c)
    outs = pl.pallas_call(
        _prompt_kernel,
        grid=(t // tm,),
        in_specs=in_specs,
        out_specs=out_specs,
        out_shape=out_shape,
        scratch_shapes=scratch,
        compiler_params=pltpu.CompilerParams(dimension_semantics=("arbitrary",),
                                             vmem_limit_bytes=V7X_VMEM_LIMIT_BYTES),
        name="prompt_layer",
    )(x, p, *pre, c["sink"], *post)
    y, ko, vo, co = outs[:4]
    return (y, _heads_last(ko), _heads_last(vo), co), dict(c, **dict(zip(_STAGED_KEYS, outs[4:])))


def _sample_layer(x, p, cache_k, cache_v, state_conv, c):
    nb, t_new, d = x.shape
    cache_len = cache_k.shape[1]
    bb = SAMPLE_BATCH_TILE
    assert nb % bb == 0 and t_new >= CONV_WIDTH - 1 and t_new <= cache_len
    rows = bb * t_new
    x2 = x.reshape(nb * t_new, d)
    p2 = p.reshape(nb * t_new, p.shape[-1])
    ck = _heads_first(cache_k)
    cv = _heads_first(cache_v)
    pre = [c[k] for k in _VEC_KEYS]
    post = [c[k] for k in _MAT_KEYS]
    in_specs = ([pl.BlockSpec((rows, d), lambda i: (i, 0)),
                 pl.BlockSpec((rows, p2.shape[1]), lambda i: (i, 0)),
                 pl.BlockSpec((bb, KV_DIM, cache_len), lambda i: (i, 0, 0)),
                 pl.BlockSpec((bb, KV_DIM, cache_len), lambda i: (i, 0, 0)),
                 pl.BlockSpec((bb, CONV_WIDTH - 1, CONV_DIM), lambda i: (i, 0, 0))]
                + _operand_specs(c, _VEC_KEYS, False) + [_smem_spec()]
                + _operand_specs(c, _MAT_KEYS, False))
    out_shape = (jax.ShapeDtypeStruct((nb * t_new, d), F32),
                 jax.ShapeDtypeStruct((nb, KV_DIM, cache_len), F32),
                 jax.ShapeDtypeStruct((nb, KV_DIM, cache_len), F32),
                 jax.ShapeDtypeStruct((nb, CONV_WIDTH - 1, CONV_DIM), F32))
    out_specs = (pl.BlockSpec((rows, d), lambda i: (i, 0)),
                 pl.BlockSpec((bb, KV_DIM, cache_len), lambda i: (i, 0, 0)),
                 pl.BlockSpec((bb, KV_DIM, cache_len), lambda i: (i, 0, 0)),
                 pl.BlockSpec((bb, CONV_WIDTH - 1, CONV_DIM), lambda i: (i, 0, 0)))
    scratch = [pltpu.VMEM((bb, V7X_SUBLANES + t_new, CONV_DIM), F32),
               pltpu.VMEM((rows, ATTN_DIM), F32),
               pltpu.VMEM((rows, CONV_DIM), F32),
               pltpu.VMEM((2, rows, V7X_LANES), F32)]
    y, ko, vo, co = pl.pallas_call(
        _sample_kernel,
        grid=(nb // bb,),
        in_specs=in_specs,
        out_specs=out_specs,
        out_shape=out_shape,
        scratch_shapes=scratch,
        compiler_params=pltpu.CompilerParams(dimension_semantics=("arbitrary",),
                                             vmem_limit_bytes=V7X_VMEM_LIMIT_BYTES),
        name="sample_layer",
    )(x2, p2, ck, cv, state_conv, *pre, c["sink"], *post)
    return y.reshape(nb, t_new, d), _heads_last(ko), _heads_last(vo), co


def kernel(x_prompt, x_sample, p_prompt, p_sample, cache_k, cache_v, state_conv, ln_g, w_in,
           q_norm_g, k_norm_g, sink, conv_w, w_attn_out, w_conv_out, w_o, w_ple_gate, w_ple_proj):
    depth = ln_g.shape[0]
    batch = x_prompt.shape[0]
    hp, hs = x_prompt, x_sample
    kp_l, vp_l, cp_l, ks_l, vs_l, cs_l = [], [], [], [], [], []
    for i in range(depth):
        c = _layer_consts(ln_g[i], w_in[i], q_norm_g[i], k_norm_g[i], sink[i], conv_w[i],
                          w_attn_out[i], w_conv_out[i], w_o[i], w_ple_gate[i], w_ple_proj[i])
        ys, kps, vps, cps = [], [], [], []
        for b in range(batch):
            (y, ko, vo, co), c_bf16 = _prompt_layer(hp[b], p_prompt[i, b], c)
            ys.append(y)
            kps.append(ko)
            vps.append(vo)
            cps.append(co)
        hp = jnp.stack(ys)
        kp_l.append(jnp.stack(kps))
        vp_l.append(jnp.stack(vps))
        cp_l.append(jnp.stack(cps))
        hs, ko, vo, co = _sample_layer(hs, p_sample[i], cache_k[i], cache_v[i], state_conv[i], c_bf16)
        ks_l.append(ko)
        vs_l.append(vo)
        cs_l.append(co)
    return (hp, hs, jnp.stack(kp_l), jnp.stack(vp_l), jnp.stack(cp_l),
            jnp.stack(ks_l), jnp.stack(vs_l), jnp.stack(cs_l))
```

```python
import numpy as np
import jax
import jax.numpy as jnp
from jax import lax
from jax.experimental import pallas as pl
from jax.experimental.pallas import tpu as pltpu

F32 = jnp.float32
BF16 = jnp.bfloat16

CHUNK = 64
WINDOW = 128
WINDOW_CHUNKS = WINDOW // CHUNK
N_HEADS = 8
N_KV_HEADS = 2
GROUP_HEADS = N_HEADS // N_KV_HEADS
HEAD_DIM = 64
HALF = HEAD_DIM // 2
ATTN_DIM = N_HEADS * HEAD_DIM
KV_DIM = N_KV_HEADS * HEAD_DIM
CONV_DIM = 512
CONV_WIDTH = 3
PAST_LEN = 1024
ROPE_THETA = 10000.0
EPS = 1e-6
NEG = -1e30

OFF_Q = 0
OFF_K = OFF_Q + ATTN_DIM
OFF_V = OFF_K + KV_DIM
OFF_GA = OFF_V + KV_DIM
OFF_B = OFF_GA + ATTN_DIM
OFF_GC_END = OFF_B + 4 * CONV_DIM
OFF_MA = OFF_GC_END

V7X_LANES = 128
V7X_SUBLANES = 8
V7X_VMEM_LIMIT_BYTES = 56 * 1024 * 1024

PROMPT_TILE = 512
PROMPT_SUBTILE = 256
PROMPT_STAGE_SKEW = 1
STAGE_ROWS = 256
STAGE_COLS = 1024
STAGE_SLOTS = 6
SAMPLE_BATCH_TILE = 8


def _mm(a, w):
    return jnp.dot(a, w, preferred_element_type=F32)


def _mm_t(a, b):
    return lax.dot_general(a, b, (((1,), (1,)), ((), ())), preferred_element_type=F32)


def _sigmoid(x):
    return 1.0 / (1.0 + jnp.exp(-x))


def _silu(x):
    return x * _sigmoid(x)


def _rmsnorm(x, g):
    ms = jnp.mean(x * x, axis=-1, keepdims=True)
    return x * lax.rsqrt(ms + EPS) * g


def _group_mean(t, bd):
    return _mm(t.astype(BF16), bd)


def _head_norm(t, bd, g):
    ms = _group_mean(t * t, bd)
    return t * lax.rsqrt(ms + EPS) * g


def _rope(xc, cos_t, sin_s):
    lane = lax.broadcasted_iota(jnp.int32, xc.shape, 1)
    upper = (lane & HALF) != 0
    rot = jnp.where(upper, pltpu.roll(xc, HALF, 1), pltpu.roll(xc, V7X_LANES - HALF, 1))
    return xc * cos_t + rot * sin_s


def _rope_sign(shape):
    lane = lax.broadcasted_iota(jnp.int32, shape, 1)
    return jnp.where((lane & HALF) != 0, 1.0, -1.0).astype(F32)


def _kv_variants(t):
    lane = lax.broadcasted_iota(jnp.int32, t.shape, 1)
    lo = lane < HEAD_DIM
    sw = pltpu.roll(t, HEAD_DIM, 1)
    zero = jnp.zeros_like(t)
    return (jnp.where(lo, t, zero).astype(BF16), jnp.where(lo, zero, sw).astype(BF16),
            jnp.where(lo, sw, zero).astype(BF16), jnp.where(lo, zero, t).astype(BF16))


def _kv_variants_t(t):
    tb = t.astype(BF16)
    h0, h1 = tb[0:HEAD_DIM], tb[HEAD_DIM:2 * HEAD_DIM]
    zero = jnp.zeros_like(h0)
    return (jnp.concatenate([h0, zero], axis=0), jnp.concatenate([zero, h0], axis=0),
            jnp.concatenate([h1, zero], axis=0), jnp.concatenate([zero, h1], axis=0))


def _scores(qs, ka, kb, bad, keys_on_lanes=False):
    dot = _mm if keys_on_lanes else _mm_t
    s_e = dot(qs, ka)
    s_o = dot(qs, kb)
    if bad is not None:
        s_e = jnp.where(bad, NEG, s_e)
        s_o = jnp.where(bad, NEG, s_o)
    return s_e, s_o


def _softmax_parts(s, sink):
    m = jnp.maximum(jnp.max(s, axis=-1, keepdims=True), sink)
    e = jnp.exp(s - m)
    r = 1.0 / (jnp.sum(e, axis=-1, keepdims=True) + jnp.exp(sink - m))
    return e.astype(BF16), r


def _pv(p_e, p_o, va, vb, r_e, r_o, keys_on_lanes=False):
    dot = _mm_t if keys_on_lanes else _mm
    o = dot(p_e, va) + dot(p_o, vb)
    lane = lax.broadcasted_iota(jnp.int32, o.shape, 1)
    return o * jnp.where(lane < HEAD_DIM, r_e, r_o)


def _sink_cols(sink_ref, n_blocks, rows_per_pair):
    row = lax.broadcasted_iota(jnp.int32, (n_blocks * 2 * rows_per_pair, 1), 0)
    second_pair = (row // rows_per_pair) % 2 == 1
    second_kv = (row // (2 * rows_per_pair)) % 2 == 1

    def pick(odd):
        kv0 = jnp.where(second_pair, sink_ref[2 + odd], sink_ref[odd])
        kv1 = jnp.where(second_pair, sink_ref[GROUP_HEADS + 2 + odd], sink_ref[GROUP_HEADS + odd])
        return jnp.where(second_kv, kv1, kv0)

    return pick(0), pick(1)


def _stage_chunks(shape):
    rows, cols = shape
    assert rows % STAGE_ROWS == 0 and cols % V7X_LANES == 0
    width = max(c for c in range(V7X_LANES, STAGE_COLS + 1, V7X_LANES) if cols % c == 0)
    return [(r0, c0, STAGE_ROWS, width)
            for r0 in range(0, rows, STAGE_ROWS) for c0 in range(0, cols, width)]


def _stage_weights(pairs, stage, sem):
    chunks = [(src, dst) + ch for src, dst in pairs for ch in _stage_chunks(src.shape)]

    def copy(n):
        src, _, r0, c0, rows, cols = chunks[n]
        slot = n % STAGE_SLOTS
        return pltpu.make_async_copy(src.at[pl.ds(r0, rows), pl.ds(c0, cols)],
                                     stage.at[slot, pl.ds(0, rows), pl.ds(0, cols)],
                                     sem.at[slot])

    for n in range(min(STAGE_SLOTS, len(chunks))):
        copy(n).start()
    for n, (_, dst, r0, c0, rows, cols) in enumerate(chunks):
        copy(n).wait()
        dst[r0:r0 + rows, c0:c0 + cols] = stage[n % STAGE_SLOTS, 0:rows, 0:cols].astype(BF16)
        if n + STAGE_SLOTS < len(chunks):
            copy(n + STAGE_SLOTS).start()


def _qkv_proj(x, lng_ref, win_ref):
    hb = _rmsnorm(x, lng_ref[...]).astype(BF16)
    return hb, _mm(hb, win_ref[:, OFF_Q:OFF_GA])


def _tile_lanes(g, width):
    while g.shape[1] < width:
        g = jnp.concatenate([g, g], axis=1)
    return g


def _qk_norm_rope(qkv, gq_ref, gk_ref, bdq_ref, bdk_ref, cos_t, sin_s):
    half_q = ATTN_DIM // 2
    gq = _tile_lanes(gq_ref[...] * (HEAD_DIM ** -0.5), half_q)
    gk = _tile_lanes(gk_ref[...], KV_DIM)
    q_cols = []
    for j in range(2):
        t = qkv[:, j * half_q:(j + 1) * half_q]
        tn = _head_norm(t, bdq_ref[...], gq)
        for c in range(half_q // V7X_LANES):
            q_cols.append(_rope(tn[:, c * V7X_LANES:(c + 1) * V7X_LANES], cos_t, sin_s).astype(BF16))
    kn = _head_norm(qkv[:, OFF_K:OFF_V], bdk_ref[...], gk)
    kr = _rope(kn, cos_t, sin_s)
    v = qkv[:, OFF_V:OFF_GA]
    return q_cols, kr, v


def _merge_gates(hb, win_ref, d_model):
    mamc = _mm(hb, win_ref[:, OFF_MA:OFF_MA + 2 * d_model])
    return _sigmoid(mamc[:, 0:d_model]), _sigmoid(mamc[:, d_model:2 * d_model])


def _conv_proj(hb, win_ref):
    return _mm(hb, win_ref[:, OFF_B:OFF_GC_END])


def _conv_input(bcug, conv_fn):
    b_gate = bcug[:, 0:CONV_DIM]
    u = bcug[:, CONV_DIM:2 * CONV_DIM] * bcug[:, 2 * CONV_DIM:3 * CONV_DIM]
    gate_c = bcug[:, 3 * CONV_DIM:4 * CONV_DIM]
    conv = conv_fn(u)
    return (b_gate * conv * _silu(gate_c)).astype(BF16)


def _conv_branch(hb, win_ref, wb_ref, conv_fn):
    return _mm(_conv_input(_conv_proj(hb, win_ref), conv_fn), wb_ref[...])


def _finish(x, p, hb, attn, yc, sig_a, sig_c, win_ref, wa_ref, wo_ref, wpg_ref, wpp_ref):
    ga = _mm(hb, win_ref[:, OFF_GA:OFF_B])
    ya = _mm((attn * _silu(ga)).astype(BF16), wa_ref[...])
    mix = sig_a * ya + sig_c * yc
    r = x + _mm(mix.astype(BF16), wo_ref[...])
    gate = _sigmoid(_mm(r.astype(BF16), wpg_ref[...]))
    return r + gate * _mm(p.astype(BF16), wpp_ref[...])


def _prompt_kernel(x_ref, p_ref, lng_ref, win_hbm, gq_ref, gk_ref, invf_ref, sink_ref, cw_ref,
                   wa_hbm, wb_hbm, wo_hbm, wpg_hbm, wpp_hbm, bdq_ref, bdk_ref,
                   y_ref, ko_ref, vo_ref, co_ref,
                   win_out, wa_out, wb_out, wo_out, wpg_out, wpp_out,
                   kbuf, vbuf, ubuf, attn_buf, tab,
                   win_ref, wa_ref, wb_ref, wo_ref, wpg_ref, wpp_ref, stage, sem, out_sem):
    i = pl.program_id(0)
    tm, d_model = x_ref.shape
    invf = invf_ref[...]
    exports = [pltpu.make_async_copy(src, dst, out_sem.at[n]) for n, (src, dst) in enumerate(
        [(win_ref, win_out), (wa_ref, wa_out), (wb_ref, wb_out), (wo_ref, wo_out),
         (wpg_ref, wpg_out), (wpp_ref, wpp_out)])]

    @pl.when(i == 0)
    def _init():
        _stage_weights([(win_hbm, win_ref), (wa_hbm, wa_ref), (wb_hbm, wb_ref), (wo_hbm, wo_ref),
                        (wpg_hbm, wpg_ref), (wpp_hbm, wpp_ref)], stage, sem)
        for e in exports:
            e.start()
        kbuf[:, :, 0:WINDOW] = jnp.zeros((4, KV_DIM, WINDOW), BF16)
        vbuf[:, 0:WINDOW, :] = jnp.zeros((4, WINDOW, V7X_LANES), BF16)
        ubuf[0:V7X_SUBLANES, :] = jnp.zeros((V7X_SUBLANES, CONV_DIM), F32)
        r = lax.broadcasted_iota(jnp.int32, (tm, V7X_LANES), 0).astype(F32)
        ang = r * invf
        sgn = _rope_sign((tm, V7X_LANES))
        c_r = jnp.cos(ang)
        s_r = jnp.sin(ang)
        tab[0] = c_r
        tab[1] = s_r
        tab[2] = c_r * sgn
        tab[3] = s_r * sgn

    base = (i * tm).astype(F32) * invf
    cb = jnp.cos(base)
    sb = jnp.sin(base)
    cos_t = tab[0] * cb - tab[1] * sb
    sin_s = tab[3] * cb + tab[2] * sb

    sub = PROMPT_SUBTILE
    n_sub = tm // sub
    sub_chunks = sub // CHUNK
    assert 2 * CHUNK == V7X_LANES and WINDOW == V7X_LANES
    n_keys = 2 * V7X_LANES

    def key_window_start(r0, c):
        return ((r0 + c * CHUNK) // V7X_LANES) * V7X_LANES

    def mask_window(sc, r0, c):
        lo, hi = sc[:, 0:V7X_LANES], sc[:, V7X_LANES:n_keys]
        lane = lax.broadcasted_iota(jnp.int32, lo.shape, 1)
        no_carry = (i == 0) if key_window_start(r0, c) < WINDOW else None
        if ((r0 + c * CHUNK) // CHUNK) % 2 == 0:
            hi = jnp.where(lane >= CHUNK, NEG, hi)
            if no_carry is not None:
                lo = jnp.where(no_carry, NEG, lo)
        else:
            bad = lane < CHUNK
            lo = jnp.where(bad if no_carry is None else bad | no_carry, NEG, lo)
        return jnp.concatenate([lo, hi], axis=1)
    pad = V7X_SUBLANES
    blocks = [(c, vh) for c in range(sub_chunks) for vh in range(N_KV_HEADS)]
    st = [dict(r0=s * sub) for s in range(n_sub)]

    def stage_rms(s):
        s["x"] = x_ref[s["r0"]:s["r0"] + sub, :]
        s["hb"] = _rmsnorm(s["x"], lng_ref[...]).astype(BF16)

    def stage_qkv(s):
        s["qkv"] = _mm(s["hb"], win_ref[:, OFF_Q:OFF_GA])

    def stage_gate_a(s):
        s["sig_a"] = _sigmoid(_mm(s["hb"], win_ref[:, OFF_MA:OFF_MA + d_model]))

    def stage_gate_c(s):
        s["sig_c"] = _sigmoid(_mm(s["hb"], win_ref[:, OFF_MA + d_model:OFF_MA + 2 * d_model]))

    def stage_qk_norm(s):
        r0 = s["r0"]
        s["q_cols"], kr, v = _qk_norm_rope(s.pop("qkv"), gq_ref, gk_ref, bdq_ref, bdk_ref,
                                           cos_t[r0:r0 + sub], sin_s[r0:r0 + sub])
        kr_t = kr.T
        for n, t in enumerate(_kv_variants_t(kr_t)):
            kbuf[n, :, WINDOW + r0:WINDOW + r0 + sub] = t
        for n, t in enumerate(_kv_variants(v)):
            vbuf[n, WINDOW + r0:WINDOW + r0 + sub, :] = t
        if r0 + sub == tm:
            ko_ref[...] = kr_t[:, sub - WINDOW:sub]
            vo_ref[...] = v[sub - WINDOW:sub, :].T

    def stage_scores(s):
        r0 = s["r0"]
        s_e, s_o = [], []
        for c, vh in blocks:
            rows = slice(c * CHUNK, (c + 1) * CHUNK)
            start = key_window_start(r0, c)
            win = slice(start, start + n_keys)
            qs = jnp.concatenate([s["q_cols"][2 * vh][rows], s["q_cols"][2 * vh + 1][rows]], axis=0)
            se, so = _scores(qs, kbuf[2 * vh, :, win], kbuf[2 * vh + 1, :, win], None,
                             keys_on_lanes=True)
            s_e.append(mask_window(se, r0, c))
            s_o.append(mask_window(so, r0, c))
        s["s_e"] = jnp.concatenate(s_e, axis=0)
        s["s_o"] = jnp.concatenate(s_o, axis=0)
        del s["q_cols"]

    def stage_conv_proj(s):
        s["bcug"] = _conv_proj(s["hb"], win_ref)

    def stage_conv(s):
        r0 = s["r0"]

        def conv_fn(u):
            ubuf[pad + r0:pad + r0 + sub, :] = u
            conv = ubuf[pad + r0 - 2:pad + r0 - 2 + sub, :] * cw_ref[0]
            conv = conv + ubuf[pad + r0 - 1:pad + r0 - 1 + sub, :] * cw_ref[1]
            return conv + u * cw_ref[2]

        s["c_in"] = _conv_input(s.pop("bcug"), conv_fn)

    def stage_conv_out(s):
        s["yc"] = _mm(s.pop("c_in"), wb_ref[...])

    sink_e, sink_o = _sink_cols(sink_ref, len(blocks), CHUNK)

    def stage_softmax(s):
        s["p_e"], s["r_e"] = _softmax_parts(s.pop("s_e"), sink_e)
        s["p_o"], s["r_o"] = _softmax_parts(s.pop("s_o"), sink_o)

    def stage_pv(s):
        r0 = s["r0"]
        for n, (c, vh) in enumerate(blocks):
            rows = slice(r0 + c * CHUNK, r0 + (c + 1) * CHUNK)
            start = key_window_start(r0, c)
            win = slice(start, start + n_keys)
            br = slice(n * 2 * CHUNK, (n + 1) * 2 * CHUNK)
            o = _pv(s["p_e"][br], s["p_o"][br], vbuf[2 * vh, win, :], vbuf[2 * vh + 1, win, :],
                    s["r_e"][br], s["r_o"][br])
            attn_buf[rows, (2 * vh) * V7X_LANES:(2 * vh + 1) * V7X_LANES] = o[0:CHUNK]
            attn_buf[rows, (2 * vh + 1) * V7X_LANES:(2 * vh + 2) * V7X_LANES] = o[CHUNK:2 * CHUNK]

    def stage_attn_gate(s):
        s["silu_ga"] = _silu(_mm(s.pop("hb"), win_ref[:, OFF_GA:OFF_B]))
        s["pp"] = _mm(p_ref[s["r0"]:s["r0"] + sub, :].astype(BF16), wpp_ref[...])

    def stage_attn_out(s):
        r0 = s["r0"]
        a_in = (attn_buf[r0:r0 + sub, :] * s.pop("silu_ga")).astype(BF16)
        ya = _mm(a_in, wa_ref[...])
        s["mix"] = (s.pop("sig_a") * ya + s.pop("sig_c") * s.pop("yc")).astype(BF16)

    def stage_out_proj(s):
        s["r"] = s.pop("x") + _mm(s.pop("mix"), wo_ref[...])

    def stage_ple(s):
        r0 = s["r0"]
        r = s.pop("r")
        gate = _sigmoid(_mm(r.astype(BF16), wpg_ref[...]))
        y_ref[r0:r0 + sub, :] = r + gate * s.pop("pp")

    stages = [stage_rms, stage_qkv, stage_qk_norm, stage_gate_a, stage_scores, stage_gate_c,
              stage_softmax, stage_conv_proj, stage_conv, stage_attn_gate, stage_conv_out,
              stage_pv, stage_attn_out, stage_out_proj, stage_ple]
    for t in range(len(stages) + PROMPT_STAGE_SKEW * (n_sub - 1)):
        for j, s in enumerate(st):
            k = t - j * PROMPT_STAGE_SKEW
            if 0 <= k < len(stages):
                stages[k](s)

    co_ref[...] = ubuf[pad + tm - 2:pad + tm, :]
    ubuf[0:pad, :] = ubuf[tm:tm + pad, :]
    kbuf[:, :, 0:WINDOW] = kbuf[:, :, tm:tm + WINDOW]
    vbuf[:, 0:WINDOW, :] = vbuf[:, tm:tm + WINDOW, :]

    @pl.when(i == pl.num_programs(0) - 1)
    def _drain():
        for e in exports:
            e.wait()


def _sample_kernel(x_ref, p_ref, ck_ref, cv_ref, sc_ref, lng_ref, win_ref, gq_ref, gk_ref,
                   invf_ref, sink_ref, cw_ref, wa_ref, wb_ref, wo_ref, wpg_ref, wpp_ref,
                   bdq_ref, bdk_ref,
                   y_ref, ko_ref, vo_ref, co_ref,
                   ubuf, attn_buf, conv_buf, tab):
    i = pl.program_id(0)
    bb, _, cache_len = ck_ref.shape
    rows_total, d_model = x_ref.shape
    t_new = rows_total // bb
    n_keys = cache_len + t_new

    @pl.when(i == 0)
    def _init():
        r = lax.broadcasted_iota(jnp.int32, (rows_total, V7X_LANES), 0)
        pos = (PAST_LEN + lax.rem(r, t_new)).astype(F32)
        ang = pos * invf_ref[...]
        tab[0] = jnp.cos(ang)
        tab[1] = jnp.sin(ang) * _rope_sign((rows_total, V7X_LANES))

    def conv_fn(u):
        pad = V7X_SUBLANES
        for b in range(bb):
            rows = slice(b * t_new, (b + 1) * t_new)
            ub = u[rows]
            ubuf[b, pad - (CONV_WIDTH - 1):pad, :] = sc_ref[b]
            ubuf[b, pad:pad + t_new, :] = ub
            conv = ubuf[b, pad - 2:pad - 2 + t_new, :] * cw_ref[0]
            conv = conv + ubuf[b, pad - 1:pad - 1 + t_new, :] * cw_ref[1]
            conv_buf[rows, :] = conv + ub * cw_ref[2]
            co_ref[b] = ubuf[b, pad + t_new - (CONV_WIDTH - 1):pad + t_new, :]
        return conv_buf[...]

    x = x_ref[...]
    hb, qkv = _qkv_proj(x, lng_ref, win_ref)
    sig_a, sig_c = _merge_gates(hb, win_ref, d_model)
    q_cols, kr, v = _qk_norm_rope(qkv, gq_ref, gk_ref, bdq_ref, bdk_ref, tab[0], tab[1])

    qi = lax.broadcasted_iota(jnp.int32, (2 * t_new, n_keys), 0)
    q_pos = PAST_LEN + lax.rem(qi, t_new)
    k_pos = PAST_LEN - cache_len + lax.broadcasted_iota(jnp.int32, (2 * t_new, n_keys), 1)
    q_ch = q_pos // CHUNK
    k_ch = k_pos // CHUNK
    bad = jnp.logical_not((k_ch <= q_ch) & (k_ch >= q_ch - WINDOW_CHUNKS))

    blocks = [(b, vh) for b in range(bb) for vh in range(N_KV_HEADS)]
    s_e, s_o, vvars = [], [], []
    kr_t = kr.T
    v_t = v.T
    for b in range(bb):
        rows = slice(b * t_new, (b + 1) * t_new)
        kcat = jnp.concatenate([ck_ref[b], kr_t[:, rows]], axis=1)
        vcat = jnp.concatenate([cv_ref[b], v_t[:, rows]], axis=1)
        ko_ref[b] = kcat[:, n_keys - cache_len:n_keys]
        vo_ref[b] = vcat[:, n_keys - cache_len:n_keys]
        kvar = _kv_variants_t(kcat)
        vvars.append(_kv_variants_t(vcat))
        for vh in range(N_KV_HEADS):
            qs = jnp.concatenate([q_cols[2 * vh][rows], q_cols[2 * vh + 1][rows]], axis=0)
            se, so = _scores(qs, kvar[2 * vh], kvar[2 * vh + 1], bad, keys_on_lanes=True)
            s_e.append(se)
            s_o.append(so)

    yc = _conv_branch(hb, win_ref, wb_ref, conv_fn)

    sink_e, sink_o = _sink_cols(sink_ref, len(blocks), t_new)
    p_e, r_e = _softmax_parts(jnp.concatenate(s_e, axis=0), sink_e)
    p_o, r_o = _softmax_parts(jnp.concatenate(s_o, axis=0), sink_o)

    for n, (b, vh) in enumerate(blocks):
        rows = slice(b * t_new, (b + 1) * t_new)
        br = slice(n * 2 * t_new, (n + 1) * 2 * t_new)
        o = _pv(p_e[br], p_o[br], vvars[b][2 * vh], vvars[b][2 * vh + 1], r_e[br], r_o[br],
                keys_on_lanes=True)
        attn_buf[rows, (2 * vh) * V7X_LANES:(2 * vh + 1) * V7X_LANES] = o[0:t_new]
        attn_buf[rows, (2 * vh + 1) * V7X_LANES:(2 * vh + 2) * V7X_LANES] = o[t_new:2 * t_new]

    y_ref[...] = _finish(x, p_ref[...], hb, attn_buf[...], yc, sig_a, sig_c, win_ref,
                         wa_ref, wo_ref, wpg_ref, wpp_ref)


def _const_spec(shape):
    nd = len(shape)
    return pl.BlockSpec(shape, lambda i: (0,) * nd, pipeline_mode=pl.Buffered(1))


def _operand_specs(c, keys, staged):
    return [pl.BlockSpec(memory_space=pl.ANY) if staged and k in _STAGED_KEYS
            else _const_spec(c[k].shape) for k in keys]


def _staging_scratch(c):
    return ([pltpu.VMEM(c[k].shape, BF16) for k in _STAGED_KEYS]
            + [pltpu.VMEM((STAGE_SLOTS, STAGE_ROWS, STAGE_COLS), F32),
               pltpu.SemaphoreType.DMA((STAGE_SLOTS,)),
               pltpu.SemaphoreType.DMA((len(_STAGED_KEYS),))])


def _smem_spec():
    return pl.BlockSpec(memory_space=pltpu.SMEM)


def _block_diag_mean(width):
    idx = np.arange(width) // HEAD_DIM
    return jnp.asarray((idx[:, None] == idx[None, :]).astype(np.float32) / HEAD_DIM, dtype=BF16)


def _layer_consts(ln_g, w_in, q_norm_g, k_norm_g, sink, conv_w, w_attn_out, w_conv_out, w_o,
                  w_ple_gate, w_ple_proj):
    inv_freq = ROPE_THETA ** (-jnp.arange(0, HALF, dtype=F32) * 2.0 / HEAD_DIM)
    return dict(
        lng=ln_g.reshape(1, -1).astype(F32),
        win=w_in.astype(F32),
        gq=q_norm_g.astype(F32).reshape(1, HEAD_DIM),
        gk=k_norm_g.astype(F32).reshape(1, HEAD_DIM),
        invf=jnp.tile(inv_freq, V7X_LANES // HALF).reshape(1, V7X_LANES),
        sink=sink.astype(F32),
        cw=conv_w.astype(F32),
        wa=w_attn_out.astype(F32),
        wb=w_conv_out.astype(F32),
        wo=w_o.astype(F32),
        wpg=w_ple_gate.astype(F32),
        wpp=w_ple_proj.astype(F32),
        bdq=_block_diag_mean(ATTN_DIM // 2),
        bdk=_block_diag_mean(KV_DIM),
    )


_VEC_KEYS = ("lng", "win", "gq", "gk", "invf")
_MAT_KEYS = ("cw", "wa", "wb", "wo", "wpg", "wpp", "bdq", "bdk")
_STAGED_KEYS = ("win", "wa", "wb", "wo", "wpg", "wpp")


def _heads_first(t):
    t = jnp.moveaxis(t, -3, -1)
    return t.reshape(t.shape[:-3] + (KV_DIM, t.shape[-1]))


def _heads_last(t):
    t = t.reshape(t.shape[:-2] + (N_KV_HEADS, HEAD_DIM, t.shape[-1]))
    return jnp.moveaxis(t, -1, -3)


def _prompt_layer(x, p, c):
    t, d = x.shape
    tm = PROMPT_TILE
    assert t % tm == 0 and tm % PROMPT_SUBTILE == 0
    assert PROMPT_SUBTILE % CHUNK == 0 and PROMPT_SUBTILE >= WINDOW
    pre = [c[k] for k in _VEC_KEYS]
    post = [c[k] for k in _MAT_KEYS]
    in_specs = ([pl.BlockSpec((tm, d), lambda i: (i, 0)),
                 pl.BlockSpec((tm, p.shape[1]), lambda i: (i, 0))]
                + _operand_specs(c, _VEC_KEYS, True) + [_smem_spec()]
                + _operand_specs(c, _MAT_KEYS, True))
    out_shape = ((jax.ShapeDtypeStruct((t, d), F32),
                  jax.ShapeDtypeStruct((KV_DIM, WINDOW), F32),
                  jax.ShapeDtypeStruct((KV_DIM, WINDOW), F32),
                  jax.ShapeDtypeStruct((CONV_WIDTH - 1, CONV_DIM), F32))
                 + tuple(jax.ShapeDtypeStruct(c[k].shape, BF16) for k in _STAGED_KEYS))
    out_specs = ((pl.BlockSpec((tm, d), lambda i: (i, 0)),
                  pl.BlockSpec((KV_DIM, WINDOW), lambda i: (0, 0)),
                  pl.BlockSpec((KV_DIM, WINDOW), lambda i: (0, 0)),
                  pl.BlockSpec((CONV_WIDTH - 1, CONV_DIM), lambda i: (0, 0)))
                 + tuple(pl.BlockSpec(memory_space=pl.ANY) for _ in _STAGED_KEYS))
    scratch = [pltpu.VMEM((4, KV_DIM, WINDOW + tm), BF16),
               pltpu.VMEM((4, WINDOW + tm, V7X_LANES), BF16),
               pltpu.VMEM((V7X_SUBLANES + tm, CONV_DIM), F32),
               pltpu.VMEM((tm, ATTN_DIM), F32),
               pltpu.VMEM((4, tm, V7X_LANES), F32)] + _staging_scratch(c)
    outs = pl.pallas_call(
        _prompt_kernel,
        grid=(t // tm,),
        in_specs=in_specs,
        out_specs=out_specs,
        out_shape=out_shape,
        scratch_shapes=scratch,
        compiler_params=pltpu.CompilerParams(dimension_semantics=("arbitrary",),
                                             vmem_limit_bytes=V7X_VMEM_LIMIT_BYTES),
        name="prompt_layer",
    )(x, p, *pre, c["sink"], *post)
    y, ko, vo, co = outs[:4]
    return (y, _heads_last(ko), _heads_last(vo), co), dict(c, **dict(zip(_STAGED_KEYS, outs[4:])))


def _sample_layer(x, p, cache_k, cache_v, state_conv, c):
    nb, t_new, d = x.shape
    cache_len = cache_k.shape[1]
    bb = SAMPLE_BATCH_TILE
    assert nb % bb == 0 and t_new >= CONV_WIDTH - 1 and t_new <= cache_len
    rows = bb * t_new
    x2 = x.reshape(nb * t_new, d)
    p2 = p.reshape(nb * t_new, p.shape[-1])
    ck = _heads_first(cache_k)
    cv = _heads_first(cache_v)
    pre = [c[k] for k in _VEC_KEYS]
    post = [c[k] for k in _MAT_KEYS]
    in_specs = ([pl.BlockSpec((rows, d), lambda i: (i, 0)),
                 pl.BlockSpec((rows, p2.shape[1]), lambda i: (i, 0)),
                 pl.BlockSpec((bb, KV_DIM, cache_len), lambda i: (i, 0, 0)),
                 pl.BlockSpec((bb, KV_DIM, cache_len), lambda i: (i, 0, 0)),
                 pl.BlockSpec((bb, CONV_WIDTH - 1, CONV_DIM), lambda i: (i, 0, 0))]
                + _operand_specs(c, _VEC_KEYS, False) + [_smem_spec()]
                + _operand_specs(c, _MAT_KEYS, False))
    out_shape = (jax.ShapeDtypeStruct((nb * t_new, d), F32),
                 jax.ShapeDtypeStruct((nb, KV_DIM, cache_len), F32),
                 jax.ShapeDtypeStruct((nb, KV_DIM, cache_len), F32),
                 jax.ShapeDtypeStruct((nb, CONV_WIDTH - 1, CONV_DIM), F32))
    out_specs = (pl.BlockSpec((rows, d), lambda i: (i, 0)),
                 pl.BlockSpec((bb, KV_DIM, cache_len), lambda i: (i, 0, 0)),
                 pl.BlockSpec((bb, KV_DIM, cache_len), lambda i: (i, 0, 0)),
                 pl.BlockSpec((bb, CONV_WIDTH - 1, CONV_DIM), lambda i: (i, 0, 0)))
    scratch = [pltpu.VMEM((bb, V7X_SUBLANES + t_new, CONV_DIM), F32),
               pltpu.VMEM((rows, ATTN_DIM), F32),
               pltpu.VMEM((rows, CONV_DIM), F32),
               pltpu.VMEM((2, rows, V7X_LANES), F32)]
    y, ko, vo, co = pl.pallas_call(
        _sample_kernel,
        grid=(nb // bb,),
        in_specs=in_specs,
        out_specs=out_specs,
        out_shape=out_shape,
        scratch_shapes=scratch,
        compiler_params=pltpu.CompilerParams(dimension_semantics=("arbitrary",),
                                             vmem_limit_bytes=V7X_VMEM_LIMIT_BYTES),
        name="sample_layer",
    )(x2, p2, ck, cv, state_conv, *pre, c["sink"], *post)
    return y.reshape(nb, t_new, d), _heads_last(ko), _heads_last(vo), co


def kernel(x_prompt, x_sample, p_prompt, p_sample, cache_k, cache_v, state_conv, ln_g, w_in,
           q_norm_g, k_norm_g, sink, conv_w, w_attn_out, w_conv_out, w_o, w_ple_gate, w_ple_proj):
    depth = ln_g.shape[0]
    batch = x_prompt.shape[0]
    hp, hs = x_prompt, x_sample
    kp_l, vp_l, cp_l, ks_l, vs_l, cs_l = [], [], [], [], [], []
    for i in range(depth):
        c = _layer_consts(ln_g[i], w_in[i], q_norm_g[i], k_norm_g[i], sink[i],
                          jnp.swapaxes(conv_w, 0, 1)[:, i:i + 1, :],
                          w_attn_out[i], w_conv_out[i], w_o[i], w_ple_gate[i], w_ple_proj[i])
        ys, kps, vps, cps = [], [], [], []
        for b in range(batch):
            (y, ko, vo, co), c_bf16 = _prompt_layer(hp[b], p_prompt[i, b], c)
            ys.append(y)
            kps.append(ko)
            vps.append(vo)
            cps.append(co)
        hp = jnp.stack(ys)
        kp_l.append(jnp.stack(kps))
        vp_l.append(jnp.stack(vps))
        cp_l.append(jnp.stack(cps))
        hs, ko, vo, co = _sample_layer(hs, p_sample[i], cache_k[i], cache_v[i], state_conv[i], c_bf16)
        ks_l.append(ko)
        vs_l.append(vo)
        cs_l.append(co)
    return (hp, hs, jnp.stack(kp_l), jnp.stack(vp_l), jnp.stack(cp_l),
            jnp.stack(ks_l), jnp.stack(vs_l), jnp.stack(cs_l))
```

```python
import functools

import numpy as np
import jax
import jax.numpy as jnp
from jax import lax
from jax.experimental import pallas as pl
from jax.experimental.pallas import tpu as pltpu

F32 = jnp.float32
BF16 = jnp.bfloat16

CHUNK = 64
WINDOW = 128
WINDOW_CHUNKS = WINDOW // CHUNK
N_HEADS = 8
N_KV_HEADS = 2
GROUP_HEADS = N_HEADS // N_KV_HEADS
HEAD_DIM = 64
HALF = HEAD_DIM // 2
ATTN_DIM = N_HEADS * HEAD_DIM
KV_DIM = N_KV_HEADS * HEAD_DIM
CONV_DIM = 512
CONV_WIDTH = 3
PAST_LEN = 1024
ROPE_THETA = 10000.0
EPS = 1e-6
NEG = -1e30

OFF_Q = 0
OFF_K = OFF_Q + ATTN_DIM
OFF_V = OFF_K + KV_DIM
OFF_GA = OFF_V + KV_DIM
OFF_B = OFF_GA + ATTN_DIM
OFF_GC_END = OFF_B + 4 * CONV_DIM
OFF_MA = OFF_GC_END

V7X_LANES = 128
V7X_SUBLANES = 8
V7X_VMEM_LIMIT_BYTES = 58 * 1024 * 1024

PROMPT_TILE = 512
PROMPT_SUBTILE = 256
PROMPT_STAGE_SKEW = 1
STAGE_ROWS = 256
STAGE_COLS = 1024
STAGE_SLOTS = 4
SAMPLE_BATCH_TILE = 8


def _mm(a, w):
    return jnp.dot(a, w, preferred_element_type=F32)


def _mm_t(a, b):
    return lax.dot_general(a, b, (((1,), (1,)), ((), ())), preferred_element_type=F32)


def _sigmoid(x):
    return 1.0 / (1.0 + jnp.exp(-x))


def _silu(x):
    return x * _sigmoid(x)


def _rmsnorm(x, g):
    ms = jnp.mean(x * x, axis=-1, keepdims=True)
    return x * lax.rsqrt(ms + EPS) * g


def _group_mean(t, bd):
    return _mm(t.astype(BF16), bd)


def _head_norm(t, bd, g):
    ms = _group_mean(t * t, bd)
    return t * lax.rsqrt(ms + EPS) * g


def _rope(xc, cos_t, sin_s):
    lane = lax.broadcasted_iota(jnp.int32, xc.shape, 1)
    upper = (lane & HALF) != 0
    rot = jnp.where(upper, pltpu.roll(xc, HALF, 1), pltpu.roll(xc, V7X_LANES - HALF, 1))
    return xc * cos_t + rot * sin_s


def _rope_sign(shape):
    lane = lax.broadcasted_iota(jnp.int32, shape, 1)
    return jnp.where((lane & HALF) != 0, 1.0, -1.0).astype(F32)


def _kv_variants(t):
    lane = lax.broadcasted_iota(jnp.int32, t.shape, 1)
    lo = lane < HEAD_DIM
    sw = pltpu.roll(t, HEAD_DIM, 1)
    zero = jnp.zeros_like(t)
    return (jnp.where(lo, t, zero).astype(BF16), jnp.where(lo, zero, sw).astype(BF16),
            jnp.where(lo, sw, zero).astype(BF16), jnp.where(lo, zero, t).astype(BF16))


def _kv_variants_t(t):
    tb = t.astype(BF16)
    h0, h1 = tb[0:HEAD_DIM], tb[HEAD_DIM:2 * HEAD_DIM]
    zero = jnp.zeros_like(h0)
    return (jnp.concatenate([h0, zero], axis=0), jnp.concatenate([zero, h0], axis=0),
            jnp.concatenate([h1, zero], axis=0), jnp.concatenate([zero, h1], axis=0))


def _scores(qs, ka, kb, bad, keys_on_lanes=False):
    dot = _mm if keys_on_lanes else _mm_t
    s_e = dot(qs, ka)
    s_o = dot(qs, kb)
    if bad is not None:
        s_e = jnp.where(bad, NEG, s_e)
        s_o = jnp.where(bad, NEG, s_o)
    return s_e, s_o


def _softmax_parts(s, sink):
    m = jnp.maximum(jnp.max(s, axis=-1, keepdims=True), sink)
    e = jnp.exp(s - m)
    r = 1.0 / (jnp.sum(e, axis=-1, keepdims=True) + jnp.exp(sink - m))
    return e.astype(BF16), r


def _pv(p_e, p_o, va, vb, r_e, r_o, keys_on_lanes=False):
    dot = _mm_t if keys_on_lanes else _mm
    o = dot(p_e, va) + dot(p_o, vb)
    lane = lax.broadcasted_iota(jnp.int32, o.shape, 1)
    return o * jnp.where(lane < HEAD_DIM, r_e, r_o)


def _sink_cols(sink_ref, n_blocks, rows_per_pair):
    row = lax.broadcasted_iota(jnp.int32, (n_blocks * 2 * rows_per_pair, 1), 0)
    second_pair = (row // rows_per_pair) % 2 == 1
    second_kv = (row // (2 * rows_per_pair)) % 2 == 1

    def pick(odd):
        kv0 = jnp.where(second_pair, sink_ref[2 + odd], sink_ref[odd])
        kv1 = jnp.where(second_pair, sink_ref[GROUP_HEADS + 2 + odd], sink_ref[GROUP_HEADS + odd])
        return jnp.where(second_kv, kv1, kv0)

    return pick(0), pick(1)


def _stage_chunks(shape):
    rows, cols = shape
    assert rows % STAGE_ROWS == 0 and cols % V7X_LANES == 0
    width = max(c for c in range(V7X_LANES, STAGE_COLS + 1, V7X_LANES) if cols % c == 0)
    return [(r0, c0, STAGE_ROWS, width)
            for r0 in range(0, rows, STAGE_ROWS) for c0 in range(0, cols, width)]


def _stage_weights(pairs, stage, sem):
    chunks = [(src, dst) + ch for src, dst in pairs for ch in _stage_chunks(src.shape)]

    def copy(n):
        src, _, r0, c0, rows, cols = chunks[n]
        slot = n % STAGE_SLOTS
        return pltpu.make_async_copy(src.at[pl.ds(r0, rows), pl.ds(c0, cols)],
                                     stage.at[slot, pl.ds(0, rows), pl.ds(0, cols)],
                                     sem.at[slot])

    for n in range(min(STAGE_SLOTS, len(chunks))):
        copy(n).start()
    for n, (_, dst, r0, c0, rows, cols) in enumerate(chunks):
        copy(n).wait()
        dst[r0:r0 + rows, c0:c0 + cols] = stage[n % STAGE_SLOTS, 0:rows, 0:cols].astype(BF16)
        if n + STAGE_SLOTS < len(chunks):
            copy(n + STAGE_SLOTS).start()


def _qkv_proj(x, lng_ref, win_ref):
    hb = _rmsnorm(x, lng_ref[...]).astype(BF16)
    return hb, _mm(hb, win_ref[:, OFF_Q:OFF_GA])


def _tile_lanes(g, width):
    while g.shape[1] < width:
        g = jnp.concatenate([g, g], axis=1)
    return g


def _qk_norm_rope(qkv, gq_ref, gk_ref, bdq_ref, bdk_ref, cos_t, sin_s):
    half_q = ATTN_DIM // 2
    gq = _tile_lanes(gq_ref[...] * (HEAD_DIM ** -0.5), half_q)
    gk = _tile_lanes(gk_ref[...], KV_DIM)
    q_cols = []
    for j in range(2):
        t = qkv[:, j * half_q:(j + 1) * half_q]
        tn = _head_norm(t, bdq_ref[...], gq)
        for c in range(half_q // V7X_LANES):
            q_cols.append(_rope(tn[:, c * V7X_LANES:(c + 1) * V7X_LANES], cos_t, sin_s).astype(BF16))
    kn = _head_norm(qkv[:, OFF_K:OFF_V], bdk_ref[...], gk)
    kr = _rope(kn, cos_t, sin_s)
    v = qkv[:, OFF_V:OFF_GA]
    return q_cols, kr, v


def _conv_proj(hb, win_ref):
    return _mm(hb, win_ref[:, OFF_B:OFF_GC_END])


def _conv_input(bcug, conv_fn):
    b_gate = bcug[:, 0:CONV_DIM]
    u = bcug[:, CONV_DIM:2 * CONV_DIM] * bcug[:, 2 * CONV_DIM:3 * CONV_DIM]
    gate_c = bcug[:, 3 * CONV_DIM:4 * CONV_DIM]
    conv = conv_fn(u)
    return (b_gate * conv * _silu(gate_c)).astype(BF16)


def _prompt_body(i, x_ref, p_ref, lng_ref, gq_ref, gk_ref, invf_ref, sink_ref, cw_ref,
                 bdq_ref, bdk_ref, win_ref, wa_ref, wb_ref, wo_ref, wpg_ref, wpp_ref,
                 y_ref, ko_ref, vo_ref, co_ref, kbuf, vbuf, ubuf, attn_buf, tab):
    tm, d_model = x_ref.shape
    invf = invf_ref[...]

    @pl.when(i == 0)
    def _init():
        kbuf[:, :, 0:WINDOW] = jnp.zeros((4, KV_DIM, WINDOW), BF16)
        vbuf[:, 0:WINDOW, :] = jnp.zeros((4, WINDOW, V7X_LANES), BF16)
        ubuf[0:V7X_SUBLANES, :] = jnp.zeros((V7X_SUBLANES, CONV_DIM), F32)
        r = lax.broadcasted_iota(jnp.int32, (tm, V7X_LANES), 0).astype(F32)
        ang = r * invf
        sgn = _rope_sign((tm, V7X_LANES))
        c_r = jnp.cos(ang)
        s_r = jnp.sin(ang)
        tab[0] = c_r
        tab[1] = s_r
        tab[2] = c_r * sgn
        tab[3] = s_r * sgn

    base = (i * tm).astype(F32) * invf
    cb = jnp.cos(base)
    sb = jnp.sin(base)
    cos_t = tab[0] * cb - tab[1] * sb
    sin_s = tab[3] * cb + tab[2] * sb

    sub = PROMPT_SUBTILE
    n_sub = tm // sub
    sub_chunks = sub // CHUNK
    assert 2 * CHUNK == V7X_LANES and WINDOW == V7X_LANES
    n_keys = 2 * V7X_LANES

    def key_window_start(r0, c):
        return ((r0 + c * CHUNK) // V7X_LANES) * V7X_LANES

    def mask_window(sc, r0, c):
        lo, hi = sc[:, 0:V7X_LANES], sc[:, V7X_LANES:n_keys]
        lane = lax.broadcasted_iota(jnp.int32, lo.shape, 1)
        no_carry = (i == 0) if key_window_start(r0, c) < WINDOW else None
        if ((r0 + c * CHUNK) // CHUNK) % 2 == 0:
            hi = jnp.where(lane >= CHUNK, NEG, hi)
            if no_carry is not None:
                lo = jnp.where(no_carry, NEG, lo)
        else:
            bad = lane < CHUNK
            lo = jnp.where(bad if no_carry is None else bad | no_carry, NEG, lo)
        return jnp.concatenate([lo, hi], axis=1)

    pad = V7X_SUBLANES
    blocks = [(c, vh) for c in range(sub_chunks) for vh in range(N_KV_HEADS)]
    st = [dict(r0=s * sub) for s in range(n_sub)]

    def stage_rms(s):
        s["x"] = x_ref[s["r0"]:s["r0"] + sub, :]
        s["hb"] = _rmsnorm(s["x"], lng_ref[...]).astype(BF16)

    def stage_qkv(s):
        s["qkv"] = _mm(s["hb"], win_ref[:, OFF_Q:OFF_GA])

    def stage_gate_a(s):
        s["sig_a"] = _sigmoid(_mm(s["hb"], win_ref[:, OFF_MA:OFF_MA + d_model]))

    def stage_gate_c(s):
        s["sig_c"] = _sigmoid(_mm(s["hb"], win_ref[:, OFF_MA + d_model:OFF_MA + 2 * d_model]))

    def stage_qk_norm(s):
        r0 = s["r0"]
        s["q_cols"], kr, v = _qk_norm_rope(s.pop("qkv"), gq_ref, gk_ref, bdq_ref, bdk_ref,
                                           cos_t[r0:r0 + sub], sin_s[r0:r0 + sub])
        kr_t = kr.T
        for n, t in enumerate(_kv_variants_t(kr_t)):
            kbuf[n, :, WINDOW + r0:WINDOW + r0 + sub] = t
        for n, t in enumerate(_kv_variants(v)):
            vbuf[n, WINDOW + r0:WINDOW + r0 + sub, :] = t
        if r0 + sub == tm:
            ko_ref[...] = kr_t[:, sub - WINDOW:sub]
            vo_ref[...] = v[sub - WINDOW:sub, :].T

    def stage_scores(s):
        r0 = s["r0"]
        s_e, s_o = [], []
        for c, vh in blocks:
            rows = slice(c * CHUNK, (c + 1) * CHUNK)
            start = key_window_start(r0, c)
            win = slice(start, start + n_keys)
            qs = jnp.concatenate([s["q_cols"][2 * vh][rows], s["q_cols"][2 * vh + 1][rows]], axis=0)
            se, so = _scores(qs, kbuf[2 * vh, :, win], kbuf[2 * vh + 1, :, win], None,
                             keys_on_lanes=True)
            s_e.append(mask_window(se, r0, c))
            s_o.append(mask_window(so, r0, c))
        s["s_e"] = jnp.concatenate(s_e, axis=0)
        s["s_o"] = jnp.concatenate(s_o, axis=0)
        del s["q_cols"]

    def stage_conv_proj(s):
        s["bcug"] = _conv_proj(s["hb"], win_ref)

    def stage_conv(s):
        r0 = s["r0"]

        def conv_fn(u):
            ubuf[pad + r0:pad + r0 + sub, :] = u
            conv = ubuf[pad + r0 - 2:pad + r0 - 2 + sub, :] * cw_ref[0]
            conv = conv + ubuf[pad + r0 - 1:pad + r0 - 1 + sub, :] * cw_ref[1]
            return conv + u * cw_ref[2]

        s["c_in"] = _conv_input(s.pop("bcug"), conv_fn)

    def stage_conv_out(s):
        s["yc"] = _mm(s.pop("c_in"), wb_ref[...])

    sink_e, sink_o = _sink_cols(sink_ref, len(blocks), CHUNK)

    def stage_softmax(s):
        s["p_e"], s["r_e"] = _softmax_parts(s.pop("s_e"), sink_e)
        s["p_o"], s["r_o"] = _softmax_parts(s.pop("s_o"), sink_o)

    def stage_pv(s):
        r0 = s["r0"]
        for n, (c, vh) in enumerate(blocks):
            rows = slice(r0 + c * CHUNK, r0 + (c + 1) * CHUNK)
            start = key_window_start(r0, c)
            win = slice(start, start + n_keys)
            br = slice(n * 2 * CHUNK, (n + 1) * 2 * CHUNK)
            o = _pv(s["p_e"][br], s["p_o"][br], vbuf[2 * vh, win, :], vbuf[2 * vh + 1, win, :],
                    s["r_e"][br], s["r_o"][br])
            attn_buf[rows, (2 * vh) * V7X_LANES:(2 * vh + 1) * V7X_LANES] = o[0:CHUNK]
            attn_buf[rows, (2 * vh + 1) * V7X_LANES:(2 * vh + 2) * V7X_LANES] = o[CHUNK:2 * CHUNK]

    def stage_attn_gate(s):
        s["silu_ga"] = _silu(_mm(s["hb"], win_ref[:, OFF_GA:OFF_B]))
        s["pp"] = _mm(p_ref[s["r0"]:s["r0"] + sub, :].astype(BF16), wpp_ref[...])

    def stage_attn_out(s):
        r0 = s["r0"]
        a_in = (attn_buf[r0:r0 + sub, :] * s.pop("silu_ga")).astype(BF16)
        ya = _mm(a_in, wa_ref[...])
        s["mix"] = (s.pop("sig_a") * ya + s.pop("sig_c") * s.pop("yc")).astype(BF16)

    def stage_out_proj(s):
        s["r"] = s.pop("x") + _mm(s.pop("mix"), wo_ref[...])

    def stage_ple(s):
        r0 = s["r0"]
        r = s.pop("r")
        gate = _sigmoid(_mm(r.astype(BF16), wpg_ref[...]))
        y_ref[r0:r0 + sub, :] = r + gate * s.pop("pp")

    stages = [stage_rms, stage_qkv, stage_qk_norm, stage_gate_a, stage_scores, stage_gate_c,
              stage_softmax, stage_conv_proj, stage_conv, stage_attn_gate, stage_conv_out,
              stage_pv, stage_attn_out, stage_out_proj, stage_ple]
    for t in range(len(stages) + PROMPT_STAGE_SKEW * (n_sub - 1)):
        for j, s in enumerate(st):
            k = t - j * PROMPT_STAGE_SKEW
            if 0 <= k < len(stages):
                stages[k](s)

    co_ref[...] = ubuf[pad + tm - 2:pad + tm, :]
    ubuf[0:pad, :] = ubuf[tm:tm + pad, :]
    kbuf[:, :, 0:WINDOW] = kbuf[:, :, tm:tm + WINDOW]
    vbuf[:, 0:WINDOW, :] = vbuf[:, tm:tm + WINDOW, :]


def _sample_body(i, x_ref, p_ref, ck_ref, cv_ref, sc_ref, lng_ref, gq_ref, gk_ref, invf_ref,
                 sink_ref, cw_ref, bdq_ref, bdk_ref, win_ref, wa_ref, wb_ref, wo_ref, wpg_ref,
                 wpp_ref, y_ref, ko_ref, vo_ref, co_ref, ubuf, attn_buf, conv_buf, tab):
    bb, _, cache_len = ck_ref.shape
    rows_total, d_model = x_ref.shape
    t_new = rows_total // bb
    n_keys = cache_len + t_new

    @pl.when(i == 0)
    def _init():
        r = lax.broadcasted_iota(jnp.int32, (rows_total, V7X_LANES), 0)
        pos = (PAST_LEN + lax.rem(r, t_new)).astype(F32)
        ang = pos * invf_ref[...]
        tab[0] = jnp.cos(ang)
        tab[1] = jnp.sin(ang) * _rope_sign((rows_total, V7X_LANES))

    def conv_fn(u):
        pad = V7X_SUBLANES
        for b in range(bb):
            rows = slice(b * t_new, (b + 1) * t_new)
            ub = u[rows]
            ubuf[b, pad - (CONV_WIDTH - 1):pad, :] = sc_ref[b]
            ubuf[b, pad:pad + t_new, :] = ub
            conv = ubuf[b, pad - 2:pad - 2 + t_new, :] * cw_ref[0]
            conv = conv + ubuf[b, pad - 1:pad - 1 + t_new, :] * cw_ref[1]
            conv_buf[rows, :] = conv + ub * cw_ref[2]
            co_ref[b] = ubuf[b, pad + t_new - (CONV_WIDTH - 1):pad + t_new, :]
        return conv_buf[...]

    x = x_ref[...]
    hb, qkv = _qkv_proj(x, lng_ref, win_ref)
    sig_a = _sigmoid(_mm(hb, win_ref[:, OFF_MA:OFF_MA + d_model]))
    q_cols, kr, v = _qk_norm_rope(qkv, gq_ref, gk_ref, bdq_ref, bdk_ref, tab[0], tab[1])

    qi = lax.broadcasted_iota(jnp.int32, (2 * t_new, n_keys), 0)
    q_pos = PAST_LEN + lax.rem(qi, t_new)
    k_pos = PAST_LEN - cache_len + lax.broadcasted_iota(jnp.int32, (2 * t_new, n_keys), 1)
    q_ch = q_pos // CHUNK
    k_ch = k_pos // CHUNK
    bad = jnp.logical_not((k_ch <= q_ch) & (k_ch >= q_ch - WINDOW_CHUNKS))

    blocks = [(b, vh) for b in range(bb) for vh in range(N_KV_HEADS)]
    s_e, s_o, vvars = [], [], []
    kr_t = kr.T
    v_t = v.T
    for b in range(bb):
        rows = slice(b * t_new, (b + 1) * t_new)
        kcat = jnp.concatenate([ck_ref[b], kr_t[:, rows]], axis=1)
        vcat = jnp.concatenate([cv_ref[b], v_t[:, rows]], axis=1)
        ko_ref[b] = kcat[:, n_keys - cache_len:n_keys]
        vo_ref[b] = vcat[:, n_keys - cache_len:n_keys]
        kvar = _kv_variants_t(kcat)
        vvars.append(_kv_variants_t(vcat))
        for vh in range(N_KV_HEADS):
            qs = jnp.concatenate([q_cols[2 * vh][rows], q_cols[2 * vh + 1][rows]], axis=0)
            se, so = _scores(qs, kvar[2 * vh], kvar[2 * vh + 1], bad, keys_on_lanes=True)
            s_e.append(se)
            s_o.append(so)

    sig_c = _sigmoid(_mm(hb, win_ref[:, OFF_MA + d_model:OFF_MA + 2 * d_model]))
    sink_e, sink_o = _sink_cols(sink_ref, len(blocks), t_new)
    p_e, r_e = _softmax_parts(jnp.concatenate(s_e, axis=0), sink_e)
    bcug = _conv_proj(hb, win_ref)
    p_o, r_o = _softmax_parts(jnp.concatenate(s_o, axis=0), sink_o)
    c_in = _conv_input(bcug, conv_fn)
    silu_ga = _silu(_mm(hb, win_ref[:, OFF_GA:OFF_B]))
    pp = _mm(p_ref[...].astype(BF16), wpp_ref[...])
    yc = _mm(c_in, wb_ref[...])

    for n, (b, vh) in enumerate(blocks):
        rows = slice(b * t_new, (b + 1) * t_new)
        br = slice(n * 2 * t_new, (n + 1) * 2 * t_new)
        o = _pv(p_e[br], p_o[br], vvars[b][2 * vh], vvars[b][2 * vh + 1], r_e[br], r_o[br],
                keys_on_lanes=True)
        attn_buf[rows, (2 * vh) * V7X_LANES:(2 * vh + 1) * V7X_LANES] = o[0:t_new]
        attn_buf[rows, (2 * vh + 1) * V7X_LANES:(2 * vh + 2) * V7X_LANES] = o[t_new:2 * t_new]

    ya = _mm((attn_buf[0:rows_total, :] * silu_ga).astype(BF16), wa_ref[...])
    mix = (sig_a * ya + sig_c * yc).astype(BF16)
    r = x + _mm(mix, wo_ref[...])
    gate = _sigmoid(_mm(r.astype(BF16), wpg_ref[...]))
    y_ref[...] = r + gate * pp


def _layer_kernel(n_prompt_steps, *refs):
    (xp_ref, pp_ref, xs_ref, ps_ref, ck_ref, cv_ref, sc_ref,
     lng_ref, gq_ref, gk_ref, invf_ref, sink_ref, cw_ref, bdq_ref, bdk_ref,
     win_hbm, wa_hbm, wb_hbm, wo_hbm, wpg_hbm, wpp_hbm,
     yp_ref, kpo_ref, vpo_ref, cpo_ref, ys_ref, kso_ref, vso_ref, cso_ref,
     kbuf, vbuf, ubuf_p, attn_buf, tab_p, ubuf_s, conv_buf, tab_s,
     win_ref, wa_ref, wb_ref, wo_ref, wpg_ref, wpp_ref, stage, sem) = refs
    i = pl.program_id(0)
    small = (lng_ref, gq_ref, gk_ref, invf_ref, sink_ref, cw_ref, bdq_ref, bdk_ref)
    weights = (win_ref, wa_ref, wb_ref, wo_ref, wpg_ref, wpp_ref)

    @pl.when(i == 0)
    def _stage():
        _stage_weights([(win_hbm, win_ref), (wa_hbm, wa_ref), (wb_hbm, wb_ref), (wo_hbm, wo_ref),
                        (wpg_hbm, wpg_ref), (wpp_hbm, wpp_ref)], stage, sem)

    @pl.when(i < n_prompt_steps)
    def _prompt():
        _prompt_body(i, xp_ref, pp_ref, *small, *weights, yp_ref, kpo_ref, vpo_ref, cpo_ref,
                     kbuf, vbuf, ubuf_p, attn_buf, tab_p)

    @pl.when(i >= n_prompt_steps)
    def _sample():
        _sample_body(i - n_prompt_steps, xs_ref, ps_ref, ck_ref, cv_ref, sc_ref, *small, *weights,
                     ys_ref, kso_ref, vso_ref, cso_ref, ubuf_s, attn_buf, conv_buf, tab_s)


def _const_spec(shape):
    nd = len(shape)
    return pl.BlockSpec(shape, lambda i: (0,) * nd, pipeline_mode=pl.Buffered(1))


def _smem_spec():
    return pl.BlockSpec(memory_space=pltpu.SMEM)


def _block_diag_mean(width):
    idx = np.arange(width) // HEAD_DIM
    return jnp.asarray((idx[:, None] == idx[None, :]).astype(np.float32) / HEAD_DIM, dtype=BF16)


def _layer_consts(ln_g, w_in, q_norm_g, k_norm_g, sink, conv_w, w_attn_out, w_conv_out, w_o,
                  w_ple_gate, w_ple_proj):
    inv_freq = ROPE_THETA ** (-jnp.arange(0, HALF, dtype=F32) * 2.0 / HEAD_DIM)
    return dict(
        lng=ln_g.reshape(1, -1).astype(F32),
        gq=q_norm_g.astype(F32).reshape(1, HEAD_DIM),
        gk=k_norm_g.astype(F32).reshape(1, HEAD_DIM),
        invf=jnp.tile(inv_freq, V7X_LANES // HALF).reshape(1, V7X_LANES),
        sink=sink.astype(F32),
        cw=conv_w.astype(F32),
        bdq=_block_diag_mean(ATTN_DIM // 2),
        bdk=_block_diag_mean(KV_DIM),
        win=w_in.astype(F32),
        wa=w_attn_out.astype(F32),
        wb=w_conv_out.astype(F32),
        wo=w_o.astype(F32),
        wpg=w_ple_gate.astype(F32),
        wpp=w_ple_proj.astype(F32),
    )


_SMALL_KEYS = ("lng", "gq", "gk", "invf", "sink", "cw", "bdq", "bdk")
_STAGED_KEYS = ("win", "wa", "wb", "wo", "wpg", "wpp")


def _heads_first(t):
    t = jnp.moveaxis(t, -3, -1)
    return t.reshape(t.shape[:-3] + (KV_DIM, t.shape[-1]))


def _heads_last(t):
    t = t.reshape(t.shape[:-2] + (N_KV_HEADS, HEAD_DIM, t.shape[-1]))
    return jnp.moveaxis(t, -1, -3)


def _layer(xp, pp, xs, ps, cache_k, cache_v, state_conv, c):
    t, d = xp.shape
    nb, t_new, _ = xs.shape
    cache_len = cache_k.shape[1]
    tm, bb = PROMPT_TILE, SAMPLE_BATCH_TILE
    assert t % tm == 0 and tm % PROMPT_SUBTILE == 0
    assert PROMPT_SUBTILE % CHUNK == 0 and PROMPT_SUBTILE >= WINDOW
    assert nb % bb == 0 and CONV_WIDTH - 1 <= t_new <= cache_len
    n_p, n_s = t // tm, nb // bb
    rows = bb * t_new
    xs2 = xs.reshape(nb * t_new, d)
    ps2 = ps.reshape(nb * t_new, ps.shape[-1])
    ck = _heads_first(cache_k)
    cv = _heads_first(cache_v)

    def p_idx(i):
        return jnp.minimum(i, n_p - 1)

    def s_idx(i):
        return jnp.maximum(i - n_p, 0)

    in_specs = ([pl.BlockSpec((tm, d), lambda i: (p_idx(i), 0)),
                 pl.BlockSpec((tm, pp.shape[1]), lambda i: (p_idx(i), 0)),
                 pl.BlockSpec((rows, d), lambda i: (s_idx(i), 0)),
                 pl.BlockSpec((rows, ps2.shape[1]), lambda i: (s_idx(i), 0)),
                 pl.BlockSpec((bb, KV_DIM, cache_len), lambda i: (s_idx(i), 0, 0)),
                 pl.BlockSpec((bb, KV_DIM, cache_len), lambda i: (s_idx(i), 0, 0)),
                 pl.BlockSpec((bb, CONV_WIDTH - 1, CONV_DIM), lambda i: (s_idx(i), 0, 0))]
                + [_smem_spec() if k == "sink" else _const_spec(c[k].shape) for k in _SMALL_KEYS]
                + [pl.BlockSpec(memory_space=pl.ANY) for _ in _STAGED_KEYS])
    out_shape = (jax.ShapeDtypeStruct((t, d), F32),
                 jax.ShapeDtypeStruct((KV_DIM, WINDOW), F32),
                 jax.ShapeDtypeStruct((KV_DIM, WINDOW), F32),
                 jax.ShapeDtypeStruct((CONV_WIDTH - 1, CONV_DIM), F32),
                 jax.ShapeDtypeStruct((nb * t_new, d), F32),
                 jax.ShapeDtypeStruct((nb, KV_DIM, cache_len), F32),
                 jax.ShapeDtypeStruct((nb, KV_DIM, cache_len), F32),
                 jax.ShapeDtypeStruct((nb, CONV_WIDTH - 1, CONV_DIM), F32))
    out_specs = (pl.BlockSpec((tm, d), lambda i: (p_idx(i), 0)),
                 pl.BlockSpec((KV_DIM, WINDOW), lambda i: (0, 0)),
                 pl.BlockSpec((KV_DIM, WINDOW), lambda i: (0, 0)),
                 pl.BlockSpec((CONV_WIDTH - 1, CONV_DIM), lambda i: (0, 0)),
                 pl.BlockSpec((rows, d), lambda i: (s_idx(i), 0)),
                 pl.BlockSpec((bb, KV_DIM, cache_len), lambda i: (s_idx(i), 0, 0)),
                 pl.BlockSpec((bb, KV_DIM, cache_len), lambda i: (s_idx(i), 0, 0)),
                 pl.BlockSpec((bb, CONV_WIDTH - 1, CONV_DIM), lambda i: (s_idx(i), 0, 0)))
    scratch = ([pltpu.VMEM((4, KV_DIM, WINDOW + tm), BF16),
                pltpu.VMEM((4, WINDOW + tm, V7X_LANES), BF16),
                pltpu.VMEM((V7X_SUBLANES + tm, CONV_DIM), F32),
                pltpu.VMEM((max(tm, rows), ATTN_DIM), F32),
                pltpu.VMEM((4, tm, V7X_LANES), F32),
                pltpu.VMEM((bb, V7X_SUBLANES + t_new, CONV_DIM), F32),
                pltpu.VMEM((rows, CONV_DIM), F32),
                pltpu.VMEM((2, rows, V7X_LANES), F32)]
               + [pltpu.VMEM(c[k].shape, BF16) for k in _STAGED_KEYS]
               + [pltpu.VMEM((STAGE_SLOTS, STAGE_ROWS, STAGE_COLS), F32),
                  pltpu.SemaphoreType.DMA((STAGE_SLOTS,))])
    yp, kpo, vpo, cpo, ys, kso, vso, cso = pl.pallas_call(
        functools.partial(_layer_kernel, n_p),
        grid=(n_p + n_s,),
        in_specs=in_specs,
        out_specs=out_specs,
        out_shape=out_shape,
        scratch_shapes=scratch,
        compiler_params=pltpu.CompilerParams(dimension_semantics=("arbitrary",),
                                             vmem_limit_bytes=V7X_VMEM_LIMIT_BYTES),
        name="hybrid_layer",
    )(xp, pp, xs2, ps2, ck, cv, state_conv, *[c[k] for k in _SMALL_KEYS],
      *[c[k] for k in _STAGED_KEYS])
    return ((yp, _heads_last(kpo), _heads_last(vpo), cpo),
            (ys.reshape(nb, t_new, d), _heads_last(kso), _heads_last(vso), cso))


def kernel(x_prompt, x_sample, p_prompt, p_sample, cache_k, cache_v, state_conv, ln_g, w_in,
           q_norm_g, k_norm_g, sink, conv_w, w_attn_out, w_conv_out, w_o, w_ple_gate, w_ple_proj):
    depth = ln_g.shape[0]
    assert x_prompt.shape[0] == 1, "one prompt sequence per call"
    hp, hs = x_prompt[0], x_sample
    kp_l, vp_l, cp_l, ks_l, vs_l, cs_l = [], [], [], [], [], []
    for i in range(depth):
        c = _layer_consts(ln_g[i], w_in[i], q_norm_g[i], k_norm_g[i], sink[i],
                          jnp.swapaxes(conv_w, 0, 1)[:, i:i + 1, :],
                          w_attn_out[i], w_conv_out[i], w_o[i], w_ple_gate[i], w_ple_proj[i])
        (hp, kp, vp, cp), (hs, ks, vs, cs) = _layer(hp, p_prompt[i, 0], hs, p_sample[i], cache_k[i],
                                                    cache_v[i], state_conv[i], c)
        kp_l.append(kp[None])
        vp_l.append(vp[None])
        cp_l.append(cp[None])
        ks_l.append(ks)
        vs_l.append(vs)
        cs_l.append(cs)
    return (hp[None], hs, jnp.stack(kp_l), jnp.stack(vp_l), jnp.stack(cp_l),
            jnp.stack(ks_l), jnp.stack(vs_l), jnp.stack(cs_l))
```

```python
import functools

import numpy as np
import jax
import jax.numpy as jnp
from jax import lax
from jax.experimental import pallas as pl
from jax.experimental.pallas import tpu as pltpu

F32 = jnp.float32
BF16 = jnp.bfloat16

CHUNK = 64
WINDOW = 128
WINDOW_CHUNKS = WINDOW // CHUNK
N_HEADS = 8
N_KV_HEADS = 2
GROUP_HEADS = N_HEADS // N_KV_HEADS
HEAD_DIM = 64
HALF = HEAD_DIM // 2
ATTN_DIM = N_HEADS * HEAD_DIM
KV_DIM = N_KV_HEADS * HEAD_DIM
CONV_DIM = 512
CONV_WIDTH = 3
PAST_LEN = 1024
ROPE_THETA = 10000.0
EPS = 1e-6
NEG = -1e30

OFF_Q = 0
OFF_K = OFF_Q + ATTN_DIM
OFF_V = OFF_K + KV_DIM
OFF_GA = OFF_V + KV_DIM
OFF_B = OFF_GA + ATTN_DIM
OFF_GC_END = OFF_B + 4 * CONV_DIM
OFF_MA = OFF_GC_END

V7X_LANES = 128
V7X_SUBLANES = 8
V7X_VMEM_LIMIT_BYTES = 58 * 1024 * 1024

PROMPT_TILE = 512
PROMPT_SUBTILE = 256
PROMPT_STAGE_SKEW = 1
STAGE_ROWS = 256
STAGE_COLS = 1024
STAGE_SLOTS = 4
SAMPLE_BATCH_TILE = 8


def _mm(a, w):
    return jnp.dot(a, w, preferred_element_type=F32)


def _mm_t(a, b):
    return lax.dot_general(a, b, (((1,), (1,)), ((), ())), preferred_element_type=F32)


def _sigmoid(x):
    return 1.0 / (1.0 + jnp.exp(-x))


def _silu(x):
    return x * _sigmoid(x)


def _rmsnorm(x, g):
    ms = jnp.mean(x * x, axis=-1, keepdims=True)
    return x * lax.rsqrt(ms + EPS) * g


def _group_mean(t, bd):
    return _mm(t.astype(BF16), bd)


def _head_norm(t, bd, g):
    ms = _group_mean(t * t, bd)
    return t * lax.rsqrt(ms + EPS) * g


def _rope(xc, cos_t, sin_s):
    lane = lax.broadcasted_iota(jnp.int32, xc.shape, 1)
    upper = (lane & HALF) != 0
    rot = jnp.where(upper, pltpu.roll(xc, HALF, 1), pltpu.roll(xc, V7X_LANES - HALF, 1))
    return xc * cos_t + rot * sin_s


def _rope_sign(shape):
    lane = lax.broadcasted_iota(jnp.int32, shape, 1)
    return jnp.where((lane & HALF) != 0, 1.0, -1.0).astype(F32)


def _kv_variants(t):
    lane = lax.broadcasted_iota(jnp.int32, t.shape, 1)
    lo = lane < HEAD_DIM
    sw = pltpu.roll(t, HEAD_DIM, 1)
    zero = jnp.zeros_like(t)
    return (jnp.where(lo, t, zero).astype(BF16), jnp.where(lo, zero, sw).astype(BF16),
            jnp.where(lo, sw, zero).astype(BF16), jnp.where(lo, zero, t).astype(BF16))


def _kv_variants_t(t):
    tb = t.astype(BF16)
    h0, h1 = tb[0:HEAD_DIM], tb[HEAD_DIM:2 * HEAD_DIM]
    zero = jnp.zeros_like(h0)
    return (jnp.concatenate([h0, zero], axis=0), jnp.concatenate([zero, h0], axis=0),
            jnp.concatenate([h1, zero], axis=0), jnp.concatenate([zero, h1], axis=0))


def _scores(qs, ka, kb, bad, keys_on_lanes=False):
    dot = _mm if keys_on_lanes else _mm_t
    s_e = dot(qs, ka)
    s_o = dot(qs, kb)
    if bad is not None:
        s_e = jnp.where(bad, NEG, s_e)
        s_o = jnp.where(bad, NEG, s_o)
    return s_e, s_o


def _softmax_parts(s, sink):
    m = jnp.maximum(jnp.max(s, axis=-1, keepdims=True), sink)
    e = jnp.exp(s - m)
    r = 1.0 / (jnp.sum(e, axis=-1, keepdims=True) + jnp.exp(sink - m))
    return e.astype(BF16), r


def _pv(p_e, p_o, va, vb, r_e, r_o, keys_on_lanes=False):
    dot = _mm_t if keys_on_lanes else _mm
    o = dot(p_e, va) + dot(p_o, vb)
    lane = lax.broadcasted_iota(jnp.int32, o.shape, 1)
    return o * jnp.where(lane < HEAD_DIM, r_e, r_o)


def _sink_cols(sink_ref, n_blocks, rows_per_pair):
    row = lax.broadcasted_iota(jnp.int32, (n_blocks * 2 * rows_per_pair, 1), 0)
    second_pair = (row // rows_per_pair) % 2 == 1
    second_kv = (row // (2 * rows_per_pair)) % 2 == 1

    def pick(odd):
        kv0 = jnp.where(second_pair, sink_ref[2 + odd], sink_ref[odd])
        kv1 = jnp.where(second_pair, sink_ref[GROUP_HEADS + 2 + odd], sink_ref[GROUP_HEADS + odd])
        return jnp.where(second_kv, kv1, kv0)

    return pick(0), pick(1)


def _stage_chunks(shape):
    rows, cols = shape
    assert rows % STAGE_ROWS == 0 and cols % V7X_LANES == 0
    width = max(c for c in range(V7X_LANES, STAGE_COLS + 1, V7X_LANES) if cols % c == 0)
    return [(r0, c0, STAGE_ROWS, width)
            for r0 in range(0, rows, STAGE_ROWS) for c0 in range(0, cols, width)]


def _stage_weights(pairs, stage, sem):
    chunks = [(src, dst) + ch for src, dst in pairs for ch in _stage_chunks(src.shape)]

    def copy(n):
        src, _, r0, c0, rows, cols = chunks[n]
        slot = n % STAGE_SLOTS
        return pltpu.make_async_copy(src.at[pl.ds(r0, rows), pl.ds(c0, cols)],
                                     stage.at[slot, pl.ds(0, rows), pl.ds(0, cols)],
                                     sem.at[slot])

    for n in range(min(STAGE_SLOTS, len(chunks))):
        copy(n).start()
    for n, (_, dst, r0, c0, rows, cols) in enumerate(chunks):
        copy(n).wait()
        dst[r0:r0 + rows, c0:c0 + cols] = stage[n % STAGE_SLOTS, 0:rows, 0:cols].astype(BF16)
        if n + STAGE_SLOTS < len(chunks):
            copy(n + STAGE_SLOTS).start()


def _qkv_proj(x, lng_ref, win_ref):
    hb = _rmsnorm(x, lng_ref[...]).astype(BF16)
    return hb, _mm(hb, win_ref[:, OFF_Q:OFF_GA])


def _tile_lanes(g, width):
    while g.shape[1] < width:
        g = jnp.concatenate([g, g], axis=1)
    return g


def _qk_norm_rope(qkv, gq_ref, gk_ref, bdq_ref, bdk_ref, cos_t, sin_s):
    half_q = ATTN_DIM // 2
    gq = _tile_lanes(gq_ref[...] * (HEAD_DIM ** -0.5), half_q)
    gk = _tile_lanes(gk_ref[...], KV_DIM)
    q_cols = []
    for j in range(2):
        t = qkv[:, j * half_q:(j + 1) * half_q]
        tn = _head_norm(t, bdq_ref[...], gq)
        for c in range(half_q // V7X_LANES):
            q_cols.append(_rope(tn[:, c * V7X_LANES:(c + 1) * V7X_LANES], cos_t, sin_s).astype(BF16))
    kn = _head_norm(qkv[:, OFF_K:OFF_V], bdk_ref[...], gk)
    kr = _rope(kn, cos_t, sin_s)
    v = qkv[:, OFF_V:OFF_GA]
    return q_cols, kr, v


def _conv_proj(hb, win_ref):
    return _mm(hb, win_ref[:, OFF_B:OFF_GC_END])


def _conv_input(bcug, conv_fn):
    b_gate = bcug[:, 0:CONV_DIM]
    u = bcug[:, CONV_DIM:2 * CONV_DIM] * bcug[:, 2 * CONV_DIM:3 * CONV_DIM]
    gate_c = bcug[:, 3 * CONV_DIM:4 * CONV_DIM]
    conv = conv_fn(u)
    return (b_gate * conv * _silu(gate_c)).astype(BF16)


def _prompt_body(i, x_ref, p_ref, lng_ref, gq_ref, gk_ref, invf_ref, sink_ref, cw_ref,
                 bdq_ref, bdk_ref, win_ref, wa_ref, wb_ref, wo_ref, wpg_ref, wpp_ref,
                 y_ref, ko_ref, vo_ref, co_ref, kbuf, vbuf, ubuf, attn_buf, tab):
    tm, d_model = x_ref.shape
    invf = invf_ref[...]

    @pl.when(i == 0)
    def _init():
        kbuf[:, :, 0:WINDOW] = jnp.zeros((4, KV_DIM, WINDOW), BF16)
        vbuf[:, 0:WINDOW, :] = jnp.zeros((4, WINDOW, V7X_LANES), BF16)
        ubuf[0:V7X_SUBLANES, :] = jnp.zeros((V7X_SUBLANES, CONV_DIM), F32)
        r = lax.broadcasted_iota(jnp.int32, (tm, V7X_LANES), 0).astype(F32)
        ang = r * invf
        sgn = _rope_sign((tm, V7X_LANES))
        c_r = jnp.cos(ang)
        s_r = jnp.sin(ang)
        tab[0] = c_r
        tab[1] = s_r
        tab[2] = c_r * sgn
        tab[3] = s_r * sgn

    base = (i * tm).astype(F32) * invf
    cb = jnp.cos(base)
    sb = jnp.sin(base)
    cos_t = tab[0] * cb - tab[1] * sb
    sin_s = tab[3] * cb + tab[2] * sb

    sub = PROMPT_SUBTILE
    n_sub = tm // sub
    sub_chunks = sub // CHUNK
    assert 2 * CHUNK == V7X_LANES and WINDOW == V7X_LANES
    n_keys = 2 * V7X_LANES

    def key_window_start(r0, c):
        return ((r0 + c * CHUNK) // V7X_LANES) * V7X_LANES

    def mask_window(sc, r0, c):
        lo, hi = sc[:, 0:V7X_LANES], sc[:, V7X_LANES:n_keys]
        lane = lax.broadcasted_iota(jnp.int32, lo.shape, 1)
        no_carry = (i == 0) if key_window_start(r0, c) < WINDOW else None
        if ((r0 + c * CHUNK) // CHUNK) % 2 == 0:
            hi = jnp.where(lane >= CHUNK, NEG, hi)
            if no_carry is not None:
                lo = jnp.where(no_carry, NEG, lo)
        else:
            bad = lane < CHUNK
            lo = jnp.where(bad if no_carry is None else bad | no_carry, NEG, lo)
        return jnp.concatenate([lo, hi], axis=1)

    pad = V7X_SUBLANES
    blocks = [(c, vh) for c in range(sub_chunks) for vh in range(N_KV_HEADS)]
    st = [dict(r0=s * sub) for s in range(n_sub)]

    def stage_rms(s):
        s["hb"] = _rmsnorm(x_ref[s["r0"]:s["r0"] + sub, :], lng_ref[...]).astype(BF16)

    def stage_qkv(s):
        s["qkv"] = _mm(s["hb"], win_ref[:, OFF_Q:OFF_GA])

    def stage_gate_a(s):
        s["sig_a"] = _sigmoid(_mm(s["hb"], win_ref[:, OFF_MA:OFF_MA + d_model]))

    def stage_gate_c(s):
        s["sig_c"] = _sigmoid(_mm(s["hb"], win_ref[:, OFF_MA + d_model:OFF_MA + 2 * d_model]))

    def stage_qk_norm(s):
        r0 = s["r0"]
        s["q_cols"], kr, v = _qk_norm_rope(s.pop("qkv"), gq_ref, gk_ref, bdq_ref, bdk_ref,
                                           cos_t[r0:r0 + sub], sin_s[r0:r0 + sub])
        kr_t = kr.T
        for n, t in enumerate(_kv_variants_t(kr_t)):
            kbuf[n, :, WINDOW + r0:WINDOW + r0 + sub] = t
        for n, t in enumerate(_kv_variants(v)):
            vbuf[n, WINDOW + r0:WINDOW + r0 + sub, :] = t
        if r0 + sub == tm:
            ko_ref[...] = kr_t[:, sub - WINDOW:sub]
            vo_ref[...] = v[sub - WINDOW:sub, :].T

    def stage_scores(s):
        r0 = s["r0"]
        s_e, s_o = [], []
        for c, vh in blocks:
            rows = slice(c * CHUNK, (c + 1) * CHUNK)
            start = key_window_start(r0, c)
            win = slice(start, start + n_keys)
            qs = jnp.concatenate([s["q_cols"][2 * vh][rows], s["q_cols"][2 * vh + 1][rows]], axis=0)
            se, so = _scores(qs, kbuf[2 * vh, :, win], kbuf[2 * vh + 1, :, win], None,
                             keys_on_lanes=True)
            s_e.append(mask_window(se, r0, c))
            s_o.append(mask_window(so, r0, c))
        s["s_e"] = jnp.concatenate(s_e, axis=0)
        s["s_o"] = jnp.concatenate(s_o, axis=0)
        del s["q_cols"]

    def stage_conv_proj(s):
        s["bcug"] = _conv_proj(s["hb"], win_ref)

    def stage_conv(s):
        r0 = s["r0"]

        def conv_fn(u):
            ubuf[pad + r0:pad + r0 + sub, :] = u
            conv = ubuf[pad + r0 - 2:pad + r0 - 2 + sub, :] * cw_ref[0]
            conv = conv + ubuf[pad + r0 - 1:pad + r0 - 1 + sub, :] * cw_ref[1]
            return conv + u * cw_ref[2]

        s["c_in"] = _conv_input(s.pop("bcug"), conv_fn)

    def stage_conv_out(s):
        s["yc"] = _mm(s.pop("c_in"), wb_ref[...])

    sink_e, sink_o = _sink_cols(sink_ref, len(blocks), CHUNK)

    def stage_softmax(s):
        s["p_e"], s["r_e"] = _softmax_parts(s.pop("s_e"), sink_e)
        s["p_o"], s["r_o"] = _softmax_parts(s.pop("s_o"), sink_o)

    def stage_pv(s):
        r0 = s["r0"]
        for n, (c, vh) in enumerate(blocks):
            rows = slice(r0 + c * CHUNK, r0 + (c + 1) * CHUNK)
            start = key_window_start(r0, c)
            win = slice(start, start + n_keys)
            br = slice(n * 2 * CHUNK, (n + 1) * 2 * CHUNK)
            o = _pv(s["p_e"][br], s["p_o"][br], vbuf[2 * vh, win, :], vbuf[2 * vh + 1, win, :],
                    s["r_e"][br], s["r_o"][br])
            attn_buf[rows, (2 * vh) * V7X_LANES:(2 * vh + 1) * V7X_LANES] = o[0:CHUNK]
            attn_buf[rows, (2 * vh + 1) * V7X_LANES:(2 * vh + 2) * V7X_LANES] = o[CHUNK:2 * CHUNK]

    def stage_attn_gate(s):
        s["silu_ga"] = _silu(_mm(s["hb"], win_ref[:, OFF_GA:OFF_B]))

    def stage_attn_out(s):
        r0 = s["r0"]
        a_in = (attn_buf[r0:r0 + sub, :] * s.pop("silu_ga")).astype(BF16)
        ya = _mm(a_in, wa_ref[...])
        s["mix"] = (s.pop("sig_a") * ya + s.pop("sig_c") * s.pop("yc")).astype(BF16)

    def stage_out_proj(s):
        r0 = s["r0"]
        s["r"] = x_ref[r0:r0 + sub, :] + _mm(s.pop("mix"), wo_ref[...])
        s["pp"] = _mm(p_ref[r0:r0 + sub, :].astype(BF16), wpp_ref[...])

    def stage_ple(s):
        r0 = s["r0"]
        r = s.pop("r")
        gate = _sigmoid(_mm(r.astype(BF16), wpg_ref[...]))
        y_ref[r0:r0 + sub, :] = r + gate * s.pop("pp")

    stages = [stage_rms, stage_qkv, stage_qk_norm, stage_gate_a, stage_scores, stage_gate_c,
              stage_softmax, stage_conv_proj, stage_conv, stage_attn_gate, stage_conv_out,
              stage_pv, stage_attn_out, stage_out_proj, stage_ple]
    for t in range(len(stages) + PROMPT_STAGE_SKEW * (n_sub - 1)):
        for j, s in enumerate(st):
            k = t - j * PROMPT_STAGE_SKEW
            if 0 <= k < len(stages):
                stages[k](s)

    co_ref[...] = ubuf[pad + tm - 2:pad + tm, :]
    ubuf[0:pad, :] = ubuf[tm:tm + pad, :]
    kbuf[:, :, 0:WINDOW] = kbuf[:, :, tm:tm + WINDOW]
    vbuf[:, 0:WINDOW, :] = vbuf[:, tm:tm + WINDOW, :]


def _sample_body(i, x_ref, p_ref, ck_ref, cv_ref, sc_ref, lng_ref, gq_ref, gk_ref, invf_ref,
                 sink_ref, cw_ref, bdq_ref, bdk_ref, win_ref, wa_ref, wb_ref, wo_ref, wpg_ref,
                 wpp_ref, y_ref, ko_ref, vo_ref, co_ref, ubuf, attn_buf, conv_buf, tab):
    bb, _, cache_len = ck_ref.shape
    rows_total, d_model = x_ref.shape
    t_new = rows_total // bb
    n_keys = cache_len + t_new

    @pl.when(i == 0)
    def _init():
        r = lax.broadcasted_iota(jnp.int32, (rows_total, V7X_LANES), 0)
        pos = (PAST_LEN + lax.rem(r, t_new)).astype(F32)
        ang = pos * invf_ref[...]
        tab[0] = jnp.cos(ang)
        tab[1] = jnp.sin(ang) * _rope_sign((rows_total, V7X_LANES))

    def conv_fn(u):
        pad = V7X_SUBLANES
        for b in range(bb):
            rows = slice(b * t_new, (b + 1) * t_new)
            ub = u[rows]
            ubuf[b, pad - (CONV_WIDTH - 1):pad, :] = sc_ref[b]
            ubuf[b, pad:pad + t_new, :] = ub
            conv = ubuf[b, pad - 2:pad - 2 + t_new, :] * cw_ref[0]
            conv = conv + ubuf[b, pad - 1:pad - 1 + t_new, :] * cw_ref[1]
            conv_buf[rows, :] = conv + ub * cw_ref[2]
            co_ref[b] = ubuf[b, pad + t_new - (CONV_WIDTH - 1):pad + t_new, :]
        return conv_buf[...]

    x = x_ref[...]
    hb, qkv = _qkv_proj(x, lng_ref, win_ref)
    sig_a = _sigmoid(_mm(hb, win_ref[:, OFF_MA:OFF_MA + d_model]))
    q_cols, kr, v = _qk_norm_rope(qkv, gq_ref, gk_ref, bdq_ref, bdk_ref, tab[0], tab[1])

    qi = lax.broadcasted_iota(jnp.int32, (2 * t_new, n_keys), 0)
    q_pos = PAST_LEN + lax.rem(qi, t_new)
    k_pos = PAST_LEN - cache_len + lax.broadcasted_iota(jnp.int32, (2 * t_new, n_keys), 1)
    q_ch = q_pos // CHUNK
    k_ch = k_pos // CHUNK
    bad = jnp.logical_not((k_ch <= q_ch) & (k_ch >= q_ch - WINDOW_CHUNKS))

    blocks = [(b, vh) for b in range(bb) for vh in range(N_KV_HEADS)]
    s_e, s_o, vvars = [], [], []
    kr_t = kr.T
    v_t = v.T
    for b in range(bb):
        rows = slice(b * t_new, (b + 1) * t_new)
        kcat = jnp.concatenate([ck_ref[b], kr_t[:, rows]], axis=1)
        vcat = jnp.concatenate([cv_ref[b], v_t[:, rows]], axis=1)
        ko_ref[b] = kcat[:, n_keys - cache_len:n_keys]
        vo_ref[b] = vcat[:, n_keys - cache_len:n_keys]
        kvar = _kv_variants_t(kcat)
        vvars.append(_kv_variants_t(vcat))
        for vh in range(N_KV_HEADS):
            qs = jnp.concatenate([q_cols[2 * vh][rows], q_cols[2 * vh + 1][rows]], axis=0)
            se, so = _scores(qs, kvar[2 * vh], kvar[2 * vh + 1], bad, keys_on_lanes=True)
            s_e.append(se)
            s_o.append(so)

    sig_c = _sigmoid(_mm(hb, win_ref[:, OFF_MA + d_model:OFF_MA + 2 * d_model]))
    sink_e, sink_o = _sink_cols(sink_ref, len(blocks), t_new)
    p_e, r_e = _softmax_parts(jnp.concatenate(s_e, axis=0), sink_e)
    bcug = _conv_proj(hb, win_ref)
    p_o, r_o = _softmax_parts(jnp.concatenate(s_o, axis=0), sink_o)
    c_in = _conv_input(bcug, conv_fn)
    silu_ga = _silu(_mm(hb, win_ref[:, OFF_GA:OFF_B]))
    pp = _mm(p_ref[...].astype(BF16), wpp_ref[...])
    yc = _mm(c_in, wb_ref[...])

    for n, (b, vh) in enumerate(blocks):
        rows = slice(b * t_new, (b + 1) * t_new)
        br = slice(n * 2 * t_new, (n + 1) * 2 * t_new)
        o = _pv(p_e[br], p_o[br], vvars[b][2 * vh], vvars[b][2 * vh + 1], r_e[br], r_o[br],
                keys_on_lanes=True)
        attn_buf[rows, (2 * vh) * V7X_LANES:(2 * vh + 1) * V7X_LANES] = o[0:t_new]
        attn_buf[rows, (2 * vh + 1) * V7X_LANES:(2 * vh + 2) * V7X_LANES] = o[t_new:2 * t_new]

    ya = _mm((attn_buf[0:rows_total, :] * silu_ga).astype(BF16), wa_ref[...])
    mix = (sig_a * ya + sig_c * yc).astype(BF16)
    r = x + _mm(mix, wo_ref[...])
    gate = _sigmoid(_mm(r.astype(BF16), wpg_ref[...]))
    y_ref[...] = r + gate * pp


def _layer_kernel(n_prompt_steps, *refs):
    (xp_ref, pp_ref, xs_ref, ps_ref, ck_ref, cv_ref, sc_ref,
     lng_ref, gq_ref, gk_ref, invf_ref, sink_ref, cw_ref, bdq_ref, bdk_ref,
     win_hbm, wa_hbm, wb_hbm, wo_hbm, wpg_hbm, wpp_hbm,
     yp_ref, kpo_ref, vpo_ref, cpo_ref, ys_ref, kso_ref, vso_ref, cso_ref,
     kbuf, vbuf, ubuf_p, attn_buf, tab_p, ubuf_s, conv_buf, tab_s,
     win_ref, wa_ref, wb_ref, wo_ref, wpg_ref, wpp_ref, stage, sem) = refs
    i = pl.program_id(0)
    small = (lng_ref, gq_ref, gk_ref, invf_ref, sink_ref, cw_ref, bdq_ref, bdk_ref)
    weights = (win_ref, wa_ref, wb_ref, wo_ref, wpg_ref, wpp_ref)

    @pl.when(i == 0)
    def _stage():
        _stage_weights([(win_hbm, win_ref), (wa_hbm, wa_ref), (wb_hbm, wb_ref), (wo_hbm, wo_ref),
                        (wpg_hbm, wpg_ref), (wpp_hbm, wpp_ref)], stage, sem)

    @pl.when(i < n_prompt_steps)
    def _prompt():
        _prompt_body(i, xp_ref, pp_ref, *small, *weights, yp_ref, kpo_ref, vpo_ref, cpo_ref,
                     kbuf, vbuf, ubuf_p, attn_buf, tab_p)

    @pl.when(i >= n_prompt_steps)
    def _sample():
        _sample_body(i - n_prompt_steps, xs_ref, ps_ref, ck_ref, cv_ref, sc_ref, *small, *weights,
                     ys_ref, kso_ref, vso_ref, cso_ref, ubuf_s, attn_buf, conv_buf, tab_s)


def _const_spec(shape):
    nd = len(shape)
    return pl.BlockSpec(shape, lambda i: (0,) * nd, pipeline_mode=pl.Buffered(1))


def _smem_spec():
    return pl.BlockSpec(memory_space=pltpu.SMEM)


def _block_diag_mean(width):
    idx = np.arange(width) // HEAD_DIM
    return jnp.asarray((idx[:, None] == idx[None, :]).astype(np.float32) / HEAD_DIM, dtype=BF16)


def _layer_consts(ln_g, w_in, q_norm_g, k_norm_g, sink, conv_w, w_attn_out, w_conv_out, w_o,
                  w_ple_gate, w_ple_proj):
    inv_freq = ROPE_THETA ** (-jnp.arange(0, HALF, dtype=F32) * 2.0 / HEAD_DIM)
    return dict(
        lng=ln_g.reshape(1, -1).astype(F32),
        gq=q_norm_g.astype(F32).reshape(1, HEAD_DIM),
        gk=k_norm_g.astype(F32).reshape(1, HEAD_DIM),
        invf=jnp.tile(inv_freq, V7X_LANES // HALF).reshape(1, V7X_LANES),
        sink=sink.astype(F32),
        cw=conv_w.astype(F32),
        bdq=_block_diag_mean(ATTN_DIM // 2),
        bdk=_block_diag_mean(KV_DIM),
        win=w_in.astype(F32),
        wa=w_attn_out.astype(F32),
        wb=w_conv_out.astype(F32),
        wo=w_o.astype(F32),
        wpg=w_ple_gate.astype(F32),
        wpp=w_ple_proj.astype(F32),
    )


_SMALL_KEYS = ("lng", "gq", "gk", "invf", "sink", "cw", "bdq", "bdk")
_STAGED_KEYS = ("win", "wa", "wb", "wo", "wpg", "wpp")


def _heads_first(t):
    t = jnp.moveaxis(t, -3, -1)
    return t.reshape(t.shape[:-3] + (KV_DIM, t.shape[-1]))


def _heads_last(t):
    t = t.reshape(t.shape[:-2] + (N_KV_HEADS, HEAD_DIM, t.shape[-1]))
    return jnp.moveaxis(t, -1, -3)


def _layer(xp, pp, xs, ps, cache_k, cache_v, state_conv, c):
    t, d = xp.shape
    nb, t_new, _ = xs.shape
    cache_len = cache_k.shape[1]
    tm, bb = PROMPT_TILE, SAMPLE_BATCH_TILE
    assert t % tm == 0 and tm % PROMPT_SUBTILE == 0
    assert PROMPT_SUBTILE % CHUNK == 0 and PROMPT_SUBTILE >= WINDOW
    assert nb % bb == 0 and CONV_WIDTH - 1 <= t_new <= cache_len
    n_p, n_s = t // tm, nb // bb
    rows = bb * t_new
    xs2 = xs.reshape(nb * t_new, d)
    ps2 = ps.reshape(nb * t_new, ps.shape[-1])
    ck = _heads_first(cache_k)
    cv = _heads_first(cache_v)

    def p_idx(i):
        return jnp.minimum(i, n_p - 1)

    def s_idx(i):
        return jnp.maximum(i - n_p, 0)

    in_specs = ([pl.BlockSpec((tm, d), lambda i: (p_idx(i), 0)),
                 pl.BlockSpec((tm, pp.shape[1]), lambda i: (p_idx(i), 0)),
                 pl.BlockSpec((rows, d), lambda i: (s_idx(i), 0)),
                 pl.BlockSpec((rows, ps2.shape[1]), lambda i: (s_idx(i), 0)),
                 pl.BlockSpec((bb, KV_DIM, cache_len), lambda i: (s_idx(i), 0, 0)),
                 pl.BlockSpec((bb, KV_DIM, cache_len), lambda i: (s_idx(i), 0, 0)),
                 pl.BlockSpec((bb, CONV_WIDTH - 1, CONV_DIM), lambda i: (s_idx(i), 0, 0))]
                + [_smem_spec() if k == "sink" else _const_spec(c[k].shape) for k in _SMALL_KEYS]
                + [pl.BlockSpec(memory_space=pl.ANY) for _ in _STAGED_KEYS])
    out_shape = (jax.ShapeDtypeStruct((t, d), F32),
                 jax.ShapeDtypeStruct((KV_DIM, WINDOW), F32),
                 jax.ShapeDtypeStruct((KV_DIM, WINDOW), F32),
                 jax.ShapeDtypeStruct((CONV_WIDTH - 1, CONV_DIM), F32),
                 jax.ShapeDtypeStruct((nb * t_new, d), F32),
                 jax.ShapeDtypeStruct((nb, KV_DIM, cache_len), F32),
                 jax.ShapeDtypeStruct((nb, KV_DIM, cache_len), F32),
                 jax.ShapeDtypeStruct((nb, CONV_WIDTH - 1, CONV_DIM), F32))
    out_specs = (pl.BlockSpec((tm, d), lambda i: (p_idx(i), 0)),
                 pl.BlockSpec((KV_DIM, WINDOW), lambda i: (0, 0)),
                 pl.BlockSpec((KV_DIM, WINDOW), lambda i: (0, 0)),
                 pl.BlockSpec((CONV_WIDTH - 1, CONV_DIM), lambda i: (0, 0)),
                 pl.BlockSpec((rows, d), lambda i: (s_idx(i), 0)),
                 pl.BlockSpec((bb, KV_DIM, cache_len), lambda i: (s_idx(i), 0, 0)),
                 pl.BlockSpec((bb, KV_DIM, cache_len), lambda i: (s_idx(i), 0, 0)),
                 pl.BlockSpec((bb, CONV_WIDTH - 1, CONV_DIM), lambda i: (s_idx(i), 0, 0)))
    scratch = ([pltpu.VMEM((4, KV_DIM, WINDOW + tm), BF16),
                pltpu.VMEM((4, WINDOW + tm, V7X_LANES), BF16),
                pltpu.VMEM((V7X_SUBLANES + tm, CONV_DIM), F32),
                pltpu.VMEM((max(tm, rows), ATTN_DIM), F32),
                pltpu.VMEM((4, tm, V7X_LANES), F32),
                pltpu.VMEM((bb, V7X_SUBLANES + t_new, CONV_DIM), F32),
                pltpu.VMEM((rows, CONV_DIM), F32),
                pltpu.VMEM((2, rows, V7X_LANES), F32)]
               + [pltpu.VMEM(c[k].shape, BF16) for k in _STAGED_KEYS]
               + [pltpu.VMEM((STAGE_SLOTS, STAGE_ROWS, STAGE_COLS), F32),
                  pltpu.SemaphoreType.DMA((STAGE_SLOTS,))])
    yp, kpo, vpo, cpo, ys, kso, vso, cso = pl.pallas_call(
        functools.partial(_layer_kernel, n_p),
        grid=(n_p + n_s,),
        in_specs=in_specs,
        out_specs=out_specs,
        out_shape=out_shape,
        scratch_shapes=scratch,
        compiler_params=pltpu.CompilerParams(dimension_semantics=("arbitrary",),
                                             vmem_limit_bytes=V7X_VMEM_LIMIT_BYTES),
        name="hybrid_layer",
    )(xp, pp, xs2, ps2, ck, cv, state_conv, *[c[k] for k in _SMALL_KEYS],
      *[c[k] for k in _STAGED_KEYS])
    return ((yp, _heads_last(kpo), _heads_last(vpo), cpo),
            (ys.reshape(nb, t_new, d), _heads_last(kso), _heads_last(vso), cso))


def kernel(x_prompt, x_sample, p_prompt, p_sample, cache_k, cache_v, state_conv, ln_g, w_in,
           q_norm_g, k_norm_g, sink, conv_w, w_attn_out, w_conv_out, w_o, w_ple_gate, w_ple_proj):
    depth = ln_g.shape[0]
    assert x_prompt.shape[0] == 1, "one prompt sequence per call"
    hp, hs = x_prompt[0], x_sample
    kp_l, vp_l, cp_l, ks_l, vs_l, cs_l = [], [], [], [], [], []
    for i in range(depth):
        c = _layer_consts(ln_g[i], w_in[i], q_norm_g[i], k_norm_g[i], sink[i],
                          jnp.swapaxes(conv_w, 0, 1)[:, i:i + 1, :],
                          w_attn_out[i], w_conv_out[i], w_o[i], w_ple_gate[i], w_ple_proj[i])
        (hp, kp, vp, cp), (hs, ks, vs, cs) = _layer(hp, p_prompt[i, 0], hs, p_sample[i], cache_k[i],
                                                    cache_v[i], state_conv[i], c)
        kp_l.append(kp[None])
        vp_l.append(vp[None])
        cp_l.append(cp[None])
        ks_l.append(ks)
        vs_l.append(vs)
        cs_l.append(cs)
    return (hp[None], hs, jnp.stack(kp_l), jnp.stack(vp_l), jnp.stack(cp_l),
            jnp.stack(ks_l), jnp.stack(vs_l), jnp.stack(cs_l))
```

```python
import functools

import numpy as np
import jax
import jax.numpy as jnp
from jax import lax
from jax.experimental import pallas as pl
from jax.experimental.pallas import tpu as pltpu

F32 = jnp.float32
BF16 = jnp.bfloat16

CHUNK = 64
WINDOW = 128
WINDOW_CHUNKS = WINDOW // CHUNK
N_HEADS = 8
N_KV_HEADS = 2
GROUP_HEADS = N_HEADS // N_KV_HEADS
HEAD_DIM = 64
HALF = HEAD_DIM // 2
ATTN_DIM = N_HEADS * HEAD_DIM
KV_DIM = N_KV_HEADS * HEAD_DIM
CONV_DIM = 512
CONV_WIDTH = 3
PAST_LEN = 1024
ROPE_THETA = 10000.0
EPS = 1e-6
NEG = -1e30

OFF_Q = 0
OFF_K = OFF_Q + ATTN_DIM
OFF_V = OFF_K + KV_DIM
OFF_GA = OFF_V + KV_DIM
OFF_B = OFF_GA + ATTN_DIM
OFF_GC_END = OFF_B + 4 * CONV_DIM
OFF_MA = OFF_GC_END

V7X_LANES = 128
V7X_SUBLANES = 8
V7X_VMEM_LIMIT_BYTES = 59 * 1024 * 1024

PROMPT_TILE = 512
PROMPT_SUBTILE = 256
PROMPT_STAGE_SKEW = 1
STAGE_ROWS = 256
STAGE_COLS = 1024
STAGE_SLOTS = 4
_WIN_BLOCK_ORDER = (0, 3, 4, 5, 1, 2)
SAMPLE_BATCH_TILE = 8


def _mm(a, w):
    return jnp.dot(a, w, preferred_element_type=F32)


def _mm_t(a, b):
    return lax.dot_general(a, b, (((1,), (1,)), ((), ())), preferred_element_type=F32)


def _sigmoid(x):
    return 1.0 / (1.0 + jnp.exp(-x))


def _silu(x):
    return x * _sigmoid(x)


def _rmsnorm(x, g):
    ms = jnp.mean(x * x, axis=-1, keepdims=True)
    return x * lax.rsqrt(ms + EPS) * g


def _group_mean(t, bd):
    return _mm(t.astype(BF16), bd)


def _head_norm(t, bd, g):
    ms = _group_mean(t * t, bd)
    return t * lax.rsqrt(ms + EPS) * g


def _rope(xc, cos_t, sin_s):
    lane = lax.broadcasted_iota(jnp.int32, xc.shape, 1)
    upper = (lane & HALF) != 0
    rot = jnp.where(upper, pltpu.roll(xc, HALF, 1), pltpu.roll(xc, V7X_LANES - HALF, 1))
    return xc * cos_t + rot * sin_s


def _rope_sign(shape):
    lane = lax.broadcasted_iota(jnp.int32, shape, 1)
    return jnp.where((lane & HALF) != 0, 1.0, -1.0).astype(F32)


def _kv_variants(t):
    lane = lax.broadcasted_iota(jnp.int32, t.shape, 1)
    lo = lane < HEAD_DIM
    sw = pltpu.roll(t, HEAD_DIM, 1)
    zero = jnp.zeros_like(t)
    return (jnp.where(lo, t, zero).astype(BF16), jnp.where(lo, zero, sw).astype(BF16),
            jnp.where(lo, sw, zero).astype(BF16), jnp.where(lo, zero, t).astype(BF16))


def _kv_variants_t(t):
    tb = t.astype(BF16)
    h0, h1 = tb[0:HEAD_DIM], tb[HEAD_DIM:2 * HEAD_DIM]
    zero = jnp.zeros_like(h0)
    return (jnp.concatenate([h0, zero], axis=0), jnp.concatenate([zero, h0], axis=0),
            jnp.concatenate([h1, zero], axis=0), jnp.concatenate([zero, h1], axis=0))


def _scores(qs, ka, kb, bad, keys_on_lanes=False):
    dot = _mm if keys_on_lanes else _mm_t
    s_e = dot(qs, ka)
    s_o = dot(qs, kb)
    if bad is not None:
        s_e = jnp.where(bad, NEG, s_e)
        s_o = jnp.where(bad, NEG, s_o)
    return s_e, s_o


def _softmax_parts(s, sink):
    m = jnp.maximum(jnp.max(s, axis=-1, keepdims=True), sink)
    e = jnp.exp(s - m)
    r = 1.0 / (jnp.sum(e, axis=-1, keepdims=True) + jnp.exp(sink - m))
    return e.astype(BF16), r


def _pv(p_e, p_o, va, vb, r_e, r_o, keys_on_lanes=False):
    dot = _mm_t if keys_on_lanes else _mm
    o = dot(p_e, va) + dot(p_o, vb)
    lane = lax.broadcasted_iota(jnp.int32, o.shape, 1)
    return o * jnp.where(lane < HEAD_DIM, r_e, r_o)


def _sink_cols(sink_ref, n_blocks, rows_per_pair, blocks_per_kv=1):
    row = lax.broadcasted_iota(jnp.int32, (n_blocks * 2 * rows_per_pair, 1), 0)
    second_pair = (row // rows_per_pair) % 2 == 1
    second_kv = (row // (2 * rows_per_pair * blocks_per_kv)) % 2 == 1

    def pick(odd):
        kv0 = jnp.where(second_pair, sink_ref[2 + odd], sink_ref[odd])
        kv1 = jnp.where(second_pair, sink_ref[GROUP_HEADS + 2 + odd], sink_ref[GROUP_HEADS + odd])
        return jnp.where(second_kv, kv1, kv0)

    return pick(0), pick(1)


def _stage_chunks(shape, col_order=None):
    rows, cols = shape
    assert rows % STAGE_ROWS == 0 and cols % V7X_LANES == 0
    width = max(c for c in range(V7X_LANES, STAGE_COLS + 1, V7X_LANES) if cols % c == 0)
    blocks = list(range(cols // width)) if col_order is None else list(col_order)
    assert sorted(blocks) == list(range(cols // width))
    return [(r0, b * width, STAGE_ROWS, width) for b in blocks for r0 in range(0, rows, STAGE_ROWS)]


class _WeightStager:
    def __init__(self, chunks, stage, sem):
        self.chunks, self.stage, self.sem, self.done = chunks, stage, sem, 0

    def _copy(self, n):
        src, _, r0, c0, rows, cols = self.chunks[n]
        slot = n % STAGE_SLOTS
        return pltpu.make_async_copy(src.at[pl.ds(r0, rows), pl.ds(c0, cols)],
                                     self.stage.at[slot, pl.ds(0, rows), pl.ds(0, cols)],
                                     self.sem.at[slot])

    def start(self):
        for n in range(min(STAGE_SLOTS, len(self.chunks))):
            self._copy(n).start()

    def pump(self, count):
        while self.done < min(count, len(self.chunks)):
            n = self.done
            _, dst, r0, c0, rows, cols = self.chunks[n]
            self._copy(n).wait()
            dst[r0:r0 + rows, c0:c0 + cols] = (
                self.stage[n % STAGE_SLOTS, 0:rows, 0:cols].astype(BF16))
            if n + STAGE_SLOTS < len(self.chunks):
                self._copy(n + STAGE_SLOTS).start()
            self.done += 1

    def finish(self):
        self.pump(len(self.chunks))


def _qkv_proj(x, lng_ref, win_ref):
    hb = _rmsnorm(x, lng_ref[...]).astype(BF16)
    return hb, _mm(hb, win_ref[:, OFF_Q:OFF_GA])


def _tile_lanes(g, width):
    while g.shape[1] < width:
        g = jnp.concatenate([g, g], axis=1)
    return g


def _qk_norm_rope(qkv, gq_ref, gk_ref, bdq_ref, bdk_ref, cos_t, sin_s):
    half_q = ATTN_DIM // 2
    gq = _tile_lanes(gq_ref[...] * (HEAD_DIM ** -0.5), half_q)
    gk = _tile_lanes(gk_ref[...], KV_DIM)
    q_cols = []
    for j in range(2):
        t = qkv[:, j * half_q:(j + 1) * half_q]
        tn = _head_norm(t, bdq_ref[...], gq)
        for c in range(half_q // V7X_LANES):
            q_cols.append(_rope(tn[:, c * V7X_LANES:(c + 1) * V7X_LANES], cos_t, sin_s).astype(BF16))
    kn = _head_norm(qkv[:, OFF_K:OFF_V], bdk_ref[...], gk)
    kr = _rope(kn, cos_t, sin_s)
    v = qkv[:, OFF_V:OFF_GA]
    return q_cols, kr, v


def _conv_proj(hb, win_ref):
    return _mm(hb, win_ref[:, OFF_B:OFF_GC_END])


def _conv_input(bcug, conv_fn):
    b_gate = bcug[:, 0:CONV_DIM]
    u = bcug[:, CONV_DIM:2 * CONV_DIM] * bcug[:, 2 * CONV_DIM:3 * CONV_DIM]
    gate_c = bcug[:, 3 * CONV_DIM:4 * CONV_DIM]
    conv = conv_fn(u)
    return (b_gate * conv * _silu(gate_c)).astype(BF16)


def _prompt_body(i, stager, x_ref, p_ref, lng_ref, gq_ref, gk_ref, invf_ref, sink_ref, cw_ref,
                 bdq_ref, bdk_ref, win_ref, wa_ref, wb_ref, wo_ref, wpg_ref, wpp_ref,
                 y_ref, ko_ref, vo_ref, co_ref, kbuf, vbuf, ubuf, attn_buf, tab):
    tm, d_model = x_ref.shape
    invf = invf_ref[...]
    first = stager is not None

    if first:
        kbuf[:, :, 0:WINDOW] = jnp.zeros((4, KV_DIM, WINDOW), BF16)
        vbuf[:, 0:WINDOW, :] = jnp.zeros((4, WINDOW, V7X_LANES), BF16)
        ubuf[0:V7X_SUBLANES, :] = jnp.zeros((V7X_SUBLANES, CONV_DIM), F32)
        r = lax.broadcasted_iota(jnp.int32, (tm, V7X_LANES), 0).astype(F32)
        ang = r * invf
        sgn = _rope_sign((tm, V7X_LANES))
        c_r = jnp.cos(ang)
        s_r = jnp.sin(ang)
        tab[0] = c_r
        tab[1] = s_r
        tab[2] = c_r * sgn
        tab[3] = s_r * sgn

    base = (i * tm).astype(F32) * invf
    cb = jnp.cos(base)
    sb = jnp.sin(base)
    cos_t = tab[0] * cb - tab[1] * sb
    sin_s = tab[3] * cb + tab[2] * sb

    sub = PROMPT_SUBTILE
    n_sub = tm // sub
    sub_chunks = sub // CHUNK
    assert 2 * CHUNK == V7X_LANES and WINDOW == V7X_LANES
    n_keys = 2 * V7X_LANES

    def key_window_start(r0, c):
        return ((r0 + c * CHUNK) // V7X_LANES) * V7X_LANES

    def mask_window(sc, r0, c):
        lo, hi = sc[:, 0:V7X_LANES], sc[:, V7X_LANES:n_keys]
        lane = lax.broadcasted_iota(jnp.int32, lo.shape, 1)
        no_carry = first and key_window_start(r0, c) < WINDOW
        if no_carry:
            lo = jnp.full_like(lo, NEG)
        if ((r0 + c * CHUNK) // CHUNK) % 2 == 0:
            hi = jnp.where(lane >= CHUNK, NEG, hi)
        elif not no_carry:
            lo = jnp.where(lane < CHUNK, NEG, lo)
        return jnp.concatenate([lo, hi], axis=1)

    pad = V7X_SUBLANES
    blocks = [(c0, vh) for c0 in range(0, sub_chunks, 2) for vh in range(N_KV_HEADS)]
    st = [dict(r0=s * sub) for s in range(n_sub)]

    def stage_rms(s):
        s["hb"] = _rmsnorm(x_ref[s["r0"]:s["r0"] + sub, :], lng_ref[...]).astype(BF16)

    def stage_qkv(s):
        s["qkv"] = _mm(s["hb"], win_ref[:, OFF_Q:OFF_GA])

    def stage_gate_a(s):
        s["sig_a"] = _sigmoid(_mm(s["hb"], win_ref[:, OFF_MA:OFF_MA + d_model]))

    def stage_gate_c(s):
        s["sig_c"] = _sigmoid(_mm(s["hb"], win_ref[:, OFF_MA + d_model:OFF_MA + 2 * d_model]))

    def stage_qk_norm(s):
        r0 = s["r0"]
        s["q_cols"], kr, v = _qk_norm_rope(s.pop("qkv"), gq_ref, gk_ref, bdq_ref, bdk_ref,
                                           cos_t[r0:r0 + sub], sin_s[r0:r0 + sub])
        kr_t = kr.T
        for n, t in enumerate(_kv_variants_t(kr_t)):
            kbuf[n, :, WINDOW + r0:WINDOW + r0 + sub] = t
        for n, t in enumerate(_kv_variants(v)):
            vbuf[n, WINDOW + r0:WINDOW + r0 + sub, :] = t
        if r0 + sub == tm:
            ko_ref[...] = kr_t[:, sub - WINDOW:sub]
            vo_ref[...] = v[sub - WINDOW:sub, :].T

    def stage_scores(s):
        r0 = s["r0"]
        s_e, s_o = [], []
        for c0, vh in blocks:
            start = key_window_start(r0, c0)
            win = slice(start, start + n_keys)
            qs = jnp.concatenate(
                [s["q_cols"][2 * vh + pair][(c0 + dc) * CHUNK:(c0 + dc + 1) * CHUNK]
                 for dc in range(2) for pair in range(2)], axis=0)
            se, so = _scores(qs, kbuf[2 * vh, :, win], kbuf[2 * vh + 1, :, win], None,
                             keys_on_lanes=True)
            for dc in range(2):
                half = slice(dc * 2 * CHUNK, (dc + 1) * 2 * CHUNK)
                s_e.append(mask_window(se[half], r0, c0 + dc))
                s_o.append(mask_window(so[half], r0, c0 + dc))
        s["s_e"] = jnp.concatenate(s_e, axis=0)
        s["s_o"] = jnp.concatenate(s_o, axis=0)
        del s["q_cols"]

    def stage_conv_proj(s):
        s["bcug"] = _conv_proj(s["hb"], win_ref)

    def stage_conv(s):
        r0 = s["r0"]

        def conv_fn(u):
            ubuf[pad + r0:pad + r0 + sub, :] = u
            conv = ubuf[pad + r0 - 2:pad + r0 - 2 + sub, :] * cw_ref[0]
            conv = conv + ubuf[pad + r0 - 1:pad + r0 - 1 + sub, :] * cw_ref[1]
            return conv + u * cw_ref[2]

        s["c_in"] = _conv_input(s.pop("bcug"), conv_fn)

    def stage_conv_out(s):
        s["yc"] = _mm(s.pop("c_in"), wb_ref[...])

    sink_e, sink_o = _sink_cols(sink_ref, 2 * len(blocks), CHUNK, blocks_per_kv=2)

    def stage_softmax(s):
        s["p_e"], s["r_e"] = _softmax_parts(s.pop("s_e"), sink_e)
        s["p_o"], s["r_o"] = _softmax_parts(s.pop("s_o"), sink_o)

    def stage_pv(s):
        r0 = s["r0"]
        for n, (c0, vh) in enumerate(blocks):
            start = key_window_start(r0, c0)
            win = slice(start, start + n_keys)
            br = slice(n * 4 * CHUNK, (n + 1) * 4 * CHUNK)
            o = _pv(s["p_e"][br], s["p_o"][br], vbuf[2 * vh, win, :], vbuf[2 * vh + 1, win, :],
                    s["r_e"][br], s["r_o"][br])
            for dc in range(2):
                rows = slice(r0 + (c0 + dc) * CHUNK, r0 + (c0 + dc + 1) * CHUNK)
                for pair in range(2):
                    part = o[(2 * dc + pair) * CHUNK:(2 * dc + pair + 1) * CHUNK]
                    attn_buf[rows, (2 * vh + pair) * V7X_LANES:(2 * vh + pair + 1) * V7X_LANES] = part

    def stage_attn_gate(s):
        s["silu_ga"] = _silu(_mm(s["hb"], win_ref[:, OFF_GA:OFF_B]))

    def stage_attn_out(s):
        r0 = s["r0"]
        a_in = (attn_buf[r0:r0 + sub, :] * s.pop("silu_ga")).astype(BF16)
        ya = _mm(a_in, wa_ref[...])
        s["mix"] = (s.pop("sig_a") * ya + s.pop("sig_c") * s.pop("yc")).astype(BF16)

    def stage_out_proj(s):
        r0 = s["r0"]
        s["r"] = x_ref[r0:r0 + sub, :] + _mm(s.pop("mix"), wo_ref[...])
        s["pp"] = _mm(p_ref[r0:r0 + sub, :].astype(BF16), wpp_ref[...])

    def stage_ple(s):
        r0 = s["r0"]
        r = s.pop("r")
        gate = _sigmoid(_mm(r.astype(BF16), wpg_ref[...]))
        y_ref[r0:r0 + sub, :] = r + gate * s.pop("pp")

    stages = [stage_rms, stage_qkv, stage_qk_norm, stage_gate_a, stage_scores, stage_gate_c,
              stage_softmax, stage_conv_proj, stage_conv, stage_attn_gate, stage_conv_out,
              stage_pv, stage_attn_out, stage_out_proj, stage_ple]
    needs = {stage_rms: 0, stage_qkv: 4, stage_qk_norm: 8, stage_gate_a: 12, stage_scores: 14,
             stage_gate_c: 16, stage_softmax: 20, stage_conv_proj: 24, stage_conv: 25,
             stage_attn_gate: 25, stage_conv_out: 26, stage_pv: 27, stage_attn_out: 28,
             stage_out_proj: 33, stage_ple: 37}
    for t in range(len(stages) + PROMPT_STAGE_SKEW * (n_sub - 1)):
        for j, s in enumerate(st):
            k = t - j * PROMPT_STAGE_SKEW
            if 0 <= k < len(stages):
                if first and j == 0:
                    stager.pump(needs[stages[k]])
                stages[k](s)

    co_ref[...] = ubuf[pad + tm - 2:pad + tm, :]
    ubuf[0:pad, :] = ubuf[tm:tm + pad, :]
    kbuf[:, :, 0:WINDOW] = kbuf[:, :, tm:tm + WINDOW]
    vbuf[:, 0:WINDOW, :] = vbuf[:, tm:tm + WINDOW, :]


def _sample_body(i, x_ref, p_ref, ck_ref, cv_ref, sc_ref, lng_ref, gq_ref, gk_ref, invf_ref,
                 sink_ref, cw_ref, bdq_ref, bdk_ref, win_ref, wa_ref, wb_ref, wo_ref, wpg_ref,
                 wpp_ref, y_ref, ko_ref, vo_ref, co_ref, ubuf, attn_buf, conv_buf, tab):
    bb, _, cache_len = ck_ref.shape
    rows_total, d_model = x_ref.shape
    t_new = rows_total // bb
    n_keys = cache_len + t_new

    @pl.when(i == 0)
    def _init():
        r = lax.broadcasted_iota(jnp.int32, (rows_total, V7X_LANES), 0)
        pos = (PAST_LEN + lax.rem(r, t_new)).astype(F32)
        ang = pos * invf_ref[...]
        tab[0] = jnp.cos(ang)
        tab[1] = jnp.sin(ang) * _rope_sign((rows_total, V7X_LANES))

    def conv_fn(u):
        pad = V7X_SUBLANES
        for b in range(bb):
            rows = slice(b * t_new, (b + 1) * t_new)
            ub = u[rows]
            ubuf[b, pad - (CONV_WIDTH - 1):pad, :] = sc_ref[b]
            ubuf[b, pad:pad + t_new, :] = ub
            conv = ubuf[b, pad - 2:pad - 2 + t_new, :] * cw_ref[0]
            conv = conv + ubuf[b, pad - 1:pad - 1 + t_new, :] * cw_ref[1]
            conv_buf[rows, :] = conv + ub * cw_ref[2]
            co_ref[b] = ubuf[b, pad + t_new - (CONV_WIDTH - 1):pad + t_new, :]
        return conv_buf[...]

    x = x_ref[...]
    hb, qkv = _qkv_proj(x, lng_ref, win_ref)
    sig_a = _sigmoid(_mm(hb, win_ref[:, OFF_MA:OFF_MA + d_model]))
    q_cols, kr, v = _qk_norm_rope(qkv, gq_ref, gk_ref, bdq_ref, bdk_ref, tab[0], tab[1])

    qi = lax.broadcasted_iota(jnp.int32, (2 * t_new, n_keys), 0)
    q_pos = PAST_LEN + lax.rem(qi, t_new)
    k_pos = PAST_LEN - cache_len + lax.broadcasted_iota(jnp.int32, (2 * t_new, n_keys), 1)
    q_ch = q_pos // CHUNK
    k_ch = k_pos // CHUNK
    bad = jnp.logical_not((k_ch <= q_ch) & (k_ch >= q_ch - WINDOW_CHUNKS))

    blocks = [(b, vh) for b in range(bb) for vh in range(N_KV_HEADS)]
    s_e, s_o, vvars = [], [], []
    kr_t = kr.T
    v_t = v.T
    for b in range(bb):
        rows = slice(b * t_new, (b + 1) * t_new)
        kcat = jnp.concatenate([ck_ref[b], kr_t[:, rows]], axis=1)
        vcat = jnp.concatenate([cv_ref[b], v_t[:, rows]], axis=1)
        ko_ref[b] = kcat[:, n_keys - cache_len:n_keys]
        vo_ref[b] = vcat[:, n_keys - cache_len:n_keys]
        kvar = _kv_variants_t(kcat)
        vvars.append(_kv_variants_t(vcat))
        for vh in range(N_KV_HEADS):
            qs = jnp.concatenate([q_cols[2 * vh][rows], q_cols[2 * vh + 1][rows]], axis=0)
            se, so = _scores(qs, kvar[2 * vh], kvar[2 * vh + 1], bad, keys_on_lanes=True)
            s_e.append(se)
            s_o.append(so)

    sig_c = _sigmoid(_mm(hb, win_ref[:, OFF_MA + d_model:OFF_MA + 2 * d_model]))
    sink_e, sink_o = _sink_cols(sink_ref, len(blocks), t_new)
    p_e, r_e = _softmax_parts(jnp.concatenate(s_e, axis=0), sink_e)
    bcug = _conv_proj(hb, win_ref)
    p_o, r_o = _softmax_parts(jnp.concatenate(s_o, axis=0), sink_o)
    c_in = _conv_input(bcug, conv_fn)
    silu_ga = _silu(_mm(hb, win_ref[:, OFF_GA:OFF_B]))
    pp = _mm(p_ref[...].astype(BF16), wpp_ref[...])
    yc = _mm(c_in, wb_ref[...])

    for n, (b, vh) in enumerate(blocks):
        rows = slice(b * t_new, (b + 1) * t_new)
        br = slice(n * 2 * t_new, (n + 1) * 2 * t_new)
        o = _pv(p_e[br], p_o[br], vvars[b][2 * vh], vvars[b][2 * vh + 1], r_e[br], r_o[br],
                keys_on_lanes=True)
        attn_buf[rows, (2 * vh) * V7X_LANES:(2 * vh + 1) * V7X_LANES] = o[0:t_new]
        attn_buf[rows, (2 * vh + 1) * V7X_LANES:(2 * vh + 2) * V7X_LANES] = o[t_new:2 * t_new]

    ya = _mm((attn_buf[0:rows_total, :] * silu_ga).astype(BF16), wa_ref[...])
    mix = (sig_a * ya + sig_c * yc).astype(BF16)
    r = x + _mm(mix, wo_ref[...])
    gate = _sigmoid(_mm(r.astype(BF16), wpg_ref[...]))
    y_ref[...] = r + gate * pp


def _layer_kernel(n_prompt_steps, *refs):
    (xp_ref, pp_ref, xs_ref, ps_ref, ck_ref, cv_ref, sc_ref,
     lng_ref, gq_ref, gk_ref, invf_ref, sink_ref, cw_ref, bdq_ref, bdk_ref,
     win_hbm, wa_hbm, wb_hbm, wo_hbm, wpg_hbm, wpp_hbm,
     yp_ref, kpo_ref, vpo_ref, cpo_ref, ys_ref, kso_ref, vso_ref, cso_ref,
     kbuf, vbuf, ubuf_p, attn_buf, tab_p, ubuf_s, conv_buf, tab_s,
     win_ref, wa_ref, wb_ref, wo_ref, wpg_ref, wpp_ref, stage, sem) = refs
    i = pl.program_id(0)
    small = (lng_ref, gq_ref, gk_ref, invf_ref, sink_ref, cw_ref, bdq_ref, bdk_ref)
    weights = (win_ref, wa_ref, wb_ref, wo_ref, wpg_ref, wpp_ref)

    prompt_refs = (xp_ref, pp_ref, *small, *weights, yp_ref, kpo_ref, vpo_ref, cpo_ref,
                   kbuf, vbuf, ubuf_p, attn_buf, tab_p)

    @pl.when(i == 0)
    def _first_prompt():
        chunks = [(src, dst) + ch for src, dst, order in (
            (win_hbm, win_ref, _WIN_BLOCK_ORDER), (wb_hbm, wb_ref, None), (wa_hbm, wa_ref, None),
            (wo_hbm, wo_ref, None), (wpp_hbm, wpp_ref, None), (wpg_hbm, wpg_ref, None))
            for ch in _stage_chunks(src.shape, order)]
        stager = _WeightStager(chunks, stage, sem)
        stager.start()
        _prompt_body(i, stager, *prompt_refs)
        stager.finish()

    @pl.when((i > 0) & (i < n_prompt_steps))
    def _prompt():
        _prompt_body(i, None, *prompt_refs)

    @pl.when(i >= n_prompt_steps)
    def _sample():
        _sample_body(i - n_prompt_steps, xs_ref, ps_ref, ck_ref, cv_ref, sc_ref, *small, *weights,
                     ys_ref, kso_ref, vso_ref, cso_ref, ubuf_s, attn_buf, conv_buf, tab_s)


def _const_spec(shape):
    nd = len(shape)
    return pl.BlockSpec(shape, lambda i: (0,) * nd, pipeline_mode=pl.Buffered(1))


def _smem_spec():
    return pl.BlockSpec(memory_space=pltpu.SMEM)


def _block_diag_mean(width):
    idx = np.arange(width) // HEAD_DIM
    return jnp.asarray((idx[:, None] == idx[None, :]).astype(np.float32) / HEAD_DIM, dtype=BF16)


def _layer_consts(ln_g, w_in, q_norm_g, k_norm_g, sink, conv_w, w_attn_out, w_conv_out, w_o,
                  w_ple_gate, w_ple_proj):
    inv_freq = ROPE_THETA ** (-jnp.arange(0, HALF, dtype=F32) * 2.0 / HEAD_DIM)
    return dict(
        lng=ln_g.reshape(1, -1).astype(F32),
        gq=q_norm_g.astype(F32).reshape(1, HEAD_DIM),
        gk=k_norm_g.astype(F32).reshape(1, HEAD_DIM),
        invf=jnp.tile(inv_freq, V7X_LANES // HALF).reshape(1, V7X_LANES),
        sink=sink.astype(F32),
        cw=conv_w.astype(F32),
        bdq=_block_diag_mean(ATTN_DIM // 2),
        bdk=_block_diag_mean(KV_DIM),
        win=w_in.astype(F32),
        wa=w_attn_out.astype(F32),
        wb=w_conv_out.astype(F32),
        wo=w_o.astype(F32),
        wpg=w_ple_gate.astype(F32),
        wpp=w_ple_proj.astype(F32),
    )


_SMALL_KEYS = ("lng", "gq", "gk", "invf", "sink", "cw", "bdq", "bdk")
_STAGED_KEYS = ("win", "wa", "wb", "wo", "wpg", "wpp")


def _heads_first(t):
    t = jnp.moveaxis(t, -3, -1)
    return t.reshape(t.shape[:-3] + (KV_DIM, t.shape[-1]))


def _heads_last(t):
    t = t.reshape(t.shape[:-2] + (N_KV_HEADS, HEAD_DIM, t.shape[-1]))
    return jnp.moveaxis(t, -1, -3)


def _layer(xp, pp, xs, ps, cache_k, cache_v, state_conv, c):
    t, d = xp.shape
    nb, t_new, _ = xs.shape
    cache_len = cache_k.shape[1]
    tm, bb = PROMPT_TILE, SAMPLE_BATCH_TILE
    assert t % tm == 0 and tm % PROMPT_SUBTILE == 0
    assert PROMPT_SUBTILE % CHUNK == 0 and PROMPT_SUBTILE >= WINDOW
    assert nb % bb == 0 and CONV_WIDTH - 1 <= t_new <= cache_len
    n_p, n_s = t // tm, nb // bb
    rows = bb * t_new
    xs2 = xs.reshape(nb * t_new, d)
    ps2 = ps.reshape(nb * t_new, ps.shape[-1])
    ck = _heads_first(cache_k)
    cv = _heads_first(cache_v)

    def p_idx(i):
        return jnp.minimum(i, n_p - 1)

    def s_idx(i):
        return jnp.maximum(i - n_p, 0)

    in_specs = ([pl.BlockSpec((tm, d), lambda i: (p_idx(i), 0)),
                 pl.BlockSpec((tm, pp.shape[1]), lambda i: (p_idx(i), 0)),
                 pl.BlockSpec((rows, d), lambda i: (s_idx(i), 0)),
                 pl.BlockSpec((rows, ps2.shape[1]), lambda i: (s_idx(i), 0)),
                 pl.BlockSpec((bb, KV_DIM, cache_len), lambda i: (s_idx(i), 0, 0)),
                 pl.BlockSpec((bb, KV_DIM, cache_len), lambda i: (s_idx(i), 0, 0)),
                 pl.BlockSpec((bb, CONV_WIDTH - 1, CONV_DIM), lambda i: (s_idx(i), 0, 0))]
                + [_smem_spec() if k == "sink" else _const_spec(c[k].shape) for k in _SMALL_KEYS]
                + [pl.BlockSpec(memory_space=pl.ANY) for _ in _STAGED_KEYS])
    out_shape = (jax.ShapeDtypeStruct((t, d), F32),
                 jax.ShapeDtypeStruct((KV_DIM, WINDOW), F32),
                 jax.ShapeDtypeStruct((KV_DIM, WINDOW), F32),
                 jax.ShapeDtypeStruct((CONV_WIDTH - 1, CONV_DIM), F32),
                 jax.ShapeDtypeStruct((nb * t_new, d), F32),
                 jax.ShapeDtypeStruct((nb, KV_DIM, cache_len), F32),
                 jax.ShapeDtypeStruct((nb, KV_DIM, cache_len), F32),
                 jax.ShapeDtypeStruct((nb, CONV_WIDTH - 1, CONV_DIM), F32))
    out_specs = (pl.BlockSpec((tm, d), lambda i: (p_idx(i), 0)),
                 pl.BlockSpec((KV_DIM, WINDOW), lambda i: (0, 0)),
                 pl.BlockSpec((KV_DIM, WINDOW), lambda i: (0, 0)),
                 pl.BlockSpec((CONV_WIDTH - 1, CONV_DIM), lambda i: (0, 0)),
                 pl.BlockSpec((rows, d), lambda i: (s_idx(i), 0)),
                 pl.BlockSpec((bb, KV_DIM, cache_len), lambda i: (s_idx(i), 0, 0)),
                 pl.BlockSpec((bb, KV_DIM, cache_len), lambda i: (s_idx(i), 0, 0)),
                 pl.BlockSpec((bb, CONV_WIDTH - 1, CONV_DIM), lambda i: (s_idx(i), 0, 0)))
    scratch = ([pltpu.VMEM((4, KV_DIM, WINDOW + tm), BF16),
                pltpu.VMEM((4, WINDOW + tm, V7X_LANES), BF16),
                pltpu.VMEM((V7X_SUBLANES + tm, CONV_DIM), F32),
                pltpu.VMEM((max(tm, rows), ATTN_DIM), F32),
                pltpu.VMEM((4, tm, V7X_LANES), F32),
                pltpu.VMEM((bb, V7X_SUBLANES + t_new, CONV_DIM), F32),
                pltpu.VMEM((rows, CONV_DIM), F32),
                pltpu.VMEM((2, rows, V7X_LANES), F32)]
               + [pltpu.VMEM(c[k].shape, BF16) for k in _STAGED_KEYS]
               + [pltpu.VMEM((STAGE_SLOTS, STAGE_ROWS, STAGE_COLS), F32),
                  pltpu.SemaphoreType.DMA((STAGE_SLOTS,))])
    yp, kpo, vpo, cpo, ys, kso, vso, cso = pl.pallas_call(
        functools.partial(_layer_kernel, n_p),
        grid=(n_p + n_s,),
        in_specs=in_specs,
        out_specs=out_specs,
        out_shape=out_shape,
        scratch_shapes=scratch,
        compiler_params=pltpu.CompilerParams(dimension_semantics=("arbitrary",),
                                             vmem_limit_bytes=V7X_VMEM_LIMIT_BYTES),
        name="hybrid_layer",
    )(xp, pp, xs2, ps2, ck, cv, state_conv, *[c[k] for k in _SMALL_KEYS],
      *[c[k] for k in _STAGED_KEYS])
    return ((yp, _heads_last(kpo), _heads_last(vpo), cpo),
            (ys.reshape(nb, t_new, d), _heads_last(kso), _heads_last(vso), cso))


def kernel(x_prompt, x_sample, p_prompt, p_sample, cache_k, cache_v, state_conv, ln_g, w_in,
           q_norm_g, k_norm_g, sink, conv_w, w_attn_out, w_conv_out, w_o, w_ple_gate, w_ple_proj):
    depth = ln_g.shape[0]
    assert x_prompt.shape[0] == 1, "one prompt sequence per call"
    hp, hs = x_prompt[0], x_sample
    kp_l, vp_l, cp_l, ks_l, vs_l, cs_l = [], [], [], [], [], []
    for i in range(depth):
        c = _layer_consts(ln_g[i], w_in[i], q_norm_g[i], k_norm_g[i], sink[i],
                          jnp.swapaxes(conv_w, 0, 1)[:, i:i + 1, :],
                          w_attn_out[i], w_conv_out[i], w_o[i], w_ple_gate[i], w_ple_proj[i])
        (hp, kp, vp, cp), (hs, ks, vs, cs) = _layer(hp, p_prompt[i, 0], hs, p_sample[i], cache_k[i],
                                                    cache_v[i], state_conv[i], c)
        kp_l.append(kp[None])
        vp_l.append(vp[None])
        cp_l.append(cp[None])
        ks_l.append(ks)
        vs_l.append(vs)
        cs_l.append(cs)
    return (hp[None], hs, jnp.stack(kp_l), jnp.stack(vp_l), jnp.stack(cp_l),
            jnp.stack(ks_l), jnp.stack(vs_l), jnp.stack(cs_l))
```

```python
import functools

import numpy as np
import jax
import jax.numpy as jnp
from jax import lax
from jax.experimental import pallas as pl
from jax.experimental.pallas import tpu as pltpu

F32 = jnp.float32
BF16 = jnp.bfloat16

CHUNK = 64
WINDOW = 128
WINDOW_CHUNKS = WINDOW // CHUNK
N_HEADS = 8
N_KV_HEADS = 2
GROUP_HEADS = N_HEADS // N_KV_HEADS
HEAD_DIM = 64
HALF = HEAD_DIM // 2
ATTN_DIM = N_HEADS * HEAD_DIM
KV_DIM = N_KV_HEADS * HEAD_DIM
CONV_DIM = 512
CONV_WIDTH = 3
PAST_LEN = 1024
ROPE_THETA = 10000.0
EPS = 1e-6
NEG = -1e30

OFF_Q = 0
OFF_K = OFF_Q + ATTN_DIM
OFF_V = OFF_K + KV_DIM
OFF_GA = OFF_V + KV_DIM
OFF_B = OFF_GA + ATTN_DIM
OFF_GC_END = OFF_B + 4 * CONV_DIM
OFF_MA = OFF_GC_END

V7X_LANES = 128
V7X_SUBLANES = 8
V7X_VMEM_LIMIT_BYTES = 59 * 1024 * 1024

PROMPT_TILE = 512
PROMPT_SUBTILE = 256
PROMPT_STAGE_SKEW = 1
STAGE_ROWS = 256
STAGE_COLS = 1024
STAGE_SLOTS = 4
_WIN_BLOCK_ORDER = (0, 3, 4, 5, 1, 2)
SAMPLE_BATCH_TILE = 8


def _mm(a, w):
    return jnp.dot(a, w, preferred_element_type=F32)


def _mm_t(a, b):
    return lax.dot_general(a, b, (((1,), (1,)), ((), ())), preferred_element_type=F32)


def _sigmoid(x):
    return 0.5 * jnp.tanh(0.5 * x) + 0.5


def _silu(x):
    return x * _sigmoid(x)


def _rmsnorm(x, g):
    ms = jnp.mean(x * x, axis=-1, keepdims=True)
    return x * lax.rsqrt(ms + EPS) * g


def _group_mean(t, bd):
    return _mm(t.astype(BF16), bd)


def _head_norm(t, bd, g):
    ms = _group_mean(t * t, bd)
    return t * lax.rsqrt(ms + EPS) * g


def _rope(xc, cos_t, sin_s):
    lane = lax.broadcasted_iota(jnp.int32, xc.shape, 1)
    upper = (lane & HALF) != 0
    rot = jnp.where(upper, pltpu.roll(xc, HALF, 1), pltpu.roll(xc, V7X_LANES - HALF, 1))
    return xc * cos_t + rot * sin_s


def _rope_sign(shape):
    lane = lax.broadcasted_iota(jnp.int32, shape, 1)
    return jnp.where((lane & HALF) != 0, 1.0, -1.0).astype(F32)


def _kv_variants(t):
    lane = lax.broadcasted_iota(jnp.int32, t.shape, 1)
    lo = lane < HEAD_DIM
    sw = pltpu.roll(t, HEAD_DIM, 1)
    zero = jnp.zeros_like(t)
    return (jnp.where(lo, t, zero).astype(BF16), jnp.where(lo, zero, sw).astype(BF16),
            jnp.where(lo, sw, zero).astype(BF16), jnp.where(lo, zero, t).astype(BF16))


def _kv_variants_t(t):
    tb = t.astype(BF16)
    h0, h1 = tb[0:HEAD_DIM], tb[HEAD_DIM:2 * HEAD_DIM]
    zero = jnp.zeros_like(h0)
    return (jnp.concatenate([h0, zero], axis=0), jnp.concatenate([zero, h0], axis=0),
            jnp.concatenate([h1, zero], axis=0), jnp.concatenate([zero, h1], axis=0))


def _scores(qs, ka, kb, bad, keys_on_lanes=False):
    dot = _mm if keys_on_lanes else _mm_t
    s_e = dot(qs, ka)
    s_o = dot(qs, kb)
    if bad is not None:
        s_e = jnp.where(bad, NEG, s_e)
        s_o = jnp.where(bad, NEG, s_o)
    return s_e, s_o


def _softmax_parts(s, sink):
    m = jnp.maximum(jnp.max(s, axis=-1, keepdims=True), sink)
    e = jnp.exp(s - m)
    r = 1.0 / (jnp.sum(e, axis=-1, keepdims=True) + jnp.exp(sink - m))
    return e.astype(BF16), r


def _pv(p_e, p_o, va, vb, r_e, r_o, keys_on_lanes=False):
    dot = _mm_t if keys_on_lanes else _mm
    o = dot(p_e, va) + dot(p_o, vb)
    lane = lax.broadcasted_iota(jnp.int32, o.shape, 1)
    return o * jnp.where(lane < HEAD_DIM, r_e, r_o)


def _sink_cols(sink_ref, n_blocks, rows_per_pair, blocks_per_kv=1):
    row = lax.broadcasted_iota(jnp.int32, (n_blocks * 2 * rows_per_pair, 1), 0)
    second_pair = (row // rows_per_pair) % 2 == 1
    second_kv = (row // (2 * rows_per_pair * blocks_per_kv)) % 2 == 1

    def pick(odd):
        kv0 = jnp.where(second_pair, sink_ref[2 + odd], sink_ref[odd])
        kv1 = jnp.where(second_pair, sink_ref[GROUP_HEADS + 2 + odd], sink_ref[GROUP_HEADS + odd])
        return jnp.where(second_kv, kv1, kv0)

    return pick(0), pick(1)


def _stage_chunks(shape, col_order=None):
    rows, cols = shape
    assert rows % STAGE_ROWS == 0 and cols % V7X_LANES == 0
    width = max(c for c in range(V7X_LANES, STAGE_COLS + 1, V7X_LANES) if cols % c == 0)
    blocks = list(range(cols // width)) if col_order is None else list(col_order)
    assert sorted(blocks) == list(range(cols // width))
    return [(r0, b * width, STAGE_ROWS, width) for b in blocks for r0 in range(0, rows, STAGE_ROWS)]


class _WeightStager:
    def __init__(self, chunks, stage, sem):
        self.chunks, self.stage, self.sem, self.done = chunks, stage, sem, 0

    def _copy(self, n):
        src, _, r0, c0, rows, cols = self.chunks[n]
        slot = n % STAGE_SLOTS
        return pltpu.make_async_copy(src.at[pl.ds(r0, rows), pl.ds(c0, cols)],
                                     self.stage.at[slot, pl.ds(0, rows), pl.ds(0, cols)],
                                     self.sem.at[slot])

    def start(self):
        for n in range(min(STAGE_SLOTS, len(self.chunks))):
            self._copy(n).start()

    def pump(self, count):
        while self.done < min(count, len(self.chunks)):
            n = self.done
            _, dst, r0, c0, rows, cols = self.chunks[n]
            self._copy(n).wait()
            dst[r0:r0 + rows, c0:c0 + cols] = (
                self.stage[n % STAGE_SLOTS, 0:rows, 0:cols].astype(BF16))
            if n + STAGE_SLOTS < len(self.chunks):
                self._copy(n + STAGE_SLOTS).start()
            self.done += 1

    def finish(self):
        self.pump(len(self.chunks))


def _qkv_proj(x, lng_ref, win_ref):
    hb = _rmsnorm(x, lng_ref[...]).astype(BF16)
    return hb, _mm(hb, win_ref[:, OFF_Q:OFF_GA])


def _tile_lanes(g, width):
    while g.shape[1] < width:
        g = jnp.concatenate([g, g], axis=1)
    return g


def _qk_norm_rope(qkv, gq_ref, gk_ref, bdq_ref, bdk_ref, cos_t, sin_s):
    half_q = ATTN_DIM // 2
    gq = _tile_lanes(gq_ref[...] * (HEAD_DIM ** -0.5), half_q)
    gk = _tile_lanes(gk_ref[...], KV_DIM)
    q_cols = []
    for j in range(2):
        t = qkv[:, j * half_q:(j + 1) * half_q]
        tn = _head_norm(t, bdq_ref[...], gq)
        for c in range(half_q // V7X_LANES):
            q_cols.append(_rope(tn[:, c * V7X_LANES:(c + 1) * V7X_LANES], cos_t, sin_s).astype(BF16))
    kn = _head_norm(qkv[:, OFF_K:OFF_V], bdk_ref[...], gk)
    kr = _rope(kn, cos_t, sin_s)
    v = qkv[:, OFF_V:OFF_GA]
    return q_cols, kr, v


def _conv_proj(hb, win_ref):
    return _mm(hb, win_ref[:, OFF_B:OFF_GC_END])


def _conv_input(bcug, conv_fn):
    b_gate = bcug[:, 0:CONV_DIM]
    u = bcug[:, CONV_DIM:2 * CONV_DIM] * bcug[:, 2 * CONV_DIM:3 * CONV_DIM]
    gate_c = bcug[:, 3 * CONV_DIM:4 * CONV_DIM]
    conv = conv_fn(u)
    return (b_gate * conv * _silu(gate_c)).astype(BF16)


def _prompt_body(i, stager, x_ref, p_ref, lng_ref, gq_ref, gk_ref, invf_ref, sink_ref, cw_ref,
                 bdq_ref, bdk_ref, win_ref, wa_ref, wb_ref, wo_ref, wpg_ref, wpp_ref,
                 y_ref, ko_ref, vo_ref, co_ref, kbuf, vbuf, ubuf, attn_buf, tab):
    tm, d_model = x_ref.shape
    invf = invf_ref[...]
    first = stager is not None

    if first:
        kbuf[:, :, 0:WINDOW] = jnp.zeros((4, KV_DIM, WINDOW), BF16)
        vbuf[:, 0:WINDOW, :] = jnp.zeros((4, WINDOW, V7X_LANES), BF16)
        ubuf[0:V7X_SUBLANES, :] = jnp.zeros((V7X_SUBLANES, CONV_DIM), F32)
        r = lax.broadcasted_iota(jnp.int32, (tm, V7X_LANES), 0).astype(F32)
        ang = r * invf
        sgn = _rope_sign((tm, V7X_LANES))
        c_r = jnp.cos(ang)
        s_r = jnp.sin(ang)
        tab[0] = c_r
        tab[1] = s_r
        tab[2] = c_r * sgn
        tab[3] = s_r * sgn

    base = (i * tm).astype(F32) * invf
    cb = jnp.cos(base)
    sb = jnp.sin(base)
    cos_t = tab[0] * cb - tab[1] * sb
    sin_s = tab[3] * cb + tab[2] * sb

    sub = PROMPT_SUBTILE
    n_sub = tm // sub
    sub_chunks = sub // CHUNK
    assert 2 * CHUNK == V7X_LANES and WINDOW == V7X_LANES
    n_keys = 2 * V7X_LANES

    def key_window_start(r0, c):
        return ((r0 + c * CHUNK) // V7X_LANES) * V7X_LANES

    def mask_window(sc, r0, c):
        lo, hi = sc[:, 0:V7X_LANES], sc[:, V7X_LANES:n_keys]
        lane = lax.broadcasted_iota(jnp.int32, lo.shape, 1)
        no_carry = first and key_window_start(r0, c) < WINDOW
        if no_carry:
            lo = jnp.full_like(lo, NEG)
        if ((r0 + c * CHUNK) // CHUNK) % 2 == 0:
            hi = jnp.where(lane >= CHUNK, NEG, hi)
        elif not no_carry:
            lo = jnp.where(lane < CHUNK, NEG, lo)
        return jnp.concatenate([lo, hi], axis=1)

    pad = V7X_SUBLANES
    blocks = [(c0, vh) for c0 in range(0, sub_chunks, 2) for vh in range(N_KV_HEADS)]
    st = [dict(r0=s * sub) for s in range(n_sub)]

    def stage_rms(s):
        s["hb"] = _rmsnorm(x_ref[s["r0"]:s["r0"] + sub, :], lng_ref[...]).astype(BF16)

    def stage_qkv(s):
        s["qkv"] = _mm(s["hb"], win_ref[:, OFF_Q:OFF_GA])

    def stage_gate_a(s):
        s["sig_a"] = _sigmoid(_mm(s["hb"], win_ref[:, OFF_MA:OFF_MA + d_model]))

    def stage_gate_c(s):
        s["sig_c"] = _sigmoid(_mm(s["hb"], win_ref[:, OFF_MA + d_model:OFF_MA + 2 * d_model]))

    def stage_qk_norm(s):
        r0 = s["r0"]
        s["q_cols"], kr, v = _qk_norm_rope(s.pop("qkv"), gq_ref, gk_ref, bdq_ref, bdk_ref,
                                           cos_t[r0:r0 + sub], sin_s[r0:r0 + sub])
        kr_t = kr.T
        for n, t in enumerate(_kv_variants_t(kr_t)):
            kbuf[n, :, WINDOW + r0:WINDOW + r0 + sub] = t
        for n, t in enumerate(_kv_variants(v)):
            vbuf[n, WINDOW + r0:WINDOW + r0 + sub, :] = t
        if r0 + sub == tm:
            ko_ref[...] = kr_t[:, sub - WINDOW:sub]
            vo_ref[...] = v[sub - WINDOW:sub, :].T

    def stage_scores(s):
        r0 = s["r0"]
        s_e, s_o = [], []
        for c0, vh in blocks:
            start = key_window_start(r0, c0)
            win = slice(start, start + n_keys)
            qs = jnp.concatenate(
                [s["q_cols"][2 * vh + pair][(c0 + dc) * CHUNK:(c0 + dc + 1) * CHUNK]
                 for dc in range(2) for pair in range(2)], axis=0)
            se, so = _scores(qs, kbuf[2 * vh, :, win], kbuf[2 * vh + 1, :, win], None,
                             keys_on_lanes=True)
            for dc in range(2):
                half = slice(dc * 2 * CHUNK, (dc + 1) * 2 * CHUNK)
                s_e.append(mask_window(se[half], r0, c0 + dc))
                s_o.append(mask_window(so[half], r0, c0 + dc))
        s["s_e"] = jnp.concatenate(s_e, axis=0)
        s["s_o"] = jnp.concatenate(s_o, axis=0)
        del s["q_cols"]

    def stage_conv_proj(s):
        s["bcug"] = _conv_proj(s["hb"], win_ref)

    def stage_conv(s):
        r0 = s["r0"]

        def conv_fn(u):
            ubuf[pad + r0:pad + r0 + sub, :] = u
            conv = ubuf[pad + r0 - 2:pad + r0 - 2 + sub, :] * cw_ref[0]
            conv = conv + ubuf[pad + r0 - 1:pad + r0 - 1 + sub, :] * cw_ref[1]
            return conv + u * cw_ref[2]

        s["c_in"] = _conv_input(s.pop("bcug"), conv_fn)

    def stage_conv_out(s):
        s["yc"] = _mm(s.pop("c_in"), wb_ref[...])

    sink_e, sink_o = _sink_cols(sink_ref, 2 * len(blocks), CHUNK, blocks_per_kv=2)

    def stage_softmax(s):
        s["p_e"], s["r_e"] = _softmax_parts(s.pop("s_e"), sink_e)
        s["p_o"], s["r_o"] = _softmax_parts(s.pop("s_o"), sink_o)

    def stage_pv(s):
        r0 = s["r0"]
        for n, (c0, vh) in enumerate(blocks):
            start = key_window_start(r0, c0)
            win = slice(start, start + n_keys)
            br = slice(n * 4 * CHUNK, (n + 1) * 4 * CHUNK)
            o = _pv(s["p_e"][br], s["p_o"][br], vbuf[2 * vh, win, :], vbuf[2 * vh + 1, win, :],
                    s["r_e"][br], s["r_o"][br])
            for dc in range(2):
                rows = slice(r0 + (c0 + dc) * CHUNK, r0 + (c0 + dc + 1) * CHUNK)
                for pair in range(2):
                    part = o[(2 * dc + pair) * CHUNK:(2 * dc + pair + 1) * CHUNK]
                    attn_buf[rows, (2 * vh + pair) * V7X_LANES:(2 * vh + pair + 1) * V7X_LANES] = part

    def stage_attn_gate(s):
        s["silu_ga"] = _silu(_mm(s["hb"], win_ref[:, OFF_GA:OFF_B]))

    def stage_attn_out(s):
        r0 = s["r0"]
        a_in = (attn_buf[r0:r0 + sub, :] * s.pop("silu_ga")).astype(BF16)
        ya = _mm(a_in, wa_ref[...])
        s["mix"] = (s.pop("sig_a") * ya + s.pop("sig_c") * s.pop("yc")).astype(BF16)

    def stage_out_proj(s):
        r0 = s["r0"]
        s["r"] = x_ref[r0:r0 + sub, :] + _mm(s.pop("mix"), wo_ref[...])
        s["pp"] = _mm(p_ref[r0:r0 + sub, :].astype(BF16), wpp_ref[...])

    def stage_ple(s):
        r0 = s["r0"]
        r = s.pop("r")
        gate = _sigmoid(_mm(r.astype(BF16), wpg_ref[...]))
        y_ref[r0:r0 + sub, :] = r + gate * s.pop("pp")

    stages = [stage_rms, stage_qkv, stage_qk_norm, stage_gate_a, stage_scores, stage_gate_c,
              stage_softmax, stage_conv_proj, stage_conv, stage_attn_gate, stage_conv_out,
              stage_pv, stage_attn_out, stage_out_proj, stage_ple]
    needs = {stage_rms: 0, stage_qkv: 4, stage_qk_norm: 8, stage_gate_a: 12, stage_scores: 14,
             stage_gate_c: 16, stage_softmax: 20, stage_conv_proj: 24, stage_conv: 25,
             stage_attn_gate: 25, stage_conv_out: 26, stage_pv: 27, stage_attn_out: 28,
             stage_out_proj: 33, stage_ple: 37}
    for t in range(len(stages) + PROMPT_STAGE_SKEW * (n_sub - 1)):
        for j, s in enumerate(st):
            k = t - j * PROMPT_STAGE_SKEW
            if 0 <= k < len(stages):
                if first and j == 0:
                    stager.pump(needs[stages[k]])
                stages[k](s)

    co_ref[...] = ubuf[pad + tm - 2:pad + tm, :]
    ubuf[0:pad, :] = ubuf[tm:tm + pad, :]
    kbuf[:, :, 0:WINDOW] = kbuf[:, :, tm:tm + WINDOW]
    vbuf[:, 0:WINDOW, :] = vbuf[:, tm:tm + WINDOW, :]


def _sample_body(i, x_ref, p_ref, ck_ref, cv_ref, sc_ref, lng_ref, gq_ref, gk_ref, invf_ref,
                 sink_ref, cw_ref, bdq_ref, bdk_ref, win_ref, wa_ref, wb_ref, wo_ref, wpg_ref,
                 wpp_ref, y_ref, ko_ref, vo_ref, co_ref, ubuf, attn_buf, conv_buf, tab):
    bb, _, cache_len = ck_ref.shape
    rows_total, d_model = x_ref.shape
    t_new = rows_total // bb
    n_keys = cache_len + t_new

    @pl.when(i == 0)
    def _init():
        r = lax.broadcasted_iota(jnp.int32, (rows_total, V7X_LANES), 0)
        pos = (PAST_LEN + lax.rem(r, t_new)).astype(F32)
        ang = pos * invf_ref[...]
        tab[0] = jnp.cos(ang)
        tab[1] = jnp.sin(ang) * _rope_sign((rows_total, V7X_LANES))

    def conv_fn(u):
        pad = V7X_SUBLANES
        for b in range(bb):
            rows = slice(b * t_new, (b + 1) * t_new)
            ub = u[rows]
            ubuf[b, pad - (CONV_WIDTH - 1):pad, :] = sc_ref[b]
            ubuf[b, pad:pad + t_new, :] = ub
            conv = ubuf[b, pad - 2:pad - 2 + t_new, :] * cw_ref[0]
            conv = conv + ubuf[b, pad - 1:pad - 1 + t_new, :] * cw_ref[1]
            conv_buf[rows, :] = conv + ub * cw_ref[2]
            co_ref[b] = ubuf[b, pad + t_new - (CONV_WIDTH - 1):pad + t_new, :]
        return conv_buf[...]

    x = x_ref[...]
    hb, qkv = _qkv_proj(x, lng_ref, win_ref)
    sig_a = _sigmoid(_mm(hb, win_ref[:, OFF_MA:OFF_MA + d_model]))
    q_cols, kr, v = _qk_norm_rope(qkv, gq_ref, gk_ref, bdq_ref, bdk_ref, tab[0], tab[1])

    qi = lax.broadcasted_iota(jnp.int32, (2 * t_new, n_keys), 0)
    q_pos = PAST_LEN + lax.rem(qi, t_new)
    k_pos = PAST_LEN - cache_len + lax.broadcasted_iota(jnp.int32, (2 * t_new, n_keys), 1)
    q_ch = q_pos // CHUNK
    k_ch = k_pos // CHUNK
    bad = jnp.logical_not((k_ch <= q_ch) & (k_ch >= q_ch - WINDOW_CHUNKS))

    blocks = [(b, vh) for b in range(bb) for vh in range(N_KV_HEADS)]
    s_e, s_o, vvars = [], [], []
    kr_t = kr.T
    v_t = v.T
    for b in range(bb):
        rows = slice(b * t_new, (b + 1) * t_new)
        kcat = jnp.concatenate([ck_ref[b], kr_t[:, rows]], axis=1)
        vcat = jnp.concatenate([cv_ref[b], v_t[:, rows]], axis=1)
        ko_ref[b] = kcat[:, n_keys - cache_len:n_keys]
        vo_ref[b] = vcat[:, n_keys - cache_len:n_keys]
        kvar = _kv_variants_t(kcat)
        vvars.append(_kv_variants_t(vcat))
        for vh in range(N_KV_HEADS):
            qs = jnp.concatenate([q_cols[2 * vh][rows], q_cols[2 * vh + 1][rows]], axis=0)
            se, so = _scores(qs, kvar[2 * vh], kvar[2 * vh + 1], bad, keys_on_lanes=True)
            s_e.append(se)
            s_o.append(so)

    sig_c = _sigmoid(_mm(hb, win_ref[:, OFF_MA + d_model:OFF_MA + 2 * d_model]))
    sink_e, sink_o = _sink_cols(sink_ref, len(blocks), t_new)
    p_e, r_e = _softmax_parts(jnp.concatenate(s_e, axis=0), sink_e)
    bcug = _conv_proj(hb, win_ref)
    p_o, r_o = _softmax_parts(jnp.concatenate(s_o, axis=0), sink_o)
    c_in = _conv_input(bcug, conv_fn)
    silu_ga = _silu(_mm(hb, win_ref[:, OFF_GA:OFF_B]))
    pp = _mm(p_ref[...].astype(BF16), wpp_ref[...])
    yc = _mm(c_in, wb_ref[...])

    for n, (b, vh) in enumerate(blocks):
        rows = slice(b * t_new, (b + 1) * t_new)
        br = slice(n * 2 * t_new, (n + 1) * 2 * t_new)
        o = _pv(p_e[br], p_o[br], vvars[b][2 * vh], vvars[b][2 * vh + 1], r_e[br], r_o[br],
                keys_on_lanes=True)
        attn_buf[rows, (2 * vh) * V7X_LANES:(2 * vh + 1) * V7X_LANES] = o[0:t_new]
        attn_buf[rows, (2 * vh + 1) * V7X_LANES:(2 * vh + 2) * V7X_LANES] = o[t_new:2 * t_new]

    ya = _mm((attn_buf[0:rows_total, :] * silu_ga).astype(BF16), wa_ref[...])
    mix = (sig_a * ya + sig_c * yc).astype(BF16)
    r = x + _mm(mix, wo_ref[...])
    gate = _sigmoid(_mm(r.astype(BF16), wpg_ref[...]))
    y_ref[...] = r + gate * pp


def _layer_kernel(n_prompt_steps, *refs):
    (xp_ref, pp_ref, xs_ref, ps_ref, ck_ref, cv_ref, sc_ref,
     lng_ref, gq_ref, gk_ref, invf_ref, sink_ref, cw_ref, bdq_ref, bdk_ref,
     win_hbm, wa_hbm, wb_hbm, wo_hbm, wpg_hbm, wpp_hbm,
     yp_ref, kpo_ref, vpo_ref, cpo_ref, ys_ref, kso_ref, vso_ref, cso_ref,
     kbuf, vbuf, ubuf_p, attn_buf, tab_p, ubuf_s, conv_buf, tab_s,
     win_ref, wa_ref, wb_ref, wo_ref, wpg_ref, wpp_ref, stage, sem) = refs
    i = pl.program_id(0)
    small = (lng_ref, gq_ref, gk_ref, invf_ref, sink_ref, cw_ref, bdq_ref, bdk_ref)
    weights = (win_ref, wa_ref, wb_ref, wo_ref, wpg_ref, wpp_ref)

    prompt_refs = (xp_ref, pp_ref, *small, *weights, yp_ref, kpo_ref, vpo_ref, cpo_ref,
                   kbuf, vbuf, ubuf_p, attn_buf, tab_p)

    @pl.when(i == 0)
    def _first_prompt():
        chunks = [(src, dst) + ch for src, dst, order in (
            (win_hbm, win_ref, _WIN_BLOCK_ORDER), (wb_hbm, wb_ref, None), (wa_hbm, wa_ref, None),
            (wo_hbm, wo_ref, None), (wpp_hbm, wpp_ref, None), (wpg_hbm, wpg_ref, None))
            for ch in _stage_chunks(src.shape, order)]
        stager = _WeightStager(chunks, stage, sem)
        stager.start()
        _prompt_body(i, stager, *prompt_refs)
        stager.finish()

    @pl.when((i > 0) & (i < n_prompt_steps))
    def _prompt():
        _prompt_body(i, None, *prompt_refs)

    @pl.when(i >= n_prompt_steps)
    def _sample():
        _sample_body(i - n_prompt_steps, xs_ref, ps_ref, ck_ref, cv_ref, sc_ref, *small, *weights,
                     ys_ref, kso_ref, vso_ref, cso_ref, ubuf_s, attn_buf, conv_buf, tab_s)


def _const_spec(shape):
    nd = len(shape)
    return pl.BlockSpec(shape, lambda i: (0,) * nd, pipeline_mode=pl.Buffered(1))


def _smem_spec():
    return pl.BlockSpec(memory_space=pltpu.SMEM)


def _block_diag_mean(width):
    idx = np.arange(width) // HEAD_DIM
    return jnp.asarray((idx[:, None] == idx[None, :]).astype(np.float32) / HEAD_DIM, dtype=BF16)


def _layer_consts(ln_g, w_in, q_norm_g, k_norm_g, sink, conv_w, w_attn_out, w_conv_out, w_o,
                  w_ple_gate, w_ple_proj):
    inv_freq = ROPE_THETA ** (-jnp.arange(0, HALF, dtype=F32) * 2.0 / HEAD_DIM)
    return dict(
        lng=ln_g.reshape(1, -1).astype(F32),
        gq=q_norm_g.astype(F32).reshape(1, HEAD_DIM),
        gk=k_norm_g.astype(F32).reshape(1, HEAD_DIM),
        invf=jnp.tile(inv_freq, V7X_LANES // HALF).reshape(1, V7X_LANES),
        sink=sink.astype(F32),
        cw=conv_w.astype(F32),
        bdq=_block_diag_mean(ATTN_DIM // 2),
        bdk=_block_diag_mean(KV_DIM),
        win=w_in.astype(F32),
        wa=w_attn_out.astype(F32),
        wb=w_conv_out.astype(F32),
        wo=w_o.astype(F32),
        wpg=w_ple_gate.astype(F32),
        wpp=w_ple_proj.astype(F32),
    )


_SMALL_KEYS = ("lng", "gq", "gk", "invf", "sink", "cw", "bdq", "bdk")
_STAGED_KEYS = ("win", "wa", "wb", "wo", "wpg", "wpp")


def _heads_first(t):
    t = jnp.moveaxis(t, -3, -1)
    return t.reshape(t.shape[:-3] + (KV_DIM, t.shape[-1]))


def _heads_last(t):
    t = t.reshape(t.shape[:-2] + (N_KV_HEADS, HEAD_DIM, t.shape[-1]))
    return jnp.moveaxis(t, -1, -3)


def _layer(xp, pp, xs, ps, cache_k, cache_v, state_conv, c):
    t, d = xp.shape
    nb, t_new, _ = xs.shape
    cache_len = cache_k.shape[1]
    tm, bb = PROMPT_TILE, SAMPLE_BATCH_TILE
    assert t % tm == 0 and tm % PROMPT_SUBTILE == 0
    assert PROMPT_SUBTILE % CHUNK == 0 and PROMPT_SUBTILE >= WINDOW
    assert nb % bb == 0 and CONV_WIDTH - 1 <= t_new <= cache_len
    n_p, n_s = t // tm, nb // bb
    rows = bb * t_new
    xs2 = xs.reshape(nb * t_new, d)
    ps2 = ps.reshape(nb * t_new, ps.shape[-1])
    ck = _heads_first(cache_k)
    cv = _heads_first(cache_v)

    def p_idx(i):
        return jnp.minimum(i, n_p - 1)

    def s_idx(i):
        return jnp.maximum(i - n_p, 0)

    in_specs = ([pl.BlockSpec((tm, d), lambda i: (p_idx(i), 0)),
                 pl.BlockSpec((tm, pp.shape[1]), lambda i: (p_idx(i), 0)),
                 pl.BlockSpec((rows, d), lambda i: (s_idx(i), 0)),
                 pl.BlockSpec((rows, ps2.shape[1]), lambda i: (s_idx(i), 0)),
                 pl.BlockSpec((bb, KV_DIM, cache_len), lambda i: (s_idx(i), 0, 0)),
                 pl.BlockSpec((bb, KV_DIM, cache_len), lambda i: (s_idx(i), 0, 0)),
                 pl.BlockSpec((bb, CONV_WIDTH - 1, CONV_DIM), lambda i: (s_idx(i), 0, 0))]
                + [_smem_spec() if k == "sink" else _const_spec(c[k].shape) for k in _SMALL_KEYS]
                + [pl.BlockSpec(memory_space=pl.ANY) for _ in _STAGED_KEYS])
    out_shape = (jax.ShapeDtypeStruct((t, d), F32),
                 jax.ShapeDtypeStruct((KV_DIM, WINDOW), F32),
                 jax.ShapeDtypeStruct((KV_DIM, WINDOW), F32),
                 jax.ShapeDtypeStruct((CONV_WIDTH - 1, CONV_DIM), F32),
                 jax.ShapeDtypeStruct((nb * t_new, d), F32),
                 jax.ShapeDtypeStruct((nb, KV_DIM, cache_len), F32),
                 jax.ShapeDtypeStruct((nb, KV_DIM, cache_len), F32),
                 jax.ShapeDtypeStruct((nb, CONV_WIDTH - 1, CONV_DIM), F32))
    out_specs = (pl.BlockSpec((tm, d), lambda i: (p_idx(i), 0)),
                 pl.BlockSpec((KV_DIM, WINDOW), lambda i: (0, 0)),
                 pl.BlockSpec((KV_DIM, WINDOW), lambda i: (0, 0)),
                 pl.BlockSpec((CONV_WIDTH - 1, CONV_DIM), lambda i: (0, 0)),
                 pl.BlockSpec((rows, d), lambda i: (s_idx(i), 0)),
                 pl.BlockSpec((bb, KV_DIM, cache_len), lambda i: (s_idx(i), 0, 0)),
                 pl.BlockSpec((bb, KV_DIM, cache_len), lambda i: (s_idx(i), 0, 0)),
                 pl.BlockSpec((bb, CONV_WIDTH - 1, CONV_DIM), lambda i: (s_idx(i), 0, 0)))
    scratch = ([pltpu.VMEM((4, KV_DIM, WINDOW + tm), BF16),
                pltpu.VMEM((4, WINDOW + tm, V7X_LANES), BF16),
                pltpu.VMEM((V7X_SUBLANES + tm, CONV_DIM), F32),
                pltpu.VMEM((max(tm, rows), ATTN_DIM), F32),
                pltpu.VMEM((4, tm, V7X_LANES), F32),
                pltpu.VMEM((bb, V7X_SUBLANES + t_new, CONV_DIM), F32),
                pltpu.VMEM((rows, CONV_DIM), F32),
                pltpu.VMEM((2, rows, V7X_LANES), F32)]
               + [pltpu.VMEM(c[k].shape, BF16) for k in _STAGED_KEYS]
               + [pltpu.VMEM((STAGE_SLOTS, STAGE_ROWS, STAGE_COLS), F32),
                  pltpu.SemaphoreType.DMA((STAGE_SLOTS,))])
    yp, kpo, vpo, cpo, ys, kso, vso, cso = pl.pallas_call(
        functools.partial(_layer_kernel, n_p),
        grid=(n_p + n_s,),
        in_specs=in_specs,
        out_specs=out_specs,
        out_shape=out_shape,
        scratch_shapes=scratch,
        compiler_params=pltpu.CompilerParams(dimension_semantics=("arbitrary",),
                                             vmem_limit_bytes=V7X_VMEM_LIMIT_BYTES),
        name="hybrid_layer",
    )(xp, pp, xs2, ps2, ck, cv, state_conv, *[c[k] for k in _SMALL_KEYS],
      *[c[k] for k in _STAGED_KEYS])
    return ((yp, _heads_last(kpo), _heads_last(vpo), cpo),
            (ys.reshape(nb, t_new, d), _heads_last(kso), _heads_last(vso), cso))


def kernel(x_prompt, x_sample, p_prompt, p_sample, cache_k, cache_v, state_conv, ln_g, w_in,
           q_norm_g, k_norm_g, sink, conv_w, w_attn_out, w_conv_out, w_o, w_ple_gate, w_ple_proj):
    depth = ln_g.shape[0]
    assert x_prompt.shape[0] == 1, "one prompt sequence per call"
    hp, hs = x_prompt[0], x_sample
    kp_l, vp_l, cp_l, ks_l, vs_l, cs_l = [], [], [], [], [], []
    for i in range(depth):
        c = _layer_consts(ln_g[i], w_in[i], q_norm_g[i], k_norm_g[i], sink[i],
                          jnp.swapaxes(conv_w, 0, 1)[:, i:i + 1, :],
                          w_attn_out[i], w_conv_out[i], w_o[i], w_ple_gate[i], w_ple_proj[i])
        (hp, kp, vp, cp), (hs, ks, vs, cs) = _layer(hp, p_prompt[i, 0], hs, p_sample[i], cache_k[i],
                                                    cache_v[i], state_conv[i], c)
        kp_l.append(kp[None])
        vp_l.append(vp[None])
        cp_l.append(cp[None])
        ks_l.append(ks)
        vs_l.append(vs)
        cs_l.append(cs)
    return (hp[None], hs, jnp.stack(kp_l), jnp.stack(vp_l), jnp.stack(cp_l),
            jnp.stack(ks_l), jnp.stack(vs_l), jnp.stack(cs_l))
```

```python
import functools

import numpy as np
import jax
import jax.numpy as jnp
from jax import lax
from jax.experimental import pallas as pl
from jax.experimental.pallas import tpu as pltpu

F32 = jnp.float32
BF16 = jnp.bfloat16

CHUNK = 64
WINDOW = 128
WINDOW_CHUNKS = WINDOW // CHUNK
N_HEADS = 8
N_KV_HEADS = 2
GROUP_HEADS = N_HEADS // N_KV_HEADS
HEAD_DIM = 64
HALF = HEAD_DIM // 2
ATTN_DIM = N_HEADS * HEAD_DIM
KV_DIM = N_KV_HEADS * HEAD_DIM
CONV_DIM = 512
CONV_WIDTH = 3
PAST_LEN = 1024
ROPE_THETA = 10000.0
EPS = 1e-6
NEG = -1e30

OFF_Q = 0
OFF_K = OFF_Q + ATTN_DIM
OFF_V = OFF_K + KV_DIM
OFF_GA = OFF_V + KV_DIM
OFF_B = OFF_GA + ATTN_DIM
OFF_GC_END = OFF_B + 4 * CONV_DIM
OFF_MA = OFF_GC_END

V7X_LANES = 128
V7X_SUBLANES = 8
V7X_VMEM_LIMIT_BYTES = 59 * 1024 * 1024

PROMPT_TILE = 512
PROMPT_SUBTILE = 256
PROMPT_STAGE_SKEW = 1
STAGE_ROWS = 256
STAGE_COLS = 1024
STAGE_SLOTS = 4
_WIN_BLOCK_ORDER = (0, 3, 4, 5, 1, 2)
SAMPLE_BATCH_TILE = 8


def _mm(a, w):
    return jnp.dot(a, w, preferred_element_type=F32)


def _mm_t(a, b):
    return lax.dot_general(a, b, (((1,), (1,)), ((), ())), preferred_element_type=F32)


def _sigmoid(x):
    return 0.5 * jnp.tanh(0.5 * x) + 0.5


def _silu(x):
    h = 0.5 * x
    return h + h * jnp.tanh(h)


def _rmsnorm(x, g):
    ms = jnp.mean(x * x, axis=-1, keepdims=True)
    return x * lax.rsqrt(ms + EPS) * g


def _group_mean(t, bd):
    return _mm(t.astype(BF16), bd)


def _head_norm(t, bd, g):
    ms = _group_mean(t * t, bd)
    return t * lax.rsqrt(ms + EPS) * g


def _rope(xc, cos_t, sin_s):
    lane = lax.broadcasted_iota(jnp.int32, xc.shape, 1)
    upper = (lane & HALF) != 0
    rot = jnp.where(upper, pltpu.roll(xc, HALF, 1), pltpu.roll(xc, V7X_LANES - HALF, 1))
    return xc * cos_t + rot * sin_s


def _rope_sign(shape):
    lane = lax.broadcasted_iota(jnp.int32, shape, 1)
    return jnp.where((lane & HALF) != 0, 1.0, -1.0).astype(F32)


def _kv_variants(t):
    lane = lax.broadcasted_iota(jnp.int32, t.shape, 1)
    lo = lane < HEAD_DIM
    sw = pltpu.roll(t, HEAD_DIM, 1)
    zero = jnp.zeros_like(t)
    return (jnp.where(lo, t, zero).astype(BF16), jnp.where(lo, zero, sw).astype(BF16),
            jnp.where(lo, sw, zero).astype(BF16), jnp.where(lo, zero, t).astype(BF16))


def _kv_variants_t(t):
    tb = t.astype(BF16)
    h0, h1 = tb[0:HEAD_DIM], tb[HEAD_DIM:2 * HEAD_DIM]
    zero = jnp.zeros_like(h0)
    return (jnp.concatenate([h0, zero], axis=0), jnp.concatenate([zero, h0], axis=0),
            jnp.concatenate([h1, zero], axis=0), jnp.concatenate([zero, h1], axis=0))


def _scores(qs, ka, kb, bad, keys_on_lanes=False):
    dot = _mm if keys_on_lanes else _mm_t
    s_e = dot(qs, ka)
    s_o = dot(qs, kb)
    if bad is not None:
        s_e = jnp.where(bad, NEG, s_e)
        s_o = jnp.where(bad, NEG, s_o)
    return s_e, s_o


def _softmax_parts(s, sink):
    m = jnp.maximum(jnp.max(s, axis=-1, keepdims=True), sink)
    e = jnp.exp(s - m)
    return e.astype(BF16), jnp.sum(e, axis=-1, keepdims=True) + jnp.exp(sink - m)


def _pv(p_e, p_o, va, vb, den_e, den_o, keys_on_lanes=False):
    dot = _mm_t if keys_on_lanes else _mm
    o = dot(p_e, va) + dot(p_o, vb)
    lane = lax.broadcasted_iota(jnp.int32, o.shape, 1)
    return o * (1.0 / jnp.where(lane < HEAD_DIM, den_e, den_o))


def _sink_cols(sink_ref, n_blocks, rows_per_pair, blocks_per_kv=1):
    row = lax.broadcasted_iota(jnp.int32, (n_blocks * 2 * rows_per_pair, 1), 0)
    second_pair = (row // rows_per_pair) % 2 == 1
    second_kv = (row // (2 * rows_per_pair * blocks_per_kv)) % 2 == 1

    def pick(odd):
        kv0 = jnp.where(second_pair, sink_ref[2 + odd], sink_ref[odd])
        kv1 = jnp.where(second_pair, sink_ref[GROUP_HEADS + 2 + odd], sink_ref[GROUP_HEADS + odd])
        return jnp.where(second_kv, kv1, kv0)

    return pick(0), pick(1)


def _stage_chunks(shape, col_order=None):
    rows, cols = shape
    assert rows % STAGE_ROWS == 0 and cols % V7X_LANES == 0
    width = max(c for c in range(V7X_LANES, STAGE_COLS + 1, V7X_LANES) if cols % c == 0)
    blocks = list(range(cols // width)) if col_order is None else list(col_order)
    assert sorted(blocks) == list(range(cols // width))
    return [(r0, b * width, STAGE_ROWS, width) for b in blocks for r0 in range(0, rows, STAGE_ROWS)]


class _WeightStager:
    def __init__(self, chunks, stage, sem):
        self.chunks, self.stage, self.sem, self.done = chunks, stage, sem, 0

    def _copy(self, n):
        src, _, r0, c0, rows, cols = self.chunks[n]
        slot = n % STAGE_SLOTS
        return pltpu.make_async_copy(src.at[pl.ds(r0, rows), pl.ds(c0, cols)],
                                     self.stage.at[slot, pl.ds(0, rows), pl.ds(0, cols)],
                                     self.sem.at[slot])

    def start(self):
        for n in range(min(STAGE_SLOTS, len(self.chunks))):
            self._copy(n).start()

    def pump(self, count):
        while self.done < min(count, len(self.chunks)):
            n = self.done
            _, dst, r0, c0, rows, cols = self.chunks[n]
            self._copy(n).wait()
            dst[r0:r0 + rows, c0:c0 + cols] = (
                self.stage[n % STAGE_SLOTS, 0:rows, 0:cols].astype(BF16))
            if n + STAGE_SLOTS < len(self.chunks):
                self._copy(n + STAGE_SLOTS).start()
            self.done += 1

    def finish(self):
        self.pump(len(self.chunks))


def _qkv_proj(x, lng_ref, win_ref):
    hb = _rmsnorm(x, lng_ref[...]).astype(BF16)
    return hb, _mm(hb, win_ref[:, OFF_Q:OFF_GA])


def _tile_lanes(g, width):
    while g.shape[1] < width:
        g = jnp.concatenate([g, g], axis=1)
    return g


def _qk_norm_rope(qkv, gq_ref, gk_ref, bdq_ref, bdk_ref, cos_t, sin_s):
    half_q = ATTN_DIM // 2
    gq = _tile_lanes(gq_ref[...] * (HEAD_DIM ** -0.5), half_q)
    gk = _tile_lanes(gk_ref[...], KV_DIM)
    q_cols = []
    for j in range(2):
        t = qkv[:, j * half_q:(j + 1) * half_q]
        tn = _head_norm(t, bdq_ref[...], gq)
        for c in range(half_q // V7X_LANES):
            q_cols.append(_rope(tn[:, c * V7X_LANES:(c + 1) * V7X_LANES], cos_t, sin_s).astype(BF16))
    kn = _head_norm(qkv[:, OFF_K:OFF_V], bdk_ref[...], gk)
    kr = _rope(kn, cos_t, sin_s)
    v = qkv[:, OFF_V:OFF_GA]
    return q_cols, kr, v


def _conv_proj(hb, win_ref):
    return _mm(hb, win_ref[:, OFF_B:OFF_GC_END])


def _conv_input(bcug, conv_fn):
    b_gate = bcug[:, 0:CONV_DIM]
    u = bcug[:, CONV_DIM:2 * CONV_DIM] * bcug[:, 2 * CONV_DIM:3 * CONV_DIM]
    gate_c = bcug[:, 3 * CONV_DIM:4 * CONV_DIM]
    conv = conv_fn(u)
    return (b_gate * conv * _silu(gate_c)).astype(BF16)


def _prompt_body(i, stager, x_ref, p_ref, lng_ref, gq_ref, gk_ref, invf_ref, sink_ref, cw_ref,
                 bdq_ref, bdk_ref, win_ref, wa_ref, wb_ref, wo_ref, wpg_ref, wpp_ref,
                 y_ref, ko_ref, vo_ref, co_ref, kbuf, vbuf, ubuf, attn_buf, tab):
    tm, d_model = x_ref.shape
    invf = invf_ref[...]
    first = stager is not None

    if first:
        kbuf[:, :, 0:WINDOW] = jnp.zeros((4, KV_DIM, WINDOW), BF16)
        vbuf[:, 0:WINDOW, :] = jnp.zeros((4, WINDOW, V7X_LANES), BF16)
        ubuf[0:V7X_SUBLANES, :] = jnp.zeros((V7X_SUBLANES, CONV_DIM), F32)
        r = lax.broadcasted_iota(jnp.int32, (tm, V7X_LANES), 0).astype(F32)
        ang = r * invf
        sgn = _rope_sign((tm, V7X_LANES))
        c_r = jnp.cos(ang)
        s_r = jnp.sin(ang)
        tab[0] = c_r
        tab[1] = s_r
        tab[2] = c_r * sgn
        tab[3] = s_r * sgn

    base = (i * tm).astype(F32) * invf
    cb = jnp.cos(base)
    sb = jnp.sin(base)
    cos_t = tab[0] * cb - tab[1] * sb
    sin_s = tab[3] * cb + tab[2] * sb

    sub = PROMPT_SUBTILE
    n_sub = tm // sub
    sub_chunks = sub // CHUNK
    assert 2 * CHUNK == V7X_LANES and WINDOW == V7X_LANES
    n_keys = 2 * V7X_LANES

    def key_window_start(r0, c):
        return ((r0 + c * CHUNK) // V7X_LANES) * V7X_LANES

    def mask_window(sc, r0, c):
        lo, hi = sc[:, 0:V7X_LANES], sc[:, V7X_LANES:n_keys]
        lane = lax.broadcasted_iota(jnp.int32, lo.shape, 1)
        no_carry = first and key_window_start(r0, c) < WINDOW
        if no_carry:
            lo = jnp.full_like(lo, NEG)
        if ((r0 + c * CHUNK) // CHUNK) % 2 == 0:
            hi = jnp.where(lane >= CHUNK, NEG, hi)
        elif not no_carry:
            lo = jnp.where(lane < CHUNK, NEG, lo)
        return jnp.concatenate([lo, hi], axis=1)

    pad = V7X_SUBLANES
    blocks = [(c0, vh) for c0 in range(0, sub_chunks, 2) for vh in range(N_KV_HEADS)]
    st = [dict(r0=s * sub) for s in range(n_sub)]

    def stage_rms(s):
        s["hb"] = _rmsnorm(x_ref[s["r0"]:s["r0"] + sub, :], lng_ref[...]).astype(BF16)

    def stage_qkv(s):
        s["qkv"] = _mm(s["hb"], win_ref[:, OFF_Q:OFF_GA])

    def stage_gate_a(s):
        s["sig_a"] = _sigmoid(_mm(s["hb"], win_ref[:, OFF_MA:OFF_MA + d_model]))

    def stage_gate_c(s):
        s["sig_c"] = _sigmoid(_mm(s["hb"], win_ref[:, OFF_MA + d_model:OFF_MA + 2 * d_model]))

    def stage_qk_norm(s):
        r0 = s["r0"]
        s["q_cols"], kr, v = _qk_norm_rope(s.pop("qkv"), gq_ref, gk_ref, bdq_ref, bdk_ref,
                                           cos_t[r0:r0 + sub], sin_s[r0:r0 + sub])
        kr_t = kr.T
        for n, t in enumerate(_kv_variants_t(kr_t)):
            kbuf[n, :, WINDOW + r0:WINDOW + r0 + sub] = t
        for n, t in enumerate(_kv_variants(v)):
            vbuf[n, WINDOW + r0:WINDOW + r0 + sub, :] = t
        if r0 + sub == tm:
            ko_ref[...] = kr_t[:, sub - WINDOW:sub]
            vo_ref[...] = v[sub - WINDOW:sub, :].T

    def stage_scores(s):
        r0 = s["r0"]
        s_e, s_o = [], []
        for c0, vh in blocks:
            start = key_window_start(r0, c0)
            win = slice(start, start + n_keys)
            qs = jnp.concatenate(
                [s["q_cols"][2 * vh + pair][(c0 + dc) * CHUNK:(c0 + dc + 1) * CHUNK]
                 for dc in range(2) for pair in range(2)], axis=0)
            se, so = _scores(qs, kbuf[2 * vh, :, win], kbuf[2 * vh + 1, :, win], None,
                             keys_on_lanes=True)
            for dc in range(2):
                half = slice(dc * 2 * CHUNK, (dc + 1) * 2 * CHUNK)
                s_e.append(mask_window(se[half], r0, c0 + dc))
                s_o.append(mask_window(so[half], r0, c0 + dc))
        s["s_e"] = jnp.concatenate(s_e, axis=0)
        s["s_o"] = jnp.concatenate(s_o, axis=0)
        del s["q_cols"]

    def stage_conv_proj(s):
        s["bcug"] = _conv_proj(s["hb"], win_ref)

    def stage_conv(s):
        r0 = s["r0"]

        def conv_fn(u):
            ubuf[pad + r0:pad + r0 + sub, :] = u
            conv = ubuf[pad + r0 - 2:pad + r0 - 2 + sub, :] * cw_ref[0]
            conv = conv + ubuf[pad + r0 - 1:pad + r0 - 1 + sub, :] * cw_ref[1]
            return conv + u * cw_ref[2]

        s["c_in"] = _conv_input(s.pop("bcug"), conv_fn)

    def stage_conv_out(s):
        s["yc"] = _mm(s.pop("c_in"), wb_ref[...])

    sink_e, sink_o = _sink_cols(sink_ref, 2 * len(blocks), CHUNK, blocks_per_kv=2)

    def stage_softmax(s):
        s["p_e"], s["r_e"] = _softmax_parts(s.pop("s_e"), sink_e)
        s["p_o"], s["r_o"] = _softmax_parts(s.pop("s_o"), sink_o)

    def stage_pv(s):
        r0 = s["r0"]
        for n, (c0, vh) in enumerate(blocks):
            start = key_window_start(r0, c0)
            win = slice(start, start + n_keys)
            br = slice(n * 4 * CHUNK, (n + 1) * 4 * CHUNK)
            o = _pv(s["p_e"][br], s["p_o"][br], vbuf[2 * vh, win, :], vbuf[2 * vh + 1, win, :],
                    s["r_e"][br], s["r_o"][br])
            for dc in range(2):
                rows = slice(r0 + (c0 + dc) * CHUNK, r0 + (c0 + dc + 1) * CHUNK)
                for pair in range(2):
                    part = o[(2 * dc + pair) * CHUNK:(2 * dc + pair + 1) * CHUNK]
                    attn_buf[rows, (2 * vh + pair) * V7X_LANES:(2 * vh + pair + 1) * V7X_LANES] = part

    def stage_attn_gate(s):
        s["silu_ga"] = _silu(_mm(s["hb"], win_ref[:, OFF_GA:OFF_B]))

    def stage_attn_out(s):
        r0 = s["r0"]
        a_in = (attn_buf[r0:r0 + sub, :] * s.pop("silu_ga")).astype(BF16)
        ya = _mm(a_in, wa_ref[...])
        s["mix"] = (s.pop("sig_a") * ya + s.pop("sig_c") * s.pop("yc")).astype(BF16)

    def stage_out_proj(s):
        r0 = s["r0"]
        s["r"] = x_ref[r0:r0 + sub, :] + _mm(s.pop("mix"), wo_ref[...])
        s["pp"] = _mm(p_ref[r0:r0 + sub, :].astype(BF16), wpp_ref[...])

    def stage_ple(s):
        r0 = s["r0"]
        r = s.pop("r")
        gate = _sigmoid(_mm(r.astype(BF16), wpg_ref[...]))
        y_ref[r0:r0 + sub, :] = r + gate * s.pop("pp")

    stages = [stage_rms, stage_qkv, stage_qk_norm, stage_gate_a, stage_scores, stage_gate_c,
              stage_softmax, stage_conv_proj, stage_conv, stage_attn_gate, stage_conv_out,
              stage_pv, stage_attn_out, stage_out_proj, stage_ple]
    needs = {stage_rms: 0, stage_qkv: 4, stage_qk_norm: 8, stage_gate_a: 12, stage_scores: 14,
             stage_gate_c: 16, stage_softmax: 20, stage_conv_proj: 24, stage_conv: 25,
             stage_attn_gate: 25, stage_conv_out: 26, stage_pv: 27, stage_attn_out: 28,
             stage_out_proj: 33, stage_ple: 37}
    for t in range(len(stages) + PROMPT_STAGE_SKEW * (n_sub - 1)):
        for j, s in enumerate(st):
            k = t - j * PROMPT_STAGE_SKEW
            if 0 <= k < len(stages):
                if first and j == 0:
                    stager.pump(needs[stages[k]])
                stages[k](s)

    co_ref[...] = ubuf[pad + tm - 2:pad + tm, :]
    ubuf[0:pad, :] = ubuf[tm:tm + pad, :]
    kbuf[:, :, 0:WINDOW] = kbuf[:, :, tm:tm + WINDOW]
    vbuf[:, 0:WINDOW, :] = vbuf[:, tm:tm + WINDOW, :]


def _sample_body(i, x_ref, p_ref, ck_ref, cv_ref, sc_ref, lng_ref, gq_ref, gk_ref, invf_ref,
                 sink_ref, cw_ref, bdq_ref, bdk_ref, win_ref, wa_ref, wb_ref, wo_ref, wpg_ref,
                 wpp_ref, y_ref, ko_ref, vo_ref, co_ref, ubuf, attn_buf, conv_buf, tab):
    bb, _, cache_len = ck_ref.shape
    rows_total, d_model = x_ref.shape
    t_new = rows_total // bb
    n_keys = cache_len + t_new

    @pl.when(i == 0)
    def _init():
        r = lax.broadcasted_iota(jnp.int32, (rows_total, V7X_LANES), 0)
        pos = (PAST_LEN + lax.rem(r, t_new)).astype(F32)
        ang = pos * invf_ref[...]
        tab[0] = jnp.cos(ang)
        tab[1] = jnp.sin(ang) * _rope_sign((rows_total, V7X_LANES))

    def conv_fn(u):
        pad = V7X_SUBLANES
        for b in range(bb):
            rows = slice(b * t_new, (b + 1) * t_new)
            ub = u[rows]
            ubuf[b, pad - (CONV_WIDTH - 1):pad, :] = sc_ref[b]
            ubuf[b, pad:pad + t_new, :] = ub
            conv = ubuf[b, pad - 2:pad - 2 + t_new, :] * cw_ref[0]
            conv = conv + ubuf[b, pad - 1:pad - 1 + t_new, :] * cw_ref[1]
            conv_buf[rows, :] = conv + ub * cw_ref[2]
            co_ref[b] = ubuf[b, pad + t_new - (CONV_WIDTH - 1):pad + t_new, :]
        return conv_buf[...]

    x = x_ref[...]
    hb, qkv = _qkv_proj(x, lng_ref, win_ref)
    sig_a = _sigmoid(_mm(hb, win_ref[:, OFF_MA:OFF_MA + d_model]))
    q_cols, kr, v = _qk_norm_rope(qkv, gq_ref, gk_ref, bdq_ref, bdk_ref, tab[0], tab[1])

    qi = lax.broadcasted_iota(jnp.int32, (2 * t_new, n_keys), 0)
    q_pos = PAST_LEN + lax.rem(qi, t_new)
    k_pos = PAST_LEN - cache_len + lax.broadcasted_iota(jnp.int32, (2 * t_new, n_keys), 1)
    q_ch = q_pos // CHUNK
    k_ch = k_pos // CHUNK
    bad = jnp.logical_not((k_ch <= q_ch) & (k_ch >= q_ch - WINDOW_CHUNKS))

    blocks = [(b, vh) for b in range(bb) for vh in range(N_KV_HEADS)]
    s_e, s_o, vvars = [], [], []
    kr_t = kr.T
    v_t = v.T
    for b in range(bb):
        rows = slice(b * t_new, (b + 1) * t_new)
        kcat = jnp.concatenate([ck_ref[b], kr_t[:, rows]], axis=1)
        vcat = jnp.concatenate([cv_ref[b], v_t[:, rows]], axis=1)
        ko_ref[b] = kcat[:, n_keys - cache_len:n_keys]
        vo_ref[b] = vcat[:, n_keys - cache_len:n_keys]
        kvar = _kv_variants_t(kcat)
        vvars.append(_kv_variants_t(vcat))
        for vh in range(N_KV_HEADS):
            qs = jnp.concatenate([q_cols[2 * vh][rows], q_cols[2 * vh + 1][rows]], axis=0)
            se, so = _scores(qs, kvar[2 * vh], kvar[2 * vh + 1], bad, keys_on_lanes=True)
            s_e.append(se)
            s_o.append(so)

    sig_c = _sigmoid(_mm(hb, win_ref[:, OFF_MA + d_model:OFF_MA + 2 * d_model]))
    sink_e, sink_o = _sink_cols(sink_ref, len(blocks), t_new)
    p_e, r_e = _softmax_parts(jnp.concatenate(s_e, axis=0), sink_e)
    bcug = _conv_proj(hb, win_ref)
    p_o, r_o = _softmax_parts(jnp.concatenate(s_o, axis=0), sink_o)
    c_in = _conv_input(bcug, conv_fn)
    silu_ga = _silu(_mm(hb, win_ref[:, OFF_GA:OFF_B]))
    pp = _mm(p_ref[...].astype(BF16), wpp_ref[...])
    yc = _mm(c_in, wb_ref[...])

    for n, (b, vh) in enumerate(blocks):
        rows = slice(b * t_new, (b + 1) * t_new)
        br = slice(n * 2 * t_new, (n + 1) * 2 * t_new)
        o = _pv(p_e[br], p_o[br], vvars[b][2 * vh], vvars[b][2 * vh + 1], r_e[br], r_o[br],
                keys_on_lanes=True)
        attn_buf[rows, (2 * vh) * V7X_LANES:(2 * vh + 1) * V7X_LANES] = o[0:t_new]
        attn_buf[rows, (2 * vh + 1) * V7X_LANES:(2 * vh + 2) * V7X_LANES] = o[t_new:2 * t_new]

    ya = _mm((attn_buf[0:rows_total, :] * silu_ga).astype(BF16), wa_ref[...])
    mix = (sig_a * ya + sig_c * yc).astype(BF16)
    r = x + _mm(mix, wo_ref[...])
    gate = _sigmoid(_mm(r.astype(BF16), wpg_ref[...]))
    y_ref[...] = r + gate * pp


def _layer_kernel(n_prompt_steps, *refs):
    (xp_ref, pp_ref, xs_ref, ps_ref, ck_ref, cv_ref, sc_ref,
     lng_ref, gq_ref, gk_ref, invf_ref, sink_ref, cw_ref, bdq_ref, bdk_ref,
     win_hbm, wa_hbm, wb_hbm, wo_hbm, wpg_hbm, wpp_hbm,
     yp_ref, kpo_ref, vpo_ref, cpo_ref, ys_ref, kso_ref, vso_ref, cso_ref,
     kbuf, vbuf, ubuf_p, attn_buf, tab_p, ubuf_s, conv_buf, tab_s,
     win_ref, wa_ref, wb_ref, wo_ref, wpg_ref, wpp_ref, stage, sem) = refs
    i = pl.program_id(0)
    small = (lng_ref, gq_ref, gk_ref, invf_ref, sink_ref, cw_ref, bdq_ref, bdk_ref)
    weights = (win_ref, wa_ref, wb_ref, wo_ref, wpg_ref, wpp_ref)

    prompt_refs = (xp_ref, pp_ref, *small, *weights, yp_ref, kpo_ref, vpo_ref, cpo_ref,
                   kbuf, vbuf, ubuf_p, attn_buf, tab_p)

    @pl.when(i == 0)
    def _first_prompt():
        chunks = [(src, dst) + ch for src, dst, order in (
            (win_hbm, win_ref, _WIN_BLOCK_ORDER), (wb_hbm, wb_ref, None), (wa_hbm, wa_ref, None),
            (wo_hbm, wo_ref, None), (wpp_hbm, wpp_ref, None), (wpg_hbm, wpg_ref, None))
            for ch in _stage_chunks(src.shape, order)]
        stager = _WeightStager(chunks, stage, sem)
        stager.start()
        _prompt_body(i, stager, *prompt_refs)
        stager.finish()

    @pl.when((i > 0) & (i < n_prompt_steps))
    def _prompt():
        _prompt_body(i, None, *prompt_refs)

    @pl.when(i >= n_prompt_steps)
    def _sample():
        _sample_body(i - n_prompt_steps, xs_ref, ps_ref, ck_ref, cv_ref, sc_ref, *small, *weights,
                     ys_ref, kso_ref, vso_ref, cso_ref, ubuf_s, attn_buf, conv_buf, tab_s)


def _const_spec(shape):
    nd = len(shape)
    return pl.BlockSpec(shape, lambda i: (0,) * nd, pipeline_mode=pl.Buffered(1))


def _smem_spec():
    return pl.BlockSpec(memory_space=pltpu.SMEM)


def _block_diag_mean(width):
    idx = np.arange(width) // HEAD_DIM
    return jnp.asarray((idx[:, None] == idx[None, :]).astype(np.float32) / HEAD_DIM, dtype=BF16)


def _layer_consts(ln_g, w_in, q_norm_g, k_norm_g, sink, conv_w, w_attn_out, w_conv_out, w_o,
                  w_ple_gate, w_ple_proj):
    inv_freq = ROPE_THETA ** (-jnp.arange(0, HALF, dtype=F32) * 2.0 / HEAD_DIM)
    return dict(
        lng=ln_g.reshape(1, -1).astype(F32),
        gq=q_norm_g.astype(F32).reshape(1, HEAD_DIM),
        gk=k_norm_g.astype(F32).reshape(1, HEAD_DIM),
        invf=jnp.tile(inv_freq, V7X_LANES // HALF).reshape(1, V7X_LANES),
        sink=sink.astype(F32),
        cw=conv_w.astype(F32),
        bdq=_block_diag_mean(ATTN_DIM // 2),
        bdk=_block_diag_mean(KV_DIM),
        win=w_in.astype(F32),
        wa=w_attn_out.astype(F32),
        wb=w_conv_out.astype(F32),
        wo=w_o.astype(F32),
        wpg=w_ple_gate.astype(F32),
        wpp=w_ple_proj.astype(F32),
    )


_SMALL_KEYS = ("lng", "gq", "gk", "invf", "sink", "cw", "bdq", "bdk")
_STAGED_KEYS = ("win", "wa", "wb", "wo", "wpg", "wpp")


def _heads_first(t):
    t = jnp.moveaxis(t, -3, -1)
    return t.reshape(t.shape[:-3] + (KV_DIM, t.shape[-1]))


def _heads_last(t):
    t = t.reshape(t.shape[:-2] + (N_KV_HEADS, HEAD_DIM, t.shape[-1]))
    return jnp.moveaxis(t, -1, -3)


def _layer(xp, pp, xs, ps, cache_k, cache_v, state_conv, c):
    t, d = xp.shape
    nb, t_new, _ = xs.shape
    cache_len = cache_k.shape[1]
    tm, bb = PROMPT_TILE, SAMPLE_BATCH_TILE
    assert t % tm == 0 and tm % PROMPT_SUBTILE == 0
    assert PROMPT_SUBTILE % CHUNK == 0 and PROMPT_SUBTILE >= WINDOW
    assert nb % bb == 0 and CONV_WIDTH - 1 <= t_new <= cache_len
    n_p, n_s = t // tm, nb // bb
    rows = bb * t_new
    xs2 = xs.reshape(nb * t_new, d)
    ps2 = ps.reshape(nb * t_new, ps.shape[-1])
    ck = _heads_first(cache_k)
    cv = _heads_first(cache_v)

    def p_idx(i):
        return jnp.minimum(i, n_p - 1)

    def s_idx(i):
        return jnp.maximum(i - n_p, 0)

    in_specs = ([pl.BlockSpec((tm, d), lambda i: (p_idx(i), 0)),
                 pl.BlockSpec((tm, pp.shape[1]), lambda i: (p_idx(i), 0)),
                 pl.BlockSpec((rows, d), lambda i: (s_idx(i), 0)),
                 pl.BlockSpec((rows, ps2.shape[1]), lambda i: (s_idx(i), 0)),
                 pl.BlockSpec((bb, KV_DIM, cache_len), lambda i: (s_idx(i), 0, 0)),
                 pl.BlockSpec((bb, KV_DIM, cache_len), lambda i: (s_idx(i), 0, 0)),
                 pl.BlockSpec((bb, CONV_WIDTH - 1, CONV_DIM), lambda i: (s_idx(i), 0, 0))]
                + [_smem_spec() if k == "sink" else _const_spec(c[k].shape) for k in _SMALL_KEYS]
                + [pl.BlockSpec(memory_space=pl.ANY) for _ in _STAGED_KEYS])
    out_shape = (jax.ShapeDtypeStruct((t, d), F32),
                 jax.ShapeDtypeStruct((KV_DIM, WINDOW), F32),
                 jax.ShapeDtypeStruct((KV_DIM, WINDOW), F32),
                 jax.ShapeDtypeStruct((CONV_WIDTH - 1, CONV_DIM), F32),
                 jax.ShapeDtypeStruct((nb * t_new, d), F32),
                 jax.ShapeDtypeStruct((nb, KV_DIM, cache_len), F32),
                 jax.ShapeDtypeStruct((nb, KV_DIM, cache_len), F32),
                 jax.ShapeDtypeStruct((nb, CONV_WIDTH - 1, CONV_DIM), F32))
    out_specs = (pl.BlockSpec((tm, d), lambda i: (p_idx(i), 0)),
                 pl.BlockSpec((KV_DIM, WINDOW), lambda i: (0, 0)),
                 pl.BlockSpec((KV_DIM, WINDOW), lambda i: (0, 0)),
                 pl.BlockSpec((CONV_WIDTH - 1, CONV_DIM), lambda i: (0, 0)),
                 pl.BlockSpec((rows, d), lambda i: (s_idx(i), 0)),
                 pl.BlockSpec((bb, KV_DIM, cache_len), lambda i: (s_idx(i), 0, 0)),
                 pl.BlockSpec((bb, KV_DIM, cache_len), lambda i: (s_idx(i), 0, 0)),
                 pl.BlockSpec((bb, CONV_WIDTH - 1, CONV_DIM), lambda i: (s_idx(i), 0, 0)))
    scratch = ([pltpu.VMEM((4, KV_DIM, WINDOW + tm), BF16),
                pltpu.VMEM((4, WINDOW + tm, V7X_LANES), BF16),
                pltpu.VMEM((V7X_SUBLANES + tm, CONV_DIM), F32),
                pltpu.VMEM((max(tm, rows), ATTN_DIM), F32),
                pltpu.VMEM((4, tm, V7X_LANES), F32),
                pltpu.VMEM((bb, V7X_SUBLANES + t_new, CONV_DIM), F32),
                pltpu.VMEM((rows, CONV_DIM), F32),
                pltpu.VMEM((2, rows, V7X_LANES), F32)]
               + [pltpu.VMEM(c[k].shape, BF16) for k in _STAGED_KEYS]
               + [pltpu.VMEM((STAGE_SLOTS, STAGE_ROWS, STAGE_COLS), F32),
                  pltpu.SemaphoreType.DMA((STAGE_SLOTS,))])
    yp, kpo, vpo, cpo, ys, kso, vso, cso = pl.pallas_call(
        functools.partial(_layer_kernel, n_p),
        grid=(n_p + n_s,),
        in_specs=in_specs,
        out_specs=out_specs,
        out_shape=out_shape,
        scratch_shapes=scratch,
        compiler_params=pltpu.CompilerParams(dimension_semantics=("arbitrary",),
                                             vmem_limit_bytes=V7X_VMEM_LIMIT_BYTES),
        name="hybrid_layer",
    )(xp, pp, xs2, ps2, ck, cv, state_conv, *[c[k] for k in _SMALL_KEYS],
      *[c[k] for k in _STAGED_KEYS])
    return ((yp, _heads_last(kpo), _heads_last(vpo), cpo),
            (ys.reshape(nb, t_new, d), _heads_last(kso), _heads_last(vso), cso))


def kernel(x_prompt, x_sample, p_prompt, p_sample, cache_k, cache_v, state_conv, ln_g, w_in,
           q_norm_g, k_norm_g, sink, conv_w, w_attn_out, w_conv_out, w_o, w_ple_gate, w_ple_proj):
    depth = ln_g.shape[0]
    assert x_prompt.shape[0] == 1, "one prompt sequence per call"
    hp, hs = x_prompt[0], x_sample
    kp_l, vp_l, cp_l, ks_l, vs_l, cs_l = [], [], [], [], [], []
    for i in range(depth):
        c = _layer_consts(ln_g[i], w_in[i], q_norm_g[i], k_norm_g[i], sink[i],
                          jnp.swapaxes(conv_w, 0, 1)[:, i:i + 1, :],
                          w_attn_out[i], w_conv_out[i], w_o[i], w_ple_gate[i], w_ple_proj[i])
        (hp, kp, vp, cp), (hs, ks, vs, cs) = _layer(hp, p_prompt[i, 0], hs, p_sample[i], cache_k[i],
                                                    cache_v[i], state_conv[i], c)
        kp_l.append(kp[None])
        vp_l.append(vp[None])
        cp_l.append(cp[None])
        ks_l.append(ks)
        vs_l.append(vs)
        cs_l.append(cs)
    return (hp[None], hs, jnp.stack(kp_l), jnp.stack(vp_l), jnp.stack(cp_l),
            jnp.stack(ks_l), jnp.stack(vs_l), jnp.stack(cs_l))
```

```python
import functools

import numpy as np
import jax
import jax.numpy as jnp
from jax import lax
from jax.experimental import pallas as pl
from jax.experimental.pallas import tpu as pltpu

F32 = jnp.float32
BF16 = jnp.bfloat16

CHUNK = 64
WINDOW = 128
WINDOW_CHUNKS = WINDOW // CHUNK
N_HEADS = 8
N_KV_HEADS = 2
GROUP_HEADS = N_HEADS // N_KV_HEADS
HEAD_DIM = 64
HALF = HEAD_DIM // 2
ATTN_DIM = N_HEADS * HEAD_DIM
KV_DIM = N_KV_HEADS * HEAD_DIM
CONV_DIM = 512
CONV_WIDTH = 3
PAST_LEN = 1024
ROPE_THETA = 10000.0
EPS = 1e-6
NEG = -1e30

OFF_Q = 0
OFF_K = OFF_Q + ATTN_DIM
OFF_V = OFF_K + KV_DIM
OFF_GA = OFF_V + KV_DIM
OFF_B = OFF_GA + ATTN_DIM
OFF_GC_END = OFF_B + 4 * CONV_DIM
OFF_MA = OFF_GC_END

V7X_LANES = 128
V7X_SUBLANES = 8
V7X_VMEM_LIMIT_BYTES = 59 * 1024 * 1024

PROMPT_TILE = 512
PROMPT_SUBTILE = 256
PROMPT_STAGE_SKEW = 1
STAGE_ROWS = 256
STAGE_COLS = 1024
STAGE_SLOTS = 4
_WIN_BLOCK_ORDER = (0, 3, 4, 5, 1, 2)
SAMPLE_BATCH_TILE = 8


def _mm(a, w):
    return jnp.dot(a, w, preferred_element_type=F32)


def _mm_t(a, b):
    return lax.dot_general(a, b, (((1,), (1,)), ((), ())), preferred_element_type=F32)


def _sigmoid(x):
    return 0.5 * jnp.tanh(0.5 * x) + 0.5


def _silu(x):
    return x * _sigmoid(x)


def _rmsnorm(x, g):
    ms = jnp.mean(x * x, axis=-1, keepdims=True)
    return x * lax.rsqrt(ms + EPS) * g


def _group_mean(t, bd):
    return _mm(t.astype(BF16), bd)


def _head_norm(t, bd, g):
    ms = _group_mean(t * t, bd)
    return t * lax.rsqrt(ms + EPS) * g


def _rope(xc, cos_t, sin_s):
    lane = lax.broadcasted_iota(jnp.int32, xc.shape, 1)
    upper = (lane & HALF) != 0
    rot = jnp.where(upper, pltpu.roll(xc, HALF, 1), pltpu.roll(xc, V7X_LANES - HALF, 1))
    return xc * cos_t + rot * sin_s


def _rope_sign(shape):
    lane = lax.broadcasted_iota(jnp.int32, shape, 1)
    return jnp.where((lane & HALF) != 0, 1.0, -1.0).astype(F32)


def _kv_variants(t):
    lane = lax.broadcasted_iota(jnp.int32, t.shape, 1)
    lo = lane < HEAD_DIM
    sw = pltpu.roll(t, HEAD_DIM, 1)
    zero = jnp.zeros_like(t)
    return (jnp.where(lo, t, zero).astype(BF16), jnp.where(lo, zero, sw).astype(BF16),
            jnp.where(lo, sw, zero).astype(BF16), jnp.where(lo, zero, t).astype(BF16))


def _kv_variants_t(t):
    tb = t.astype(BF16)
    h0, h1 = tb[0:HEAD_DIM], tb[HEAD_DIM:2 * HEAD_DIM]
    zero = jnp.zeros_like(h0)
    return (jnp.concatenate([h0, zero], axis=0), jnp.concatenate([zero, h0], axis=0),
            jnp.concatenate([h1, zero], axis=0), jnp.concatenate([zero, h1], axis=0))


def _scores(qs, ka, kb, bad, keys_on_lanes=False):
    dot = _mm if keys_on_lanes else _mm_t
    s_e = dot(qs, ka)
    s_o = dot(qs, kb)
    if bad is not None:
        s_e = jnp.where(bad, NEG, s_e)
        s_o = jnp.where(bad, NEG, s_o)
    return s_e, s_o


def _softmax_parts(s, sink):
    m = jnp.maximum(jnp.max(s, axis=-1, keepdims=True), sink)
    e = jnp.exp(s - m)
    r = 1.0 / (jnp.sum(e, axis=-1, keepdims=True) + jnp.exp(sink - m))
    return e.astype(BF16), r


def _softmax_parts_inline_sink(s):
    m = jnp.max(s, axis=-1, keepdims=True)
    e = jnp.exp(s - m)
    return e.astype(BF16), 1.0 / jnp.sum(e, axis=-1, keepdims=True)


def _pv(p_e, p_o, va, vb, r_e, r_o, keys_on_lanes=False):
    dot = _mm_t if keys_on_lanes else _mm
    o = dot(p_e, va) + dot(p_o, vb)
    lane = lax.broadcasted_iota(jnp.int32, o.shape, 1)
    return o * jnp.where(lane < HEAD_DIM, r_e, r_o)


def _sink_cols(sink_ref, n_blocks, rows_per_pair, blocks_per_kv=1):
    row = lax.broadcasted_iota(jnp.int32, (n_blocks * 2 * rows_per_pair, 1), 0)
    second_pair = (row // rows_per_pair) % 2 == 1
    second_kv = (row // (2 * rows_per_pair * blocks_per_kv)) % 2 == 1

    def pick(odd):
        kv0 = jnp.where(second_pair, sink_ref[2 + odd], sink_ref[odd])
        kv1 = jnp.where(second_pair, sink_ref[GROUP_HEADS + 2 + odd], sink_ref[GROUP_HEADS + odd])
        return jnp.where(second_kv, kv1, kv0)

    return pick(0), pick(1)


def _stage_chunks(shape, col_order=None):
    rows, cols = shape
    assert rows % STAGE_ROWS == 0 and cols % V7X_LANES == 0
    width = max(c for c in range(V7X_LANES, STAGE_COLS + 1, V7X_LANES) if cols % c == 0)
    blocks = list(range(cols // width)) if col_order is None else list(col_order)
    assert sorted(blocks) == list(range(cols // width))
    return [(r0, b * width, STAGE_ROWS, width) for b in blocks for r0 in range(0, rows, STAGE_ROWS)]


class _WeightStager:
    def __init__(self, chunks, stage, sem):
        self.chunks, self.stage, self.sem, self.done = chunks, stage, sem, 0

    def _copy(self, n):
        src, _, r0, c0, rows, cols = self.chunks[n]
        slot = n % STAGE_SLOTS
        return pltpu.make_async_copy(src.at[pl.ds(r0, rows), pl.ds(c0, cols)],
                                     self.stage.at[slot, pl.ds(0, rows), pl.ds(0, cols)],
                                     self.sem.at[slot])

    def start(self):
        for n in range(min(STAGE_SLOTS, len(self.chunks))):
            self._copy(n).start()

    def pump(self, count):
        while self.done < min(count, len(self.chunks)):
            n = self.done
            _, dst, r0, c0, rows, cols = self.chunks[n]
            self._copy(n).wait()
            dst[r0:r0 + rows, c0:c0 + cols] = (
                self.stage[n % STAGE_SLOTS, 0:rows, 0:cols].astype(BF16))
            if n + STAGE_SLOTS < len(self.chunks):
                self._copy(n + STAGE_SLOTS).start()
            self.done += 1

    def finish(self):
        self.pump(len(self.chunks))


def _qkv_proj(x, lng_ref, win_ref):
    hb = _rmsnorm(x, lng_ref[...]).astype(BF16)
    return hb, _mm(hb, win_ref[:, OFF_Q:OFF_GA])


def _tile_lanes(g, width):
    while g.shape[1] < width:
        g = jnp.concatenate([g, g], axis=1)
    return g


def _qk_norm_rope(qkv, gq_ref, gk_ref, bdq_ref, bdk_ref, cos_t, sin_s):
    half_q = ATTN_DIM // 2
    gq = _tile_lanes(gq_ref[...] * (HEAD_DIM ** -0.5), half_q)
    gk = _tile_lanes(gk_ref[...], KV_DIM)
    q_cols = []
    for j in range(2):
        t = qkv[:, j * half_q:(j + 1) * half_q]
        tn = _head_norm(t, bdq_ref[...], gq)
        for c in range(half_q // V7X_LANES):
            q_cols.append(_rope(tn[:, c * V7X_LANES:(c + 1) * V7X_LANES], cos_t, sin_s).astype(BF16))
    kn = _head_norm(qkv[:, OFF_K:OFF_V], bdk_ref[...], gk)
    kr = _rope(kn, cos_t, sin_s)
    v = qkv[:, OFF_V:OFF_GA]
    return q_cols, kr, v


def _conv_proj(hb, win_ref):
    return _mm(hb, win_ref[:, OFF_B:OFF_GC_END])


def _conv_input(bcug, conv_fn):
    b_gate = bcug[:, 0:CONV_DIM]
    u = bcug[:, CONV_DIM:2 * CONV_DIM] * bcug[:, 2 * CONV_DIM:3 * CONV_DIM]
    gate_c = bcug[:, 3 * CONV_DIM:4 * CONV_DIM]
    conv = conv_fn(u)
    return (b_gate * conv * _silu(gate_c)).astype(BF16)


def _prompt_body(i, stager, x_ref, p_ref, lng_ref, gq_ref, gk_ref, invf_ref, sink_ref, cw_ref,
                 bdq_ref, bdk_ref, win_ref, wa_ref, wb_ref, wo_ref, wpg_ref, wpp_ref,
                 y_ref, ko_ref, vo_ref, co_ref, kbuf, vbuf, ubuf, attn_buf, tab):
    tm, d_model = x_ref.shape
    invf = invf_ref[...]
    first = stager is not None

    if first:
        kbuf[:, :, 0:WINDOW] = jnp.zeros((4, KV_DIM, WINDOW), BF16)
        vbuf[:, 0:WINDOW, :] = jnp.zeros((4, WINDOW, V7X_LANES), BF16)
        ubuf[0:V7X_SUBLANES, :] = jnp.zeros((V7X_SUBLANES, CONV_DIM), F32)
        r = lax.broadcasted_iota(jnp.int32, (tm, V7X_LANES), 0).astype(F32)
        ang = r * invf
        sgn = _rope_sign((tm, V7X_LANES))
        c_r = jnp.cos(ang)
        s_r = jnp.sin(ang)
        tab[0] = c_r
        tab[1] = s_r
        tab[2] = c_r * sgn
        tab[3] = s_r * sgn

    base = (i * tm).astype(F32) * invf
    cb = jnp.cos(base)
    sb = jnp.sin(base)
    cos_t = tab[0] * cb - tab[1] * sb
    sin_s = tab[3] * cb + tab[2] * sb

    sub = PROMPT_SUBTILE
    n_sub = tm // sub
    sub_chunks = sub // CHUNK
    assert 2 * CHUNK == V7X_LANES and WINDOW == V7X_LANES
    n_keys = 2 * V7X_LANES

    def key_window_start(r0, c):
        return ((r0 + c * CHUNK) // V7X_LANES) * V7X_LANES

    def mask_window(sc, r0, c, sink):
        lo, hi = sc[:, 0:V7X_LANES], sc[:, V7X_LANES:n_keys]
        lane = lax.broadcasted_iota(jnp.int32, lo.shape, 1)
        no_carry = first and key_window_start(r0, c) < WINDOW
        if no_carry:
            lo = jnp.full_like(lo, NEG)
        if ((r0 + c * CHUNK) // CHUNK) % 2 == 0:
            hi = jnp.where(lane < CHUNK, hi, jnp.where(lane == V7X_LANES - 1, sink, NEG))
        else:
            lo = jnp.where((lane >= CHUNK) & (not no_carry), lo, jnp.where(lane == 0, sink, NEG))
        return jnp.concatenate([lo, hi], axis=1)

    def drop_sink_lane(p, c):
        lo, hi = p[:, 0:V7X_LANES], p[:, V7X_LANES:n_keys]
        lane = lax.broadcasted_iota(jnp.int32, lo.shape, 1)
        if c % 2 == 0:
            hi = jnp.where(lane == V7X_LANES - 1, jnp.zeros_like(hi), hi)
        else:
            lo = jnp.where(lane == 0, jnp.zeros_like(lo), lo)
        return jnp.concatenate([lo, hi], axis=1)

    pad = V7X_SUBLANES
    blocks = [(c0, vh) for c0 in range(0, sub_chunks, 2) for vh in range(N_KV_HEADS)]
    st = [dict(r0=s * sub) for s in range(n_sub)]

    def stage_rms(s):
        s["hb"] = _rmsnorm(x_ref[s["r0"]:s["r0"] + sub, :], lng_ref[...]).astype(BF16)

    def stage_qkv(s):
        s["qkv"] = _mm(s["hb"], win_ref[:, OFF_Q:OFF_GA])

    def stage_gate_a(s):
        s["sig_a"] = _sigmoid(_mm(s["hb"], win_ref[:, OFF_MA:OFF_MA + d_model]))

    def stage_gate_c(s):
        s["sig_c"] = _sigmoid(_mm(s["hb"], win_ref[:, OFF_MA + d_model:OFF_MA + 2 * d_model]))

    def stage_qk_norm(s):
        r0 = s["r0"]
        s["q_cols"], kr, v = _qk_norm_rope(s.pop("qkv"), gq_ref, gk_ref, bdq_ref, bdk_ref,
                                           cos_t[r0:r0 + sub], sin_s[r0:r0 + sub])
        kr_t = kr.T
        for n, t in enumerate(_kv_variants_t(kr_t)):
            kbuf[n, :, WINDOW + r0:WINDOW + r0 + sub] = t
        for n, t in enumerate(_kv_variants(v)):
            vbuf[n, WINDOW + r0:WINDOW + r0 + sub, :] = t
        if r0 + sub == tm:
            ko_ref[...] = kr_t[:, sub - WINDOW:sub]
            vo_ref[...] = v[sub - WINDOW:sub, :].T

    def stage_scores(s):
        r0 = s["r0"]
        s_e, s_o = [], []
        for c0, vh in blocks:
            start = key_window_start(r0, c0)
            win = slice(start, start + n_keys)
            qs = jnp.concatenate(
                [s["q_cols"][2 * vh + pair][(c0 + dc) * CHUNK:(c0 + dc + 1) * CHUNK]
                 for dc in range(2) for pair in range(2)], axis=0)
            se, so = _scores(qs, kbuf[2 * vh, :, win], kbuf[2 * vh + 1, :, win], None,
                             keys_on_lanes=True)
            for dc in range(2):
                half = slice(dc * 2 * CHUNK, (dc + 1) * 2 * CHUNK)
                blk = slice(len(s_e) * 2 * CHUNK, (len(s_e) + 1) * 2 * CHUNK)
                s_e.append(mask_window(se[half], r0, c0 + dc, sink_e[blk]))
                s_o.append(mask_window(so[half], r0, c0 + dc, sink_o[blk]))
        s["s_e"] = jnp.concatenate(s_e, axis=0)
        s["s_o"] = jnp.concatenate(s_o, axis=0)
        del s["q_cols"]

    def stage_conv_proj(s):
        s["bcug"] = _conv_proj(s["hb"], win_ref)

    def stage_conv(s):
        r0 = s["r0"]

        def conv_fn(u):
            ubuf[pad + r0:pad + r0 + sub, :] = u
            conv = ubuf[pad + r0 - 2:pad + r0 - 2 + sub, :] * cw_ref[0]
            conv = conv + ubuf[pad + r0 - 1:pad + r0 - 1 + sub, :] * cw_ref[1]
            return conv + u * cw_ref[2]

        s["c_in"] = _conv_input(s.pop("bcug"), conv_fn)

    def stage_conv_out(s):
        s["yc"] = _mm(s.pop("c_in"), wb_ref[...])

    sink_e, sink_o = _sink_cols(sink_ref, 2 * len(blocks), CHUNK, blocks_per_kv=2)

    def stage_softmax(s):
        s["p_e"], s["r_e"] = _softmax_parts_inline_sink(s.pop("s_e"))
        s["p_o"], s["r_o"] = _softmax_parts_inline_sink(s.pop("s_o"))

    def stage_pv(s):
        r0 = s["r0"]
        for n, (c0, vh) in enumerate(blocks):
            start = key_window_start(r0, c0)
            win = slice(start, start + n_keys)
            br = slice(n * 4 * CHUNK, (n + 1) * 4 * CHUNK)
            p_e, p_o = (jnp.concatenate(
                [drop_sink_lane(p[(2 * n + dc) * 2 * CHUNK:(2 * n + dc + 1) * 2 * CHUNK], c0 + dc)
                 for dc in range(2)], axis=0) for p in (s["p_e"], s["p_o"]))
            o = _pv(p_e, p_o, vbuf[2 * vh, win, :], vbuf[2 * vh + 1, win, :],
                    s["r_e"][br], s["r_o"][br])
            for dc in range(2):
                rows = slice(r0 + (c0 + dc) * CHUNK, r0 + (c0 + dc + 1) * CHUNK)
                for pair in range(2):
                    part = o[(2 * dc + pair) * CHUNK:(2 * dc + pair + 1) * CHUNK]
                    attn_buf[rows, (2 * vh + pair) * V7X_LANES:(2 * vh + pair + 1) * V7X_LANES] = part

    def stage_attn_gate(s):
        s["silu_ga"] = _silu(_mm(s["hb"], win_ref[:, OFF_GA:OFF_B]))

    def stage_attn_out(s):
        r0 = s["r0"]
        a_in = (attn_buf[r0:r0 + sub, :] * s.pop("silu_ga")).astype(BF16)
        ya = _mm(a_in, wa_ref[...])
        s["mix"] = (s.pop("sig_a") * ya + s.pop("sig_c") * s.pop("yc")).astype(BF16)

    def stage_out_proj(s):
        r0 = s["r0"]
        s["r"] = x_ref[r0:r0 + sub, :] + _mm(s.pop("mix"), wo_ref[...])
        s["pp"] = _mm(p_ref[r0:r0 + sub, :].astype(BF16), wpp_ref[...])

    def stage_ple(s):
        r0 = s["r0"]
        r = s.pop("r")
        gate = _sigmoid(_mm(r.astype(BF16), wpg_ref[...]))
        y_ref[r0:r0 + sub, :] = r + gate * s.pop("pp")

    stages = [stage_rms, stage_qkv, stage_qk_norm, stage_gate_a, stage_scores, stage_gate_c,
              stage_softmax, stage_conv_proj, stage_conv, stage_attn_gate, stage_conv_out,
              stage_pv, stage_attn_out, stage_out_proj, stage_ple]
    needs = {stage_rms: 0, stage_qkv: 4, stage_qk_norm: 8, stage_gate_a: 12, stage_scores: 14,
             stage_gate_c: 16, stage_softmax: 20, stage_conv_proj: 24, stage_conv: 25,
             stage_attn_gate: 25, stage_conv_out: 26, stage_pv: 27, stage_attn_out: 28,
             stage_out_proj: 33, stage_ple: 37}
    for t in range(len(stages) + PROMPT_STAGE_SKEW * (n_sub - 1)):
        for j, s in enumerate(st):
            k = t - j * PROMPT_STAGE_SKEW
            if 0 <= k < len(stages):
                if first and j == 0:
                    stager.pump(needs[stages[k]])
                stages[k](s)

    co_ref[...] = ubuf[pad + tm - 2:pad + tm, :]
    ubuf[0:pad, :] = ubuf[tm:tm + pad, :]
    kbuf[:, :, 0:WINDOW] = kbuf[:, :, tm:tm + WINDOW]
    vbuf[:, 0:WINDOW, :] = vbuf[:, tm:tm + WINDOW, :]


def _sample_body(i, x_ref, p_ref, ck_ref, cv_ref, sc_ref, lng_ref, gq_ref, gk_ref, invf_ref,
                 sink_ref, cw_ref, bdq_ref, bdk_ref, win_ref, wa_ref, wb_ref, wo_ref, wpg_ref,
                 wpp_ref, y_ref, ko_ref, vo_ref, co_ref, ubuf, attn_buf, conv_buf, tab):
    bb, _, cache_len = ck_ref.shape
    rows_total, d_model = x_ref.shape
    t_new = rows_total // bb
    n_keys = cache_len + t_new

    @pl.when(i == 0)
    def _init():
        r = lax.broadcasted_iota(jnp.int32, (rows_total, V7X_LANES), 0)
        pos = (PAST_LEN + lax.rem(r, t_new)).astype(F32)
        ang = pos * invf_ref[...]
        tab[0] = jnp.cos(ang)
        tab[1] = jnp.sin(ang) * _rope_sign((rows_total, V7X_LANES))

    def conv_fn(u):
        pad = V7X_SUBLANES
        for b in range(bb):
            rows = slice(b * t_new, (b + 1) * t_new)
            ub = u[rows]
            ubuf[b, pad - (CONV_WIDTH - 1):pad, :] = sc_ref[b]
            ubuf[b, pad:pad + t_new, :] = ub
            conv = ubuf[b, pad - 2:pad - 2 + t_new, :] * cw_ref[0]
            conv = conv + ubuf[b, pad - 1:pad - 1 + t_new, :] * cw_ref[1]
            conv_buf[rows, :] = conv + ub * cw_ref[2]
            co_ref[b] = ubuf[b, pad + t_new - (CONV_WIDTH - 1):pad + t_new, :]
        return conv_buf[...]

    x = x_ref[...]
    hb, qkv = _qkv_proj(x, lng_ref, win_ref)
    sig_a = _sigmoid(_mm(hb, win_ref[:, OFF_MA:OFF_MA + d_model]))
    q_cols, kr, v = _qk_norm_rope(qkv, gq_ref, gk_ref, bdq_ref, bdk_ref, tab[0], tab[1])

    qi = lax.broadcasted_iota(jnp.int32, (2 * t_new, n_keys), 0)
    q_pos = PAST_LEN + lax.rem(qi, t_new)
    k_pos = PAST_LEN - cache_len + lax.broadcasted_iota(jnp.int32, (2 * t_new, n_keys), 1)
    q_ch = q_pos // CHUNK
    k_ch = k_pos // CHUNK
    bad = jnp.logical_not((k_ch <= q_ch) & (k_ch >= q_ch - WINDOW_CHUNKS))

    blocks = [(b, vh) for b in range(bb) for vh in range(N_KV_HEADS)]
    s_e, s_o, vvars = [], [], []
    kr_t = kr.T
    v_t = v.T
    for b in range(bb):
        rows = slice(b * t_new, (b + 1) * t_new)
        kcat = jnp.concatenate([ck_ref[b], kr_t[:, rows]], axis=1)
        vcat = jnp.concatenate([cv_ref[b], v_t[:, rows]], axis=1)
        ko_ref[b] = kcat[:, n_keys - cache_len:n_keys]
        vo_ref[b] = vcat[:, n_keys - cache_len:n_keys]
        kvar = _kv_variants_t(kcat)
        vvars.append(_kv_variants_t(vcat))
        for vh in range(N_KV_HEADS):
            qs = jnp.concatenate([q_cols[2 * vh][rows], q_cols[2 * vh + 1][rows]], axis=0)
            se, so = _scores(qs, kvar[2 * vh], kvar[2 * vh + 1], bad, keys_on_lanes=True)
            s_e.append(se)
            s_o.append(so)

    sig_c = _sigmoid(_mm(hb, win_ref[:, OFF_MA + d_model:OFF_MA + 2 * d_model]))
    sink_e, sink_o = _sink_cols(sink_ref, len(blocks), t_new)
    p_e, r_e = _softmax_parts(jnp.concatenate(s_e, axis=0), sink_e)
    bcug = _conv_proj(hb, win_ref)
    p_o, r_o = _softmax_parts(jnp.concatenate(s_o, axis=0), sink_o)
    c_in = _conv_input(bcug, conv_fn)
    silu_ga = _silu(_mm(hb, win_ref[:, OFF_GA:OFF_B]))
    pp = _mm(p_ref[...].astype(BF16), wpp_ref[...])
    yc = _mm(c_in, wb_ref[...])

    for n, (b, vh) in enumerate(blocks):
        rows = slice(b * t_new, (b + 1) * t_new)
        br = slice(n * 2 * t_new, (n + 1) * 2 * t_new)
        o = _pv(p_e[br], p_o[br], vvars[b][2 * vh], vvars[b][2 * vh + 1], r_e[br], r_o[br],
                keys_on_lanes=True)
        attn_buf[rows, (2 * vh) * V7X_LANES:(2 * vh + 1) * V7X_LANES] = o[0:t_new]
        attn_buf[rows, (2 * vh + 1) * V7X_LANES:(2 * vh + 2) * V7X_LANES] = o[t_new:2 * t_new]

    ya = _mm((attn_buf[0:rows_total, :] * silu_ga).astype(BF16), wa_ref[...])
    mix = (sig_a * ya + sig_c * yc).astype(BF16)
    r = x + _mm(mix, wo_ref[...])
    gate = _sigmoid(_mm(r.astype(BF16), wpg_ref[...]))
    y_ref[...] = r + gate * pp


def _layer_kernel(n_prompt_steps, *refs):
    (xp_ref, pp_ref, xs_ref, ps_ref, ck_ref, cv_ref, sc_ref,
     lng_ref, gq_ref, gk_ref, invf_ref, sink_ref, cw_ref, bdq_ref, bdk_ref,
     win_hbm, wa_hbm, wb_hbm, wo_hbm, wpg_hbm, wpp_hbm,
     yp_ref, kpo_ref, vpo_ref, cpo_ref, ys_ref, kso_ref, vso_ref, cso_ref,
     kbuf, vbuf, ubuf_p, attn_buf, tab_p, ubuf_s, conv_buf, tab_s,
     win_ref, wa_ref, wb_ref, wo_ref, wpg_ref, wpp_ref, stage, sem) = refs
    i = pl.program_id(0)
    small = (lng_ref, gq_ref, gk_ref, invf_ref, sink_ref, cw_ref, bdq_ref, bdk_ref)
    weights = (win_ref, wa_ref, wb_ref, wo_ref, wpg_ref, wpp_ref)

    prompt_refs = (xp_ref, pp_ref, *small, *weights, yp_ref, kpo_ref, vpo_ref, cpo_ref,
                   kbuf, vbuf, ubuf_p, attn_buf, tab_p)

    @pl.when(i == 0)
    def _first_prompt():
        chunks = [(src, dst) + ch for src, dst, order in (
            (win_hbm, win_ref, _WIN_BLOCK_ORDER), (wb_hbm, wb_ref, None), (wa_hbm, wa_ref, None),
            (wo_hbm, wo_ref, None), (wpp_hbm, wpp_ref, None), (wpg_hbm, wpg_ref, None))
            for ch in _stage_chunks(src.shape, order)]
        stager = _WeightStager(chunks, stage, sem)
        stager.start()
        _prompt_body(i, stager, *prompt_refs)
        stager.finish()

    @pl.when((i > 0) & (i < n_prompt_steps))
    def _prompt():
        _prompt_body(i, None, *prompt_refs)

    @pl.when(i >= n_prompt_steps)
    def _sample():
        _sample_body(i - n_prompt_steps, xs_ref, ps_ref, ck_ref, cv_ref, sc_ref, *small, *weights,
                     ys_ref, kso_ref, vso_ref, cso_ref, ubuf_s, attn_buf, conv_buf, tab_s)


def _const_spec(shape):
    nd = len(shape)
    return pl.BlockSpec(shape, lambda i: (0,) * nd, pipeline_mode=pl.Buffered(1))


def _smem_spec():
    return pl.BlockSpec(memory_space=pltpu.SMEM)


def _block_diag_mean(width):
    idx = np.arange(width) // HEAD_DIM
    return jnp.asarray((idx[:, None] == idx[None, :]).astype(np.float32) / HEAD_DIM, dtype=BF16)


def _layer_consts(ln_g, w_in, q_norm_g, k_norm_g, sink, conv_w, w_attn_out, w_conv_out, w_o,
                  w_ple_gate, w_ple_proj):
    inv_freq = ROPE_THETA ** (-jnp.arange(0, HALF, dtype=F32) * 2.0 / HEAD_DIM)
    return dict(
        lng=ln_g.reshape(1, -1).astype(F32),
        gq=q_norm_g.astype(F32).reshape(1, HEAD_DIM),
        gk=k_norm_g.astype(F32).reshape(1, HEAD_DIM),
        invf=jnp.tile(inv_freq, V7X_LANES // HALF).reshape(1, V7X_LANES),
        sink=sink.astype(F32),
        cw=conv_w.astype(F32),
        bdq=_block_diag_mean(ATTN_DIM // 2),
        bdk=_block_diag_mean(KV_DIM),
        win=w_in.astype(F32),
        wa=w_attn_out.astype(F32),
        wb=w_conv_out.astype(F32),
        wo=w_o.astype(F32),
        wpg=w_ple_gate.astype(F32),
        wpp=w_ple_proj.astype(F32),
    )


_SMALL_KEYS = ("lng", "gq", "gk", "invf", "sink", "cw", "bdq", "bdk")
_STAGED_KEYS = ("win", "wa", "wb", "wo", "wpg", "wpp")


def _heads_first(t):
    t = jnp.moveaxis(t, -3, -1)
    return t.reshape(t.shape[:-3] + (KV_DIM, t.shape[-1]))


def _heads_last(t):
    t = t.reshape(t.shape[:-2] + (N_KV_HEADS, HEAD_DIM, t.shape[-1]))
    return jnp.moveaxis(t, -1, -3)


def _layer(xp, pp, xs, ps, cache_k, cache_v, state_conv, c):
    t, d = xp.shape
    nb, t_new, _ = xs.shape
    cache_len = cache_k.shape[1]
    tm, bb = PROMPT_TILE, SAMPLE_BATCH_TILE
    assert t % tm == 0 and tm % PROMPT_SUBTILE == 0
    assert PROMPT_SUBTILE % CHUNK == 0 and PROMPT_SUBTILE >= WINDOW
    assert nb % bb == 0 and CONV_WIDTH - 1 <= t_new <= cache_len
    n_p, n_s = t // tm, nb // bb
    rows = bb * t_new
    xs2 = xs.reshape(nb * t_new, d)
    ps2 = ps.reshape(nb * t_new, ps.shape[-1])
    ck = _heads_first(cache_k)
    cv = _heads_first(cache_v)

    def p_idx(i):
        return jnp.minimum(i, n_p - 1)

    def s_idx(i):
        return jnp.maximum(i - n_p, 0)

    in_specs = ([pl.BlockSpec((tm, d), lambda i: (p_idx(i), 0)),
                 pl.BlockSpec((tm, pp.shape[1]), lambda i: (p_idx(i), 0)),
                 pl.BlockSpec((rows, d), lambda i: (s_idx(i), 0)),
                 pl.BlockSpec((rows, ps2.shape[1]), lambda i: (s_idx(i), 0)),
                 pl.BlockSpec((bb, KV_DIM, cache_len), lambda i: (s_idx(i), 0, 0)),
                 pl.BlockSpec((bb, KV_DIM, cache_len), lambda i: (s_idx(i), 0, 0)),
                 pl.BlockSpec((bb, CONV_WIDTH - 1, CONV_DIM), lambda i: (s_idx(i), 0, 0))]
                + [_smem_spec() if k == "sink" else _const_spec(c[k].shape) for k in _SMALL_KEYS]
                + [pl.BlockSpec(memory_space=pl.ANY) for _ in _STAGED_KEYS])
    out_shape = (jax.ShapeDtypeStruct((t, d), F32),
                 jax.ShapeDtypeStruct((KV_DIM, WINDOW), F32),
                 jax.ShapeDtypeStruct((KV_DIM, WINDOW), F32),
                 jax.ShapeDtypeStruct((CONV_WIDTH - 1, CONV_DIM), F32),
                 jax.ShapeDtypeStruct((nb * t_new, d), F32),
                 jax.ShapeDtypeStruct((nb, KV_DIM, cache_len), F32),
                 jax.ShapeDtypeStruct((nb, KV_DIM, cache_len), F32),
                 jax.ShapeDtypeStruct((nb, CONV_WIDTH - 1, CONV_DIM), F32))
    out_specs = (pl.BlockSpec((tm, d), lambda i: (p_idx(i), 0)),
                 pl.BlockSpec((KV_DIM, WINDOW), lambda i: (0, 0)),
                 pl.BlockSpec((KV_DIM, WINDOW), lambda i: (0, 0)),
                 pl.BlockSpec((CONV_WIDTH - 1, CONV_DIM), lambda i: (0, 0)),
                 pl.BlockSpec((rows, d), lambda i: (s_idx(i), 0)),
                 pl.BlockSpec((bb, KV_DIM, cache_len), lambda i: (s_idx(i), 0, 0)),
                 pl.BlockSpec((bb, KV_DIM, cache_len), lambda i: (s_idx(i), 0, 0)),
                 pl.BlockSpec((bb, CONV_WIDTH - 1, CONV_DIM), lambda i: (s_idx(i), 0, 0)))
    scratch = ([pltpu.VMEM((4, KV_DIM, WINDOW + tm), BF16),
                pltpu.VMEM((4, WINDOW + tm, V7X_LANES), BF16),
                pltpu.VMEM((V7X_SUBLANES + tm, CONV_DIM), F32),
                pltpu.VMEM((max(tm, rows), ATTN_DIM), F32),
                pltpu.VMEM((4, tm, V7X_LANES), F32),
                pltpu.VMEM((bb, V7X_SUBLANES + t_new, CONV_DIM), F32),
                pltpu.VMEM((rows, CONV_DIM), F32),
                pltpu.VMEM((2, rows, V7X_LANES), F32)]
               + [pltpu.VMEM(c[k].shape, BF16) for k in _STAGED_KEYS]
               + [pltpu.VMEM((STAGE_SLOTS, STAGE_ROWS, STAGE_COLS), F32),
                  pltpu.SemaphoreType.DMA((STAGE_SLOTS,))])
    yp, kpo, vpo, cpo, ys, kso, vso, cso = pl.pallas_call(
        functools.partial(_layer_kernel, n_p),
        grid=(n_p + n_s,),
        in_specs=in_specs,
        out_specs=out_specs,
        out_shape=out_shape,
        scratch_shapes=scratch,
        compiler_params=pltpu.CompilerParams(dimension_semantics=("arbitrary",),
                                             vmem_limit_bytes=V7X_VMEM_LIMIT_BYTES),
        name="hybrid_layer",
    )(xp, pp, xs2, ps2, ck, cv, state_conv, *[c[k] for k in _SMALL_KEYS],
      *[c[k] for k in _STAGED_KEYS])
    return ((yp, _heads_last(kpo), _heads_last(vpo), cpo),
            (ys.reshape(nb, t_new, d), _heads_last(kso), _heads_last(vso), cso))


def kernel(x_prompt, x_sample, p_prompt, p_sample, cache_k, cache_v, state_conv, ln_g, w_in,
           q_norm_g, k_norm_g, sink, conv_w, w_attn_out, w_conv_out, w_o, w_ple_gate, w_ple_proj):
    depth = ln_g.shape[0]
    assert x_prompt.shape[0] == 1, "one prompt sequence per call"
    hp, hs = x_prompt[0], x_sample
    kp_l, vp_l, cp_l, ks_l, vs_l, cs_l = [], [], [], [], [], []
    for i in range(depth):
        c = _layer_consts(ln_g[i], w_in[i], q_norm_g[i], k_norm_g[i], sink[i],
                          jnp.swapaxes(conv_w, 0, 1)[:, i:i + 1, :],
                          w_attn_out[i], w_conv_out[i], w_o[i], w_ple_gate[i], w_ple_proj[i])
        (hp, kp, vp, cp), (hs, ks, vs, cs) = _layer(hp, p_prompt[i, 0], hs, p_sample[i], cache_k[i],
                                                    cache_v[i], state_conv[i], c)
        kp_l.append(kp[None])
        vp_l.append(vp[None])
        cp_l.append(cp[None])
        ks_l.append(ks)
        vs_l.append(vs)
        cs_l.append(cs)
    return (hp[None], hs, jnp.stack(kp_l), jnp.stack(vp_l), jnp.stack(cp_l),
            jnp.stack(ks_l), jnp.stack(vs_l), jnp.stack(cs_l))
```

```python
import functools

import numpy as np
import jax
import jax.numpy as jnp
from jax import lax
from jax.experimental import pallas as pl
from jax.experimental.pallas import tpu as pltpu

F32 = jnp.float32
BF16 = jnp.bfloat16

CHUNK = 64
WINDOW = 128
WINDOW_CHUNKS = WINDOW // CHUNK
N_HEADS = 8
N_KV_HEADS = 2
GROUP_HEADS = N_HEADS // N_KV_HEADS
HEAD_DIM = 64
HALF = HEAD_DIM // 2
ATTN_DIM = N_HEADS * HEAD_DIM
KV_DIM = N_KV_HEADS * HEAD_DIM
CONV_DIM = 512
CONV_WIDTH = 3
PAST_LEN = 1024
ROPE_THETA = 10000.0
EPS = 1e-6
NEG = -1e30

OFF_Q = 0
OFF_K = OFF_Q + ATTN_DIM
OFF_V = OFF_K + KV_DIM
OFF_GA = OFF_V + KV_DIM
OFF_B = OFF_GA + ATTN_DIM
OFF_GC_END = OFF_B + 4 * CONV_DIM
OFF_MA = OFF_GC_END

V7X_LANES = 128
V7X_SUBLANES = 8
V7X_VMEM_LIMIT_BYTES = 59 * 1024 * 1024

PROMPT_TILE = 512
PROMPT_SUBTILE = 256
PROMPT_STAGE_SKEW = 1
STAGE_ROWS = 256
STAGE_COLS = 1024
STAGE_SLOTS = 4
_WIN_BLOCK_ORDER = (0, 3, 4, 5, 1, 2)
SAMPLE_BATCH_TILE = 8


def _mm(a, w):
    return jnp.dot(a, w, preferred_element_type=F32)


def _mm_t(a, b):
    return lax.dot_general(a, b, (((1,), (1,)), ((), ())), preferred_element_type=F32)


def _sigmoid(x):
    return 0.5 * jnp.tanh(0.5 * x) + 0.5


def _silu(x):
    return x * _sigmoid(x)


def _rmsnorm(x, g):
    ms = jnp.mean(x * x, axis=-1, keepdims=True)
    return x * lax.rsqrt(ms + EPS) * g


def _group_mean(t, bd):
    return _mm(t.astype(BF16), bd)


def _head_rinv(t, bd):
    return lax.rsqrt(_group_mean(t * t, bd) + EPS)


def _rope(xc, cos_t, sin_s):
    lane = lax.broadcasted_iota(jnp.int32, xc.shape, 1)
    upper = (lane & HALF) != 0
    rot = jnp.where(upper, pltpu.roll(xc, HALF, 1), pltpu.roll(xc, V7X_LANES - HALF, 1))
    return xc * cos_t + rot * sin_s


def _rope_sign(shape):
    lane = lax.broadcasted_iota(jnp.int32, shape, 1)
    return jnp.where((lane & HALF) != 0, 1.0, -1.0).astype(F32)


def _kv_variants(t):
    lane = lax.broadcasted_iota(jnp.int32, t.shape, 1)
    lo = lane < HEAD_DIM
    sw = pltpu.roll(t, HEAD_DIM, 1)
    zero = jnp.zeros_like(t)
    return (jnp.where(lo, t, zero).astype(BF16), jnp.where(lo, zero, sw).astype(BF16),
            jnp.where(lo, sw, zero).astype(BF16), jnp.where(lo, zero, t).astype(BF16))


def _kv_variants_t(t):
    tb = t.astype(BF16)
    h0, h1 = tb[0:HEAD_DIM], tb[HEAD_DIM:2 * HEAD_DIM]
    zero = jnp.zeros_like(h0)
    return (jnp.concatenate([h0, zero], axis=0), jnp.concatenate([zero, h0], axis=0),
            jnp.concatenate([h1, zero], axis=0), jnp.concatenate([zero, h1], axis=0))


def _scores(qs, ka, kb, bad, keys_on_lanes=False):
    dot = _mm if keys_on_lanes else _mm_t
    s_e = dot(qs, ka)
    s_o = dot(qs, kb)
    if bad is not None:
        s_e = jnp.where(bad, NEG, s_e)
        s_o = jnp.where(bad, NEG, s_o)
    return s_e, s_o


def _softmax_parts(s, sink):
    m = jnp.maximum(jnp.max(s, axis=-1, keepdims=True), sink)
    e = jnp.exp(s - m)
    r = 1.0 / (jnp.sum(e, axis=-1, keepdims=True) + jnp.exp(sink - m))
    return e.astype(BF16), r


def _softmax_parts_inline_sink(s):
    m = jnp.max(s, axis=-1, keepdims=True)
    e = jnp.exp(s - m)
    return e.astype(BF16), 1.0 / jnp.sum(e, axis=-1, keepdims=True)


def _pv(p_e, p_o, va, vb, r_e, r_o, keys_on_lanes=False):
    dot = _mm_t if keys_on_lanes else _mm
    o = dot(p_e, va) + dot(p_o, vb)
    lane = lax.broadcasted_iota(jnp.int32, o.shape, 1)
    return o * jnp.where(lane < HEAD_DIM, r_e, r_o)


def _sink_cols(sink_ref, n_blocks, rows_per_pair, blocks_per_kv=1):
    row = lax.broadcasted_iota(jnp.int32, (n_blocks * 2 * rows_per_pair, 1), 0)
    second_pair = (row // rows_per_pair) % 2 == 1
    second_kv = (row // (2 * rows_per_pair * blocks_per_kv)) % 2 == 1

    def pick(odd):
        kv0 = jnp.where(second_pair, sink_ref[2 + odd], sink_ref[odd])
        kv1 = jnp.where(second_pair, sink_ref[GROUP_HEADS + 2 + odd], sink_ref[GROUP_HEADS + odd])
        return jnp.where(second_kv, kv1, kv0)

    return pick(0), pick(1)


def _stage_chunks(shape, col_order=None):
    rows, cols = shape
    assert rows % STAGE_ROWS == 0 and cols % V7X_LANES == 0
    width = max(c for c in range(V7X_LANES, STAGE_COLS + 1, V7X_LANES) if cols % c == 0)
    blocks = list(range(cols // width)) if col_order is None else list(col_order)
    assert sorted(blocks) == list(range(cols // width))
    return [(r0, b * width, STAGE_ROWS, width) for b in blocks for r0 in range(0, rows, STAGE_ROWS)]


class _WeightStager:
    def __init__(self, chunks, stage, sem):
        self.chunks, self.stage, self.sem, self.done = chunks, stage, sem, 0

    def _copy(self, n):
        src, _, r0, c0, rows, cols = self.chunks[n]
        slot = n % STAGE_SLOTS
        return pltpu.make_async_copy(src.at[pl.ds(r0, rows), pl.ds(c0, cols)],
                                     self.stage.at[slot, pl.ds(0, rows), pl.ds(0, cols)],
                                     self.sem.at[slot])

    def start(self):
        for n in range(min(STAGE_SLOTS, len(self.chunks))):
            self._copy(n).start()

    def pump(self, count):
        while self.done < min(count, len(self.chunks)):
            n = self.done
            _, dst, r0, c0, rows, cols = self.chunks[n]
            self._copy(n).wait()
            dst[r0:r0 + rows, c0:c0 + cols] = (
                self.stage[n % STAGE_SLOTS, 0:rows, 0:cols].astype(BF16))
            if n + STAGE_SLOTS < len(self.chunks):
                self._copy(n + STAGE_SLOTS).start()
            self.done += 1

    def finish(self):
        self.pump(len(self.chunks))


def _qkv_proj(x, lng_ref, win_ref):
    hb = _rmsnorm(x, lng_ref[...]).astype(BF16)
    return hb, _mm(hb, win_ref[:, OFF_Q:OFF_GA])


def _tile_lanes(g, width):
    while g.shape[1] < width:
        g = jnp.concatenate([g, g], axis=1)
    return g


def _qk_norm_rope(qkv, gq_ref, gk_ref, bdq_ref, bdk_ref, cos_t, sin_s):
    half_q = ATTN_DIM // 2
    gq = _tile_lanes(gq_ref[...] * (HEAD_DIM ** -0.5), half_q)
    gk = _tile_lanes(gk_ref[...], KV_DIM)
    q_cols = []
    for j in range(2):
        t = qkv[:, j * half_q:(j + 1) * half_q]
        rinv = _head_rinv(t, bdq_ref[...])
        tg = t * gq
        for c in range(half_q // V7X_LANES):
            cols = slice(c * V7X_LANES, (c + 1) * V7X_LANES)
            q_cols.append((_rope(tg[:, cols], cos_t, sin_s) * rinv[:, cols]).astype(BF16))
    k = qkv[:, OFF_K:OFF_V]
    kr = _rope(k * gk, cos_t, sin_s) * _head_rinv(k, bdk_ref[...])
    v = qkv[:, OFF_V:OFF_GA]
    return q_cols, kr, v


def _conv_proj(hb, win_ref):
    return _mm(hb, win_ref[:, OFF_B:OFF_GC_END])


def _conv_input(bcug, conv_fn):
    b_gate = bcug[:, 0:CONV_DIM]
    u = bcug[:, CONV_DIM:2 * CONV_DIM] * bcug[:, 2 * CONV_DIM:3 * CONV_DIM]
    gate_c = bcug[:, 3 * CONV_DIM:4 * CONV_DIM]
    conv = conv_fn(u)
    return (b_gate * conv * _silu(gate_c)).astype(BF16)


def _prompt_body(i, stager, x_ref, p_ref, lng_ref, gq_ref, gk_ref, invf_ref, sink_ref, cw_ref,
                 bdq_ref, bdk_ref, win_ref, wa_ref, wb_ref, wo_ref, wpg_ref, wpp_ref,
                 y_ref, ko_ref, vo_ref, co_ref, kbuf, vbuf, ubuf, attn_buf, tab):
    tm, d_model = x_ref.shape
    invf = invf_ref[...]
    first = stager is not None

    if first:
        kbuf[:, :, 0:WINDOW] = jnp.zeros((4, KV_DIM, WINDOW), BF16)
        vbuf[:, 0:WINDOW, :] = jnp.zeros((4, WINDOW, V7X_LANES), BF16)
        ubuf[0:V7X_SUBLANES, :] = jnp.zeros((V7X_SUBLANES, CONV_DIM), F32)
        r = lax.broadcasted_iota(jnp.int32, (tm, V7X_LANES), 0).astype(F32)
        ang = r * invf
        sgn = _rope_sign((tm, V7X_LANES))
        c_r = jnp.cos(ang)
        s_r = jnp.sin(ang)
        tab[0] = c_r
        tab[1] = s_r
        tab[2] = c_r * sgn
        tab[3] = s_r * sgn

    base = (i * tm).astype(F32) * invf
    cb = jnp.cos(base)
    sb = jnp.sin(base)
    cos_t = tab[0] * cb - tab[1] * sb
    sin_s = tab[3] * cb + tab[2] * sb

    sub = PROMPT_SUBTILE
    n_sub = tm // sub
    sub_chunks = sub // CHUNK
    assert 2 * CHUNK == V7X_LANES and WINDOW == V7X_LANES
    n_keys = 2 * V7X_LANES

    def key_window_start(r0, c):
        return ((r0 + c * CHUNK) // V7X_LANES) * V7X_LANES

    def mask_window(sc, r0, c, sink):
        lo, hi = sc[:, 0:V7X_LANES], sc[:, V7X_LANES:n_keys]
        lane = lax.broadcasted_iota(jnp.int32, lo.shape, 1)
        no_carry = first and key_window_start(r0, c) < WINDOW
        if no_carry:
            lo = jnp.full_like(lo, NEG)
        if ((r0 + c * CHUNK) // CHUNK) % 2 == 0:
            hi = jnp.where(lane < CHUNK, hi, jnp.where(lane == V7X_LANES - 1, sink, NEG))
        else:
            lo = jnp.where((lane >= CHUNK) & (not no_carry), lo, jnp.where(lane == 0, sink, NEG))
        return jnp.concatenate([lo, hi], axis=1)

    def drop_sink_lane(p, c):
        lo, hi = p[:, 0:V7X_LANES], p[:, V7X_LANES:n_keys]
        lane = lax.broadcasted_iota(jnp.int32, lo.shape, 1)
        if c % 2 == 0:
            hi = jnp.where(lane == V7X_LANES - 1, jnp.zeros_like(hi), hi)
        else:
            lo = jnp.where(lane == 0, jnp.zeros_like(lo), lo)
        return jnp.concatenate([lo, hi], axis=1)

    pad = V7X_SUBLANES
    blocks = [(c0, vh) for c0 in range(0, sub_chunks, 2) for vh in range(N_KV_HEADS)]
    st = [dict(r0=s * sub) for s in range(n_sub)]

    def stage_rms(s):
        s["hb"] = _rmsnorm(x_ref[s["r0"]:s["r0"] + sub, :], lng_ref[...]).astype(BF16)

    def stage_qkv(s):
        s["qkv"] = _mm(s["hb"], win_ref[:, OFF_Q:OFF_GA])

    def stage_gate_a(s):
        s["sig_a"] = _sigmoid(_mm(s["hb"], win_ref[:, OFF_MA:OFF_MA + d_model]))

    def stage_gate_c(s):
        s["sig_c"] = _sigmoid(_mm(s["hb"], win_ref[:, OFF_MA + d_model:OFF_MA + 2 * d_model]))

    def stage_qk_norm(s):
        r0 = s["r0"]
        s["q_cols"], kr, v = _qk_norm_rope(s.pop("qkv"), gq_ref, gk_ref, bdq_ref, bdk_ref,
                                           cos_t[r0:r0 + sub], sin_s[r0:r0 + sub])
        kr_t = kr.T
        for n, t in enumerate(_kv_variants_t(kr_t)):
            kbuf[n, :, WINDOW + r0:WINDOW + r0 + sub] = t
        for n, t in enumerate(_kv_variants(v)):
            vbuf[n, WINDOW + r0:WINDOW + r0 + sub, :] = t
        if r0 + sub == tm:
            ko_ref[...] = kr_t[:, sub - WINDOW:sub]
            vo_ref[...] = v[sub - WINDOW:sub, :].T

    def stage_scores(s):
        r0 = s["r0"]
        s_e, s_o = [], []
        for c0, vh in blocks:
            start = key_window_start(r0, c0)
            win = slice(start, start + n_keys)
            qs = jnp.concatenate(
                [s["q_cols"][2 * vh + pair][(c0 + dc) * CHUNK:(c0 + dc + 1) * CHUNK]
                 for dc in range(2) for pair in range(2)], axis=0)
            se, so = _scores(qs, kbuf[2 * vh, :, win], kbuf[2 * vh + 1, :, win], None,
                             keys_on_lanes=True)
            for dc in range(2):
                half = slice(dc * 2 * CHUNK, (dc + 1) * 2 * CHUNK)
                blk = slice(len(s_e) * 2 * CHUNK, (len(s_e) + 1) * 2 * CHUNK)
                s_e.append(mask_window(se[half], r0, c0 + dc, sink_e[blk]))
                s_o.append(mask_window(so[half], r0, c0 + dc, sink_o[blk]))
        s["s_e"] = jnp.concatenate(s_e, axis=0)
        s["s_o"] = jnp.concatenate(s_o, axis=0)
        del s["q_cols"]

    def stage_conv_proj(s):
        s["bcug"] = _conv_proj(s["hb"], win_ref)

    def stage_conv(s):
        r0 = s["r0"]

        def conv_fn(u):
            ubuf[pad + r0:pad + r0 + sub, :] = u
            conv = ubuf[pad + r0 - 2:pad + r0 - 2 + sub, :] * cw_ref[0]
            conv = conv + ubuf[pad + r0 - 1:pad + r0 - 1 + sub, :] * cw_ref[1]
            return conv + u * cw_ref[2]

        s["c_in"] = _conv_input(s.pop("bcug"), conv_fn)

    def stage_conv_out(s):
        s["yc"] = _mm(s.pop("c_in"), wb_ref[...])

    sink_e, sink_o = _sink_cols(sink_ref, 2 * len(blocks), CHUNK, blocks_per_kv=2)

    def stage_softmax(s):
        s["p_e"], s["r_e"] = _softmax_parts_inline_sink(s.pop("s_e"))
        s["p_o"], s["r_o"] = _softmax_parts_inline_sink(s.pop("s_o"))

    def stage_pv(s):
        r0 = s["r0"]
        for n, (c0, vh) in enumerate(blocks):
            start = key_window_start(r0, c0)
            win = slice(start, start + n_keys)
            br = slice(n * 4 * CHUNK, (n + 1) * 4 * CHUNK)
            p_e, p_o = (jnp.concatenate(
                [drop_sink_lane(p[(2 * n + dc) * 2 * CHUNK:(2 * n + dc + 1) * 2 * CHUNK], c0 + dc)
                 for dc in range(2)], axis=0) for p in (s["p_e"], s["p_o"]))
            o = _pv(p_e, p_o, vbuf[2 * vh, win, :], vbuf[2 * vh + 1, win, :],
                    s["r_e"][br], s["r_o"][br])
            for dc in range(2):
                rows = slice(r0 + (c0 + dc) * CHUNK, r0 + (c0 + dc + 1) * CHUNK)
                for pair in range(2):
                    part = o[(2 * dc + pair) * CHUNK:(2 * dc + pair + 1) * CHUNK]
                    attn_buf[rows, (2 * vh + pair) * V7X_LANES:(2 * vh + pair + 1) * V7X_LANES] = part

    def stage_attn_gate(s):
        s["silu_ga"] = _silu(_mm(s["hb"], win_ref[:, OFF_GA:OFF_B]))

    def stage_attn_out(s):
        r0 = s["r0"]
        a_in = (attn_buf[r0:r0 + sub, :] * s.pop("silu_ga")).astype(BF16)
        ya = _mm(a_in, wa_ref[...])
        s["mix"] = (s.pop("sig_a") * ya + s.pop("sig_c") * s.pop("yc")).astype(BF16)

    def stage_out_proj(s):
        r0 = s["r0"]
        s["r"] = x_ref[r0:r0 + sub, :] + _mm(s.pop("mix"), wo_ref[...])
        s["pp"] = _mm(p_ref[r0:r0 + sub, :].astype(BF16), wpp_ref[...])

    def stage_ple(s):
        r0 = s["r0"]
        r = s.pop("r")
        gate = _sigmoid(_mm(r.astype(BF16), wpg_ref[...]))
        y_ref[r0:r0 + sub, :] = r + gate * s.pop("pp")

    stages = [stage_rms, stage_qkv, stage_qk_norm, stage_gate_a, stage_scores, stage_gate_c,
              stage_softmax, stage_conv_proj, stage_conv, stage_attn_gate, stage_conv_out,
              stage_pv, stage_attn_out, stage_out_proj, stage_ple]
    needs = {stage_rms: 0, stage_qkv: 4, stage_qk_norm: 8, stage_gate_a: 12, stage_scores: 14,
             stage_gate_c: 16, stage_softmax: 20, stage_conv_proj: 24, stage_conv: 25,
             stage_attn_gate: 25, stage_conv_out: 26, stage_pv: 27, stage_attn_out: 28,
             stage_out_proj: 33, stage_ple: 37}
    for t in range(len(stages) + PROMPT_STAGE_SKEW * (n_sub - 1)):
        for j, s in enumerate(st):
            k = t - j * PROMPT_STAGE_SKEW
            if 0 <= k < len(stages):
                if first and j == 0:
                    stager.pump(needs[stages[k]])
                stages[k](s)

    co_ref[...] = ubuf[pad + tm - 2:pad + tm, :]
    ubuf[0:pad, :] = ubuf[tm:tm + pad, :]
    kbuf[:, :, 0:WINDOW] = kbuf[:, :, tm:tm + WINDOW]
    vbuf[:, 0:WINDOW, :] = vbuf[:, tm:tm + WINDOW, :]


def _sample_body(i, x_ref, p_ref, ck_ref, cv_ref, sc_ref, lng_ref, gq_ref, gk_ref, invf_ref,
                 sink_ref, cw_ref, bdq_ref, bdk_ref, win_ref, wa_ref, wb_ref, wo_ref, wpg_ref,
                 wpp_ref, y_ref, ko_ref, vo_ref, co_ref, ubuf, attn_buf, conv_buf, tab):
    bb, _, cache_len = ck_ref.shape
    rows_total, d_model = x_ref.shape
    t_new = rows_total // bb
    n_keys = cache_len + t_new

    @pl.when(i == 0)
    def _init():
        r = lax.broadcasted_iota(jnp.int32, (rows_total, V7X_LANES), 0)
        pos = (PAST_LEN + lax.rem(r, t_new)).astype(F32)
        ang = pos * invf_ref[...]
        tab[0] = jnp.cos(ang)
        tab[1] = jnp.sin(ang) * _rope_sign((rows_total, V7X_LANES))

    def conv_fn(u):
        pad = V7X_SUBLANES
        for b in range(bb):
            rows = slice(b * t_new, (b + 1) * t_new)
            ub = u[rows]
            ubuf[b, pad - (CONV_WIDTH - 1):pad, :] = sc_ref[b]
            ubuf[b, pad:pad + t_new, :] = ub
            conv = ubuf[b, pad - 2:pad - 2 + t_new, :] * cw_ref[0]
            conv = conv + ubuf[b, pad - 1:pad - 1 + t_new, :] * cw_ref[1]
            conv_buf[rows, :] = conv + ub * cw_ref[2]
            co_ref[b] = ubuf[b, pad + t_new - (CONV_WIDTH - 1):pad + t_new, :]
        return conv_buf[...]

    x = x_ref[...]
    hb, qkv = _qkv_proj(x, lng_ref, win_ref)
    sig_a = _sigmoid(_mm(hb, win_ref[:, OFF_MA:OFF_MA + d_model]))
    q_cols, kr, v = _qk_norm_rope(qkv, gq_ref, gk_ref, bdq_ref, bdk_ref, tab[0], tab[1])

    qi = lax.broadcasted_iota(jnp.int32, (2 * t_new, n_keys), 0)
    q_pos = PAST_LEN + lax.rem(qi, t_new)
    k_pos = PAST_LEN - cache_len + lax.broadcasted_iota(jnp.int32, (2 * t_new, n_keys), 1)
    q_ch = q_pos // CHUNK
    k_ch = k_pos // CHUNK
    bad = jnp.logical_not((k_ch <= q_ch) & (k_ch >= q_ch - WINDOW_CHUNKS))

    blocks = [(b, vh) for b in range(bb) for vh in range(N_KV_HEADS)]
    s_e, s_o, vvars = [], [], []
    kr_t = kr.T
    v_t = v.T
    for b in range(bb):
        rows = slice(b * t_new, (b + 1) * t_new)
        kcat = jnp.concatenate([ck_ref[b], kr_t[:, rows]], axis=1)
        vcat = jnp.concatenate([cv_ref[b], v_t[:, rows]], axis=1)
        ko_ref[b] = kcat[:, n_keys - cache_len:n_keys]
        vo_ref[b] = vcat[:, n_keys - cache_len:n_keys]
        kvar = _kv_variants_t(kcat)
        vvars.append(_kv_variants_t(vcat))
        for vh in range(N_KV_HEADS):
            qs = jnp.concatenate([q_cols[2 * vh][rows], q_cols[2 * vh + 1][rows]], axis=0)
            se, so = _scores(qs, kvar[2 * vh], kvar[2 * vh + 1], bad, keys_on_lanes=True)
            s_e.append(se)
            s_o.append(so)

    sig_c = _sigmoid(_mm(hb, win_ref[:, OFF_MA + d_model:OFF_MA + 2 * d_model]))
    sink_e, sink_o = _sink_cols(sink_ref, len(blocks), t_new)
    p_e, r_e = _softmax_parts(jnp.concatenate(s_e, axis=0), sink_e)
    bcug = _conv_proj(hb, win_ref)
    p_o, r_o = _softmax_parts(jnp.concatenate(s_o, axis=0), sink_o)
    c_in = _conv_input(bcug, conv_fn)
    silu_ga = _silu(_mm(hb, win_ref[:, OFF_GA:OFF_B]))
    pp = _mm(p_ref[...].astype(BF16), wpp_ref[...])
    yc = _mm(c_in, wb_ref[...])

    for n, (b, vh) in enumerate(blocks):
        rows = slice(b * t_new, (b + 1) * t_new)
        br = slice(n * 2 * t_new, (n + 1) * 2 * t_new)
        o = _pv(p_e[br], p_o[br], vvars[b][2 * vh], vvars[b][2 * vh + 1], r_e[br], r_o[br],
                keys_on_lanes=True)
        attn_buf[rows, (2 * vh) * V7X_LANES:(2 * vh + 1) * V7X_LANES] = o[0:t_new]
        attn_buf[rows, (2 * vh + 1) * V7X_LANES:(2 * vh + 2) * V7X_LANES] = o[t_new:2 * t_new]

    ya = _mm((attn_buf[0:rows_total, :] * silu_ga).astype(BF16), wa_ref[...])
    mix = (sig_a * ya + sig_c * yc).astype(BF16)
    r = x + _mm(mix, wo_ref[...])
    gate = _sigmoid(_mm(r.astype(BF16), wpg_ref[...]))
    y_ref[...] = r + gate * pp


def _layer_kernel(n_prompt_steps, *refs):
    (xp_ref, pp_ref, xs_ref, ps_ref, ck_ref, cv_ref, sc_ref,
     lng_ref, gq_ref, gk_ref, invf_ref, sink_ref, cw_ref, bdq_ref, bdk_ref,
     win_hbm, wa_hbm, wb_hbm, wo_hbm, wpg_hbm, wpp_hbm,
     yp_ref, kpo_ref, vpo_ref, cpo_ref, ys_ref, kso_ref, vso_ref, cso_ref,
     kbuf, vbuf, ubuf_p, attn_buf, tab_p, ubuf_s, conv_buf, tab_s,
     win_ref, wa_ref, wb_ref, wo_ref, wpg_ref, wpp_ref, stage, sem) = refs
    i = pl.program_id(0)
    small = (lng_ref, gq_ref, gk_ref, invf_ref, sink_ref, cw_ref, bdq_ref, bdk_ref)
    weights = (win_ref, wa_ref, wb_ref, wo_ref, wpg_ref, wpp_ref)

    prompt_refs = (xp_ref, pp_ref, *small, *weights, yp_ref, kpo_ref, vpo_ref, cpo_ref,
                   kbuf, vbuf, ubuf_p, attn_buf, tab_p)

    @pl.when(i == 0)
    def _first_prompt():
        chunks = [(src, dst) + ch for src, dst, order in (
            (win_hbm, win_ref, _WIN_BLOCK_ORDER), (wb_hbm, wb_ref, None), (wa_hbm, wa_ref, None),
            (wo_hbm, wo_ref, None), (wpp_hbm, wpp_ref, None), (wpg_hbm, wpg_ref, None))
            for ch in _stage_chunks(src.shape, order)]
        stager = _WeightStager(chunks, stage, sem)
        stager.start()
        _prompt_body(i, stager, *prompt_refs)
        stager.finish()

    @pl.when((i > 0) & (i < n_prompt_steps))
    def _prompt():
        _prompt_body(i, None, *prompt_refs)

    @pl.when(i >= n_prompt_steps)
    def _sample():
        _sample_body(i - n_prompt_steps, xs_ref, ps_ref, ck_ref, cv_ref, sc_ref, *small, *weights,
                     ys_ref, kso_ref, vso_ref, cso_ref, ubuf_s, attn_buf, conv_buf, tab_s)


def _const_spec(shape):
    nd = len(shape)
    return pl.BlockSpec(shape, lambda i: (0,) * nd, pipeline_mode=pl.Buffered(1))


def _smem_spec():
    return pl.BlockSpec(memory_space=pltpu.SMEM)


def _block_diag_mean(width):
    idx = np.arange(width) // HEAD_DIM
    return jnp.asarray((idx[:, None] == idx[None, :]).astype(np.float32) / HEAD_DIM, dtype=BF16)


def _layer_consts(ln_g, w_in, q_norm_g, k_norm_g, sink, conv_w, w_attn_out, w_conv_out, w_o,
                  w_ple_gate, w_ple_proj):
    inv_freq = ROPE_THETA ** (-jnp.arange(0, HALF, dtype=F32) * 2.0 / HEAD_DIM)
    return dict(
        lng=ln_g.reshape(1, -1).astype(F32),
        gq=q_norm_g.astype(F32).reshape(1, HEAD_DIM),
        gk=k_norm_g.astype(F32).reshape(1, HEAD_DIM),
        invf=jnp.tile(inv_freq, V7X_LANES // HALF).reshape(1, V7X_LANES),
        sink=sink.astype(F32),
        cw=conv_w.astype(F32),
        bdq=_block_diag_mean(ATTN_DIM // 2),
        bdk=_block_diag_mean(KV_DIM),
        win=w_in.astype(F32),
        wa=w_attn_out.astype(F32),
        wb=w_conv_out.astype(F32),
        wo=w_o.astype(F32),
        wpg=w_ple_gate.astype(F32),
        wpp=w_ple_proj.astype(F32),
    )


_SMALL_KEYS = ("lng", "gq", "gk", "invf", "sink", "cw", "bdq", "bdk")
_STAGED_KEYS = ("win", "wa", "wb", "wo", "wpg", "wpp")


def _heads_first(t):
    t = jnp.moveaxis(t, -3, -1)
    return t.reshape(t.shape[:-3] + (KV_DIM, t.shape[-1]))


def _heads_last(t):
    t = t.reshape(t.shape[:-2] + (N_KV_HEADS, HEAD_DIM, t.shape[-1]))
    return jnp.moveaxis(t, -1, -3)


def _layer(xp, pp, xs, ps, cache_k, cache_v, state_conv, c):
    t, d = xp.shape
    nb, t_new, _ = xs.shape
    cache_len = cache_k.shape[1]
    tm, bb = PROMPT_TILE, SAMPLE_BATCH_TILE
    assert t % tm == 0 and tm % PROMPT_SUBTILE == 0
    assert PROMPT_SUBTILE % CHUNK == 0 and PROMPT_SUBTILE >= WINDOW
    assert nb % bb == 0 and CONV_WIDTH - 1 <= t_new <= cache_len
    n_p, n_s = t // tm, nb // bb
    rows = bb * t_new
    xs2 = xs.reshape(nb * t_new, d)
    ps2 = ps.reshape(nb * t_new, ps.shape[-1])
    ck = _heads_first(cache_k)
    cv = _heads_first(cache_v)

    def p_idx(i):
        return jnp.minimum(i, n_p - 1)

    def s_idx(i):
        return jnp.maximum(i - n_p, 0)

    in_specs = ([pl.BlockSpec((tm, d), lambda i: (p_idx(i), 0)),
                 pl.BlockSpec((tm, pp.shape[1]), lambda i: (p_idx(i), 0)),
                 pl.BlockSpec((rows, d), lambda i: (s_idx(i), 0)),
                 pl.BlockSpec((rows, ps2.shape[1]), lambda i: (s_idx(i), 0)),
                 pl.BlockSpec((bb, KV_DIM, cache_len), lambda i: (s_idx(i), 0, 0)),
                 pl.BlockSpec((bb, KV_DIM, cache_len), lambda i: (s_idx(i), 0, 0)),
                 pl.BlockSpec((bb, CONV_WIDTH - 1, CONV_DIM), lambda i: (s_idx(i), 0, 0))]
                + [_smem_spec() if k == "sink" else _const_spec(c[k].shape) for k in _SMALL_KEYS]
                + [pl.BlockSpec(memory_space=pl.ANY) for _ in _STAGED_KEYS])
    out_shape = (jax.ShapeDtypeStruct((t, d), F32),
                 jax.ShapeDtypeStruct((KV_DIM, WINDOW), F32),
                 jax.ShapeDtypeStruct((KV_DIM, WINDOW), F32),
                 jax.ShapeDtypeStruct((CONV_WIDTH - 1, CONV_DIM), F32),
                 jax.ShapeDtypeStruct((nb * t_new, d), F32),
                 jax.ShapeDtypeStruct((nb, KV_DIM, cache_len), F32),
                 jax.ShapeDtypeStruct((nb, KV_DIM, cache_len), F32),
                 jax.ShapeDtypeStruct((nb, CONV_WIDTH - 1, CONV_DIM), F32))
    out_specs = (pl.BlockSpec((tm, d), lambda i: (p_idx(i), 0)),
                 pl.BlockSpec((KV_DIM, WINDOW), lambda i: (0, 0)),
                 pl.BlockSpec((KV_DIM, WINDOW), lambda i: (0, 0)),
                 pl.BlockSpec((CONV_WIDTH - 1, CONV_DIM), lambda i: (0, 0)),
                 pl.BlockSpec((rows, d), lambda i: (s_idx(i), 0)),
                 pl.BlockSpec((bb, KV_DIM, cache_len), lambda i: (s_idx(i), 0, 0)),
                 pl.BlockSpec((bb, KV_DIM, cache_len), lambda i: (s_idx(i), 0, 0)),
                 pl.BlockSpec((bb, CONV_WIDTH - 1, CONV_DIM), lambda i: (s_idx(i), 0, 0)))
    scratch = ([pltpu.VMEM((4, KV_DIM, WINDOW + tm), BF16),
                pltpu.VMEM((4, WINDOW + tm, V7X_LANES), BF16),
                pltpu.VMEM((V7X_SUBLANES + tm, CONV_DIM), F32),
                pltpu.VMEM((max(tm, rows), ATTN_DIM), F32),
                pltpu.VMEM((4, tm, V7X_LANES), F32),
                pltpu.VMEM((bb, V7X_SUBLANES + t_new, CONV_DIM), F32),
                pltpu.VMEM((rows, CONV_DIM), F32),
                pltpu.VMEM((2, rows, V7X_LANES), F32)]
               + [pltpu.VMEM(c[k].shape, BF16) for k in _STAGED_KEYS]
               + [pltpu.VMEM((STAGE_SLOTS, STAGE_ROWS, STAGE_COLS), F32),
                  pltpu.SemaphoreType.DMA((STAGE_SLOTS,))])
    yp, kpo, vpo, cpo, ys, kso, vso, cso = pl.pallas_call(
        functools.partial(_layer_kernel, n_p),
        grid=(n_p + n_s,),
        in_specs=in_specs,
        out_specs=out_specs,
        out_shape=out_shape,
        scratch_shapes=scratch,
        compiler_params=pltpu.CompilerParams(dimension_semantics=("arbitrary",),
                                             vmem_limit_bytes=V7X_VMEM_LIMIT_BYTES),
        name="hybrid_layer",
    )(xp, pp, xs2, ps2, ck, cv, state_conv, *[c[k] for k in _SMALL_KEYS],
      *[c[k] for k in _STAGED_KEYS])
    return ((yp, _heads_last(kpo), _heads_last(vpo), cpo),
            (ys.reshape(nb, t_new, d), _heads_last(kso), _heads_last(vso), cso))


def kernel(x_prompt, x_sample, p_prompt, p_sample, cache_k, cache_v, state_conv, ln_g, w_in,
           q_norm_g, k_norm_g, sink, conv_w, w_attn_out, w_conv_out, w_o, w_ple_gate, w_ple_proj):
    depth = ln_g.shape[0]
    assert x_prompt.shape[0] == 1, "one prompt sequence per call"
    hp, hs = x_prompt[0], x_sample
    kp_l, vp_l, cp_l, ks_l, vs_l, cs_l = [], [], [], [], [], []
    for i in range(depth):
        c = _layer_consts(ln_g[i], w_in[i], q_norm_g[i], k_norm_g[i], sink[i],
                          jnp.swapaxes(conv_w, 0, 1)[:, i:i + 1, :],
                          w_attn_out[i], w_conv_out[i], w_o[i], w_ple_gate[i], w_ple_proj[i])
        (hp, kp, vp, cp), (hs, ks, vs, cs) = _layer(hp, p_prompt[i, 0], hs, p_sample[i], cache_k[i],
                                                    cache_v[i], state_conv[i], c)
        kp_l.append(kp[None])
        vp_l.append(vp[None])
        cp_l.append(cp[None])
        ks_l.append(ks)
        vs_l.append(vs)
        cs_l.append(cs)
    return (hp[None], hs, jnp.stack(kp_l), jnp.stack(vp_l), jnp.stack(cp_l),
            jnp.stack(ks_l), jnp.stack(vs_l), jnp.stack(cs_l))
```

```python
import functools

import numpy as np
import jax
import jax.numpy as jnp
from jax import lax
from jax.experimental import pallas as pl
from jax.experimental.pallas import tpu as pltpu

F32 = jnp.float32
BF16 = jnp.bfloat16

CHUNK = 64
WINDOW = 128
WINDOW_CHUNKS = WINDOW // CHUNK
N_HEADS = 8
N_KV_HEADS = 2
GROUP_HEADS = N_HEADS // N_KV_HEADS
HEAD_DIM = 64
HALF = HEAD_DIM // 2
ATTN_DIM = N_HEADS * HEAD_DIM
KV_DIM = N_KV_HEADS * HEAD_DIM
CONV_DIM = 512
CONV_WIDTH = 3
PAST_LEN = 1024
ROPE_THETA = 10000.0
EPS = 1e-6
NEG = -1e30

OFF_Q = 0
OFF_K = OFF_Q + ATTN_DIM
OFF_V = OFF_K + KV_DIM
OFF_GA = OFF_V + KV_DIM
OFF_B = OFF_GA + ATTN_DIM
OFF_GC_END = OFF_B + 4 * CONV_DIM
OFF_MA = OFF_GC_END

V7X_LANES = 128
V7X_SUBLANES = 8
V7X_VMEM_LIMIT_BYTES = 59 * 1024 * 1024

PROMPT_TILE = 512
PROMPT_SUBTILE = 256
PROMPT_STAGE_SKEW = 1
STAGE_ROWS = 256
STAGE_COLS = 1024
STAGE_SLOTS = 4
_WIN_BLOCK_ORDER = (0, 3, 4, 5, 1, 2)
SAMPLE_BATCH_TILE = 8


def _mm(a, w):
    return jnp.dot(a, w, preferred_element_type=F32)


def _mm_t(a, b):
    return lax.dot_general(a, b, (((1,), (1,)), ((), ())), preferred_element_type=F32)


def _sigmoid(x):
    return 0.5 * jnp.tanh(0.5 * x) + 0.5


def _silu(x):
    h = 0.5 * x
    return h + h * jnp.tanh(h)


def _rmsnorm(x, g):
    ms = jnp.mean(x * x, axis=-1, keepdims=True)
    return x * lax.rsqrt(ms + EPS) * g


def _group_mean(t, bd):
    return _mm(t.astype(BF16), bd)


def _head_norm(t, bd, g):
    ms = _group_mean(t * t, bd)
    return t * lax.rsqrt(ms + EPS) * g


def _rope(xc, cos_t, sin_s):
    lane = lax.broadcasted_iota(jnp.int32, xc.shape, 1)
    upper = (lane & HALF) != 0
    rot = jnp.where(upper, pltpu.roll(xc, HALF, 1), pltpu.roll(xc, V7X_LANES - HALF, 1))
    return xc * cos_t + rot * sin_s


def _rope_sign(shape):
    lane = lax.broadcasted_iota(jnp.int32, shape, 1)
    return jnp.where((lane & HALF) != 0, 1.0, -1.0).astype(F32)


def _kv_variants(t):
    lane = lax.broadcasted_iota(jnp.int32, t.shape, 1)
    lo = lane < HEAD_DIM
    sw = pltpu.roll(t, HEAD_DIM, 1)
    zero = jnp.zeros_like(t)
    return (jnp.where(lo, t, zero).astype(BF16), jnp.where(lo, zero, sw).astype(BF16),
            jnp.where(lo, sw, zero).astype(BF16), jnp.where(lo, zero, t).astype(BF16))


def _kv_variants_t(t):
    tb = t.astype(BF16)
    h0, h1 = tb[0:HEAD_DIM], tb[HEAD_DIM:2 * HEAD_DIM]
    zero = jnp.zeros_like(h0)
    return (jnp.concatenate([h0, zero], axis=0), jnp.concatenate([zero, h0], axis=0),
            jnp.concatenate([h1, zero], axis=0), jnp.concatenate([zero, h1], axis=0))


def _scores(qs, ka, kb, bad, keys_on_lanes=False):
    dot = _mm if keys_on_lanes else _mm_t
    s_e = dot(qs, ka)
    s_o = dot(qs, kb)
    if bad is not None:
        s_e = jnp.where(bad, NEG, s_e)
        s_o = jnp.where(bad, NEG, s_o)
    return s_e, s_o


def _softmax_parts(s, sink):
    m = jnp.maximum(jnp.max(s, axis=-1, keepdims=True), sink)
    e = jnp.exp(s - m)
    r = 1.0 / (jnp.sum(e, axis=-1, keepdims=True) + jnp.exp(sink - m))
    return e.astype(BF16), r


def _softmax_parts_inline_sink(s):
    m = jnp.max(s, axis=-1, keepdims=True)
    e = jnp.exp(s - m)
    return e.astype(BF16), 1.0 / jnp.sum(e, axis=-1, keepdims=True)


def _pv(p_e, p_o, va, vb, r_e, r_o, keys_on_lanes=False):
    dot = _mm_t if keys_on_lanes else _mm
    o = dot(p_e, va) + dot(p_o, vb)
    lane = lax.broadcasted_iota(jnp.int32, o.shape, 1)
    return o * jnp.where(lane < HEAD_DIM, r_e, r_o)


def _sink_cols(sink_ref, n_blocks, rows_per_pair, blocks_per_kv=1):
    row = lax.broadcasted_iota(jnp.int32, (n_blocks * 2 * rows_per_pair, 1), 0)
    second_pair = (row // rows_per_pair) % 2 == 1
    second_kv = (row // (2 * rows_per_pair * blocks_per_kv)) % 2 == 1

    def pick(odd):
        kv0 = jnp.where(second_pair, sink_ref[2 + odd], sink_ref[odd])
        kv1 = jnp.where(second_pair, sink_ref[GROUP_HEADS + 2 + odd], sink_ref[GROUP_HEADS + odd])
        return jnp.where(second_kv, kv1, kv0)

    return pick(0), pick(1)


def _stage_chunks(shape, col_order=None):
    rows, cols = shape
    assert rows % STAGE_ROWS == 0 and cols % V7X_LANES == 0
    width = max(c for c in range(V7X_LANES, STAGE_COLS + 1, V7X_LANES) if cols % c == 0)
    blocks = list(range(cols // width)) if col_order is None else list(col_order)
    assert sorted(blocks) == list(range(cols // width))
    return [(r0, b * width, STAGE_ROWS, width) for b in blocks for r0 in range(0, rows, STAGE_ROWS)]


class _WeightStager:
    def __init__(self, chunks, stage, sem):
        self.chunks, self.stage, self.sem, self.done = chunks, stage, sem, 0

    def _copy(self, n):
        src, _, r0, c0, rows, cols = self.chunks[n]
        slot = n % STAGE_SLOTS
        return pltpu.make_async_copy(src.at[pl.ds(r0, rows), pl.ds(c0, cols)],
                                     self.stage.at[slot, pl.ds(0, rows), pl.ds(0, cols)],
                                     self.sem.at[slot])

    def start(self):
        for n in range(min(STAGE_SLOTS, len(self.chunks))):
            self._copy(n).start()

    def pump(self, count):
        while self.done < min(count, len(self.chunks)):
            n = self.done
            _, dst, r0, c0, rows, cols = self.chunks[n]
            self._copy(n).wait()
            dst[r0:r0 + rows, c0:c0 + cols] = (
                self.stage[n % STAGE_SLOTS, 0:rows, 0:cols].astype(BF16))
            if n + STAGE_SLOTS < len(self.chunks):
                self._copy(n + STAGE_SLOTS).start()
            self.done += 1

    def finish(self):
        self.pump(len(self.chunks))


def _qkv_proj(x, lng_ref, win_ref):
    hb = _rmsnorm(x, lng_ref[...]).astype(BF16)
    return hb, _mm(hb, win_ref[:, OFF_Q:OFF_GA])


def _tile_lanes(g, width):
    while g.shape[1] < width:
        g = jnp.concatenate([g, g], axis=1)
    return g


def _qk_norm_rope(qkv, gq_ref, gk_ref, bdq_ref, bdk_ref, cos_t, sin_s):
    half_q = ATTN_DIM // 2
    gq = _tile_lanes(gq_ref[...] * (HEAD_DIM ** -0.5), half_q)
    gk = _tile_lanes(gk_ref[...], KV_DIM)
    q_cols = []
    for j in range(2):
        t = qkv[:, j * half_q:(j + 1) * half_q]
        tn = _head_norm(t, bdq_ref[...], gq)
        for c in range(half_q // V7X_LANES):
            q_cols.append(_rope(tn[:, c * V7X_LANES:(c + 1) * V7X_LANES], cos_t, sin_s).astype(BF16))
    kn = _head_norm(qkv[:, OFF_K:OFF_V], bdk_ref[...], gk)
    kr = _rope(kn, cos_t, sin_s)
    v = qkv[:, OFF_V:OFF_GA]
    return q_cols, kr, v


def _conv_proj(hb, win_ref):
    return _mm(hb, win_ref[:, OFF_B:OFF_GC_END])


def _conv_input(bcug, conv_fn):
    b_gate = bcug[:, 0:CONV_DIM]
    u = bcug[:, CONV_DIM:2 * CONV_DIM] * bcug[:, 2 * CONV_DIM:3 * CONV_DIM]
    gate_c = bcug[:, 3 * CONV_DIM:4 * CONV_DIM]
    conv = conv_fn(u)
    return (b_gate * conv * _silu(gate_c)).astype(BF16)


def _prompt_body(i, stager, x_ref, p_ref, lng_ref, gq_ref, gk_ref, invf_ref, sink_ref, cw_ref,
                 bdq_ref, bdk_ref, win_ref, wa_ref, wb_ref, wo_ref, wpg_ref, wpp_ref,
                 y_ref, ko_ref, vo_ref, co_ref, kbuf, vbuf, ubuf, attn_buf, tab):
    tm, d_model = x_ref.shape
    invf = invf_ref[...]
    first = stager is not None

    if first:
        kbuf[:, :, 0:WINDOW] = jnp.zeros((4, KV_DIM, WINDOW), BF16)
        vbuf[:, 0:WINDOW, :] = jnp.zeros((4, WINDOW, V7X_LANES), BF16)
        ubuf[0:V7X_SUBLANES, :] = jnp.zeros((V7X_SUBLANES, CONV_DIM), F32)
        r = lax.broadcasted_iota(jnp.int32, (tm, V7X_LANES), 0).astype(F32)
        ang = r * invf
        sgn = _rope_sign((tm, V7X_LANES))
        c_r = jnp.cos(ang)
        s_r = jnp.sin(ang)
        tab[0] = c_r
        tab[1] = s_r
        tab[2] = c_r * sgn
        tab[3] = s_r * sgn

    base = (i * tm).astype(F32) * invf
    cb = jnp.cos(base)
    sb = jnp.sin(base)
    cos_t = tab[0] * cb - tab[1] * sb
    sin_s = tab[3] * cb + tab[2] * sb

    sub = PROMPT_SUBTILE
    n_sub = tm // sub
    sub_chunks = sub // CHUNK
    assert 2 * CHUNK == V7X_LANES and WINDOW == V7X_LANES
    n_keys = 2 * V7X_LANES

    def key_window_start(r0, c):
        return ((r0 + c * CHUNK) // V7X_LANES) * V7X_LANES

    def mask_window(sc, r0, c, sink):
        lo, hi = sc[:, 0:V7X_LANES], sc[:, V7X_LANES:n_keys]
        lane = lax.broadcasted_iota(jnp.int32, lo.shape, 1)
        no_carry = first and key_window_start(r0, c) < WINDOW
        if no_carry:
            lo = jnp.full_like(lo, NEG)
        if ((r0 + c * CHUNK) // CHUNK) % 2 == 0:
            hi = jnp.where(lane < CHUNK, hi, jnp.where(lane == V7X_LANES - 1, sink, NEG))
        else:
            lo = jnp.where((lane >= CHUNK) & (not no_carry), lo, jnp.where(lane == 0, sink, NEG))
        return jnp.concatenate([lo, hi], axis=1)

    def drop_sink_lane(p, c):
        lo, hi = p[:, 0:V7X_LANES], p[:, V7X_LANES:n_keys]
        lane = lax.broadcasted_iota(jnp.int32, lo.shape, 1)
        if c % 2 == 0:
            hi = jnp.where(lane == V7X_LANES - 1, jnp.zeros_like(hi), hi)
        else:
            lo = jnp.where(lane == 0, jnp.zeros_like(lo), lo)
        return jnp.concatenate([lo, hi], axis=1)

    pad = V7X_SUBLANES
    blocks = [(c0, vh) for c0 in range(0, sub_chunks, 2) for vh in range(N_KV_HEADS)]
    st = [dict(r0=s * sub) for s in range(n_sub)]

    def stage_rms(s):
        s["hb"] = _rmsnorm(x_ref[s["r0"]:s["r0"] + sub, :], lng_ref[...]).astype(BF16)

    def stage_qkv(s):
        s["qkv"] = _mm(s["hb"], win_ref[:, OFF_Q:OFF_GA])

    def stage_gate_a(s):
        s["sig_a"] = _sigmoid(_mm(s["hb"], win_ref[:, OFF_MA:OFF_MA + d_model]))

    def stage_gate_c(s):
        s["sig_c"] = _sigmoid(_mm(s["hb"], win_ref[:, OFF_MA + d_model:OFF_MA + 2 * d_model]))

    def stage_qk_norm(s):
        r0 = s["r0"]
        s["q_cols"], kr, v = _qk_norm_rope(s.pop("qkv"), gq_ref, gk_ref, bdq_ref, bdk_ref,
                                           cos_t[r0:r0 + sub], sin_s[r0:r0 + sub])
        kr_t = kr.T
        for n, t in enumerate(_kv_variants_t(kr_t)):
            kbuf[n, :, WINDOW + r0:WINDOW + r0 + sub] = t
        for n, t in enumerate(_kv_variants(v)):
            vbuf[n, WINDOW + r0:WINDOW + r0 + sub, :] = t
        if r0 + sub == tm:
            ko_ref[...] = kr_t[:, sub - WINDOW:sub]
            vo_ref[...] = v[sub - WINDOW:sub, :].T

    def stage_scores(s):
        r0 = s["r0"]
        s_e, s_o = [], []
        for c0, vh in blocks:
            start = key_window_start(r0, c0)
            win = slice(start, start + n_keys)
            qs = jnp.concatenate(
                [s["q_cols"][2 * vh + pair][(c0 + dc) * CHUNK:(c0 + dc + 1) * CHUNK]
                 for dc in range(2) for pair in range(2)], axis=0)
            se, so = _scores(qs, kbuf[2 * vh, :, win], kbuf[2 * vh + 1, :, win], None,
                             keys_on_lanes=True)
            for dc in range(2):
                half = slice(dc * 2 * CHUNK, (dc + 1) * 2 * CHUNK)
                s_e.append(mask_window(se[half], r0, c0 + dc, sink_cols[vh][0]))
                s_o.append(mask_window(so[half], r0, c0 + dc, sink_cols[vh][1]))
        s["s_e"] = jnp.concatenate(s_e, axis=0)
        s["s_o"] = jnp.concatenate(s_o, axis=0)
        del s["q_cols"]

    def stage_conv_proj(s):
        s["bcug"] = _conv_proj(s["hb"], win_ref)

    def stage_conv(s):
        r0 = s["r0"]

        def conv_fn(u):
            ubuf[pad + r0:pad + r0 + sub, :] = u
            conv = ubuf[pad + r0 - 2:pad + r0 - 2 + sub, :] * cw_ref[0]
            conv = conv + ubuf[pad + r0 - 1:pad + r0 - 1 + sub, :] * cw_ref[1]
            return conv + u * cw_ref[2]

        s["c_in"] = _conv_input(s.pop("bcug"), conv_fn)

    def stage_conv_out(s):
        s["yc"] = _mm(s.pop("c_in"), wb_ref[...])

    pair_b = lax.broadcasted_iota(jnp.int32, (2 * CHUNK, 1), 0) >= CHUNK
    sink_cols = [[jnp.where(pair_b, sink_ref[GROUP_HEADS * vh + 2 + odd],
                            sink_ref[GROUP_HEADS * vh + odd])
                  for odd in range(2)] for vh in range(N_KV_HEADS)]

    def stage_softmax(s):
        s["p_e"], s["r_e"] = _softmax_parts_inline_sink(s.pop("s_e"))
        s["p_o"], s["r_o"] = _softmax_parts_inline_sink(s.pop("s_o"))

    def stage_pv(s):
        r0 = s["r0"]
        for n, (c0, vh) in enumerate(blocks):
            start = key_window_start(r0, c0)
            win = slice(start, start + n_keys)
            br = slice(n * 4 * CHUNK, (n + 1) * 4 * CHUNK)
            p_e, p_o = (jnp.concatenate(
                [drop_sink_lane(p[(2 * n + dc) * 2 * CHUNK:(2 * n + dc + 1) * 2 * CHUNK], c0 + dc)
                 for dc in range(2)], axis=0) for p in (s["p_e"], s["p_o"]))
            o = _pv(p_e, p_o, vbuf[2 * vh, win, :], vbuf[2 * vh + 1, win, :],
                    s["r_e"][br], s["r_o"][br])
            for dc in range(2):
                rows = slice(r0 + (c0 + dc) * CHUNK, r0 + (c0 + dc + 1) * CHUNK)
                for pair in range(2):
                    part = o[(2 * dc + pair) * CHUNK:(2 * dc + pair + 1) * CHUNK]
                    attn_buf[rows, (2 * vh + pair) * V7X_LANES:(2 * vh + pair + 1) * V7X_LANES] = part

    def stage_attn_gate(s):
        s["silu_ga"] = _silu(_mm(s["hb"], win_ref[:, OFF_GA:OFF_B]))

    def stage_attn_out(s):
        r0 = s["r0"]
        a_in = (attn_buf[r0:r0 + sub, :] * s.pop("silu_ga")).astype(BF16)
        ya = _mm(a_in, wa_ref[...])
        s["mix"] = (s.pop("sig_a") * ya + s.pop("sig_c") * s.pop("yc")).astype(BF16)

    def stage_out_proj(s):
        r0 = s["r0"]
        s["r"] = x_ref[r0:r0 + sub, :] + _mm(s.pop("mix"), wo_ref[...])
        s["pp"] = _mm(p_ref[r0:r0 + sub, :].astype(BF16), wpp_ref[...])

    def stage_ple(s):
        r0 = s["r0"]
        r = s.pop("r")
        gate = _sigmoid(_mm(r.astype(BF16), wpg_ref[...]))
        y_ref[r0:r0 + sub, :] = r + gate * s.pop("pp")

    stages = [stage_rms, stage_qkv, stage_qk_norm, stage_gate_a, stage_scores, stage_gate_c,
              stage_softmax, stage_conv_proj, stage_conv, stage_attn_gate, stage_conv_out,
              stage_pv, stage_attn_out, stage_out_proj, stage_ple]
    needs = {stage_rms: 0, stage_qkv: 4, stage_qk_norm: 8, stage_gate_a: 12, stage_scores: 14,
             stage_gate_c: 16, stage_softmax: 20, stage_conv_proj: 24, stage_conv: 25,
             stage_attn_gate: 25, stage_conv_out: 26, stage_pv: 27, stage_attn_out: 28,
             stage_out_proj: 33, stage_ple: 37}
    for t in range(len(stages) + PROMPT_STAGE_SKEW * (n_sub - 1)):
        for j, s in enumerate(st):
            k = t - j * PROMPT_STAGE_SKEW
            if 0 <= k < len(stages):
                if first and j == 0:
                    stager.pump(needs[stages[k]])
                stages[k](s)

    co_ref[...] = ubuf[pad + tm - 2:pad + tm, :]
    ubuf[0:pad, :] = ubuf[tm:tm + pad, :]
    kbuf[:, :, 0:WINDOW] = kbuf[:, :, tm:tm + WINDOW]
    vbuf[:, 0:WINDOW, :] = vbuf[:, tm:tm + WINDOW, :]


def _sample_body(i, x_ref, p_ref, ck_ref, cv_ref, sc_ref, lng_ref, gq_ref, gk_ref, invf_ref,
                 sink_ref, cw_ref, bdq_ref, bdk_ref, win_ref, wa_ref, wb_ref, wo_ref, wpg_ref,
                 wpp_ref, y_ref, ko_ref, vo_ref, co_ref, ubuf, attn_buf, conv_buf, tab):
    bb, _, cache_len = ck_ref.shape
    rows_total, d_model = x_ref.shape
    t_new = rows_total // bb
    n_keys = cache_len + t_new

    @pl.when(i == 0)
    def _init():
        r = lax.broadcasted_iota(jnp.int32, (rows_total, V7X_LANES), 0)
        pos = (PAST_LEN + lax.rem(r, t_new)).astype(F32)
        ang = pos * invf_ref[...]
        tab[0] = jnp.cos(ang)
        tab[1] = jnp.sin(ang) * _rope_sign((rows_total, V7X_LANES))

    def conv_fn(u):
        pad = V7X_SUBLANES
        for b in range(bb):
            rows = slice(b * t_new, (b + 1) * t_new)
            ub = u[rows]
            ubuf[b, pad - (CONV_WIDTH - 1):pad, :] = sc_ref[b]
            ubuf[b, pad:pad + t_new, :] = ub
            conv = ubuf[b, pad - 2:pad - 2 + t_new, :] * cw_ref[0]
            conv = conv + ubuf[b, pad - 1:pad - 1 + t_new, :] * cw_ref[1]
            conv_buf[rows, :] = conv + ub * cw_ref[2]
            co_ref[b] = ubuf[b, pad + t_new - (CONV_WIDTH - 1):pad + t_new, :]
        return conv_buf[...]

    x = x_ref[...]
    hb, qkv = _qkv_proj(x, lng_ref, win_ref)
    sig_a = _sigmoid(_mm(hb, win_ref[:, OFF_MA:OFF_MA + d_model]))
    q_cols, kr, v = _qk_norm_rope(qkv, gq_ref, gk_ref, bdq_ref, bdk_ref, tab[0], tab[1])

    qi = lax.broadcasted_iota(jnp.int32, (2 * t_new, n_keys), 0)
    q_pos = PAST_LEN + lax.rem(qi, t_new)
    k_pos = PAST_LEN - cache_len + lax.broadcasted_iota(jnp.int32, (2 * t_new, n_keys), 1)
    q_ch = q_pos // CHUNK
    k_ch = k_pos // CHUNK
    bad = jnp.logical_not((k_ch <= q_ch) & (k_ch >= q_ch - WINDOW_CHUNKS))

    blocks = [(b, vh) for b in range(bb) for vh in range(N_KV_HEADS)]
    s_e, s_o, vvars = [], [], []
    kr_t = kr.T
    v_t = v.T
    for b in range(bb):
        rows = slice(b * t_new, (b + 1) * t_new)
        kcat = jnp.concatenate([ck_ref[b], kr_t[:, rows]], axis=1)
        vcat = jnp.concatenate([cv_ref[b], v_t[:, rows]], axis=1)
        ko_ref[b] = kcat[:, n_keys - cache_len:n_keys]
        vo_ref[b] = vcat[:, n_keys - cache_len:n_keys]
        kvar = _kv_variants_t(kcat)
        vvars.append(_kv_variants_t(vcat))
        for vh in range(N_KV_HEADS):
            qs = jnp.concatenate([q_cols[2 * vh][rows], q_cols[2 * vh + 1][rows]], axis=0)
            se, so = _scores(qs, kvar[2 * vh], kvar[2 * vh + 1], bad, keys_on_lanes=True)
            s_e.append(se)
            s_o.append(so)

    sig_c = _sigmoid(_mm(hb, win_ref[:, OFF_MA + d_model:OFF_MA + 2 * d_model]))
    sink_e, sink_o = _sink_cols(sink_ref, len(blocks), t_new)
    p_e, r_e = _softmax_parts(jnp.concatenate(s_e, axis=0), sink_e)
    bcug = _conv_proj(hb, win_ref)
    p_o, r_o = _softmax_parts(jnp.concatenate(s_o, axis=0), sink_o)
    c_in = _conv_input(bcug, conv_fn)
    silu_ga = _silu(_mm(hb, win_ref[:, OFF_GA:OFF_B]))
    pp = _mm(p_ref[...].astype(BF16), wpp_ref[...])
    yc = _mm(c_in, wb_ref[...])

    for n, (b, vh) in enumerate(blocks):
        rows = slice(b * t_new, (b + 1) * t_new)
        br = slice(n * 2 * t_new, (n + 1) * 2 * t_new)
        o = _pv(p_e[br], p_o[br], vvars[b][2 * vh], vvars[b][2 * vh + 1], r_e[br], r_o[br],
                keys_on_lanes=True)
        attn_buf[rows, (2 * vh) * V7X_LANES:(2 * vh + 1) * V7X_LANES] = o[0:t_new]
        attn_buf[rows, (2 * vh + 1) * V7X_LANES:(2 * vh + 2) * V7X_LANES] = o[t_new:2 * t_new]

    ya = _mm((attn_buf[0:rows_total, :] * silu_ga).astype(BF16), wa_ref[...])
    mix = (sig_a * ya + sig_c * yc).astype(BF16)
    r = x + _mm(mix, wo_ref[...])
    gate = _sigmoid(_mm(r.astype(BF16), wpg_ref[...]))
    y_ref[...] = r + gate * pp


def _layer_kernel(n_prompt_steps, *refs):
    (xp_ref, pp_ref, xs_ref, ps_ref, ck_ref, cv_ref, sc_ref,
     lng_ref, gq_ref, gk_ref, invf_ref, sink_ref, cw_ref, bdq_ref, bdk_ref,
     win_hbm, wa_hbm, wb_hbm, wo_hbm, wpg_hbm, wpp_hbm,
     yp_ref, kpo_ref, vpo_ref, cpo_ref, ys_ref, kso_ref, vso_ref, cso_ref,
     kbuf, vbuf, ubuf_p, attn_buf, tab_p, ubuf_s, conv_buf, tab_s,
     win_ref, wa_ref, wb_ref, wo_ref, wpg_ref, wpp_ref, stage, sem) = refs
    i = pl.program_id(0)
    small = (lng_ref, gq_ref, gk_ref, invf_ref, sink_ref, cw_ref, bdq_ref, bdk_ref)
    weights = (win_ref, wa_ref, wb_ref, wo_ref, wpg_ref, wpp_ref)

    prompt_refs = (xp_ref, pp_ref, *small, *weights, yp_ref, kpo_ref, vpo_ref, cpo_ref,
                   kbuf, vbuf, ubuf_p, attn_buf, tab_p)

    @pl.when(i == 0)
    def _first_prompt():
        chunks = [(src, dst) + ch for src, dst, order in (
            (win_hbm, win_ref, _WIN_BLOCK_ORDER), (wb_hbm, wb_ref, None), (wa_hbm, wa_ref, None),
            (wo_hbm, wo_ref, None), (wpp_hbm, wpp_ref, None), (wpg_hbm, wpg_ref, None))
            for ch in _stage_chunks(src.shape, order)]
        stager = _WeightStager(chunks, stage, sem)
        stager.start()
        _prompt_body(i, stager, *prompt_refs)
        stager.finish()

    @pl.when((i > 0) & (i < n_prompt_steps))
    def _prompt():
        _prompt_body(i, None, *prompt_refs)

    @pl.when(i >= n_prompt_steps)
    def _sample():
        _sample_body(i - n_prompt_steps, xs_ref, ps_ref, ck_ref, cv_ref, sc_ref, *small, *weights,
                     ys_ref, kso_ref, vso_ref, cso_ref, ubuf_s, attn_buf, conv_buf, tab_s)


def _const_spec(shape):
    nd = len(shape)
    return pl.BlockSpec(shape, lambda i: (0,) * nd, pipeline_mode=pl.Buffered(1))


def _smem_spec():
    return pl.BlockSpec(memory_space=pltpu.SMEM)


def _block_diag_mean(width):
    idx = np.arange(width) // HEAD_DIM
    return jnp.asarray((idx[:, None] == idx[None, :]).astype(np.float32) / HEAD_DIM, dtype=BF16)


def _layer_consts(ln_g, w_in, q_norm_g, k_norm_g, sink, conv_w, w_attn_out, w_conv_out, w_o,
                  w_ple_gate, w_ple_proj):
    inv_freq = ROPE_THETA ** (-jnp.arange(0, HALF, dtype=F32) * 2.0 / HEAD_DIM)
    return dict(
        lng=ln_g.reshape(1, -1).astype(F32),
        gq=q_norm_g.astype(F32).reshape(1, HEAD_DIM),
        gk=k_norm_g.astype(F32).reshape(1, HEAD_DIM),
        invf=jnp.tile(inv_freq, V7X_LANES // HALF).reshape(1, V7X_LANES),
        sink=sink.astype(F32),
        cw=conv_w.astype(F32),
        bdq=_block_diag_mean(ATTN_DIM // 2),
        bdk=_block_diag_mean(KV_DIM),
        win=w_in.astype(F32),
        wa=w_attn_out.astype(F32),
        wb=w_conv_out.astype(F32),
        wo=w_o.astype(F32),
        wpg=w_ple_gate.astype(F32),
        wpp=w_ple_proj.astype(F32),
    )


_SMALL_KEYS = ("lng", "gq", "gk", "invf", "sink", "cw", "bdq", "bdk")
_STAGED_KEYS = ("win", "wa", "wb", "wo", "wpg", "wpp")


def _heads_first(t):
    t = jnp.moveaxis(t, -3, -1)
    return t.reshape(t.shape[:-3] + (KV_DIM, t.shape[-1]))


def _heads_last(t):
    t = t.reshape(t.shape[:-2] + (N_KV_HEADS, HEAD_DIM, t.shape[-1]))
    return jnp.moveaxis(t, -1, -3)


def _layer(xp, pp, xs, ps, cache_k, cache_v, state_conv, c):
    t, d = xp.shape
    nb, t_new, _ = xs.shape
    cache_len = cache_k.shape[1]
    tm, bb = PROMPT_TILE, SAMPLE_BATCH_TILE
    assert t % tm == 0 and tm % PROMPT_SUBTILE == 0
    assert PROMPT_SUBTILE % CHUNK == 0 and PROMPT_SUBTILE >= WINDOW
    assert nb % bb == 0 and CONV_WIDTH - 1 <= t_new <= cache_len
    n_p, n_s = t // tm, nb // bb
    rows = bb * t_new
    xs2 = xs.reshape(nb * t_new, d)
    ps2 = ps.reshape(nb * t_new, ps.shape[-1])
    ck = _heads_first(cache_k)
    cv = _heads_first(cache_v)

    def p_idx(i):
        return jnp.minimum(i, n_p - 1)

    def s_idx(i):
        return jnp.maximum(i - n_p, 0)

    in_specs = ([pl.BlockSpec((tm, d), lambda i: (p_idx(i), 0)),
                 pl.BlockSpec((tm, pp.shape[1]), lambda i: (p_idx(i), 0)),
                 pl.BlockSpec((rows, d), lambda i: (s_idx(i), 0)),
                 pl.BlockSpec((rows, ps2.shape[1]), lambda i: (s_idx(i), 0)),
                 pl.BlockSpec((bb, KV_DIM, cache_len), lambda i: (s_idx(i), 0, 0)),
                 pl.BlockSpec((bb, KV_DIM, cache_len), lambda i: (s_idx(i), 0, 0)),
                 pl.BlockSpec((bb, CONV_WIDTH - 1, CONV_DIM), lambda i: (s_idx(i), 0, 0))]
                + [_smem_spec() if k == "sink" else _const_spec(c[k].shape) for k in _SMALL_KEYS]
                + [pl.BlockSpec(memory_space=pl.ANY) for _ in _STAGED_KEYS])
    out_shape = (jax.ShapeDtypeStruct((t, d), F32),
                 jax.ShapeDtypeStruct((KV_DIM, WINDOW), F32),
                 jax.ShapeDtypeStruct((KV_DIM, WINDOW), F32),
                 jax.ShapeDtypeStruct((CONV_WIDTH - 1, CONV_DIM), F32),
                 jax.ShapeDtypeStruct((nb * t_new, d), F32),
                 jax.ShapeDtypeStruct((nb, KV_DIM, cache_len), F32),
                 jax.ShapeDtypeStruct((nb, KV_DIM, cache_len), F32),
                 jax.ShapeDtypeStruct((nb, CONV_WIDTH - 1, CONV_DIM), F32))
    out_specs = (pl.BlockSpec((tm, d), lambda i: (p_idx(i), 0)),
                 pl.BlockSpec((KV_DIM, WINDOW), lambda i: (0, 0)),
                 pl.BlockSpec((KV_DIM, WINDOW), lambda i: (0, 0)),
                 pl.BlockSpec((CONV_WIDTH - 1, CONV_DIM), lambda i: (0, 0)),
                 pl.BlockSpec((rows, d), lambda i: (s_idx(i), 0)),
                 pl.BlockSpec((bb, KV_DIM, cache_len), lambda i: (s_idx(i), 0, 0)),
                 pl.BlockSpec((bb, KV_DIM, cache_len), lambda i: (s_idx(i), 0, 0)),
                 pl.BlockSpec((bb, CONV_WIDTH - 1, CONV_DIM), lambda i: (s_idx(i), 0, 0)))
    scratch = ([pltpu.VMEM((4, KV_DIM, WINDOW + tm), BF16),
                pltpu.VMEM((4, WINDOW + tm, V7X_LANES), BF16),
                pltpu.VMEM((V7X_SUBLANES + tm, CONV_DIM), F32),
                pltpu.VMEM((max(tm, rows), ATTN_DIM), F32),
                pltpu.VMEM((4, tm, V7X_LANES), F32),
                pltpu.VMEM((bb, V7X_SUBLANES + t_new, CONV_DIM), F32),
                pltpu.VMEM((rows, CONV_DIM), F32),
                pltpu.VMEM((2, rows, V7X_LANES), F32)]
               + [pltpu.VMEM(c[k].shape, BF16) for k in _STAGED_KEYS]
               + [pltpu.VMEM((STAGE_SLOTS, STAGE_ROWS, STAGE_COLS), F32),
                  pltpu.SemaphoreType.DMA((STAGE_SLOTS,))])
    yp, kpo, vpo, cpo, ys, kso, vso, cso = pl.pallas_call(
        functools.partial(_layer_kernel, n_p),
        grid=(n_p + n_s,),
        in_specs=in_specs,
        out_specs=out_specs,
        out_shape=out_shape,
        scratch_shapes=scratch,
        compiler_params=pltpu.CompilerParams(dimension_semantics=("arbitrary",),
                                             vmem_limit_bytes=V7X_VMEM_LIMIT_BYTES),
        name="hybrid_layer",
    )(xp, pp, xs2, ps2, ck, cv, state_conv, *[c[k] for k in _SMALL_KEYS],
      *[c[k] for k in _STAGED_KEYS])
    return ((yp, _heads_last(kpo), _heads_last(vpo), cpo),
            (ys.reshape(nb, t_new, d), _heads_last(kso), _heads_last(vso), cso))


def kernel(x_prompt, x_sample, p_prompt, p_sample, cache_k, cache_v, state_conv, ln_g, w_in,
           q_norm_g, k_norm_g, sink, conv_w, w_attn_out, w_conv_out, w_o, w_ple_gate, w_ple_proj):
    depth = ln_g.shape[0]
    assert x_prompt.shape[0] == 1, "one prompt sequence per call"
    hp, hs = x_prompt[0], x_sample
    kp_l, vp_l, cp_l, ks_l, vs_l, cs_l = [], [], [], [], [], []
    for i in range(depth):
        c = _layer_consts(ln_g[i], w_in[i], q_norm_g[i], k_norm_g[i], sink[i],
                          jnp.swapaxes(conv_w, 0, 1)[:, i:i + 1, :],
                          w_attn_out[i], w_conv_out[i], w_o[i], w_ple_gate[i], w_ple_proj[i])
        (hp, kp, vp, cp), (hs, ks, vs, cs) = _layer(hp, p_prompt[i, 0], hs, p_sample[i], cache_k[i],
                                                    cache_v[i], state_conv[i], c)
        kp_l.append(kp[None])
        vp_l.append(vp[None])
        cp_l.append(cp[None])
        ks_l.append(ks)
        vs_l.append(vs)
        cs_l.append(cs)
    return (hp[None], hs, jnp.stack(kp_l), jnp.stack(vp_l), jnp.stack(cp_l),
            jnp.stack(ks_l), jnp.stack(vs_l), jnp.stack(cs_l))
```

```python
import functools

import numpy as np
import jax
import jax.numpy as jnp
from jax import lax
from jax.experimental import pallas as pl
from jax.experimental.pallas import tpu as pltpu

F32 = jnp.float32
BF16 = jnp.bfloat16

CHUNK = 64
WINDOW = 128
WINDOW_CHUNKS = WINDOW // CHUNK
N_HEADS = 8
N_KV_HEADS = 2
GROUP_HEADS = N_HEADS // N_KV_HEADS
HEAD_DIM = 64
HALF = HEAD_DIM // 2
ATTN_DIM = N_HEADS * HEAD_DIM
KV_DIM = N_KV_HEADS * HEAD_DIM
CONV_DIM = 512
CONV_WIDTH = 3
PAST_LEN = 1024
ROPE_THETA = 10000.0
EPS = 1e-6
NEG = -1e30

OFF_Q = 0
OFF_K = OFF_Q + ATTN_DIM
OFF_V = OFF_K + KV_DIM
OFF_GA = OFF_V + KV_DIM
OFF_B = OFF_GA + ATTN_DIM
OFF_GC_END = OFF_B + 4 * CONV_DIM
OFF_MA = OFF_GC_END

V7X_LANES = 128
V7X_SUBLANES = 8
V7X_VMEM_LIMIT_BYTES = 59 * 1024 * 1024

PROMPT_TILE = 512
PROMPT_SUBTILE = 256
PROMPT_STAGE_SKEW = 1
STAGE_ROWS = 256
STAGE_COLS = 1024
STAGE_SLOTS = 4
_WIN_BLOCK_ORDER = (0, 3, 4, 5, 1, 2)
SAMPLE_BATCH_TILE = 8


def _mm(a, w):
    return jnp.dot(a, w, preferred_element_type=F32)


def _mm_t(a, b):
    return lax.dot_general(a, b, (((1,), (1,)), ((), ())), preferred_element_type=F32)


def _sigmoid(x):
    return 0.5 * jnp.tanh(0.5 * x) + 0.5


def _silu(x):
    return x * _sigmoid(x)


def _gated_add(r, gate_logits, pp):
    hp = 0.5 * pp
    return (r + hp) + hp * jnp.tanh(0.5 * gate_logits)


def _rmsnorm(x, g):
    ms = jnp.mean(x * x, axis=-1, keepdims=True)
    return x * lax.rsqrt(ms + EPS) * g


def _group_mean(t, bd):
    return _mm(t.astype(BF16), bd)


def _head_norm(t, bd, g):
    ms = _group_mean(t * t, bd)
    return t * lax.rsqrt(ms + EPS) * g


def _rope(xc, cos_t, sin_s):
    lane = lax.broadcasted_iota(jnp.int32, xc.shape, 1)
    upper = (lane & HALF) != 0
    rot = jnp.where(upper, pltpu.roll(xc, HALF, 1), pltpu.roll(xc, V7X_LANES - HALF, 1))
    return xc * cos_t + rot * sin_s


def _rope_sign(shape):
    lane = lax.broadcasted_iota(jnp.int32, shape, 1)
    return jnp.where((lane & HALF) != 0, 1.0, -1.0).astype(F32)


def _kv_variants(t):
    lane = lax.broadcasted_iota(jnp.int32, t.shape, 1)
    lo = lane < HEAD_DIM
    sw = pltpu.roll(t, HEAD_DIM, 1)
    zero = jnp.zeros_like(t)
    return (jnp.where(lo, t, zero).astype(BF16), jnp.where(lo, zero, sw).astype(BF16),
            jnp.where(lo, sw, zero).astype(BF16), jnp.where(lo, zero, t).astype(BF16))


def _kv_variants_t(t):
    tb = t.astype(BF16)
    h0, h1 = tb[0:HEAD_DIM], tb[HEAD_DIM:2 * HEAD_DIM]
    zero = jnp.zeros_like(h0)
    return (jnp.concatenate([h0, zero], axis=0), jnp.concatenate([zero, h0], axis=0),
            jnp.concatenate([h1, zero], axis=0), jnp.concatenate([zero, h1], axis=0))


def _scores(qs, ka, kb, bad, keys_on_lanes=False):
    dot = _mm if keys_on_lanes else _mm_t
    s_e = dot(qs, ka)
    s_o = dot(qs, kb)
    if bad is not None:
        s_e = jnp.where(bad, NEG, s_e)
        s_o = jnp.where(bad, NEG, s_o)
    return s_e, s_o


def _softmax_parts(s, sink):
    m = jnp.maximum(jnp.max(s, axis=-1, keepdims=True), sink)
    e = jnp.exp(s - m)
    r = 1.0 / (jnp.sum(e, axis=-1, keepdims=True) + jnp.exp(sink - m))
    return e.astype(BF16), r


def _softmax_parts_inline_sink(s):
    m = jnp.max(s, axis=-1, keepdims=True)
    e = jnp.exp(s - m)
    return e.astype(BF16), 1.0 / jnp.sum(e, axis=-1, keepdims=True)


def _pv(p_e, p_o, va, vb, r_e, r_o, keys_on_lanes=False):
    dot = _mm_t if keys_on_lanes else _mm
    o = dot(p_e, va) + dot(p_o, vb)
    lane = lax.broadcasted_iota(jnp.int32, o.shape, 1)
    return o * jnp.where(lane < HEAD_DIM, r_e, r_o)


def _sink_cols(sink_ref, n_blocks, rows_per_pair, blocks_per_kv=1):
    row = lax.broadcasted_iota(jnp.int32, (n_blocks * 2 * rows_per_pair, 1), 0)
    second_pair = (row // rows_per_pair) % 2 == 1
    second_kv = (row // (2 * rows_per_pair * blocks_per_kv)) % 2 == 1

    def pick(odd):
        kv0 = jnp.where(second_pair, sink_ref[2 + odd], sink_ref[odd])
        kv1 = jnp.where(second_pair, sink_ref[GROUP_HEADS + 2 + odd], sink_ref[GROUP_HEADS + odd])
        return jnp.where(second_kv, kv1, kv0)

    return pick(0), pick(1)


def _stage_chunks(shape, col_order=None):
    rows, cols = shape
    assert rows % STAGE_ROWS == 0 and cols % V7X_LANES == 0
    width = max(c for c in range(V7X_LANES, STAGE_COLS + 1, V7X_LANES) if cols % c == 0)
    blocks = list(range(cols // width)) if col_order is None else list(col_order)
    assert sorted(blocks) == list(range(cols // width))
    return [(r0, b * width, STAGE_ROWS, width) for b in blocks for r0 in range(0, rows, STAGE_ROWS)]


class _WeightStager:
    def __init__(self, chunks, stage, sem):
        self.chunks, self.stage, self.sem, self.done = chunks, stage, sem, 0

    def _copy(self, n):
        src, _, r0, c0, rows, cols = self.chunks[n]
        slot = n % STAGE_SLOTS
        return pltpu.make_async_copy(src.at[pl.ds(r0, rows), pl.ds(c0, cols)],
                                     self.stage.at[slot, pl.ds(0, rows), pl.ds(0, cols)],
                                     self.sem.at[slot])

    def start(self):
        for n in range(min(STAGE_SLOTS, len(self.chunks))):
            self._copy(n).start()

    def pump(self, count):
        while self.done < min(count, len(self.chunks)):
            n = self.done
            _, dst, r0, c0, rows, cols = self.chunks[n]
            self._copy(n).wait()
            dst[r0:r0 + rows, c0:c0 + cols] = (
                self.stage[n % STAGE_SLOTS, 0:rows, 0:cols].astype(BF16))
            if n + STAGE_SLOTS < len(self.chunks):
                self._copy(n + STAGE_SLOTS).start()
            self.done += 1

    def finish(self):
        self.pump(len(self.chunks))


def _qkv_proj(x, lng_ref, win_ref):
    hb = _rmsnorm(x, lng_ref[...]).astype(BF16)
    return hb, _mm(hb, win_ref[:, OFF_Q:OFF_GA])


def _tile_lanes(g, width):
    while g.shape[1] < width:
        g = jnp.concatenate([g, g], axis=1)
    return g


def _qk_norm_rope(qkv, gq_ref, gk_ref, bdq_ref, bdk_ref, cos_t, sin_s):
    half_q = ATTN_DIM // 2
    gq = _tile_lanes(gq_ref[...] * (HEAD_DIM ** -0.5), half_q)
    gk = _tile_lanes(gk_ref[...], KV_DIM)
    q_cols = []
    for j in range(2):
        t = qkv[:, j * half_q:(j + 1) * half_q]
        tn = _head_norm(t, bdq_ref[...], gq)
        for c in range(half_q // V7X_LANES):
            q_cols.append(_rope(tn[:, c * V7X_LANES:(c + 1) * V7X_LANES], cos_t, sin_s).astype(BF16))
    kn = _head_norm(qkv[:, OFF_K:OFF_V], bdk_ref[...], gk)
    kr = _rope(kn, cos_t, sin_s)
    v = qkv[:, OFF_V:OFF_GA]
    return q_cols, kr, v


def _conv_proj(hb, win_ref):
    return _mm(hb, win_ref[:, OFF_B:OFF_GC_END])


def _conv_input(bcug, conv_fn):
    b_gate = bcug[:, 0:CONV_DIM]
    u = bcug[:, CONV_DIM:2 * CONV_DIM] * bcug[:, 2 * CONV_DIM:3 * CONV_DIM]
    gate_c = bcug[:, 3 * CONV_DIM:4 * CONV_DIM]
    conv = conv_fn(u)
    return (b_gate * conv * _silu(gate_c)).astype(BF16)


def _prompt_body(i, stager, x_ref, p_ref, lng_ref, gq_ref, gk_ref, invf_ref, sink_ref, cw_ref,
                 bdq_ref, bdk_ref, win_ref, wa_ref, wb_ref, wo_ref, wpg_ref, wpp_ref,
                 y_ref, ko_ref, vo_ref, co_ref, kbuf, vbuf, ubuf, attn_buf, tab):
    tm, d_model = x_ref.shape
    invf = invf_ref[...]
    first = stager is not None

    if first:
        kbuf[:, :, 0:WINDOW] = jnp.zeros((4, KV_DIM, WINDOW), BF16)
        vbuf[:, 0:WINDOW, :] = jnp.zeros((4, WINDOW, V7X_LANES), BF16)
        ubuf[0:V7X_SUBLANES, :] = jnp.zeros((V7X_SUBLANES, CONV_DIM), F32)
        r = lax.broadcasted_iota(jnp.int32, (tm, V7X_LANES), 0).astype(F32)
        ang = r * invf
        sgn = _rope_sign((tm, V7X_LANES))
        c_r = jnp.cos(ang)
        s_r = jnp.sin(ang)
        tab[0] = c_r
        tab[1] = s_r
        tab[2] = c_r * sgn
        tab[3] = s_r * sgn

    base = (i * tm).astype(F32) * invf
    cb = jnp.cos(base)
    sb = jnp.sin(base)
    cos_t = tab[0] * cb - tab[1] * sb
    sin_s = tab[3] * cb + tab[2] * sb

    sub = PROMPT_SUBTILE
    n_sub = tm // sub
    sub_chunks = sub // CHUNK
    assert 2 * CHUNK == V7X_LANES and WINDOW == V7X_LANES
    n_keys = 2 * V7X_LANES

    def key_window_start(r0, c):
        return ((r0 + c * CHUNK) // V7X_LANES) * V7X_LANES

    def mask_window(sc, r0, c, sink):
        lo, hi = sc[:, 0:V7X_LANES], sc[:, V7X_LANES:n_keys]
        lane = lax.broadcasted_iota(jnp.int32, lo.shape, 1)
        no_carry = first and key_window_start(r0, c) < WINDOW
        if no_carry:
            lo = jnp.full_like(lo, NEG)
        if ((r0 + c * CHUNK) // CHUNK) % 2 == 0:
            hi = jnp.where(lane < CHUNK, hi, jnp.where(lane == V7X_LANES - 1, sink, NEG))
        else:
            lo = jnp.where((lane >= CHUNK) & (not no_carry), lo, jnp.where(lane == 0, sink, NEG))
        return jnp.concatenate([lo, hi], axis=1)

    def drop_sink_lane(p, c):
        lo, hi = p[:, 0:V7X_LANES], p[:, V7X_LANES:n_keys]
        lane = lax.broadcasted_iota(jnp.int32, lo.shape, 1)
        if c % 2 == 0:
            hi = jnp.where(lane == V7X_LANES - 1, jnp.zeros_like(hi), hi)
        else:
            lo = jnp.where(lane == 0, jnp.zeros_like(lo), lo)
        return jnp.concatenate([lo, hi], axis=1)

    pad = V7X_SUBLANES
    blocks = [(c0, vh) for c0 in range(0, sub_chunks, 2) for vh in range(N_KV_HEADS)]
    st = [dict(r0=s * sub) for s in range(n_sub)]

    def stage_rms(s):
        s["hb"] = _rmsnorm(x_ref[s["r0"]:s["r0"] + sub, :], lng_ref[...]).astype(BF16)

    def stage_qkv(s):
        s["qkv"] = _mm(s["hb"], win_ref[:, OFF_Q:OFF_GA])

    def stage_gate_a(s):
        s["sig_a"] = _sigmoid(_mm(s["hb"], win_ref[:, OFF_MA:OFF_MA + d_model]))

    def stage_gate_c(s):
        s["sig_c"] = _sigmoid(_mm(s["hb"], win_ref[:, OFF_MA + d_model:OFF_MA + 2 * d_model]))

    def stage_qk_norm(s):
        r0 = s["r0"]
        s["q_cols"], kr, v = _qk_norm_rope(s.pop("qkv"), gq_ref, gk_ref, bdq_ref, bdk_ref,
                                           cos_t[r0:r0 + sub], sin_s[r0:r0 + sub])
        kr_t = kr.T
        for n, t in enumerate(_kv_variants_t(kr_t)):
            kbuf[n, :, WINDOW + r0:WINDOW + r0 + sub] = t
        for n, t in enumerate(_kv_variants(v)):
            vbuf[n, WINDOW + r0:WINDOW + r0 + sub, :] = t
        if r0 + sub == tm:
            ko_ref[...] = kr_t[:, sub - WINDOW:sub]
            vo_ref[...] = v[sub - WINDOW:sub, :].T

    def stage_scores(s):
        r0 = s["r0"]
        s_e, s_o = [], []
        for c0, vh in blocks:
            start = key_window_start(r0, c0)
            win = slice(start, start + n_keys)
            qs = jnp.concatenate(
                [s["q_cols"][2 * vh + pair][(c0 + dc) * CHUNK:(c0 + dc + 1) * CHUNK]
                 for dc in range(2) for pair in range(2)], axis=0)
            se, so = _scores(qs, kbuf[2 * vh, :, win], kbuf[2 * vh + 1, :, win], None,
                             keys_on_lanes=True)
            for dc in range(2):
                half = slice(dc * 2 * CHUNK, (dc + 1) * 2 * CHUNK)
                blk = slice(len(s_e) * 2 * CHUNK, (len(s_e) + 1) * 2 * CHUNK)
                s_e.append(mask_window(se[half], r0, c0 + dc, sink_e[blk]))
                s_o.append(mask_window(so[half], r0, c0 + dc, sink_o[blk]))
        s["s_e"] = jnp.concatenate(s_e, axis=0)
        s["s_o"] = jnp.concatenate(s_o, axis=0)
        del s["q_cols"]

    def stage_conv_proj(s):
        s["bcug"] = _conv_proj(s["hb"], win_ref)

    def stage_conv(s):
        r0 = s["r0"]

        def conv_fn(u):
            ubuf[pad + r0:pad + r0 + sub, :] = u
            conv = ubuf[pad + r0 - 2:pad + r0 - 2 + sub, :] * cw_ref[0]
            conv = conv + ubuf[pad + r0 - 1:pad + r0 - 1 + sub, :] * cw_ref[1]
            return conv + u * cw_ref[2]

        s["c_in"] = _conv_input(s.pop("bcug"), conv_fn)

    def stage_conv_out(s):
        s["yc"] = _mm(s.pop("c_in"), wb_ref[...])

    sink_e, sink_o = _sink_cols(sink_ref, 2 * len(blocks), CHUNK, blocks_per_kv=2)

    def stage_softmax(s):
        s["p_e"], s["r_e"] = _softmax_parts_inline_sink(s.pop("s_e"))
        s["p_o"], s["r_o"] = _softmax_parts_inline_sink(s.pop("s_o"))

    def stage_pv(s):
        r0 = s["r0"]
        for n, (c0, vh) in enumerate(blocks):
            start = key_window_start(r0, c0)
            win = slice(start, start + n_keys)
            br = slice(n * 4 * CHUNK, (n + 1) * 4 * CHUNK)
            p_e, p_o = (jnp.concatenate(
                [drop_sink_lane(p[(2 * n + dc) * 2 * CHUNK:(2 * n + dc + 1) * 2 * CHUNK], c0 + dc)
                 for dc in range(2)], axis=0) for p in (s["p_e"], s["p_o"]))
            o = _pv(p_e, p_o, vbuf[2 * vh, win, :], vbuf[2 * vh + 1, win, :],
                    s["r_e"][br], s["r_o"][br])
            for dc in range(2):
                rows = slice(r0 + (c0 + dc) * CHUNK, r0 + (c0 + dc + 1) * CHUNK)
                for pair in range(2):
                    part = o[(2 * dc + pair) * CHUNK:(2 * dc + pair + 1) * CHUNK]
                    attn_buf[rows, (2 * vh + pair) * V7X_LANES:(2 * vh + pair + 1) * V7X_LANES] = part

    def stage_attn_gate(s):
        s["silu_ga"] = _silu(_mm(s["hb"], win_ref[:, OFF_GA:OFF_B]))

    def stage_attn_out(s):
        r0 = s["r0"]
        a_in = (attn_buf[r0:r0 + sub, :] * s.pop("silu_ga")).astype(BF16)
        ya = _mm(a_in, wa_ref[...])
        s["mix"] = (s.pop("sig_a") * ya + s.pop("sig_c") * s.pop("yc")).astype(BF16)

    def stage_out_proj(s):
        r0 = s["r0"]
        s["r"] = x_ref[r0:r0 + sub, :] + _mm(s.pop("mix"), wo_ref[...])
        s["pp"] = _mm(p_ref[r0:r0 + sub, :].astype(BF16), wpp_ref[...])

    def stage_ple(s):
        r0 = s["r0"]
        r = s.pop("r")
        y_ref[r0:r0 + sub, :] = _gated_add(r, _mm(r.astype(BF16), wpg_ref[...]), s.pop("pp"))

    stages = [stage_rms, stage_qkv, stage_qk_norm, stage_gate_a, stage_scores, stage_gate_c,
              stage_softmax, stage_conv_proj, stage_conv, stage_attn_gate, stage_conv_out,
              stage_pv, stage_attn_out, stage_out_proj, stage_ple]
    needs = {stage_rms: 0, stage_qkv: 4, stage_qk_norm: 8, stage_gate_a: 12, stage_scores: 14,
             stage_gate_c: 16, stage_softmax: 20, stage_conv_proj: 24, stage_conv: 25,
             stage_attn_gate: 25, stage_conv_out: 26, stage_pv: 27, stage_attn_out: 28,
             stage_out_proj: 33, stage_ple: 37}
    for t in range(len(stages) + PROMPT_STAGE_SKEW * (n_sub - 1)):
        for j, s in enumerate(st):
            k = t - j * PROMPT_STAGE_SKEW
            if 0 <= k < len(stages):
                if first and j == 0:
                    stager.pump(needs[stages[k]])
                stages[k](s)

    co_ref[...] = ubuf[pad + tm - 2:pad + tm, :]
    ubuf[0:pad, :] = ubuf[tm:tm + pad, :]
    kbuf[:, :, 0:WINDOW] = kbuf[:, :, tm:tm + WINDOW]
    vbuf[:, 0:WINDOW, :] = vbuf[:, tm:tm + WINDOW, :]


def _sample_body(i, x_ref, p_ref, ck_ref, cv_ref, sc_ref, lng_ref, gq_ref, gk_ref, invf_ref,
                 sink_ref, cw_ref, bdq_ref, bdk_ref, win_ref, wa_ref, wb_ref, wo_ref, wpg_ref,
                 wpp_ref, y_ref, ko_ref, vo_ref, co_ref, ubuf, attn_buf, conv_buf, tab):
    bb, _, cache_len = ck_ref.shape
    rows_total, d_model = x_ref.shape
    t_new = rows_total // bb
    n_keys = cache_len + t_new

    @pl.when(i == 0)
    def _init():
        r = lax.broadcasted_iota(jnp.int32, (rows_total, V7X_LANES), 0)
        pos = (PAST_LEN + lax.rem(r, t_new)).astype(F32)
        ang = pos * invf_ref[...]
        tab[0] = jnp.cos(ang)
        tab[1] = jnp.sin(ang) * _rope_sign((rows_total, V7X_LANES))

    def conv_fn(u):
        pad = V7X_SUBLANES
        for b in range(bb):
            rows = slice(b * t_new, (b + 1) * t_new)
            ub = u[rows]
            ubuf[b, pad - (CONV_WIDTH - 1):pad, :] = sc_ref[b]
            ubuf[b, pad:pad + t_new, :] = ub
            conv = ubuf[b, pad - 2:pad - 2 + t_new, :] * cw_ref[0]
            conv = conv + ubuf[b, pad - 1:pad - 1 + t_new, :] * cw_ref[1]
            conv_buf[rows, :] = conv + ub * cw_ref[2]
            co_ref[b] = ubuf[b, pad + t_new - (CONV_WIDTH - 1):pad + t_new, :]
        return conv_buf[...]

    x = x_ref[...]
    hb, qkv = _qkv_proj(x, lng_ref, win_ref)
    sig_a = _sigmoid(_mm(hb, win_ref[:, OFF_MA:OFF_MA + d_model]))
    q_cols, kr, v = _qk_norm_rope(qkv, gq_ref, gk_ref, bdq_ref, bdk_ref, tab[0], tab[1])

    qi = lax.broadcasted_iota(jnp.int32, (2 * t_new, n_keys), 0)
    q_pos = PAST_LEN + lax.rem(qi, t_new)
    k_pos = PAST_LEN - cache_len + lax.broadcasted_iota(jnp.int32, (2 * t_new, n_keys), 1)
    q_ch = q_pos // CHUNK
    k_ch = k_pos // CHUNK
    bad = jnp.logical_not((k_ch <= q_ch) & (k_ch >= q_ch - WINDOW_CHUNKS))

    blocks = [(b, vh) for b in range(bb) for vh in range(N_KV_HEADS)]
    s_e, s_o, vvars = [], [], []
    kr_t = kr.T
    v_t = v.T
    for b in range(bb):
        rows = slice(b * t_new, (b + 1) * t_new)
        kcat = jnp.concatenate([ck_ref[b], kr_t[:, rows]], axis=1)
        vcat = jnp.concatenate([cv_ref[b], v_t[:, rows]], axis=1)
        ko_ref[b] = kcat[:, n_keys - cache_len:n_keys]
        vo_ref[b] = vcat[:, n_keys - cache_len:n_keys]
        kvar = _kv_variants_t(kcat)
        vvars.append(_kv_variants_t(vcat))
        for vh in range(N_KV_HEADS):
            qs = jnp.concatenate([q_cols[2 * vh][rows], q_cols[2 * vh + 1][rows]], axis=0)
            se, so = _scores(qs, kvar[2 * vh], kvar[2 * vh + 1], bad, keys_on_lanes=True)
            s_e.append(se)
            s_o.append(so)

    sig_c = _sigmoid(_mm(hb, win_ref[:, OFF_MA + d_model:OFF_MA + 2 * d_model]))
    sink_e, sink_o = _sink_cols(sink_ref, len(blocks), t_new)
    p_e, r_e = _softmax_parts(jnp.concatenate(s_e, axis=0), sink_e)
    bcug = _conv_proj(hb, win_ref)
    p_o, r_o = _softmax_parts(jnp.concatenate(s_o, axis=0), sink_o)
    c_in = _conv_input(bcug, conv_fn)
    silu_ga = _silu(_mm(hb, win_ref[:, OFF_GA:OFF_B]))
    pp = _mm(p_ref[...].astype(BF16), wpp_ref[...])
    yc = _mm(c_in, wb_ref[...])

    for n, (b, vh) in enumerate(blocks):
        rows = slice(b * t_new, (b + 1) * t_new)
        br = slice(n * 2 * t_new, (n + 1) * 2 * t_new)
        o = _pv(p_e[br], p_o[br], vvars[b][2 * vh], vvars[b][2 * vh + 1], r_e[br], r_o[br],
                keys_on_lanes=True)
        attn_buf[rows, (2 * vh) * V7X_LANES:(2 * vh + 1) * V7X_LANES] = o[0:t_new]
        attn_buf[rows, (2 * vh + 1) * V7X_LANES:(2 * vh + 2) * V7X_LANES] = o[t_new:2 * t_new]

    ya = _mm((attn_buf[0:rows_total, :] * silu_ga).astype(BF16), wa_ref[...])
    mix = (sig_a * ya + sig_c * yc).astype(BF16)
    r = x + _mm(mix, wo_ref[...])
    y_ref[...] = _gated_add(r, _mm(r.astype(BF16), wpg_ref[...]), pp)


def _layer_kernel(n_prompt_steps, *refs):
    (xp_ref, pp_ref, xs_ref, ps_ref, ck_ref, cv_ref, sc_ref,
     lng_ref, gq_ref, gk_ref, invf_ref, sink_ref, cw_ref, bdq_ref, bdk_ref,
     win_hbm, wa_hbm, wb_hbm, wo_hbm, wpg_hbm, wpp_hbm,
     yp_ref, kpo_ref, vpo_ref, cpo_ref, ys_ref, kso_ref, vso_ref, cso_ref,
     kbuf, vbuf, ubuf_p, attn_buf, tab_p, ubuf_s, conv_buf, tab_s,
     win_ref, wa_ref, wb_ref, wo_ref, wpg_ref, wpp_ref, stage, sem) = refs
    i = pl.program_id(0)
    small = (lng_ref, gq_ref, gk_ref, invf_ref, sink_ref, cw_ref, bdq_ref, bdk_ref)
    weights = (win_ref, wa_ref, wb_ref, wo_ref, wpg_ref, wpp_ref)

    prompt_refs = (xp_ref, pp_ref, *small, *weights, yp_ref, kpo_ref, vpo_ref, cpo_ref,
                   kbuf, vbuf, ubuf_p, attn_buf, tab_p)

    @pl.when(i == 0)
    def _first_prompt():
        chunks = [(src, dst) + ch for src, dst, order in (
            (win_hbm, win_ref, _WIN_BLOCK_ORDER), (wb_hbm, wb_ref, None), (wa_hbm, wa_ref, None),
            (wo_hbm, wo_ref, None), (wpp_hbm, wpp_ref, None), (wpg_hbm, wpg_ref, None))
            for ch in _stage_chunks(src.shape, order)]
        stager = _WeightStager(chunks, stage, sem)
        stager.start()
        _prompt_body(i, stager, *prompt_refs)
        stager.finish()

    @pl.when((i > 0) & (i < n_prompt_steps))
    def _prompt():
        _prompt_body(i, None, *prompt_refs)

    @pl.when(i >= n_prompt_steps)
    def _sample():
        _sample_body(i - n_prompt_steps, xs_ref, ps_ref, ck_ref, cv_ref, sc_ref, *small, *weights,
                     ys_ref, kso_ref, vso_ref, cso_ref, ubuf_s, attn_buf, conv_buf, tab_s)


def _const_spec(shape):
    nd = len(shape)
    return pl.BlockSpec(shape, lambda i: (0,) * nd, pipeline_mode=pl.Buffered(1))


def _smem_spec():
    return pl.BlockSpec(memory_space=pltpu.SMEM)


def _block_diag_mean(width):
    idx = np.arange(width) // HEAD_DIM
    return jnp.asarray((idx[:, None] == idx[None, :]).astype(np.float32) / HEAD_DIM, dtype=BF16)


def _layer_consts(ln_g, w_in, q_norm_g, k_norm_g, sink, conv_w, w_attn_out, w_conv_out, w_o,
                  w_ple_gate, w_ple_proj):
    inv_freq = ROPE_THETA ** (-jnp.arange(0, HALF, dtype=F32) * 2.0 / HEAD_DIM)
    return dict(
        lng=ln_g.reshape(1, -1).astype(F32),
        gq=q_norm_g.astype(F32).reshape(1, HEAD_DIM),
        gk=k_norm_g.astype(F32).reshape(1, HEAD_DIM),
        invf=jnp.tile(inv_freq, V7X_LANES // HALF).reshape(1, V7X_LANES),
        sink=sink.astype(F32),
        cw=conv_w.astype(F32),
        bdq=_block_diag_mean(ATTN_DIM // 2),
        bdk=_block_diag_mean(KV_DIM),
        win=w_in.astype(F32),
        wa=w_attn_out.astype(F32),
        wb=w_conv_out.astype(F32),
        wo=w_o.astype(F32),
        wpg=w_ple_gate.astype(F32),
        wpp=w_ple_proj.astype(F32),
    )


_SMALL_KEYS = ("lng", "gq", "gk", "invf", "sink", "cw", "bdq", "bdk")
_STAGED_KEYS = ("win", "wa", "wb", "wo", "wpg", "wpp")


def _heads_first(t):
    t = jnp.moveaxis(t, -3, -1)
    return t.reshape(t.shape[:-3] + (KV_DIM, t.shape[-1]))


def _heads_last(t):
    t = t.reshape(t.shape[:-2] + (N_KV_HEADS, HEAD_DIM, t.shape[-1]))
    return jnp.moveaxis(t, -1, -3)


def _layer(xp, pp, xs, ps, cache_k, cache_v, state_conv, c):
    t, d = xp.shape
    nb, t_new, _ = xs.shape
    cache_len = cache_k.shape[1]
    tm, bb = PROMPT_TILE, SAMPLE_BATCH_TILE
    assert t % tm == 0 and tm % PROMPT_SUBTILE == 0
    assert PROMPT_SUBTILE % CHUNK == 0 and PROMPT_SUBTILE >= WINDOW
    assert nb % bb == 0 and CONV_WIDTH - 1 <= t_new <= cache_len
    n_p, n_s = t // tm, nb // bb
    rows = bb * t_new
    xs2 = xs.reshape(nb * t_new, d)
    ps2 = ps.reshape(nb * t_new, ps.shape[-1])
    ck = _heads_first(cache_k)
    cv = _heads_first(cache_v)

    def p_idx(i):
        return jnp.minimum(i, n_p - 1)

    def s_idx(i):
        return jnp.maximum(i - n_p, 0)

    in_specs = ([pl.BlockSpec((tm, d), lambda i: (p_idx(i), 0)),
                 pl.BlockSpec((tm, pp.shape[1]), lambda i: (p_idx(i), 0)),
                 pl.BlockSpec((rows, d), lambda i: (s_idx(i), 0)),
                 pl.BlockSpec((rows, ps2.shape[1]), lambda i: (s_idx(i), 0)),
                 pl.BlockSpec((bb, KV_DIM, cache_len), lambda i: (s_idx(i), 0, 0)),
                 pl.BlockSpec((bb, KV_DIM, cache_len), lambda i: (s_idx(i), 0, 0)),
                 pl.BlockSpec((bb, CONV_WIDTH - 1, CONV_DIM), lambda i: (s_idx(i), 0, 0))]
                + [_smem_spec() if k == "sink" else _const_spec(c[k].shape) for k in _SMALL_KEYS]
                + [pl.BlockSpec(memory_space=pl.ANY) for _ in _STAGED_KEYS])
    out_shape = (jax.ShapeDtypeStruct((t, d), F32),
                 jax.ShapeDtypeStruct((KV_DIM, WINDOW), F32),
                 jax.ShapeDtypeStruct((KV_DIM, WINDOW), F32),
                 jax.ShapeDtypeStruct((CONV_WIDTH - 1, CONV_DIM), F32),
                 jax.ShapeDtypeStruct((nb * t_new, d), F32),
                 jax.ShapeDtypeStruct((nb, KV_DIM, cache_len), F32),
                 jax.ShapeDtypeStruct((nb, KV_DIM, cache_len), F32),
                 jax.ShapeDtypeStruct((nb, CONV_WIDTH - 1, CONV_DIM), F32))
    out_specs = (pl.BlockSpec((tm, d), lambda i: (p_idx(i), 0)),
                 pl.BlockSpec((KV_DIM, WINDOW), lambda i: (0, 0)),
                 pl.BlockSpec((KV_DIM, WINDOW), lambda i: (0, 0)),
                 pl.BlockSpec((CONV_WIDTH - 1, CONV_DIM), lambda i: (0, 0)),
                 pl.BlockSpec((rows, d), lambda i: (s_idx(i), 0)),
                 pl.BlockSpec((bb, KV_DIM, cache_len), lambda i: (s_idx(i), 0, 0)),
                 pl.BlockSpec((bb, KV_DIM, cache_len), lambda i: (s_idx(i), 0, 0)),
                 pl.BlockSpec((bb, CONV_WIDTH - 1, CONV_DIM), lambda i: (s_idx(i), 0, 0)))
    scratch = ([pltpu.VMEM((4, KV_DIM, WINDOW + tm), BF16),
                pltpu.VMEM((4, WINDOW + tm, V7X_LANES), BF16),
                pltpu.VMEM((V7X_SUBLANES + tm, CONV_DIM), F32),
                pltpu.VMEM((max(tm, rows), ATTN_DIM), F32),
                pltpu.VMEM((4, tm, V7X_LANES), F32),
                pltpu.VMEM((bb, V7X_SUBLANES + t_new, CONV_DIM), F32),
                pltpu.VMEM((rows, CONV_DIM), F32),
                pltpu.VMEM((2, rows, V7X_LANES), F32)]
               + [pltpu.VMEM(c[k].shape, BF16) for k in _STAGED_KEYS]
               + [pltpu.VMEM((STAGE_SLOTS, STAGE_ROWS, STAGE_COLS), F32),
                  pltpu.SemaphoreType.DMA((STAGE_SLOTS,))])
    yp, kpo, vpo, cpo, ys, kso, vso, cso = pl.pallas_call(
        functools.partial(_layer_kernel, n_p),
        grid=(n_p + n_s,),
        in_specs=in_specs,
        out_specs=out_specs,
        out_shape=out_shape,
        scratch_shapes=scratch,
        compiler_params=pltpu.CompilerParams(dimension_semantics=("arbitrary",),
                                             vmem_limit_bytes=V7X_VMEM_LIMIT_BYTES),
        name="hybrid_layer",
    )(xp, pp, xs2, ps2, ck, cv, state_conv, *[c[k] for k in _SMALL_KEYS],
      *[c[k] for k in _STAGED_KEYS])
    return ((yp, _heads_last(kpo), _heads_last(vpo), cpo),
            (ys.reshape(nb, t_new, d), _heads_last(kso), _heads_last(vso), cso))


def kernel(x_prompt, x_sample, p_prompt, p_sample, cache_k, cache_v, state_conv, ln_g, w_in,
           q_norm_g, k_norm_g, sink, conv_w, w_attn_out, w_conv_out, w_o, w_ple_gate, w_ple_proj):
    depth = ln_g.shape[0]
    assert x_prompt.shape[0] == 1, "one prompt sequence per call"
    hp, hs = x_prompt[0], x_sample
    kp_l, vp_l, cp_l, ks_l, vs_l, cs_l = [], [], [], [], [], []
    for i in range(depth):
        c = _layer_consts(ln_g[i], w_in[i], q_norm_g[i], k_norm_g[i], sink[i],
                          jnp.swapaxes(conv_w, 0, 1)[:, i:i + 1, :],
                          w_attn_out[i], w_conv_out[i], w_o[i], w_ple_gate[i], w_ple_proj[i])
        (hp, kp, vp, cp), (hs, ks, vs, cs) = _layer(hp, p_prompt[i, 0], hs, p_sample[i], cache_k[i],
                                                    cache_v[i], state_conv[i], c)
        kp_l.append(kp[None])
        vp_l.append(vp[None])
        cp_l.append(cp[None])
        ks_l.append(ks)
        vs_l.append(vs)
        cs_l.append(cs)
    return (hp[None], hs, jnp.stack(kp_l), jnp.stack(vp_l), jnp.stack(cp_l),
            jnp.stack(ks_l), jnp.stack(vs_l), jnp.stack(cs_l))
```

```python
import functools

import numpy as np
import jax
import jax.numpy as jnp
from jax import lax
from jax.experimental import pallas as pl
from jax.experimental.pallas import tpu as pltpu

F32 = jnp.float32
BF16 = jnp.bfloat16

CHUNK = 64
WINDOW = 128
WINDOW_CHUNKS = WINDOW // CHUNK
N_HEADS = 8
N_KV_HEADS = 2
GROUP_HEADS = N_HEADS // N_KV_HEADS
HEAD_DIM = 64
HALF = HEAD_DIM // 2
ATTN_DIM = N_HEADS * HEAD_DIM
KV_DIM = N_KV_HEADS * HEAD_DIM
CONV_DIM = 512
CONV_WIDTH = 3
PAST_LEN = 1024
ROPE_THETA = 10000.0
EPS = 1e-6
NEG = -1e30

OFF_Q = 0
OFF_K = OFF_Q + ATTN_DIM
OFF_V = OFF_K + KV_DIM
OFF_GA = OFF_V + KV_DIM
OFF_B = OFF_GA + ATTN_DIM
OFF_GC_END = OFF_B + 4 * CONV_DIM
OFF_MA = OFF_GC_END

V7X_LANES = 128
V7X_SUBLANES = 8
V7X_VMEM_LIMIT_BYTES = 59 * 1024 * 1024

PROMPT_TILE = 512
PROMPT_SUBTILE = 256
PROMPT_STAGE_SKEW = 1
STAGE_ROWS = 256
STAGE_COLS = 1024
STAGE_SLOTS = 4
_WIN_BLOCK_ORDER = (0, 3, 4, 5, 1, 2)
SAMPLE_BATCH_TILE = 8


def _mm(a, w):
    return jnp.dot(a, w, preferred_element_type=F32)


def _mm_t(a, b):
    return lax.dot_general(a, b, (((1,), (1,)), ((), ())), preferred_element_type=F32)


def _sigmoid(x):
    return 0.5 * jnp.tanh(0.5 * x) + 0.5


def _silu(x):
    return x * _sigmoid(x)


def _rmsnorm(x, g):
    ms = jnp.mean(x * x, axis=-1, keepdims=True)
    return x * lax.rsqrt(ms + EPS) * g


def _group_mean(t, bd):
    return _mm(t.astype(BF16), bd)


def _head_norm(t, bd, g):
    ms = _group_mean(t * t, bd)
    return t * lax.rsqrt(ms + EPS) * g


def _rope(xc, cos_t, sin_s):
    lane = lax.broadcasted_iota(jnp.int32, xc.shape, 1)
    upper = (lane & HALF) != 0
    rot = jnp.where(upper, pltpu.roll(xc, HALF, 1), pltpu.roll(xc, V7X_LANES - HALF, 1))
    return xc * cos_t + rot * sin_s


def _rope_sign(shape):
    lane = lax.broadcasted_iota(jnp.int32, shape, 1)
    return jnp.where((lane & HALF) != 0, 1.0, -1.0).astype(F32)


def _kv_variants(t):
    lane = lax.broadcasted_iota(jnp.int32, t.shape, 1)
    lo = lane < HEAD_DIM
    sw = pltpu.roll(t, HEAD_DIM, 1)
    zero = jnp.zeros_like(t)
    return (jnp.where(lo, t, zero).astype(BF16), jnp.where(lo, zero, sw).astype(BF16),
            jnp.where(lo, sw, zero).astype(BF16), jnp.where(lo, zero, t).astype(BF16))


def _kv_variants_t(t):
    tb = t.astype(BF16)
    h0, h1 = tb[0:HEAD_DIM], tb[HEAD_DIM:2 * HEAD_DIM]
    zero = jnp.zeros_like(h0)
    return (jnp.concatenate([h0, zero], axis=0), jnp.concatenate([zero, h0], axis=0),
            jnp.concatenate([h1, zero], axis=0), jnp.concatenate([zero, h1], axis=0))


def _scores(qs, ka, kb, bad, keys_on_lanes=False):
    dot = _mm if keys_on_lanes else _mm_t
    s_e = dot(qs, ka)
    s_o = dot(qs, kb)
    if bad is not None:
        s_e = jnp.where(bad, NEG, s_e)
        s_o = jnp.where(bad, NEG, s_o)
    return s_e, s_o


def _softmax_parts(s, sink):
    m = jnp.maximum(jnp.max(s, axis=-1, keepdims=True), sink)
    e = jnp.exp(s - m)
    r = 1.0 / (jnp.sum(e, axis=-1, keepdims=True) + jnp.exp(sink - m))
    return e.astype(BF16), r


def _softmax_parts_inline_sink(s):
    m = jnp.max(s, axis=-1, keepdims=True)
    e = jnp.exp(s - m)
    return e.astype(BF16), 1.0 / jnp.sum(e, axis=-1, keepdims=True)


def _pv(p_e, p_o, va, vb, r_e, r_o, keys_on_lanes=False):
    dot = _mm_t if keys_on_lanes else _mm
    o = dot(p_e, va) + dot(p_o, vb)
    lane = lax.broadcasted_iota(jnp.int32, o.shape, 1)
    return o * jnp.where(lane < HEAD_DIM, r_e, r_o)


def _sink_cols(sink_ref, n_blocks, rows_per_pair, blocks_per_kv=1):
    row = lax.broadcasted_iota(jnp.int32, (n_blocks * 2 * rows_per_pair, 1), 0)
    second_pair = (row // rows_per_pair) % 2 == 1
    second_kv = (row // (2 * rows_per_pair * blocks_per_kv)) % 2 == 1

    def pick(odd):
        kv0 = jnp.where(second_pair, sink_ref[2 + odd], sink_ref[odd])
        kv1 = jnp.where(second_pair, sink_ref[GROUP_HEADS + 2 + odd], sink_ref[GROUP_HEADS + odd])
        return jnp.where(second_kv, kv1, kv0)

    return pick(0), pick(1)


def _stage_chunks(shape, col_order=None):
    rows, cols = shape
    assert rows % STAGE_ROWS == 0 and cols % V7X_LANES == 0
    width = max(c for c in range(V7X_LANES, STAGE_COLS + 1, V7X_LANES) if cols % c == 0)
    blocks = list(range(cols // width)) if col_order is None else list(col_order)
    assert sorted(blocks) == list(range(cols // width))
    return [(r0, b * width, STAGE_ROWS, width) for b in blocks for r0 in range(0, rows, STAGE_ROWS)]


class _WeightStager:
    def __init__(self, chunks, stage, sem):
        self.chunks, self.stage, self.sem, self.done = chunks, stage, sem, 0

    def _copy(self, n):
        src, _, r0, c0, rows, cols = self.chunks[n]
        slot = n % STAGE_SLOTS
        return pltpu.make_async_copy(src.at[pl.ds(r0, rows), pl.ds(c0, cols)],
                                     self.stage.at[slot, pl.ds(0, rows), pl.ds(0, cols)],
                                     self.sem.at[slot])

    def start(self):
        for n in range(min(STAGE_SLOTS, len(self.chunks))):
            self._copy(n).start()

    def pump(self, count):
        while self.done < min(count, len(self.chunks)):
            n = self.done
            _, dst, r0, c0, rows, cols = self.chunks[n]
            self._copy(n).wait()
            dst[r0:r0 + rows, c0:c0 + cols] = (
                self.stage[n % STAGE_SLOTS, 0:rows, 0:cols].astype(BF16))
            if n + STAGE_SLOTS < len(self.chunks):
                self._copy(n + STAGE_SLOTS).start()
            self.done += 1

    def finish(self):
        self.pump(len(self.chunks))


def _qkv_proj(x, lng_ref, win_ref):
    hb = _rmsnorm(x, lng_ref[...]).astype(BF16)
    return hb, _mm(hb, win_ref[:, OFF_Q:OFF_GA])


def _tile_lanes(g, width):
    while g.shape[1] < width:
        g = jnp.concatenate([g, g], axis=1)
    return g


def _qk_norm_rope(qkv, gq_ref, gk_ref, bdq_ref, bdk_ref, cos_t, sin_s):
    half_q = ATTN_DIM // 2
    gq = _tile_lanes(gq_ref[...] * (HEAD_DIM ** -0.5), half_q)
    gk = _tile_lanes(gk_ref[...], KV_DIM)
    q_cols = []
    for j in range(2):
        t = qkv[:, j * half_q:(j + 1) * half_q]
        tn = _head_norm(t, bdq_ref[...], gq)
        for c in range(half_q // V7X_LANES):
            q_cols.append(_rope(tn[:, c * V7X_LANES:(c + 1) * V7X_LANES], cos_t, sin_s).astype(BF16))
    kn = _head_norm(qkv[:, OFF_K:OFF_V], bdk_ref[...], gk)
    kr = _rope(kn, cos_t, sin_s)
    v = qkv[:, OFF_V:OFF_GA]
    return q_cols, kr, v


def _conv_proj(hb, win_ref):
    return _mm(hb, win_ref[:, OFF_B:OFF_GC_END])


def _conv_input(bcug, conv_fn):
    b_gate = bcug[:, 0:CONV_DIM]
    u = bcug[:, CONV_DIM:2 * CONV_DIM] * bcug[:, 2 * CONV_DIM:3 * CONV_DIM]
    gate_c = bcug[:, 3 * CONV_DIM:4 * CONV_DIM]
    conv = conv_fn(u)
    return (b_gate * conv * _silu(gate_c)).astype(BF16)


def _prompt_body(i, stager, x_ref, p_ref, lng_ref, gq_ref, gk_ref, invf_ref, sink_ref, cw_ref,
                 bdq_ref, bdk_ref, win_ref, wa_ref, wb_ref, wo_ref, wpg_ref, wpp_ref,
                 y_ref, ko_ref, vo_ref, co_ref, kbuf, vbuf, ubuf, attn_buf, tab):
    tm, d_model = x_ref.shape
    invf = invf_ref[...]
    first = stager is not None

    if first:
        kbuf[:, :, 0:WINDOW] = jnp.zeros((4, KV_DIM, WINDOW), BF16)
        vbuf[:, 0:WINDOW, :] = jnp.zeros((4, WINDOW, V7X_LANES), BF16)
        ubuf[0:V7X_SUBLANES, :] = jnp.zeros((V7X_SUBLANES, CONV_DIM), F32)
        r = lax.broadcasted_iota(jnp.int32, (tm, V7X_LANES), 0).astype(F32)
        ang = r * invf
        sgn = _rope_sign((tm, V7X_LANES))
        c_r = jnp.cos(ang)
        s_r = jnp.sin(ang)
        tab[0] = c_r
        tab[1] = s_r
        tab[2] = c_r * sgn
        tab[3] = s_r * sgn

    base = (i * tm).astype(F32) * invf
    cb = jnp.cos(base)
    sb = jnp.sin(base)
    cos_t = tab[0] * cb - tab[1] * sb
    sin_s = tab[3] * cb + tab[2] * sb

    sub = PROMPT_SUBTILE
    n_sub = tm // sub
    sub_chunks = sub // CHUNK
    assert 2 * CHUNK == V7X_LANES and WINDOW == V7X_LANES
    n_keys = 2 * V7X_LANES

    def key_window_start(r0, c):
        return ((r0 + c * CHUNK) // V7X_LANES) * V7X_LANES

    def mask_window(sc, r0, c, sink):
        lo, hi = sc[:, 0:V7X_LANES], sc[:, V7X_LANES:n_keys]
        lane = lax.broadcasted_iota(jnp.int32, lo.shape, 1)
        no_carry = first and key_window_start(r0, c) < WINDOW
        if no_carry:
            lo = jnp.full_like(lo, NEG)
        if ((r0 + c * CHUNK) // CHUNK) % 2 == 0:
            hi = jnp.where(lane < CHUNK, hi, jnp.where(lane == V7X_LANES - 1, sink, NEG))
        else:
            lo = jnp.where((lane >= CHUNK) & (not no_carry), lo, jnp.where(lane == 0, sink, NEG))
        return jnp.concatenate([lo, hi], axis=1)

    def drop_sink_lane(p, c):
        lo, hi = p[:, 0:V7X_LANES], p[:, V7X_LANES:n_keys]
        lane = lax.broadcasted_iota(jnp.int32, lo.shape, 1)
        if c % 2 == 0:
            hi = jnp.where(lane == V7X_LANES - 1, jnp.zeros_like(hi), hi)
        else:
            lo = jnp.where(lane == 0, jnp.zeros_like(lo), lo)
        return jnp.concatenate([lo, hi], axis=1)

    pad = V7X_SUBLANES
    blocks = [(c0, vh) for c0 in range(0, sub_chunks, 2) for vh in range(N_KV_HEADS)]
    st = [dict(r0=s * sub) for s in range(n_sub)]

    def stage_rms(s):
        s["hb"] = _rmsnorm(x_ref[s["r0"]:s["r0"] + sub, :], lng_ref[...]).astype(BF16)

    def stage_qkv(s):
        s["qkv"] = _mm(s["hb"], win_ref[:, OFF_Q:OFF_GA])

    def gate_halves(s, off):
        half = d_model // 2
        return jnp.concatenate(
            [_sigmoid(_mm(s["hb"], win_ref[:, off + h * half:off + (h + 1) * half])) for h in range(2)],
            axis=1)

    def stage_gate_a(s):
        s["sig_a"] = gate_halves(s, OFF_MA)

    def stage_gate_c(s):
        s["sig_c"] = gate_halves(s, OFF_MA + d_model)

    def stage_qk_norm(s):
        r0 = s["r0"]
        s["q_cols"], kr, v = _qk_norm_rope(s.pop("qkv"), gq_ref, gk_ref, bdq_ref, bdk_ref,
                                           cos_t[r0:r0 + sub], sin_s[r0:r0 + sub])
        kr_t = kr.T
        for n, t in enumerate(_kv_variants_t(kr_t)):
            kbuf[n, :, WINDOW + r0:WINDOW + r0 + sub] = t
        for n, t in enumerate(_kv_variants(v)):
            vbuf[n, WINDOW + r0:WINDOW + r0 + sub, :] = t
        if r0 + sub == tm:
            ko_ref[...] = kr_t[:, sub - WINDOW:sub]
            vo_ref[...] = v[sub - WINDOW:sub, :].T

    def stage_scores(s):
        r0 = s["r0"]
        s_e, s_o = [], []
        for c0, vh in blocks:
            start = key_window_start(r0, c0)
            win = slice(start, start + n_keys)
            qs = jnp.concatenate(
                [s["q_cols"][2 * vh + pair][(c0 + dc) * CHUNK:(c0 + dc + 1) * CHUNK]
                 for dc in range(2) for pair in range(2)], axis=0)
            se, so = _scores(qs, kbuf[2 * vh, :, win], kbuf[2 * vh + 1, :, win], None,
                             keys_on_lanes=True)
            for dc in range(2):
                half = slice(dc * 2 * CHUNK, (dc + 1) * 2 * CHUNK)
                blk = slice(len(s_e) * 2 * CHUNK, (len(s_e) + 1) * 2 * CHUNK)
                s_e.append(mask_window(se[half], r0, c0 + dc, sink_e[blk]))
                s_o.append(mask_window(so[half], r0, c0 + dc, sink_o[blk]))
        s["s_e"] = jnp.concatenate(s_e, axis=0)
        s["s_o"] = jnp.concatenate(s_o, axis=0)
        del s["q_cols"]

    def stage_conv_proj(s):
        s["bcug"] = _conv_proj(s["hb"], win_ref)

    def stage_conv(s):
        r0 = s["r0"]

        def conv_fn(u):
            ubuf[pad + r0:pad + r0 + sub, :] = u
            conv = ubuf[pad + r0 - 2:pad + r0 - 2 + sub, :] * cw_ref[0]
            conv = conv + ubuf[pad + r0 - 1:pad + r0 - 1 + sub, :] * cw_ref[1]
            return conv + u * cw_ref[2]

        s["c_in"] = _conv_input(s.pop("bcug"), conv_fn)

    def stage_conv_out(s):
        s["yc"] = _mm(s.pop("c_in"), wb_ref[...])

    sink_e, sink_o = _sink_cols(sink_ref, 2 * len(blocks), CHUNK, blocks_per_kv=2)

    def stage_softmax(s):
        s["p_e"], s["r_e"] = _softmax_parts_inline_sink(s.pop("s_e"))
        s["p_o"], s["r_o"] = _softmax_parts_inline_sink(s.pop("s_o"))

    def stage_pv(s):
        r0 = s["r0"]
        for n, (c0, vh) in enumerate(blocks):
            start = key_window_start(r0, c0)
            win = slice(start, start + n_keys)
            br = slice(n * 4 * CHUNK, (n + 1) * 4 * CHUNK)
            p_e, p_o = (jnp.concatenate(
                [drop_sink_lane(p[(2 * n + dc) * 2 * CHUNK:(2 * n + dc + 1) * 2 * CHUNK], c0 + dc)
                 for dc in range(2)], axis=0) for p in (s["p_e"], s["p_o"]))
            o = _pv(p_e, p_o, vbuf[2 * vh, win, :], vbuf[2 * vh + 1, win, :],
                    s["r_e"][br], s["r_o"][br])
            for dc in range(2):
                rows = slice(r0 + (c0 + dc) * CHUNK, r0 + (c0 + dc + 1) * CHUNK)
                for pair in range(2):
                    part = o[(2 * dc + pair) * CHUNK:(2 * dc + pair + 1) * CHUNK]
                    attn_buf[rows, (2 * vh + pair) * V7X_LANES:(2 * vh + pair + 1) * V7X_LANES] = part

    def stage_attn_gate(s):
        s["silu_ga"] = _silu(_mm(s["hb"], win_ref[:, OFF_GA:OFF_B]))

    def stage_attn_out(s):
        r0 = s["r0"]
        a_in = (attn_buf[r0:r0 + sub, :] * s.pop("silu_ga")).astype(BF16)
        ya = _mm(a_in, wa_ref[...])
        s["mix"] = (s.pop("sig_a") * ya + s.pop("sig_c") * s.pop("yc")).astype(BF16)

    def stage_out_proj(s):
        r0 = s["r0"]
        s["r"] = x_ref[r0:r0 + sub, :] + _mm(s.pop("mix"), wo_ref[...])
        s["pp"] = _mm(p_ref[r0:r0 + sub, :].astype(BF16), wpp_ref[...])

    def stage_ple(s):
        r0 = s["r0"]
        r = s.pop("r")
        gate = _sigmoid(_mm(r.astype(BF16), wpg_ref[...]))
        y_ref[r0:r0 + sub, :] = r + gate * s.pop("pp")

    stages = [stage_rms, stage_qkv, stage_qk_norm, stage_gate_a, stage_scores, stage_gate_c,
              stage_softmax, stage_conv_proj, stage_conv, stage_attn_gate, stage_conv_out,
              stage_pv, stage_attn_out, stage_out_proj, stage_ple]
    needs = {stage_rms: 0, stage_qkv: 4, stage_qk_norm: 8, stage_gate_a: 12, stage_scores: 14,
             stage_gate_c: 16, stage_softmax: 20, stage_conv_proj: 24, stage_conv: 25,
             stage_attn_gate: 25, stage_conv_out: 26, stage_pv: 27, stage_attn_out: 28,
             stage_out_proj: 33, stage_ple: 37}
    for t in range(len(stages) + PROMPT_STAGE_SKEW * (n_sub - 1)):
        for j, s in enumerate(st):
            k = t - j * PROMPT_STAGE_SKEW
            if 0 <= k < len(stages):
                if first and j == 0:
                    stager.pump(needs[stages[k]])
                stages[k](s)

    co_ref[...] = ubuf[pad + tm - 2:pad + tm, :]
    ubuf[0:pad, :] = ubuf[tm:tm + pad, :]
    kbuf[:, :, 0:WINDOW] = kbuf[:, :, tm:tm + WINDOW]
    vbuf[:, 0:WINDOW, :] = vbuf[:, tm:tm + WINDOW, :]


def _sample_body(i, x_ref, p_ref, ck_ref, cv_ref, sc_ref, lng_ref, gq_ref, gk_ref, invf_ref,
                 sink_ref, cw_ref, bdq_ref, bdk_ref, win_ref, wa_ref, wb_ref, wo_ref, wpg_ref,
                 wpp_ref, y_ref, ko_ref, vo_ref, co_ref, ubuf, attn_buf, conv_buf, tab):
    bb, _, cache_len = ck_ref.shape
    rows_total, d_model = x_ref.shape
    t_new = rows_total // bb
    n_keys = cache_len + t_new

    @pl.when(i == 0)
    def _init():
        r = lax.broadcasted_iota(jnp.int32, (rows_total, V7X_LANES), 0)
        pos = (PAST_LEN + lax.rem(r, t_new)).astype(F32)
        ang = pos * invf_ref[...]
        tab[0] = jnp.cos(ang)
        tab[1] = jnp.sin(ang) * _rope_sign((rows_total, V7X_LANES))

    def conv_fn(u):
        pad = V7X_SUBLANES
        for b in range(bb):
            rows = slice(b * t_new, (b + 1) * t_new)
            ub = u[rows]
            ubuf[b, pad - (CONV_WIDTH - 1):pad, :] = sc_ref[b]
            ubuf[b, pad:pad + t_new, :] = ub
            conv = ubuf[b, pad - 2:pad - 2 + t_new, :] * cw_ref[0]
            conv = conv + ubuf[b, pad - 1:pad - 1 + t_new, :] * cw_ref[1]
            conv_buf[rows, :] = conv + ub * cw_ref[2]
            co_ref[b] = ubuf[b, pad + t_new - (CONV_WIDTH - 1):pad + t_new, :]
        return conv_buf[...]

    x = x_ref[...]
    hb, qkv = _qkv_proj(x, lng_ref, win_ref)
    sig_a = _sigmoid(_mm(hb, win_ref[:, OFF_MA:OFF_MA + d_model]))
    q_cols, kr, v = _qk_norm_rope(qkv, gq_ref, gk_ref, bdq_ref, bdk_ref, tab[0], tab[1])

    qi = lax.broadcasted_iota(jnp.int32, (2 * t_new, n_keys), 0)
    q_pos = PAST_LEN + lax.rem(qi, t_new)
    k_pos = PAST_LEN - cache_len + lax.broadcasted_iota(jnp.int32, (2 * t_new, n_keys), 1)
    q_ch = q_pos // CHUNK
    k_ch = k_pos // CHUNK
    bad = jnp.logical_not((k_ch <= q_ch) & (k_ch >= q_ch - WINDOW_CHUNKS))

    blocks = [(b, vh) for b in range(bb) for vh in range(N_KV_HEADS)]
    s_e, s_o, vvars = [], [], []
    kr_t = kr.T
    v_t = v.T
    for b in range(bb):
        rows = slice(b * t_new, (b + 1) * t_new)
        kcat = jnp.concatenate([ck_ref[b], kr_t[:, rows]], axis=1)
        vcat = jnp.concatenate([cv_ref[b], v_t[:, rows]], axis=1)
        ko_ref[b] = kcat[:, n_keys - cache_len:n_keys]
        vo_ref[b] = vcat[:, n_keys - cache_len:n_keys]
        kvar = _kv_variants_t(kcat)
        vvars.append(_kv_variants_t(vcat))
        for vh in range(N_KV_HEADS):
            qs = jnp.concatenate([q_cols[2 * vh][rows], q_cols[2 * vh + 1][rows]], axis=0)
            se, so = _scores(qs, kvar[2 * vh], kvar[2 * vh + 1], bad, keys_on_lanes=True)
            s_e.append(se)
            s_o.append(so)

    sig_c = _sigmoid(_mm(hb, win_ref[:, OFF_MA + d_model:OFF_MA + 2 * d_model]))
    sink_e, sink_o = _sink_cols(sink_ref, len(blocks), t_new)
    p_e, r_e = _softmax_parts(jnp.concatenate(s_e, axis=0), sink_e)
    bcug = _conv_proj(hb, win_ref)
    p_o, r_o = _softmax_parts(jnp.concatenate(s_o, axis=0), sink_o)
    c_in = _conv_input(bcug, conv_fn)
    silu_ga = _silu(_mm(hb, win_ref[:, OFF_GA:OFF_B]))
    pp = _mm(p_ref[...].astype(BF16), wpp_ref[...])
    yc = _mm(c_in, wb_ref[...])

    for n, (b, vh) in enumerate(blocks):
        rows = slice(b * t_new, (b + 1) * t_new)
        br = slice(n * 2 * t_new, (n + 1) * 2 * t_new)
        o = _pv(p_e[br], p_o[br], vvars[b][2 * vh], vvars[b][2 * vh + 1], r_e[br], r_o[br],
                keys_on_lanes=True)
        attn_buf[rows, (2 * vh) * V7X_LANES:(2 * vh + 1) * V7X_LANES] = o[0:t_new]
        attn_buf[rows, (2 * vh + 1) * V7X_LANES:(2 * vh + 2) * V7X_LANES] = o[t_new:2 * t_new]

    ya = _mm((attn_buf[0:rows_total, :] * silu_ga).astype(BF16), wa_ref[...])
    mix = (sig_a * ya + sig_c * yc).astype(BF16)
    r = x + _mm(mix, wo_ref[...])
    gate = _sigmoid(_mm(r.astype(BF16), wpg_ref[...]))
    y_ref[...] = r + gate * pp


def _layer_kernel(n_prompt_steps, *refs):
    (xp_ref, pp_ref, xs_ref, ps_ref, ck_ref, cv_ref, sc_ref,
     lng_ref, gq_ref, gk_ref, invf_ref, sink_ref, cw_ref, bdq_ref, bdk_ref,
     win_hbm, wa_hbm, wb_hbm, wo_hbm, wpg_hbm, wpp_hbm,
     yp_ref, kpo_ref, vpo_ref, cpo_ref, ys_ref, kso_ref, vso_ref, cso_ref,
     kbuf, vbuf, ubuf_p, attn_buf, tab_p, ubuf_s, conv_buf, tab_s,
     win_ref, wa_ref, wb_ref, wo_ref, wpg_ref, wpp_ref, stage, sem) = refs
    i = pl.program_id(0)
    small = (lng_ref, gq_ref, gk_ref, invf_ref, sink_ref, cw_ref, bdq_ref, bdk_ref)
    weights = (win_ref, wa_ref, wb_ref, wo_ref, wpg_ref, wpp_ref)

    prompt_refs = (xp_ref, pp_ref, *small, *weights, yp_ref, kpo_ref, vpo_ref, cpo_ref,
                   kbuf, vbuf, ubuf_p, attn_buf, tab_p)

    @pl.when(i == 0)
    def _first_prompt():
        chunks = [(src, dst) + ch for src, dst, order in (
            (win_hbm, win_ref, _WIN_BLOCK_ORDER), (wb_hbm, wb_ref, None), (wa_hbm, wa_ref, None),
            (wo_hbm, wo_ref, None), (wpp_hbm, wpp_ref, None), (wpg_hbm, wpg_ref, None))
            for ch in _stage_chunks(src.shape, order)]
        stager = _WeightStager(chunks, stage, sem)
        stager.start()
        _prompt_body(i, stager, *prompt_refs)
        stager.finish()

    @pl.when((i > 0) & (i < n_prompt_steps))
    def _prompt():
        _prompt_body(i, None, *prompt_refs)

    @pl.when(i >= n_prompt_steps)
    def _sample():
        _sample_body(i - n_prompt_steps, xs_ref, ps_ref, ck_ref, cv_ref, sc_ref, *small, *weights,
                     ys_ref, kso_ref, vso_ref, cso_ref, ubuf_s, attn_buf, conv_buf, tab_s)


def _const_spec(shape):
    nd = len(shape)
    return pl.BlockSpec(shape, lambda i: (0,) * nd, pipeline_mode=pl.Buffered(1))


def _smem_spec():
    return pl.BlockSpec(memory_space=pltpu.SMEM)


def _block_diag_mean(width):
    idx = np.arange(width) // HEAD_DIM
    return jnp.asarray((idx[:, None] == idx[None, :]).astype(np.float32) / HEAD_DIM, dtype=BF16)


def _layer_consts(ln_g, w_in, q_norm_g, k_norm_g, sink, conv_w, w_attn_out, w_conv_out, w_o,
                  w_ple_gate, w_ple_proj):
    inv_freq = ROPE_THETA ** (-jnp.arange(0, HALF, dtype=F32) * 2.0 / HEAD_DIM)
    return dict(
        lng=ln_g.reshape(1, -1).astype(F32),
        gq=q_norm_g.astype(F32).reshape(1, HEAD_DIM),
        gk=k_norm_g.astype(F32).reshape(1, HEAD_DIM),
        invf=jnp.tile(inv_freq, V7X_LANES // HALF).reshape(1, V7X_LANES),
        sink=sink.astype(F32),
        cw=conv_w.astype(F32),
        bdq=_block_diag_mean(ATTN_DIM // 2),
        bdk=_block_diag_mean(KV_DIM),
        win=w_in.astype(F32),
        wa=w_attn_out.astype(F32),
        wb=w_conv_out.astype(F32),
        wo=w_o.astype(F32),
        wpg=w_ple_gate.astype(F32),
        wpp=w_ple_proj.astype(F32),
    )


_SMALL_KEYS = ("lng", "gq", "gk", "invf", "sink", "cw", "bdq", "bdk")
_STAGED_KEYS = ("win", "wa", "wb", "wo", "wpg", "wpp")


def _heads_first(t):
    t = jnp.moveaxis(t, -3, -1)
    return t.reshape(t.shape[:-3] + (KV_DIM, t.shape[-1]))


def _heads_last(t):
    t = t.reshape(t.shape[:-2] + (N_KV_HEADS, HEAD_DIM, t.shape[-1]))
    return jnp.moveaxis(t, -1, -3)


def _layer(xp, pp, xs, ps, cache_k, cache_v, state_conv, c):
    t, d = xp.shape
    nb, t_new, _ = xs.shape
    cache_len = cache_k.shape[1]
    tm, bb = PROMPT_TILE, SAMPLE_BATCH_TILE
    assert t % tm == 0 and tm % PROMPT_SUBTILE == 0
    assert PROMPT_SUBTILE % CHUNK == 0 and PROMPT_SUBTILE >= WINDOW
    assert nb % bb == 0 and CONV_WIDTH - 1 <= t_new <= cache_len
    n_p, n_s = t // tm, nb // bb
    rows = bb * t_new
    xs2 = xs.reshape(nb * t_new, d)
    ps2 = ps.reshape(nb * t_new, ps.shape[-1])
    ck = _heads_first(cache_k)
    cv = _heads_first(cache_v)

    def p_idx(i):
        return jnp.minimum(i, n_p - 1)

    def s_idx(i):
        return jnp.maximum(i - n_p, 0)

    in_specs = ([pl.BlockSpec((tm, d), lambda i: (p_idx(i), 0)),
                 pl.BlockSpec((tm, pp.shape[1]), lambda i: (p_idx(i), 0)),
                 pl.BlockSpec((rows, d), lambda i: (s_idx(i), 0)),
                 pl.BlockSpec((rows, ps2.shape[1]), lambda i: (s_idx(i), 0)),
                 pl.BlockSpec((bb, KV_DIM, cache_len), lambda i: (s_idx(i), 0, 0)),
                 pl.BlockSpec((bb, KV_DIM, cache_len), lambda i: (s_idx(i), 0, 0)),
                 pl.BlockSpec((bb, CONV_WIDTH - 1, CONV_DIM), lambda i: (s_idx(i), 0, 0))]
                + [_smem_spec() if k == "sink" else _const_spec(c[k].shape) for k in _SMALL_KEYS]
                + [pl.BlockSpec(memory_space=pl.ANY) for _ in _STAGED_KEYS])
    out_shape = (jax.ShapeDtypeStruct((t, d), F32),
                 jax.ShapeDtypeStruct((KV_DIM, WINDOW), F32),
                 jax.ShapeDtypeStruct((KV_DIM, WINDOW), F32),
                 jax.ShapeDtypeStruct((CONV_WIDTH - 1, CONV_DIM), F32),
                 jax.ShapeDtypeStruct((nb * t_new, d), F32),
                 jax.ShapeDtypeStruct((nb, KV_DIM, cache_len), F32),
                 jax.ShapeDtypeStruct((nb, KV_DIM, cache_len), F32),
                 jax.ShapeDtypeStruct((nb, CONV_WIDTH - 1, CONV_DIM), F32))
    out_specs = (pl.BlockSpec((tm, d), lambda i: (p_idx(i), 0)),
                 pl.BlockSpec((KV_DIM, WINDOW), lambda i: (0, 0)),
                 pl.BlockSpec((KV_DIM, WINDOW), lambda i: (0, 0)),
                 pl.BlockSpec((CONV_WIDTH - 1, CONV_DIM), lambda i: (0, 0)),
                 pl.BlockSpec((rows, d), lambda i: (s_idx(i), 0)),
                 pl.BlockSpec((bb, KV_DIM, cache_len), lambda i: (s_idx(i), 0, 0)),
                 pl.BlockSpec((bb, KV_DIM, cache_len), lambda i: (s_idx(i), 0, 0)),
                 pl.BlockSpec((bb, CONV_WIDTH - 1, CONV_DIM), lambda i: (s_idx(i), 0, 0)))
    scratch = ([pltpu.VMEM((4, KV_DIM, WINDOW + tm), BF16),
                pltpu.VMEM((4, WINDOW + tm, V7X_LANES), BF16),
                pltpu.VMEM((V7X_SUBLANES + tm, CONV_DIM), F32),
                pltpu.VMEM((max(tm, rows), ATTN_DIM), F32),
                pltpu.VMEM((4, tm, V7X_LANES), F32),
                pltpu.VMEM((bb, V7X_SUBLANES + t_new, CONV_DIM), F32),
                pltpu.VMEM((rows, CONV_DIM), F32),
                pltpu.VMEM((2, rows, V7X_LANES), F32)]
               + [pltpu.VMEM(c[k].shape, BF16) for k in _STAGED_KEYS]
               + [pltpu.VMEM((STAGE_SLOTS, STAGE_ROWS, STAGE_COLS), F32),
                  pltpu.SemaphoreType.DMA((STAGE_SLOTS,))])
    yp, kpo, vpo, cpo, ys, kso, vso, cso = pl.pallas_call(
        functools.partial(_layer_kernel, n_p),
        grid=(n_p + n_s,),
        in_specs=in_specs,
        out_specs=out_specs,
        out_shape=out_shape,
        scratch_shapes=scratch,
        compiler_params=pltpu.CompilerParams(dimension_semantics=("arbitrary",),
                                             vmem_limit_bytes=V7X_VMEM_LIMIT_BYTES),
        name="hybrid_layer",
    )(xp, pp, xs2, ps2, ck, cv, state_conv, *[c[k] for k in _SMALL_KEYS],
      *[c[k] for k in _STAGED_KEYS])
    return ((yp, _heads_last(kpo), _heads_last(vpo), cpo),
            (ys.reshape(nb, t_new, d), _heads_last(kso), _heads_last(vso), cso))


def kernel(x_prompt, x_sample, p_prompt, p_sample, cache_k, cache_v, state_conv, ln_g, w_in,
           q_norm_g, k_norm_g, sink, conv_w, w_attn_out, w_conv_out, w_o, w_ple_gate, w_ple_proj):
    depth = ln_g.shape[0]
    assert x_prompt.shape[0] == 1, "one prompt sequence per call"
    hp, hs = x_prompt[0], x_sample
    kp_l, vp_l, cp_l, ks_l, vs_l, cs_l = [], [], [], [], [], []
    for i in range(depth):
        c = _layer_consts(ln_g[i], w_in[i], q_norm_g[i], k_norm_g[i], sink[i],
                          jnp.swapaxes(conv_w, 0, 1)[:, i:i + 1, :],
                          w_attn_out[i], w_conv_out[i], w_o[i], w_ple_gate[i], w_ple_proj[i])
        (hp, kp, vp, cp), (hs, ks, vs, cs) = _layer(hp, p_prompt[i, 0], hs, p_sample[i], cache_k[i],
                                                    cache_v[i], state_conv[i], c)
        kp_l.append(kp[None])
        vp_l.append(vp[None])
        cp_l.append(cp[None])
        ks_l.append(ks)
        vs_l.append(vs)
        cs_l.append(cs)
    return (hp[None], hs, jnp.stack(kp_l), jnp.stack(vp_l), jnp.stack(cp_l),
            jnp.stack(ks_l), jnp.stack(vs_l), jnp.stack(cs_l))
```

```python
import functools

import numpy as np
import jax
import jax.numpy as jnp
from jax import lax
from jax.experimental import pallas as pl
from jax.experimental.pallas import tpu as pltpu

F32 = jnp.float32
BF16 = jnp.bfloat16

CHUNK = 64
WINDOW = 128
WINDOW_CHUNKS = WINDOW // CHUNK
N_HEADS = 8
N_KV_HEADS = 2
GROUP_HEADS = N_HEADS // N_KV_HEADS
HEAD_DIM = 64
HALF = HEAD_DIM // 2
ATTN_DIM = N_HEADS * HEAD_DIM
KV_DIM = N_KV_HEADS * HEAD_DIM
CONV_DIM = 512
CONV_WIDTH = 3
PAST_LEN = 1024
ROPE_THETA = 10000.0
EPS = 1e-6
NEG = -1e30

OFF_Q = 0
OFF_K = OFF_Q + ATTN_DIM
OFF_V = OFF_K + KV_DIM
OFF_GA = OFF_V + KV_DIM
OFF_B = OFF_GA + ATTN_DIM
OFF_GC_END = OFF_B + 4 * CONV_DIM
OFF_MA = OFF_GC_END

V7X_LANES = 128
V7X_SUBLANES = 8
V7X_VMEM_LIMIT_BYTES = 59 * 1024 * 1024

PROMPT_TILE = 512
PROMPT_SUBTILE = 256
PROMPT_STAGE_SKEW = 1
STAGE_ROWS = 256
STAGE_COLS = 1024
STAGE_SLOTS = 4
_WIN_BLOCK_ORDER = (0, 3, 4, 5, 1, 2)
SAMPLE_BATCH_TILE = 8


def _mm(a, w):
    return jnp.dot(a, w, preferred_element_type=F32)


def _mm_t(a, b):
    return lax.dot_general(a, b, (((1,), (1,)), ((), ())), preferred_element_type=F32)


def _sigmoid(x):
    return 0.5 * jnp.tanh(0.5 * x) + 0.5


def _silu(x):
    return x * _sigmoid(x)


def _rmsnorm(x, g):
    ms = jnp.mean(x * x, axis=-1, keepdims=True)
    return x * lax.rsqrt(ms + EPS) * g


def _group_mean(t, bd):
    return _mm(t.astype(BF16), bd)


def _head_norm(t, bd, g):
    ms = _group_mean(t * t, bd)
    return t * lax.rsqrt(ms + EPS) * g


def _rope(xc, cos_t, sin_s):
    lane = lax.broadcasted_iota(jnp.int32, xc.shape, 1)
    upper = (lane & HALF) != 0
    rot = jnp.where(upper, pltpu.roll(xc, HALF, 1), pltpu.roll(xc, V7X_LANES - HALF, 1))
    return xc * cos_t + rot * sin_s


def _rope_sign(shape):
    lane = lax.broadcasted_iota(jnp.int32, shape, 1)
    return jnp.where((lane & HALF) != 0, 1.0, -1.0).astype(F32)


def _kv_variants(t):
    lane = lax.broadcasted_iota(jnp.int32, t.shape, 1)
    lo = lane < HEAD_DIM
    sw = pltpu.roll(t, HEAD_DIM, 1)
    zero = jnp.zeros_like(t)
    return (jnp.where(lo, t, zero).astype(BF16), jnp.where(lo, zero, sw).astype(BF16),
            jnp.where(lo, sw, zero).astype(BF16), jnp.where(lo, zero, t).astype(BF16))


def _kv_variants_t(t):
    tb = t.astype(BF16)
    h0, h1 = tb[0:HEAD_DIM], tb[HEAD_DIM:2 * HEAD_DIM]
    zero = jnp.zeros_like(h0)
    return (jnp.concatenate([h0, zero], axis=0), jnp.concatenate([zero, h0], axis=0),
            jnp.concatenate([h1, zero], axis=0), jnp.concatenate([zero, h1], axis=0))


def _scores(qs, ka, kb, bad, keys_on_lanes=False):
    dot = _mm if keys_on_lanes else _mm_t
    s_e = dot(qs, ka)
    s_o = dot(qs, kb)
    if bad is not None:
        s_e = jnp.where(bad, NEG, s_e)
        s_o = jnp.where(bad, NEG, s_o)
    return s_e, s_o


def _softmax_parts(s, sink):
    m = jnp.maximum(jnp.max(s, axis=-1, keepdims=True), sink)
    e = jnp.exp(s - m)
    r = 1.0 / (jnp.sum(e, axis=-1, keepdims=True) + jnp.exp(sink - m))
    return e.astype(BF16), r


def _softmax_parts_inline_sink(s):
    m = jnp.max(s, axis=-1, keepdims=True)
    e = jnp.exp(s - m)
    return e.astype(BF16), 1.0 / jnp.sum(e, axis=-1, keepdims=True)


def _pv(p_e, p_o, va, vb, r_e, r_o, keys_on_lanes=False):
    dot = _mm_t if keys_on_lanes else _mm
    o = dot(p_e, va) + dot(p_o, vb)
    lane = lax.broadcasted_iota(jnp.int32, o.shape, 1)
    return o * jnp.where(lane < HEAD_DIM, r_e, r_o)


def _sink_cols(sink_ref, n_blocks, rows_per_pair, blocks_per_kv=1):
    row = lax.broadcasted_iota(jnp.int32, (n_blocks * 2 * rows_per_pair, 1), 0)
    second_pair = (row // rows_per_pair) % 2 == 1
    second_kv = (row // (2 * rows_per_pair * blocks_per_kv)) % 2 == 1

    def pick(odd):
        kv0 = jnp.where(second_pair, sink_ref[2 + odd], sink_ref[odd])
        kv1 = jnp.where(second_pair, sink_ref[GROUP_HEADS + 2 + odd], sink_ref[GROUP_HEADS + odd])
        return jnp.where(second_kv, kv1, kv0)

    return pick(0), pick(1)


def _stage_chunks(shape, col_order=None):
    rows, cols = shape
    assert rows % STAGE_ROWS == 0 and cols % V7X_LANES == 0
    width = max(c for c in range(V7X_LANES, STAGE_COLS + 1, V7X_LANES) if cols % c == 0)
    blocks = list(range(cols // width)) if col_order is None else list(col_order)
    assert sorted(blocks) == list(range(cols // width))
    return [(r0, b * width, STAGE_ROWS, width) for b in blocks for r0 in range(0, rows, STAGE_ROWS)]


class _WeightStager:
    def __init__(self, chunks, stage, sem):
        self.chunks, self.stage, self.sem, self.done = chunks, stage, sem, 0

    def _copy(self, n):
        src, _, r0, c0, rows, cols = self.chunks[n]
        slot = n % STAGE_SLOTS
        return pltpu.make_async_copy(src.at[pl.ds(r0, rows), pl.ds(c0, cols)],
                                     self.stage.at[slot, pl.ds(0, rows), pl.ds(0, cols)],
                                     self.sem.at[slot])

    def start(self):
        for n in range(min(STAGE_SLOTS, len(self.chunks))):
            self._copy(n).start(priority=n % 2)

    def pump(self, count):
        while self.done < min(count, len(self.chunks)):
            n = self.done
            _, dst, r0, c0, rows, cols = self.chunks[n]
            self._copy(n).wait()
            dst[r0:r0 + rows, c0:c0 + cols] = (
                self.stage[n % STAGE_SLOTS, 0:rows, 0:cols].astype(BF16))
            if n + STAGE_SLOTS < len(self.chunks):
                self._copy(n + STAGE_SLOTS).start(priority=n % 2)
            self.done += 1

    def finish(self):
        self.pump(len(self.chunks))


def _qkv_proj(x, lng_ref, win_ref):
    hb = _rmsnorm(x, lng_ref[...]).astype(BF16)
    return hb, _mm(hb, win_ref[:, OFF_Q:OFF_GA])


def _tile_lanes(g, width):
    while g.shape[1] < width:
        g = jnp.concatenate([g, g], axis=1)
    return g


def _qk_norm_rope(qkv, gq_ref, gk_ref, bdq_ref, bdk_ref, cos_t, sin_s):
    half_q = ATTN_DIM // 2
    gq = _tile_lanes(gq_ref[...] * (HEAD_DIM ** -0.5), half_q)
    gk = _tile_lanes(gk_ref[...], KV_DIM)
    q_cols = []
    for j in range(2):
        t = qkv[:, j * half_q:(j + 1) * half_q]
        tn = _head_norm(t, bdq_ref[...], gq)
        for c in range(half_q // V7X_LANES):
            q_cols.append(_rope(tn[:, c * V7X_LANES:(c + 1) * V7X_LANES], cos_t, sin_s).astype(BF16))
    kn = _head_norm(qkv[:, OFF_K:OFF_V], bdk_ref[...], gk)
    kr = _rope(kn, cos_t, sin_s)
    v = qkv[:, OFF_V:OFF_GA]
    return q_cols, kr, v


def _conv_proj(hb, win_ref):
    return _mm(hb, win_ref[:, OFF_B:OFF_GC_END])


def _conv_input(bcug, conv_fn):
    b_gate = bcug[:, 0:CONV_DIM]
    u = bcug[:, CONV_DIM:2 * CONV_DIM] * bcug[:, 2 * CONV_DIM:3 * CONV_DIM]
    gate_c = bcug[:, 3 * CONV_DIM:4 * CONV_DIM]
    conv = conv_fn(u)
    return (b_gate * conv * _silu(gate_c)).astype(BF16)


def _prompt_body(i, stager, x_ref, p_ref, lng_ref, gq_ref, gk_ref, invf_ref, sink_ref, cw_ref,
                 bdq_ref, bdk_ref, win_ref, wa_ref, wb_ref, wo_ref, wpg_ref, wpp_ref,
                 y_ref, ko_ref, vo_ref, co_ref, kbuf, vbuf, ubuf, attn_buf, tab):
    tm, d_model = x_ref.shape
    invf = invf_ref[...]
    first = stager is not None

    if first:
        kbuf[:, :, 0:WINDOW] = jnp.zeros((4, KV_DIM, WINDOW), BF16)
        vbuf[:, 0:WINDOW, :] = jnp.zeros((4, WINDOW, V7X_LANES), BF16)
        ubuf[0:V7X_SUBLANES, :] = jnp.zeros((V7X_SUBLANES, CONV_DIM), F32)
        r = lax.broadcasted_iota(jnp.int32, (tm, V7X_LANES), 0).astype(F32)
        ang = r * invf
        sgn = _rope_sign((tm, V7X_LANES))
        c_r = jnp.cos(ang)
        s_r = jnp.sin(ang)
        tab[0] = c_r
        tab[1] = s_r
        tab[2] = c_r * sgn
        tab[3] = s_r * sgn

    base = (i * tm).astype(F32) * invf
    cb = jnp.cos(base)
    sb = jnp.sin(base)
    cos_t = tab[0] * cb - tab[1] * sb
    sin_s = tab[3] * cb + tab[2] * sb

    sub = PROMPT_SUBTILE
    n_sub = tm // sub
    sub_chunks = sub // CHUNK
    assert 2 * CHUNK == V7X_LANES and WINDOW == V7X_LANES
    n_keys = 2 * V7X_LANES

    def key_window_start(r0, c):
        return ((r0 + c * CHUNK) // V7X_LANES) * V7X_LANES

    def mask_window(sc, r0, c, sink):
        lo, hi = sc[:, 0:V7X_LANES], sc[:, V7X_LANES:n_keys]
        lane = lax.broadcasted_iota(jnp.int32, lo.shape, 1)
        no_carry = first and key_window_start(r0, c) < WINDOW
        if no_carry:
            lo = jnp.full_like(lo, NEG)
        if ((r0 + c * CHUNK) // CHUNK) % 2 == 0:
            hi = jnp.where(lane < CHUNK, hi, jnp.where(lane == V7X_LANES - 1, sink, NEG))
        else:
            lo = jnp.where((lane >= CHUNK) & (not no_carry), lo, jnp.where(lane == 0, sink, NEG))
        return jnp.concatenate([lo, hi], axis=1)

    def drop_sink_lane(p, c):
        lo, hi = p[:, 0:V7X_LANES], p[:, V7X_LANES:n_keys]
        lane = lax.broadcasted_iota(jnp.int32, lo.shape, 1)
        if c % 2 == 0:
            hi = jnp.where(lane == V7X_LANES - 1, jnp.zeros_like(hi), hi)
        else:
            lo = jnp.where(lane == 0, jnp.zeros_like(lo), lo)
        return jnp.concatenate([lo, hi], axis=1)

    pad = V7X_SUBLANES
    blocks = [(c0, vh) for c0 in range(0, sub_chunks, 2) for vh in range(N_KV_HEADS)]
    st = [dict(r0=s * sub) for s in range(n_sub)]

    def stage_rms(s):
        s["hb"] = _rmsnorm(x_ref[s["r0"]:s["r0"] + sub, :], lng_ref[...]).astype(BF16)

    def stage_qkv(s):
        s["qkv"] = _mm(s["hb"], win_ref[:, OFF_Q:OFF_GA])

    def stage_gate_a(s):
        s["sig_a"] = _sigmoid(_mm(s["hb"], win_ref[:, OFF_MA:OFF_MA + d_model]))

    def stage_gate_c(s):
        s["sig_c"] = _sigmoid(_mm(s["hb"], win_ref[:, OFF_MA + d_model:OFF_MA + 2 * d_model]))

    def stage_qk_norm(s):
        r0 = s["r0"]
        s["q_cols"], kr, v = _qk_norm_rope(s.pop("qkv"), gq_ref, gk_ref, bdq_ref, bdk_ref,
                                           cos_t[r0:r0 + sub], sin_s[r0:r0 + sub])
        kr_t = kr.T
        for n, t in enumerate(_kv_variants_t(kr_t)):
            kbuf[n, :, WINDOW + r0:WINDOW + r0 + sub] = t
        for n, t in enumerate(_kv_variants(v)):
            vbuf[n, WINDOW + r0:WINDOW + r0 + sub, :] = t
        if r0 + sub == tm:
            ko_ref[...] = kr_t[:, sub - WINDOW:sub]
            vo_ref[...] = v[sub - WINDOW:sub, :].T

    def stage_scores(s):
        r0 = s["r0"]
        s_e, s_o = [], []
        for c0, vh in blocks:
            start = key_window_start(r0, c0)
            win = slice(start, start + n_keys)
            qs = jnp.concatenate(
                [s["q_cols"][2 * vh + pair][(c0 + dc) * CHUNK:(c0 + dc + 1) * CHUNK]
                 for dc in range(2) for pair in range(2)], axis=0)
            se, so = _scores(qs, kbuf[2 * vh, :, win], kbuf[2 * vh + 1, :, win], None,
                             keys_on_lanes=True)
            for dc in range(2):
                half = slice(dc * 2 * CHUNK, (dc + 1) * 2 * CHUNK)
                blk = slice(len(s_e) * 2 * CHUNK, (len(s_e) + 1) * 2 * CHUNK)
                s_e.append(mask_window(se[half], r0, c0 + dc, sink_e[blk]))
                s_o.append(mask_window(so[half], r0, c0 + dc, sink_o[blk]))
        s["s_e"] = jnp.concatenate(s_e, axis=0)
        s["s_o"] = jnp.concatenate(s_o, axis=0)
        del s["q_cols"]

    def stage_conv_proj(s):
        s["bcug"] = _conv_proj(s["hb"], win_ref)

    def stage_conv(s):
        r0 = s["r0"]

        def conv_fn(u):
            ubuf[pad + r0:pad + r0 + sub, :] = u
            conv = ubuf[pad + r0 - 2:pad + r0 - 2 + sub, :] * cw_ref[0]
            conv = conv + ubuf[pad + r0 - 1:pad + r0 - 1 + sub, :] * cw_ref[1]
            return conv + u * cw_ref[2]

        s["c_in"] = _conv_input(s.pop("bcug"), conv_fn)

    def stage_conv_out(s):
        s["yc"] = _mm(s.pop("c_in"), wb_ref[...])

    sink_e, sink_o = _sink_cols(sink_ref, 2 * len(blocks), CHUNK, blocks_per_kv=2)

    def stage_softmax(s):
        s["p_e"], s["r_e"] = _softmax_parts_inline_sink(s.pop("s_e"))
        s["p_o"], s["r_o"] = _softmax_parts_inline_sink(s.pop("s_o"))

    def stage_pv(s):
        r0 = s["r0"]
        for n, (c0, vh) in enumerate(blocks):
            start = key_window_start(r0, c0)
            win = slice(start, start + n_keys)
            br = slice(n * 4 * CHUNK, (n + 1) * 4 * CHUNK)
            p_e, p_o = (jnp.concatenate(
                [drop_sink_lane(p[(2 * n + dc) * 2 * CHUNK:(2 * n + dc + 1) * 2 * CHUNK], c0 + dc)
                 for dc in range(2)], axis=0) for p in (s["p_e"], s["p_o"]))
            o = _pv(p_e, p_o, vbuf[2 * vh, win, :], vbuf[2 * vh + 1, win, :],
                    s["r_e"][br], s["r_o"][br])
            for dc in range(2):
                rows = slice(r0 + (c0 + dc) * CHUNK, r0 + (c0 + dc + 1) * CHUNK)
                for pair in range(2):
                    part = o[(2 * dc + pair) * CHUNK:(2 * dc + pair + 1) * CHUNK]
                    attn_buf[rows, (2 * vh + pair) * V7X_LANES:(2 * vh + pair + 1) * V7X_LANES] = part

    def stage_attn_gate(s):
        s["silu_ga"] = _silu(_mm(s["hb"], win_ref[:, OFF_GA:OFF_B]))

    def stage_attn_out(s):
        r0 = s["r0"]
        a_in = (attn_buf[r0:r0 + sub, :] * s.pop("silu_ga")).astype(BF16)
        ya = _mm(a_in, wa_ref[...])
        s["mix"] = (s.pop("sig_a") * ya + s.pop("sig_c") * s.pop("yc")).astype(BF16)

    def stage_out_proj(s):
        r0 = s["r0"]
        s["r"] = x_ref[r0:r0 + sub, :] + _mm(s.pop("mix"), wo_ref[...])
        s["pp"] = _mm(p_ref[r0:r0 + sub, :].astype(BF16), wpp_ref[...])

    def stage_ple(s):
        r0 = s["r0"]
        r = s.pop("r")
        gate = _sigmoid(_mm(r.astype(BF16), wpg_ref[...]))
        y_ref[r0:r0 + sub, :] = r + gate * s.pop("pp")

    stages = [stage_rms, stage_qkv, stage_qk_norm, stage_gate_a, stage_scores, stage_gate_c,
              stage_softmax, stage_conv_proj, stage_conv, stage_attn_gate, stage_conv_out,
              stage_pv, stage_attn_out, stage_out_proj, stage_ple]
    needs = {stage_rms: 0, stage_qkv: 4, stage_qk_norm: 8, stage_gate_a: 12, stage_scores: 14,
             stage_gate_c: 16, stage_softmax: 20, stage_conv_proj: 24, stage_conv: 25,
             stage_attn_gate: 25, stage_conv_out: 26, stage_pv: 27, stage_attn_out: 28,
             stage_out_proj: 33, stage_ple: 37}
    for t in range(len(stages) + PROMPT_STAGE_SKEW * (n_sub - 1)):
        for j, s in enumerate(st):
            k = t - j * PROMPT_STAGE_SKEW
            if 0 <= k < len(stages):
                if first and j == 0:
                    stager.pump(needs[stages[k]])
                stages[k](s)

    co_ref[...] = ubuf[pad + tm - 2:pad + tm, :]
    ubuf[0:pad, :] = ubuf[tm:tm + pad, :]
    kbuf[:, :, 0:WINDOW] = kbuf[:, :, tm:tm + WINDOW]
    vbuf[:, 0:WINDOW, :] = vbuf[:, tm:tm + WINDOW, :]


def _sample_body(i, x_ref, p_ref, ck_ref, cv_ref, sc_ref, lng_ref, gq_ref, gk_ref, invf_ref,
                 sink_ref, cw_ref, bdq_ref, bdk_ref, win_ref, wa_ref, wb_ref, wo_ref, wpg_ref,
                 wpp_ref, y_ref, ko_ref, vo_ref, co_ref, ubuf, attn_buf, conv_buf, tab):
    bb, _, cache_len = ck_ref.shape
    rows_total, d_model = x_ref.shape
    t_new = rows_total // bb
    n_keys = cache_len + t_new

    @pl.when(i == 0)
    def _init():
        r = lax.broadcasted_iota(jnp.int32, (rows_total, V7X_LANES), 0)
        pos = (PAST_LEN + lax.rem(r, t_new)).astype(F32)
        ang = pos * invf_ref[...]
        tab[0] = jnp.cos(ang)
        tab[1] = jnp.sin(ang) * _rope_sign((rows_total, V7X_LANES))

    def conv_fn(u):
        pad = V7X_SUBLANES
        for b in range(bb):
            rows = slice(b * t_new, (b + 1) * t_new)
            ub = u[rows]
            ubuf[b, pad - (CONV_WIDTH - 1):pad, :] = sc_ref[b]
            ubuf[b, pad:pad + t_new, :] = ub
            conv = ubuf[b, pad - 2:pad - 2 + t_new, :] * cw_ref[0]
            conv = conv + ubuf[b, pad - 1:pad - 1 + t_new, :] * cw_ref[1]
            conv_buf[rows, :] = conv + ub * cw_ref[2]
            co_ref[b] = ubuf[b, pad + t_new - (CONV_WIDTH - 1):pad + t_new, :]
        return conv_buf[...]

    x = x_ref[...]
    hb, qkv = _qkv_proj(x, lng_ref, win_ref)
    sig_a = _sigmoid(_mm(hb, win_ref[:, OFF_MA:OFF_MA + d_model]))
    q_cols, kr, v = _qk_norm_rope(qkv, gq_ref, gk_ref, bdq_ref, bdk_ref, tab[0], tab[1])

    qi = lax.broadcasted_iota(jnp.int32, (2 * t_new, n_keys), 0)
    q_pos = PAST_LEN + lax.rem(qi, t_new)
    k_pos = PAST_LEN - cache_len + lax.broadcasted_iota(jnp.int32, (2 * t_new, n_keys), 1)
    q_ch = q_pos // CHUNK
    k_ch = k_pos // CHUNK
    bad = jnp.logical_not((k_ch <= q_ch) & (k_ch >= q_ch - WINDOW_CHUNKS))

    blocks = [(b, vh) for b in range(bb) for vh in range(N_KV_HEADS)]
    s_e, s_o, vvars = [], [], []
    kr_t = kr.T
    v_t = v.T
    for b in range(bb):
        rows = slice(b * t_new, (b + 1) * t_new)
        kcat = jnp.concatenate([ck_ref[b], kr_t[:, rows]], axis=1)
        vcat = jnp.concatenate([cv_ref[b], v_t[:, rows]], axis=1)
        ko_ref[b] = kcat[:, n_keys - cache_len:n_keys]
        vo_ref[b] = vcat[:, n_keys - cache_len:n_keys]
        kvar = _kv_variants_t(kcat)
        vvars.append(_kv_variants_t(vcat))
        for vh in range(N_KV_HEADS):
            qs = jnp.concatenate([q_cols[2 * vh][rows], q_cols[2 * vh + 1][rows]], axis=0)
            se, so = _scores(qs, kvar[2 * vh], kvar[2 * vh + 1], bad, keys_on_lanes=True)
            s_e.append(se)
            s_o.append(so)

    sig_c = _sigmoid(_mm(hb, win_ref[:, OFF_MA + d_model:OFF_MA + 2 * d_model]))
    sink_e, sink_o = _sink_cols(sink_ref, len(blocks), t_new)
    p_e, r_e = _softmax_parts(jnp.concatenate(s_e, axis=0), sink_e)
    bcug = _conv_proj(hb, win_ref)
    p_o, r_o = _softmax_parts(jnp.concatenate(s_o, axis=0), sink_o)
    c_in = _conv_input(bcug, conv_fn)
    silu_ga = _silu(_mm(hb, win_ref[:, OFF_GA:OFF_B]))
    pp = _mm(p_ref[...].astype(BF16), wpp_ref[...])
    yc = _mm(c_in, wb_ref[...])

    for n, (b, vh) in enumerate(blocks):
        rows = slice(b * t_new, (b + 1) * t_new)
        br = slice(n * 2 * t_new, (n + 1) * 2 * t_new)
        o = _pv(p_e[br], p_o[br], vvars[b][2 * vh], vvars[b][2 * vh + 1], r_e[br], r_o[br],
                keys_on_lanes=True)
        attn_buf[rows, (2 * vh) * V7X_LANES:(2 * vh + 1) * V7X_LANES] = o[0:t_new]
        attn_buf[rows, (2 * vh + 1) * V7X_LANES:(2 * vh + 2) * V7X_LANES] = o[t_new:2 * t_new]

    ya = _mm((attn_buf[0:rows_total, :] * silu_ga).astype(BF16), wa_ref[...])
    mix = (sig_a * ya + sig_c * yc).astype(BF16)
    r = x + _mm(mix, wo_ref[...])
    gate = _sigmoid(_mm(r.astype(BF16), wpg_ref[...]))
    y_ref[...] = r + gate * pp


def _layer_kernel(n_prompt_steps, *refs):
    (xp_ref, pp_ref, xs_ref, ps_ref, ck_ref, cv_ref, sc_ref,
     lng_ref, gq_ref, gk_ref, invf_ref, sink_ref, cw_ref, bdq_ref, bdk_ref,
     win_hbm, wa_hbm, wb_hbm, wo_hbm, wpg_hbm, wpp_hbm,
     yp_ref, kpo_ref, vpo_ref, cpo_ref, ys_ref, kso_ref, vso_ref, cso_ref,
     kbuf, vbuf, ubuf_p, attn_buf, tab_p, ubuf_s, conv_buf, tab_s,
     win_ref, wa_ref, wb_ref, wo_ref, wpg_ref, wpp_ref, stage, sem) = refs
    i = pl.program_id(0)
    small = (lng_ref, gq_ref, gk_ref, invf_ref, sink_ref, cw_ref, bdq_ref, bdk_ref)
    weights = (win_ref, wa_ref, wb_ref, wo_ref, wpg_ref, wpp_ref)

    prompt_refs = (xp_ref, pp_ref, *small, *weights, yp_ref, kpo_ref, vpo_ref, cpo_ref,
                   kbuf, vbuf, ubuf_p, attn_buf, tab_p)

    @pl.when(i == 0)
    def _first_prompt():
        chunks = [(src, dst) + ch for src, dst, order in (
            (win_hbm, win_ref, _WIN_BLOCK_ORDER), (wb_hbm, wb_ref, None), (wa_hbm, wa_ref, None),
            (wo_hbm, wo_ref, None), (wpp_hbm, wpp_ref, None), (wpg_hbm, wpg_ref, None))
            for ch in _stage_chunks(src.shape, order)]
        stager = _WeightStager(chunks, stage, sem)
        stager.start()
        _prompt_body(i, stager, *prompt_refs)
        stager.finish()

    @pl.when((i > 0) & (i < n_prompt_steps))
    def _prompt():
        _prompt_body(i, None, *prompt_refs)

    @pl.when(i >= n_prompt_steps)
    def _sample():
        _sample_body(i - n_prompt_steps, xs_ref, ps_ref, ck_ref, cv_ref, sc_ref, *small, *weights,
                     ys_ref, kso_ref, vso_ref, cso_ref, ubuf_s, attn_buf, conv_buf, tab_s)


def _const_spec(shape):
    nd = len(shape)
    return pl.BlockSpec(shape, lambda i: (0,) * nd, pipeline_mode=pl.Buffered(1))


def _smem_spec():
    return pl.BlockSpec(memory_space=pltpu.SMEM)


def _block_diag_mean(width):
    idx = np.arange(width) // HEAD_DIM
    return jnp.asarray((idx[:, None] == idx[None, :]).astype(np.float32) / HEAD_DIM, dtype=BF16)


def _layer_consts(ln_g, w_in, q_norm_g, k_norm_g, sink, conv_w, w_attn_out, w_conv_out, w_o,
                  w_ple_gate, w_ple_proj):
    inv_freq = ROPE_THETA ** (-jnp.arange(0, HALF, dtype=F32) * 2.0 / HEAD_DIM)
    return dict(
        lng=ln_g.reshape(1, -1).astype(F32),
        gq=q_norm_g.astype(F32).reshape(1, HEAD_DIM),
        gk=k_norm_g.astype(F32).reshape(1, HEAD_DIM),
        invf=jnp.tile(inv_freq, V7X_LANES // HALF).reshape(1, V7X_LANES),
        sink=sink.astype(F32),
        cw=conv_w.astype(F32),
        bdq=_block_diag_mean(ATTN_DIM // 2),
        bdk=_block_diag_mean(KV_DIM),
        win=w_in.astype(F32),
        wa=w_attn_out.astype(F32),
        wb=w_conv_out.astype(F32),
        wo=w_o.astype(F32),
        wpg=w_ple_gate.astype(F32),
        wpp=w_ple_proj.astype(F32),
    )


_SMALL_KEYS = ("lng", "gq", "gk", "invf", "sink", "cw", "bdq", "bdk")
_STAGED_KEYS = ("win", "wa", "wb", "wo", "wpg", "wpp")


def _heads_first(t):
    t = jnp.moveaxis(t, -3, -1)
    return t.reshape(t.shape[:-3] + (KV_DIM, t.shape[-1]))


def _heads_last(t):
    t = t.reshape(t.shape[:-2] + (N_KV_HEADS, HEAD_DIM, t.shape[-1]))
    return jnp.moveaxis(t, -1, -3)


def _layer(xp, pp, xs, ps, cache_k, cache_v, state_conv, c):
    t, d = xp.shape
    nb, t_new, _ = xs.shape
    cache_len = cache_k.shape[1]
    tm, bb = PROMPT_TILE, SAMPLE_BATCH_TILE
    assert t % tm == 0 and tm % PROMPT_SUBTILE == 0
    assert PROMPT_SUBTILE % CHUNK == 0 and PROMPT_SUBTILE >= WINDOW
    assert nb % bb == 0 and CONV_WIDTH - 1 <= t_new <= cache_len
    n_p, n_s = t // tm, nb // bb
    rows = bb * t_new
    xs2 = xs.reshape(nb * t_new, d)
    ps2 = ps.reshape(nb * t_new, ps.shape[-1])
    ck = _heads_first(cache_k)
    cv = _heads_first(cache_v)

    def p_idx(i):
        return jnp.minimum(i, n_p - 1)

    def s_idx(i):
        return jnp.maximum(i - n_p, 0)

    in_specs = ([pl.BlockSpec((tm, d), lambda i: (p_idx(i), 0)),
                 pl.BlockSpec((tm, pp.shape[1]), lambda i: (p_idx(i), 0)),
                 pl.BlockSpec((rows, d), lambda i: (s_idx(i), 0)),
                 pl.BlockSpec((rows, ps2.shape[1]), lambda i: (s_idx(i), 0)),
                 pl.BlockSpec((bb, KV_DIM, cache_len), lambda i: (s_idx(i), 0, 0)),
                 pl.BlockSpec((bb, KV_DIM, cache_len), lambda i: (s_idx(i), 0, 0)),
                 pl.BlockSpec((bb, CONV_WIDTH - 1, CONV_DIM), lambda i: (s_idx(i), 0, 0))]
                + [_smem_spec() if k == "sink" else _const_spec(c[k].shape) for k in _SMALL_KEYS]
                + [pl.BlockSpec(memory_space=pl.ANY) for _ in _STAGED_KEYS])
    out_shape = (jax.ShapeDtypeStruct((t, d), F32),
                 jax.ShapeDtypeStruct((KV_DIM, WINDOW), F32),
                 jax.ShapeDtypeStruct((KV_DIM, WINDOW), F32),
                 jax.ShapeDtypeStruct((CONV_WIDTH - 1, CONV_DIM), F32),
                 jax.ShapeDtypeStruct((nb * t_new, d), F32),
                 jax.ShapeDtypeStruct((nb, KV_DIM, cache_len), F32),
                 jax.ShapeDtypeStruct((nb, KV_DIM, cache_len), F32),
                 jax.ShapeDtypeStruct((nb, CONV_WIDTH - 1, CONV_DIM), F32))
    out_specs = (pl.BlockSpec((tm, d), lambda i: (p_idx(i), 0)),
                 pl.BlockSpec((KV_DIM, WINDOW), lambda i: (0, 0)),
                 pl.BlockSpec((KV_DIM, WINDOW), lambda i: (0, 0)),
                 pl.BlockSpec((CONV_WIDTH - 1, CONV_DIM), lambda i: (0, 0)),
                 pl.BlockSpec((rows, d), lambda i: (s_idx(i), 0)),
                 pl.BlockSpec((bb, KV_DIM, cache_len), lambda i: (s_idx(i), 0, 0)),
                 pl.BlockSpec((bb, KV_DIM, cache_len), lambda i: (s_idx(i), 0, 0)),
                 pl.BlockSpec((bb, CONV_WIDTH - 1, CONV_DIM), lambda i: (s_idx(i), 0, 0)))
    scratch = ([pltpu.VMEM((4, KV_DIM, WINDOW + tm), BF16),
                pltpu.VMEM((4, WINDOW + tm, V7X_LANES), BF16),
                pltpu.VMEM((V7X_SUBLANES + tm, CONV_DIM), F32),
                pltpu.VMEM((max(tm, rows), ATTN_DIM), F32),
                pltpu.VMEM((4, tm, V7X_LANES), F32),
                pltpu.VMEM((bb, V7X_SUBLANES + t_new, CONV_DIM), F32),
                pltpu.VMEM((rows, CONV_DIM), F32),
                pltpu.VMEM((2, rows, V7X_LANES), F32)]
               + [pltpu.VMEM(c[k].shape, BF16) for k in _STAGED_KEYS]
               + [pltpu.VMEM((STAGE_SLOTS, STAGE_ROWS, STAGE_COLS), F32),
                  pltpu.SemaphoreType.DMA((STAGE_SLOTS,))])
    yp, kpo, vpo, cpo, ys, kso, vso, cso = pl.pallas_call(
        functools.partial(_layer_kernel, n_p),
        grid=(n_p + n_s,),
        in_specs=in_specs,
        out_specs=out_specs,
        out_shape=out_shape,
        scratch_shapes=scratch,
        compiler_params=pltpu.CompilerParams(dimension_semantics=("arbitrary",),
                                             vmem_limit_bytes=V7X_VMEM_LIMIT_BYTES),
        name="hybrid_layer",
    )(xp, pp, xs2, ps2, ck, cv, state_conv, *[c[k] for k in _SMALL_KEYS],
      *[c[k] for k in _STAGED_KEYS])
    return ((yp, _heads_last(kpo), _heads_last(vpo), cpo),
            (ys.reshape(nb, t_new, d), _heads_last(kso), _heads_last(vso), cso))


def kernel(x_prompt, x_sample, p_prompt, p_sample, cache_k, cache_v, state_conv, ln_g, w_in,
           q_norm_g, k_norm_g, sink, conv_w, w_attn_out, w_conv_out, w_o, w_ple_gate, w_ple_proj):
    depth = ln_g.shape[0]
    assert x_prompt.shape[0] == 1, "one prompt sequence per call"
    hp, hs = x_prompt[0], x_sample
    kp_l, vp_l, cp_l, ks_l, vs_l, cs_l = [], [], [], [], [], []
    for i in range(depth):
        c = _layer_consts(ln_g[i], w_in[i], q_norm_g[i], k_norm_g[i], sink[i],
                          jnp.swapaxes(conv_w, 0, 1)[:, i:i + 1, :],
                          w_attn_out[i], w_conv_out[i], w_o[i], w_ple_gate[i], w_ple_proj[i])
        (hp, kp, vp, cp), (hs, ks, vs, cs) = _layer(hp, p_prompt[i, 0], hs, p_sample[i], cache_k[i],
                                                    cache_v[i], state_conv[i], c)
        kp_l.append(kp[None])
        vp_l.append(vp[None])
        cp_l.append(cp[None])
        ks_l.append(ks)
        vs_l.append(vs)
        cs_l.append(cs)
    return (hp[None], hs, jnp.stack(kp_l), jnp.stack(vp_l), jnp.stack(cp_l),
            jnp.stack(ks_l), jnp.stack(vs_l), jnp.stack(cs_l))
```

```python
import functools

import numpy as np
import jax
import jax.numpy as jnp
from jax import lax
from jax.experimental import pallas as pl
from jax.experimental.pallas import tpu as pltpu

F32 = jnp.float32
BF16 = jnp.bfloat16

CHUNK = 64
WINDOW = 128
WINDOW_CHUNKS = WINDOW // CHUNK
N_HEADS = 8
N_KV_HEADS = 2
GROUP_HEADS = N_HEADS // N_KV_HEADS
HEAD_DIM = 64
HALF = HEAD_DIM // 2
ATTN_DIM = N_HEADS * HEAD_DIM
KV_DIM = N_KV_HEADS * HEAD_DIM
CONV_DIM = 512
CONV_WIDTH = 3
PAST_LEN = 1024
ROPE_THETA = 10000.0
EPS = 1e-6
NEG = -1e30

OFF_Q = 0
OFF_K = OFF_Q + ATTN_DIM
OFF_V = OFF_K + KV_DIM
OFF_GA = OFF_V + KV_DIM
OFF_B = OFF_GA + ATTN_DIM
OFF_GC_END = OFF_B + 4 * CONV_DIM
OFF_MA = OFF_GC_END

V7X_LANES = 128
V7X_SUBLANES = 8
V7X_VMEM_LIMIT_BYTES = 59 * 1024 * 1024

PROMPT_TILE = 512
PROMPT_SUBTILE = 256
PROMPT_STAGE_SKEW = 1
STAGE_ROWS = 256
STAGE_COLS = 1024
STAGE_SLOTS = 4
_WIN_BLOCK_ORDER = (0, 3, 4, 5, 1, 2)
SAMPLE_BATCH_TILE = 8


def _mm(a, w):
    return jnp.dot(a, w, preferred_element_type=F32)


def _mm_t(a, b):
    return lax.dot_general(a, b, (((1,), (1,)), ((), ())), preferred_element_type=F32)


def _sigmoid(x):
    return 0.5 * jnp.tanh(0.5 * x) + 0.5


def _silu(x):
    return x * _sigmoid(x)


def _rmsnorm(x, g):
    ms = jnp.mean(x * x, axis=-1, keepdims=True)
    return x * lax.rsqrt(ms + EPS) * g


def _group_mean(t, bd):
    return _mm(t.astype(BF16), bd)


def _head_norm(t, bd, g):
    ms = _group_mean(t * t, bd)
    return t * lax.rsqrt(ms + EPS) * g


def _rope(xc, cos_t, sin_s):
    lane = lax.broadcasted_iota(jnp.int32, xc.shape, 1)
    upper = (lane & HALF) != 0
    rot = jnp.where(upper, pltpu.roll(xc, HALF, 1), pltpu.roll(xc, V7X_LANES - HALF, 1))
    return xc * cos_t + rot * sin_s


def _rope_sign(shape):
    lane = lax.broadcasted_iota(jnp.int32, shape, 1)
    return jnp.where((lane & HALF) != 0, 1.0, -1.0).astype(F32)


def _kv_variants(t):
    lane = lax.broadcasted_iota(jnp.int32, t.shape, 1)
    lo = lane < HEAD_DIM
    sw = pltpu.roll(t, HEAD_DIM, 1)
    zero = jnp.zeros_like(t)
    return (jnp.where(lo, t, zero).astype(BF16), jnp.where(lo, zero, sw).astype(BF16),
            jnp.where(lo, sw, zero).astype(BF16), jnp.where(lo, zero, t).astype(BF16))


def _kv_variants_t(t):
    tb = t.astype(BF16)
    h0, h1 = tb[0:HEAD_DIM], tb[HEAD_DIM:2 * HEAD_DIM]
    zero = jnp.zeros_like(h0)
    return (jnp.concatenate([h0, zero], axis=0), jnp.concatenate([zero, h0], axis=0),
            jnp.concatenate([h1, zero], axis=0), jnp.concatenate([zero, h1], axis=0))


def _scores(qs, ka, kb, bad, keys_on_lanes=False):
    dot = _mm if keys_on_lanes else _mm_t
    s_e = dot(qs, ka)
    s_o = dot(qs, kb)
    if bad is not None:
        s_e = jnp.where(bad, NEG, s_e)
        s_o = jnp.where(bad, NEG, s_o)
    return s_e, s_o


def _softmax_parts(s, sink):
    m = jnp.maximum(jnp.max(s, axis=-1, keepdims=True), sink)
    e = jnp.exp(s - m)
    r = 1.0 / (jnp.sum(e, axis=-1, keepdims=True) + jnp.exp(sink - m))
    return e.astype(BF16), r


def _softmax_parts_inline_sink(s):
    m = jnp.max(s, axis=-1, keepdims=True)
    e = jnp.exp(s - m)
    return e.astype(BF16), 1.0 / jnp.sum(e, axis=-1, keepdims=True)


def _pv(p_e, p_o, va, vb, r_e, r_o, keys_on_lanes=False):
    dot = _mm_t if keys_on_lanes else _mm
    o = dot(p_e, va) + dot(p_o, vb)
    lane = lax.broadcasted_iota(jnp.int32, o.shape, 1)
    return o * jnp.where(lane < HEAD_DIM, r_e, r_o)


def _sink_cols(sink_ref, n_blocks, rows_per_pair, blocks_per_kv=1):
    row = lax.broadcasted_iota(jnp.int32, (n_blocks * 2 * rows_per_pair, 1), 0)
    second_pair = (row // rows_per_pair) % 2 == 1
    second_kv = (row // (2 * rows_per_pair * blocks_per_kv)) % 2 == 1

    def pick(odd):
        kv0 = jnp.where(second_pair, sink_ref[2 + odd], sink_ref[odd])
        kv1 = jnp.where(second_pair, sink_ref[GROUP_HEADS + 2 + odd], sink_ref[GROUP_HEADS + odd])
        return jnp.where(second_kv, kv1, kv0)

    return pick(0), pick(1)


def _stage_chunks(shape, col_order=None):
    rows, cols = shape
    assert rows % STAGE_ROWS == 0 and cols % V7X_LANES == 0
    width = max(c for c in range(V7X_LANES, STAGE_COLS + 1, V7X_LANES) if cols % c == 0)
    blocks = list(range(cols // width)) if col_order is None else list(col_order)
    assert sorted(blocks) == list(range(cols // width))
    return [(r0, b * width, STAGE_ROWS, width) for b in blocks for r0 in range(0, rows, STAGE_ROWS)]


class _WeightStager:
    def __init__(self, chunks, stage, sem):
        self.chunks, self.stage, self.sem, self.done = chunks, stage, sem, 0

    def _copy(self, n):
        src, _, r0, c0, rows, cols = self.chunks[n]
        slot = n % STAGE_SLOTS
        return pltpu.make_async_copy(src.at[pl.ds(r0, rows), pl.ds(c0, cols)],
                                     self.stage.at[slot, pl.ds(0, rows), pl.ds(0, cols)],
                                     self.sem.at[slot])

    def start(self):
        for n in range(min(STAGE_SLOTS, len(self.chunks))):
            self._copy(n).start()

    def pump(self, count):
        while self.done < min(count, len(self.chunks)):
            n = self.done
            _, dst, r0, c0, rows, cols = self.chunks[n]
            self._copy(n).wait()
            dst[r0:r0 + rows, c0:c0 + cols] = (
                self.stage[n % STAGE_SLOTS, 0:rows, 0:cols].astype(BF16))
            if n + STAGE_SLOTS < len(self.chunks):
                self._copy(n + STAGE_SLOTS).start()
            self.done += 1

    def finish(self):
        self.pump(len(self.chunks))


def _qkv_proj(x, lng_ref, win_ref):
    hb = _rmsnorm(x, lng_ref[...]).astype(BF16)
    return hb, _mm(hb, win_ref[:, OFF_Q:OFF_GA])


def _tile_lanes(g, width):
    while g.shape[1] < width:
        g = jnp.concatenate([g, g], axis=1)
    return g


def _qk_norm_rope(qkv, gq_ref, gk_ref, bdq_ref, bdk_ref, cos_t, sin_s):
    half_q = ATTN_DIM // 2
    gq = _tile_lanes(gq_ref[...] * (HEAD_DIM ** -0.5), half_q)
    gk = _tile_lanes(gk_ref[...], KV_DIM)
    q_cols = []
    for j in range(2):
        t = qkv[:, j * half_q:(j + 1) * half_q]
        tn = _head_norm(t, bdq_ref[...], gq)
        for c in range(half_q // V7X_LANES):
            q_cols.append(_rope(tn[:, c * V7X_LANES:(c + 1) * V7X_LANES], cos_t, sin_s).astype(BF16))
    kn = _head_norm(qkv[:, OFF_K:OFF_V], bdk_ref[...], gk)
    kr = _rope(kn, cos_t, sin_s)
    v = qkv[:, OFF_V:OFF_GA]
    return q_cols, kr, v


def _conv_proj(hb, win_ref):
    return _mm(hb, win_ref[:, OFF_B:OFF_GC_END])


def _conv_input(bcug, conv_fn):
    b_gate = bcug[:, 0:CONV_DIM]
    u = bcug[:, CONV_DIM:2 * CONV_DIM] * bcug[:, 2 * CONV_DIM:3 * CONV_DIM]
    gate_c = bcug[:, 3 * CONV_DIM:4 * CONV_DIM]
    conv = conv_fn(u)
    return (b_gate * conv * _silu(gate_c)).astype(BF16)


def _prompt_body(i, stager, x_ref, p_ref, lng_ref, gq_ref, gk_ref, invf_ref, sink_ref, cw_ref,
                 bdq_ref, bdk_ref, win_ref, wa_ref, wb_ref, wo_ref, wpg_ref, wpp_ref,
                 y_ref, ko_ref, vo_ref, co_ref, kbuf, vbuf, ubuf, attn_buf, tab):
    tm, d_model = x_ref.shape
    invf = invf_ref[...]
    first = stager is not None

    if first:
        kbuf[:, :, 0:WINDOW] = jnp.zeros((4, KV_DIM, WINDOW), BF16)
        vbuf[:, 0:WINDOW, :] = jnp.zeros((4, WINDOW, V7X_LANES), BF16)
        ubuf[0:V7X_SUBLANES, :] = jnp.zeros((V7X_SUBLANES, CONV_DIM), F32)
        r = lax.broadcasted_iota(jnp.int32, (tm, V7X_LANES), 0).astype(F32)
        ang = r * invf
        sgn = _rope_sign((tm, V7X_LANES))
        c_r = jnp.cos(ang)
        s_r = jnp.sin(ang)
        tab[0] = c_r
        tab[1] = s_r
        tab[2] = c_r * sgn
        tab[3] = s_r * sgn

    base = (i * tm).astype(F32) * invf
    cb = jnp.cos(base)
    sb = jnp.sin(base)
    cos_t = tab[0] * cb - tab[1] * sb
    sin_s = tab[3] * cb + tab[2] * sb

    sub = PROMPT_SUBTILE
    n_sub = tm // sub
    sub_chunks = sub // CHUNK
    assert 2 * CHUNK == V7X_LANES and WINDOW == V7X_LANES
    n_keys = 2 * V7X_LANES

    def key_window_start(r0, c):
        return ((r0 + c * CHUNK) // V7X_LANES) * V7X_LANES

    def mask_window(sc, r0, c, sink):
        lo, hi = sc[:, 0:V7X_LANES], sc[:, V7X_LANES:n_keys]
        lane = lax.broadcasted_iota(jnp.int32, lo.shape, 1)
        no_carry = first and key_window_start(r0, c) < WINDOW
        if no_carry:
            lo = jnp.full_like(lo, NEG)
        if ((r0 + c * CHUNK) // CHUNK) % 2 == 0:
            hi = jnp.where(lane < CHUNK, hi, jnp.where(lane == V7X_LANES - 1, sink, NEG))
        else:
            lo = jnp.where((lane >= CHUNK) & (not no_carry), lo, jnp.where(lane == 0, sink, NEG))
        return jnp.concatenate([lo, hi], axis=1)

    def drop_sink_lane(p, c):
        lo, hi = p[:, 0:V7X_LANES], p[:, V7X_LANES:n_keys]
        lane = lax.broadcasted_iota(jnp.int32, lo.shape, 1)
        if c % 2 == 0:
            hi = jnp.where(lane == V7X_LANES - 1, jnp.zeros_like(hi), hi)
        else:
            lo = jnp.where(lane == 0, jnp.zeros_like(lo), lo)
        return jnp.concatenate([lo, hi], axis=1)

    pad = V7X_SUBLANES
    blocks = [(c0, vh) for c0 in range(0, sub_chunks, 2) for vh in range(N_KV_HEADS)]
    st = [dict(r0=s * sub) for s in range(n_sub)]

    def stage_rms(s):
        s["hb"] = _rmsnorm(x_ref[s["r0"]:s["r0"] + sub, :], lng_ref[...]).astype(BF16)

    def stage_qkv(s):
        s["qkv"] = _mm(s["hb"], win_ref[:, OFF_Q:OFF_GA])

    def stage_gate_a(s):
        s["sig_a"] = _sigmoid(_mm(s["hb"], win_ref[:, OFF_MA:OFF_MA + d_model]))

    def stage_gate_c(s):
        s["sig_c"] = _sigmoid(_mm(s["hb"], win_ref[:, OFF_MA + d_model:OFF_MA + 2 * d_model]))

    def stage_qk_norm(s):
        r0 = s["r0"]
        s["q_cols"], kr, v = _qk_norm_rope(s.pop("qkv"), gq_ref, gk_ref, bdq_ref, bdk_ref,
                                           cos_t[r0:r0 + sub], sin_s[r0:r0 + sub])
        kr_t = kr.T
        for n, t in enumerate(_kv_variants_t(kr_t)):
            kbuf[n, :, WINDOW + r0:WINDOW + r0 + sub] = t
        for n, t in enumerate(_kv_variants(v)):
            vbuf[n, WINDOW + r0:WINDOW + r0 + sub, :] = t
        if r0 + sub == tm:
            ko_ref[...] = kr_t[:, sub - WINDOW:sub]
            vo_ref[...] = v[sub - WINDOW:sub, :].T

    def stage_scores(s):
        r0 = s["r0"]
        s_e, s_o = [], []
        for c0, vh in blocks:
            start = key_window_start(r0, c0)
            win = slice(start, start + n_keys)
            qs = jnp.concatenate(
                [s["q_cols"][2 * vh + pair][(c0 + dc) * CHUNK:(c0 + dc + 1) * CHUNK]
                 for dc in range(2) for pair in range(2)], axis=0)
            se, so = _scores(qs, kbuf[2 * vh, :, win], kbuf[2 * vh + 1, :, win], None,
                             keys_on_lanes=True)
            for dc in range(2):
                half = slice(dc * 2 * CHUNK, (dc + 1) * 2 * CHUNK)
                blk = slice(len(s_e) * 2 * CHUNK, (len(s_e) + 1) * 2 * CHUNK)
                s_e.append(mask_window(se[half], r0, c0 + dc, sink_e[blk]))
                s_o.append(mask_window(so[half], r0, c0 + dc, sink_o[blk]))
        s["s_e"] = jnp.concatenate(s_e, axis=0)
        s["s_o"] = jnp.concatenate(s_o, axis=0)
        del s["q_cols"]

    def stage_conv_proj(s):
        r0 = s["r0"]
        cu = _mm(s["hb"], win_ref[:, OFF_B + CONV_DIM:OFF_B + 3 * CONV_DIM])
        ubuf[pad + r0:pad + r0 + sub, :] = cu[:, 0:CONV_DIM] * cu[:, CONV_DIM:2 * CONV_DIM]

    def stage_conv(s):
        r0 = s["r0"]
        conv = ubuf[pad + r0 - 2:pad + r0 - 2 + sub, :] * cw_ref[0]
        conv = conv + ubuf[pad + r0 - 1:pad + r0 - 1 + sub, :] * cw_ref[1]
        conv = conv + ubuf[pad + r0:pad + r0 + sub, :] * cw_ref[2]
        b_gate = _mm(s["hb"], win_ref[:, OFF_B:OFF_B + CONV_DIM])
        gated = b_gate * conv
        gate_c = _mm(s["hb"], win_ref[:, OFF_B + 3 * CONV_DIM:OFF_B + 4 * CONV_DIM])
        s["c_in"] = (gated * _silu(gate_c)).astype(BF16)

    def stage_conv_out(s):
        s["yc"] = _mm(s.pop("c_in"), wb_ref[...])

    sink_e, sink_o = _sink_cols(sink_ref, 2 * len(blocks), CHUNK, blocks_per_kv=2)

    def stage_softmax(s):
        s["p_e"], s["r_e"] = _softmax_parts_inline_sink(s.pop("s_e"))
        s["p_o"], s["r_o"] = _softmax_parts_inline_sink(s.pop("s_o"))

    def stage_pv(s):
        r0 = s["r0"]
        for n, (c0, vh) in enumerate(blocks):
            start = key_window_start(r0, c0)
            win = slice(start, start + n_keys)
            br = slice(n * 4 * CHUNK, (n + 1) * 4 * CHUNK)
            p_e, p_o = (jnp.concatenate(
                [drop_sink_lane(p[(2 * n + dc) * 2 * CHUNK:(2 * n + dc + 1) * 2 * CHUNK], c0 + dc)
                 for dc in range(2)], axis=0) for p in (s["p_e"], s["p_o"]))
            o = _pv(p_e, p_o, vbuf[2 * vh, win, :], vbuf[2 * vh + 1, win, :],
                    s["r_e"][br], s["r_o"][br])
            for dc in range(2):
                rows = slice(r0 + (c0 + dc) * CHUNK, r0 + (c0 + dc + 1) * CHUNK)
                for pair in range(2):
                    part = o[(2 * dc + pair) * CHUNK:(2 * dc + pair + 1) * CHUNK]
                    attn_buf[rows, (2 * vh + pair) * V7X_LANES:(2 * vh + pair + 1) * V7X_LANES] = part

    def stage_attn_gate(s):
        s["silu_ga"] = _silu(_mm(s["hb"], win_ref[:, OFF_GA:OFF_B]))

    def stage_attn_out(s):
        r0 = s["r0"]
        a_in = (attn_buf[r0:r0 + sub, :] * s.pop("silu_ga")).astype(BF16)
        ya = _mm(a_in, wa_ref[...])
        s["mix"] = (s.pop("sig_a") * ya + s.pop("sig_c") * s.pop("yc")).astype(BF16)

    def stage_out_proj(s):
        r0 = s["r0"]
        s["r"] = x_ref[r0:r0 + sub, :] + _mm(s.pop("mix"), wo_ref[...])
        s["pp"] = _mm(p_ref[r0:r0 + sub, :].astype(BF16), wpp_ref[...])

    def stage_ple(s):
        r0 = s["r0"]
        r = s.pop("r")
        gate = _sigmoid(_mm(r.astype(BF16), wpg_ref[...]))
        y_ref[r0:r0 + sub, :] = r + gate * s.pop("pp")

    stages = [stage_rms, stage_qkv, stage_qk_norm, stage_gate_a, stage_scores, stage_gate_c,
              stage_softmax, stage_conv_proj, stage_conv, stage_attn_gate, stage_conv_out,
              stage_pv, stage_attn_out, stage_out_proj, stage_ple]
    needs = {stage_rms: 0, stage_qkv: 4, stage_qk_norm: 8, stage_gate_a: 12, stage_scores: 14,
             stage_gate_c: 16, stage_softmax: 20, stage_conv_proj: 24, stage_conv: 25,
             stage_attn_gate: 25, stage_conv_out: 26, stage_pv: 27, stage_attn_out: 28,
             stage_out_proj: 33, stage_ple: 37}
    for t in range(len(stages) + PROMPT_STAGE_SKEW * (n_sub - 1)):
        for j, s in enumerate(st):
            k = t - j * PROMPT_STAGE_SKEW
            if 0 <= k < len(stages):
                if first and j == 0:
                    stager.pump(needs[stages[k]])
                stages[k](s)

    co_ref[...] = ubuf[pad + tm - 2:pad + tm, :]
    ubuf[0:pad, :] = ubuf[tm:tm + pad, :]
    kbuf[:, :, 0:WINDOW] = kbuf[:, :, tm:tm + WINDOW]
    vbuf[:, 0:WINDOW, :] = vbuf[:, tm:tm + WINDOW, :]


def _sample_body(i, x_ref, p_ref, ck_ref, cv_ref, sc_ref, lng_ref, gq_ref, gk_ref, invf_ref,
                 sink_ref, cw_ref, bdq_ref, bdk_ref, win_ref, wa_ref, wb_ref, wo_ref, wpg_ref,
                 wpp_ref, y_ref, ko_ref, vo_ref, co_ref, ubuf, attn_buf, conv_buf, tab):
    bb, _, cache_len = ck_ref.shape
    rows_total, d_model = x_ref.shape
    t_new = rows_total // bb
    n_keys = cache_len + t_new

    @pl.when(i == 0)
    def _init():
        r = lax.broadcasted_iota(jnp.int32, (rows_total, V7X_LANES), 0)
        pos = (PAST_LEN + lax.rem(r, t_new)).astype(F32)
        ang = pos * invf_ref[...]
        tab[0] = jnp.cos(ang)
        tab[1] = jnp.sin(ang) * _rope_sign((rows_total, V7X_LANES))

    def conv_fn(u):
        pad = V7X_SUBLANES
        for b in range(bb):
            rows = slice(b * t_new, (b + 1) * t_new)
            ub = u[rows]
            ubuf[b, pad - (CONV_WIDTH - 1):pad, :] = sc_ref[b]
            ubuf[b, pad:pad + t_new, :] = ub
            conv = ubuf[b, pad - 2:pad - 2 + t_new, :] * cw_ref[0]
            conv = conv + ubuf[b, pad - 1:pad - 1 + t_new, :] * cw_ref[1]
            conv_buf[rows, :] = conv + ub * cw_ref[2]
            co_ref[b] = ubuf[b, pad + t_new - (CONV_WIDTH - 1):pad + t_new, :]
        return conv_buf[...]

    x = x_ref[...]
    hb, qkv = _qkv_proj(x, lng_ref, win_ref)
    sig_a = _sigmoid(_mm(hb, win_ref[:, OFF_MA:OFF_MA + d_model]))
    q_cols, kr, v = _qk_norm_rope(qkv, gq_ref, gk_ref, bdq_ref, bdk_ref, tab[0], tab[1])

    qi = lax.broadcasted_iota(jnp.int32, (2 * t_new, n_keys), 0)
    q_pos = PAST_LEN + lax.rem(qi, t_new)
    k_pos = PAST_LEN - cache_len + lax.broadcasted_iota(jnp.int32, (2 * t_new, n_keys), 1)
    q_ch = q_pos // CHUNK
    k_ch = k_pos // CHUNK
    bad = jnp.logical_not((k_ch <= q_ch) & (k_ch >= q_ch - WINDOW_CHUNKS))

    blocks = [(b, vh) for b in range(bb) for vh in range(N_KV_HEADS)]
    s_e, s_o, vvars = [], [], []
    kr_t = kr.T
    v_t = v.T
    for b in range(bb):
        rows = slice(b * t_new, (b + 1) * t_new)
        kcat = jnp.concatenate([ck_ref[b], kr_t[:, rows]], axis=1)
        vcat = jnp.concatenate([cv_ref[b], v_t[:, rows]], axis=1)
        ko_ref[b] = kcat[:, n_keys - cache_len:n_keys]
        vo_ref[b] = vcat[:, n_keys - cache_len:n_keys]
        kvar = _kv_variants_t(kcat)
        vvars.append(_kv_variants_t(vcat))
        for vh in range(N_KV_HEADS):
            qs = jnp.concatenate([q_cols[2 * vh][rows], q_cols[2 * vh + 1][rows]], axis=0)
            se, so = _scores(qs, kvar[2 * vh], kvar[2 * vh + 1], bad, keys_on_lanes=True)
            s_e.append(se)
            s_o.append(so)

    sig_c = _sigmoid(_mm(hb, win_ref[:, OFF_MA + d_model:OFF_MA + 2 * d_model]))
    sink_e, sink_o = _sink_cols(sink_ref, len(blocks), t_new)
    p_e, r_e = _softmax_parts(jnp.concatenate(s_e, axis=0), sink_e)
    bcug = _conv_proj(hb, win_ref)
    p_o, r_o = _softmax_parts(jnp.concatenate(s_o, axis=0), sink_o)
    c_in = _conv_input(bcug, conv_fn)
    silu_ga = _silu(_mm(hb, win_ref[:, OFF_GA:OFF_B]))
    pp = _mm(p_ref[...].astype(BF16), wpp_ref[...])
    yc = _mm(c_in, wb_ref[...])

    for n, (b, vh) in enumerate(blocks):
        rows = slice(b * t_new, (b + 1) * t_new)
        br = slice(n * 2 * t_new, (n + 1) * 2 * t_new)
        o = _pv(p_e[br], p_o[br], vvars[b][2 * vh], vvars[b][2 * vh + 1], r_e[br], r_o[br],
                keys_on_lanes=True)
        attn_buf[rows, (2 * vh) * V7X_LANES:(2 * vh + 1) * V7X_LANES] = o[0:t_new]
        attn_buf[rows, (2 * vh + 1) * V7X_LANES:(2 * vh + 2) * V7X_LANES] = o[t_new:2 * t_new]

    ya = _mm((attn_buf[0:rows_total, :] * silu_ga).astype(BF16), wa_ref[...])
    mix = (sig_a * ya + sig_c * yc).astype(BF16)
    r = x + _mm(mix, wo_ref[...])
    gate = _sigmoid(_mm(r.astype(BF16), wpg_ref[...]))
    y_ref[...] = r + gate * pp


def _layer_kernel(n_prompt_steps, *refs):
    (xp_ref, pp_ref, xs_ref, ps_ref, ck_ref, cv_ref, sc_ref,
     lng_ref, gq_ref, gk_ref, invf_ref, sink_ref, cw_ref, bdq_ref, bdk_ref,
     win_hbm, wa_hbm, wb_hbm, wo_hbm, wpg_hbm, wpp_hbm,
     yp_ref, kpo_ref, vpo_ref, cpo_ref, ys_ref, kso_ref, vso_ref, cso_ref,
     kbuf, vbuf, ubuf_p, attn_buf, tab_p, ubuf_s, conv_buf, tab_s,
     win_ref, wa_ref, wb_ref, wo_ref, wpg_ref, wpp_ref, stage, sem) = refs
    i = pl.program_id(0)
    small = (lng_ref, gq_ref, gk_ref, invf_ref, sink_ref, cw_ref, bdq_ref, bdk_ref)
    weights = (win_ref, wa_ref, wb_ref, wo_ref, wpg_ref, wpp_ref)

    prompt_refs = (xp_ref, pp_ref, *small, *weights, yp_ref, kpo_ref, vpo_ref, cpo_ref,
                   kbuf, vbuf, ubuf_p, attn_buf, tab_p)

    @pl.when(i == 0)
    def _first_prompt():
        chunks = [(src, dst) + ch for src, dst, order in (
            (win_hbm, win_ref, _WIN_BLOCK_ORDER), (wb_hbm, wb_ref, None), (wa_hbm, wa_ref, None),
            (wo_hbm, wo_ref, None), (wpp_hbm, wpp_ref, None), (wpg_hbm, wpg_ref, None))
            for ch in _stage_chunks(src.shape, order)]
        stager = _WeightStager(chunks, stage, sem)
        stager.start()
        _prompt_body(i, stager, *prompt_refs)
        stager.finish()

    @pl.when((i > 0) & (i < n_prompt_steps))
    def _prompt():
        _prompt_body(i, None, *prompt_refs)

    @pl.when(i >= n_prompt_steps)
    def _sample():
        _sample_body(i - n_prompt_steps, xs_ref, ps_ref, ck_ref, cv_ref, sc_ref, *small, *weights,
                     ys_ref, kso_ref, vso_ref, cso_ref, ubuf_s, attn_buf, conv_buf, tab_s)


def _const_spec(shape):
    nd = len(shape)
    return pl.BlockSpec(shape, lambda i: (0,) * nd, pipeline_mode=pl.Buffered(1))


def _smem_spec():
    return pl.BlockSpec(memory_space=pltpu.SMEM)


def _block_diag_mean(width):
    idx = np.arange(width) // HEAD_DIM
    return jnp.asarray((idx[:, None] == idx[None, :]).astype(np.float32) / HEAD_DIM, dtype=BF16)


def _layer_consts(ln_g, w_in, q_norm_g, k_norm_g, sink, conv_w, w_attn_out, w_conv_out, w_o,
                  w_ple_gate, w_ple_proj):
    inv_freq = ROPE_THETA ** (-jnp.arange(0, HALF, dtype=F32) * 2.0 / HEAD_DIM)
    return dict(
        lng=ln_g.reshape(1, -1).astype(F32),
        gq=q_norm_g.astype(F32).reshape(1, HEAD_DIM),
        gk=k_norm_g.astype(F32).reshape(1, HEAD_DIM),
        invf=jnp.tile(inv_freq, V7X_LANES // HALF).reshape(1, V7X_LANES),
        sink=sink.astype(F32),
        cw=conv_w.astype(F32),
        bdq=_block_diag_mean(ATTN_DIM // 2),
        bdk=_block_diag_mean(KV_DIM),
        win=w_in.astype(F32),
        wa=w_attn_out.astype(F32),
        wb=w_conv_out.astype(F32),
        wo=w_o.astype(F32),
        wpg=w_ple_gate.astype(F32),
        wpp=w_ple_proj.astype(F32),
    )


_SMALL_KEYS = ("lng", "gq", "gk", "invf", "sink", "cw", "bdq", "bdk")
_STAGED_KEYS = ("win", "wa", "wb", "wo", "wpg", "wpp")


def _heads_first(t):
    t = jnp.moveaxis(t, -3, -1)
    return t.reshape(t.shape[:-3] + (KV_DIM, t.shape[-1]))


def _heads_last(t):
    t = t.reshape(t.shape[:-2] + (N_KV_HEADS, HEAD_DIM, t.shape[-1]))
    return jnp.moveaxis(t, -1, -3)


def _layer(xp, pp, xs, ps, cache_k, cache_v, state_conv, c):
    t, d = xp.shape
    nb, t_new, _ = xs.shape
    cache_len = cache_k.shape[1]
    tm, bb = PROMPT_TILE, SAMPLE_BATCH_TILE
    assert t % tm == 0 and tm % PROMPT_SUBTILE == 0
    assert PROMPT_SUBTILE % CHUNK == 0 and PROMPT_SUBTILE >= WINDOW
    assert nb % bb == 0 and CONV_WIDTH - 1 <= t_new <= cache_len
    n_p, n_s = t // tm, nb // bb
    rows = bb * t_new
    xs2 = xs.reshape(nb * t_new, d)
    ps2 = ps.reshape(nb * t_new, ps.shape[-1])
    ck = _heads_first(cache_k)
    cv = _heads_first(cache_v)

    def p_idx(i):
        return jnp.minimum(i, n_p - 1)

    def s_idx(i):
        return jnp.maximum(i - n_p, 0)

    in_specs = ([pl.BlockSpec((tm, d), lambda i: (p_idx(i), 0)),
                 pl.BlockSpec((tm, pp.shape[1]), lambda i: (p_idx(i), 0)),
                 pl.BlockSpec((rows, d), lambda i: (s_idx(i), 0)),
                 pl.BlockSpec((rows, ps2.shape[1]), lambda i: (s_idx(i), 0)),
                 pl.BlockSpec((bb, KV_DIM, cache_len), lambda i: (s_idx(i), 0, 0)),
                 pl.BlockSpec((bb, KV_DIM, cache_len), lambda i: (s_idx(i), 0, 0)),
                 pl.BlockSpec((bb, CONV_WIDTH - 1, CONV_DIM), lambda i: (s_idx(i), 0, 0))]
                + [_smem_spec() if k == "sink" else _const_spec(c[k].shape) for k in _SMALL_KEYS]
                + [pl.BlockSpec(memory_space=pl.ANY) for _ in _STAGED_KEYS])
    out_shape = (jax.ShapeDtypeStruct((t, d), F32),
                 jax.ShapeDtypeStruct((KV_DIM, WINDOW), F32),
                 jax.ShapeDtypeStruct((KV_DIM, WINDOW), F32),
                 jax.ShapeDtypeStruct((CONV_WIDTH - 1, CONV_DIM), F32),
                 jax.ShapeDtypeStruct((nb * t_new, d), F32),
                 jax.ShapeDtypeStruct((nb, KV_DIM, cache_len), F32),
                 jax.ShapeDtypeStruct((nb, KV_DIM, cache_len), F32),
                 jax.ShapeDtypeStruct((nb, CONV_WIDTH - 1, CONV_DIM), F32))
    out_specs = (pl.BlockSpec((tm, d), lambda i: (p_idx(i), 0)),
                 pl.BlockSpec((KV_DIM, WINDOW), lambda i: (0, 0)),
                 pl.BlockSpec((KV_DIM, WINDOW), lambda i: (0, 0)),
                 pl.BlockSpec((CONV_WIDTH - 1, CONV_DIM), lambda i: (0, 0)),
                 pl.BlockSpec((rows, d), lambda i: (s_idx(i), 0)),
                 pl.BlockSpec((bb, KV_DIM, cache_len), lambda i: (s_idx(i), 0, 0)),
                 pl.BlockSpec((bb, KV_DIM, cache_len), lambda i: (s_idx(i), 0, 0)),
                 pl.BlockSpec((bb, CONV_WIDTH - 1, CONV_DIM), lambda i: (s_idx(i), 0, 0)))
    scratch = ([pltpu.VMEM((4, KV_DIM, WINDOW + tm), BF16),
                pltpu.VMEM((4, WINDOW + tm, V7X_LANES), BF16),
                pltpu.VMEM((V7X_SUBLANES + tm, CONV_DIM), F32),
                pltpu.VMEM((max(tm, rows), ATTN_DIM), F32),
                pltpu.VMEM((4, tm, V7X_LANES), F32),
                pltpu.VMEM((bb, V7X_SUBLANES + t_new, CONV_DIM), F32),
                pltpu.VMEM((rows, CONV_DIM), F32),
                pltpu.VMEM((2, rows, V7X_LANES), F32)]
               + [pltpu.VMEM(c[k].shape, BF16) for k in _STAGED_KEYS]
               + [pltpu.VMEM((STAGE_SLOTS, STAGE_ROWS, STAGE_COLS), F32),
                  pltpu.SemaphoreType.DMA((STAGE_SLOTS,))])
    yp, kpo, vpo, cpo, ys, kso, vso, cso = pl.pallas_call(
        functools.partial(_layer_kernel, n_p),
        grid=(n_p + n_s,),
        in_specs=in_specs,
        out_specs=out_specs,
        out_shape=out_shape,
        scratch_shapes=scratch,
        compiler_params=pltpu.CompilerParams(dimension_semantics=("arbitrary",),
                                             vmem_limit_bytes=V7X_VMEM_LIMIT_BYTES),
        name="hybrid_layer",
    )(xp, pp, xs2, ps2, ck, cv, state_conv, *[c[k] for k in _SMALL_KEYS],
      *[c[k] for k in _STAGED_KEYS])
    return ((yp, _heads_last(kpo), _heads_last(vpo), cpo),
            (ys.reshape(nb, t_new, d), _heads_last(kso), _heads_last(vso), cso))


def kernel(x_prompt, x_sample, p_prompt, p_sample, cache_k, cache_v, state_conv, ln_g, w_in,
           q_norm_g, k_norm_g, sink, conv_w, w_attn_out, w_conv_out, w_o, w_ple_gate, w_ple_proj):
    depth = ln_g.shape[0]
    assert x_prompt.shape[0] == 1, "one prompt sequence per call"
    hp, hs = x_prompt[0], x_sample
    kp_l, vp_l, cp_l, ks_l, vs_l, cs_l = [], [], [], [], [], []
    for i in range(depth):
        c = _layer_consts(ln_g[i], w_in[i], q_norm_g[i], k_norm_g[i], sink[i],
                          jnp.swapaxes(conv_w, 0, 1)[:, i:i + 1, :],
                          w_attn_out[i], w_conv_out[i], w_o[i], w_ple_gate[i], w_ple_proj[i])
        (hp, kp, vp, cp), (hs, ks, vs, cs) = _layer(hp, p_prompt[i, 0], hs, p_sample[i], cache_k[i],
                                                    cache_v[i], state_conv[i], c)
        kp_l.append(kp[None])
        vp_l.append(vp[None])
        cp_l.append(cp[None])
        ks_l.append(ks)
        vs_l.append(vs)
        cs_l.append(cs)
    return (hp[None], hs, jnp.stack(kp_l), jnp.stack(vp_l), jnp.stack(cp_l),
            jnp.stack(ks_l), jnp.stack(vs_l), jnp.stack(cs_l))
```
